```python
import jax
import jax.numpy as jnp
from jax import lax
import numpy as np

D_MODEL = 1024
BATCH = 2
SEQ = 16384
DEPTH = 2

GRID_W = 64
CTX_LEN = 256
EPS = 1e-6
N_MOD = 6

MLA_HEADS = 8
MLA_Q_RANK = 384
MLA_KV_RANK = 256
MLA_NOPE = 64
MLA_ROPE = 32
MLA_QK = MLA_NOPE + MLA_ROPE
MLA_V = 64
MLA_WIDTH = MLA_HEADS * MLA_V
ROPE_THETA = 10000.0
Q_BLOCK = 128

HY_WIDTH = 256
HY_SHORT = 3
HY_BANDS = 16
HY_EMB = 1 + 2 * HY_BANDS
HY_HIDDEN = 64
HY_DIRS = 2

GM_WIDTH = 256
GM_CHUNK = 128
GM_GROUPS = 4
GM_GROUP_W = GM_WIDTH // GM_GROUPS

N_BRANCH = 3
OFF_Q = 0
OFF_KV = OFF_Q + MLA_Q_RANK
OFF_KR = OFF_KV + MLA_KV_RANK
OFF_HY = OFF_KR + MLA_ROPE
OFF_GM = OFF_HY + 3 * HY_WIDTH
OFF_GT = OFF_GM + 2 * GM_WIDTH
IN_W = OFF_GT + N_BRANCH * D_MODEL

N_EXPERTS = 16
N_GROUPS = 4
EXP_PER_GROUP = N_EXPERTS // N_GROUPS
TOP_K = 2
D_EXPERT = 512

kernel_name = "hybrid_mla_hyena_gmlp_moe_dit"


def rms_norm(x, g):
    xf = x.astype(jnp.float32)
    y = xf * lax.rsqrt(jnp.mean(xf * xf, axis=-1, keepdims=True) + EPS)
    return (y * g.astype(jnp.float32)).astype(x.dtype)


def modulate(x, g, shift, scale):
    return rms_norm(x, g) * (1 + scale) + shift


def axial_rope_tables(rows):
    row = jnp.repeat(jnp.arange(rows, dtype=jnp.float32), GRID_W)
    col = jnp.tile(jnp.arange(GRID_W, dtype=jnp.float32), rows)
    n_freq = MLA_ROPE // 4
    inv = ROPE_THETA ** (-jnp.arange(n_freq, dtype=jnp.float32) / n_freq)
    ang = jnp.concatenate([row[:, None] * inv, col[:, None] * inv], axis=-1)
    return jnp.cos(ang), jnp.sin(ang)


def apply_rope(x, cos, sin):
    xr = x.astype(jnp.float32).reshape(*x.shape[:-1], MLA_ROPE // 2, 2)
    x1, x2 = xr[..., 0], xr[..., 1]
    c = cos[None, :, None, :]
    s = sin[None, :, None, :]
    out = jnp.stack([x1 * c - x2 * s, x1 * s + x2 * c], axis=-1)
    return out.reshape(x.shape).astype(x.dtype)


def mla_q(pq, g_qa, w_uq, g_qn, rope):
    B, L, _ = pq.shape
    q = (rms_norm(pq, g_qa) @ w_uq).reshape(B, L, MLA_HEADS, MLA_QK)
    q = rms_norm(q, g_qn)
    if rope is not None:
        q = jnp.concatenate([q[..., :MLA_NOPE], apply_rope(q[..., MLA_NOPE:], *rope)], axis=-1)
    return q


def mla_kv(pkv, pkr, g_kva, w_ukv, g_kn, rope):
    B, L, _ = pkv.shape
    kv = (rms_norm(pkv, g_kva) @ w_ukv).reshape(B, L, MLA_HEADS, MLA_NOPE + MLA_V)
    k_rope = jnp.broadcast_to(pkr[:, :, None, :], (B, L, MLA_HEADS, MLA_ROPE))
    k = rms_norm(jnp.concatenate([kv[..., :MLA_NOPE], k_rope.astype(kv.dtype)], axis=-1), g_kn)
    if rope is not None:
        k = jnp.concatenate([k[..., :MLA_NOPE], apply_rope(k[..., MLA_NOPE:], *rope)], axis=-1)
    return k, kv[..., MLA_NOPE:]


def block_attention(q, k, v):
    B, L, H, Dq = q.shape
    nb = L // Q_BLOCK
    qb = jnp.moveaxis(q.reshape(B, nb, Q_BLOCK, H, Dq), 1, 0)
    scale = Dq ** -0.5

    def attend_block(qblk):
        s = jnp.einsum('bqhd,bkhd->bhqk', qblk, k, preferred_element_type=jnp.float32) * scale
        p = jax.nn.softmax(s, axis=-1).astype(v.dtype)
        return jnp.einsum('bhqk,bkhd->bqhd', p, v)

    o = lax.map(attend_block, qb)
    return jnp.moveaxis(o, 0, 1).reshape(B, L, H * v.shape[-1])


def hyena_filters(L, w1, b1, w2, b2, w3, b3, freq, decay):
    f32 = lambda a: a.astype(jnp.float32)
    t = jnp.arange(L, dtype=jnp.float32)
    t_unit = t / max(L - 1, 1)
    bands = jnp.linspace(1e-4, HY_BANDS - 1, HY_BANDS, dtype=jnp.float32)
    ang = (2 * jnp.pi / L) * t[:, None] * bands[None, :]
    z = jnp.concatenate([t_unit[:, None], jnp.cos(ang), -jnp.sin(ang)], axis=-1)
    hdn = jnp.sin(f32(freq) * (z @ f32(w1) + f32(b1)))
    hdn = jnp.sin(f32(freq) * (hdn @ f32(w2) + f32(b2)))
    k = (hdn @ f32(w3) + f32(b3)).reshape(L, HY_DIRS, HY_WIDTH)
    k = k * jnp.exp(-t_unit[:, None, None] * jnp.abs(f32(decay))[None])
    return k * lax.rsqrt(jnp.sum(k * k, axis=(0, 1), keepdims=True) + EPS)


def bidir_long_conv(u, k):
    B, L, C = u.shape
    filt = jnp.concatenate([k[:, 0], jnp.zeros((1, C), k.dtype), k[:0:-1, 1]], axis=0)
    n = 2 * L
    uf = jnp.fft.rfft(u.astype(jnp.float32), n=n, axis=1)
    kf = jnp.fft.rfft(filt.astype(jnp.float32), n=n, axis=0)
    y = jnp.fft.irfft(uf * kf[None], n=n, axis=1)[:, :L]
    return y.astype(u.dtype)


def hyena_branch(p, conv_w, conv_b, filt, bias):
    L = p.shape[1]
    half = HY_SHORT // 2
    pp = jnp.pad(p, ((0, 0), (half, half), (0, 0)))
    z = conv_b + pp[:, 0:L] * conv_w[0]
    for j in range(1, HY_SHORT):
        z = z + pp[:, j:j + L] * conv_w[j]
    x0, x1, v = jnp.split(z, 3, axis=-1)
    u = v * x1
    return x0 * (bidir_long_conv(u, filt) + u * bias)


def gmlp_branch(p, g_v, w_s, b_s):
    B, L, _ = p.shape
    u, v = jnp.split(jax.nn.gelu(p), 2, axis=-1)
    v = rms_norm(v, g_v).reshape(B, L // GM_CHUNK, GM_CHUNK, GM_GROUPS, GM_GROUP_W)
    s = jnp.einsum('gij,bnjgc->bnigc', w_s, v) + jnp.swapaxes(b_s, 0, 1)[:, :, None]
    return u * s.reshape(B, L, GM_WIDTH)


def merge_branches(a, b, m, p_gate, w_pa, w_pb, w_pc, w_out):
    ga, gb, gm = jnp.split(jax.nn.sigmoid(p_gate), N_BRANCH, axis=-1)
    y = ga * (a @ w_pa) + gb * (b @ w_pb) + gm * (m @ w_pc)
    return y @ w_out


def token_mixer(h, lp, rope, ctx_kv):
    L = h.shape[1]
    p = h @ lp['w_in']
    pq, pkv, pkr = p[..., OFF_Q:OFF_KV], p[..., OFF_KV:OFF_KR], p[..., OFF_KR:OFF_HY]
    phy, pgm, pgt = p[..., OFF_HY:OFF_GM], p[..., OFF_GM:OFF_GT], p[..., OFF_GT:]
    q = mla_q(pq, lp['g_qa'], lp['w_uq'], lp['g_qn'], rope)
    k, v = mla_kv(pkv, pkr, lp['g_kva'], lp['w_ukv'], lp['g_kn'], rope)
    if ctx_kv is None:
        a = block_attention(q, k, v)
    else:
        a = block_attention(q, jnp.concatenate([k, ctx_kv[0].astype(k.dtype)], axis=1),
                            jnp.concatenate([v, ctx_kv[1].astype(v.dtype)], axis=1))
    filt = hyena_filters(L, lp['hy_w1'], lp['hy_b1'], lp['hy_w2'], lp['hy_b2'], lp['hy_w3'], lp['hy_b3'],
                         lp['hy_freq'], lp['hy_decay'])
    b = hyena_branch(phy, lp['hy_conv_w'], lp['hy_conv_b'], filt, lp['hy_bias'])
    m = gmlp_branch(pgm, lp['gm_norm_g'], lp['gm_ws'], lp['gm_bs'])
    y = merge_branches(a, b, m, pgt, lp['w_pa'], lp['w_pb'], lp['w_pc'], lp['w_out'])
    return y, k, v


def context_kv(hc, lp):
    pc = hc @ lp['w_in'][:, OFF_KV:OFF_HY]
    return mla_kv(pc[..., :MLA_KV_RANK], pc[..., MLA_KV_RANK:], lp['g_kva'], lp['w_ukv'], lp['g_kn'], None)


def moe_ffn(h, router_w, router_b, w_gate, w_up, w_down):
    aff = jax.nn.sigmoid(jnp.einsum('bld,de->ble', h, router_w, preferred_element_type=jnp.float32))
    sel = aff + router_b.astype(jnp.float32)
    grp = sel.reshape(*sel.shape[:-1], N_GROUPS, EXP_PER_GROUP)
    grp_score = jnp.sum(lax.top_k(grp, TOP_K)[0], axis=-1)
    in_grp = jnp.argmax(grp_score, axis=-1)[..., None] == jnp.arange(N_GROUPS)
    in_grp = jnp.repeat(in_grp, EXP_PER_GROUP, axis=-1)
    _, idx = lax.top_k(jnp.where(in_grp, sel, -jnp.inf), TOP_K)
    wts = jnp.take_along_axis(aff, idx, axis=-1)
    wts = wts / jnp.sum(wts, axis=-1, keepdims=True)
    comb = jnp.einsum('blk,blke->ble', wts, jax.nn.one_hot(idx, N_EXPERTS, dtype=jnp.float32)).astype(h.dtype)
    out = jnp.zeros_like(h)
    for e in range(N_EXPERTS):
        hid = jax.nn.silu(h @ w_gate[e]) * (h @ w_up[e])
        out = out + comb[..., e:e + 1] * (hid @ w_down[e])
    return out


def setup_inputs(seed: int = 0) -> dict:
    key = jax.random.key(seed)
    ks = iter(jax.random.split(key, 40))

    def nrm(shape, scale):
        return scale * jax.random.normal(next(ks), shape, jnp.float32)

    D = D_MODEL
    decay_base = jnp.linspace(3.07, 15.35, HY_WIDTH, dtype=jnp.float32)
    return {
        'x': nrm((BATCH, SEQ, D), 1.0),
        'c': nrm((BATCH, D), 1.0),
        'ctx': nrm((BATCH, CTX_LEN, D), 1.0),
        'c_ctx': nrm((D,), 1.0),
        'w_mod': nrm((DEPTH, D, N_MOD * D), 0.5 * D ** -0.5),
        'b_mod': nrm((DEPTH, N_MOD * D), 0.02),
        'norm1_g': 1.0 + nrm((DEPTH, D), 0.02),
        'norm2_g': 1.0 + nrm((DEPTH, D), 0.02),
        'w_in': nrm((DEPTH, D, IN_W), D ** -0.5),
        'g_qa': 1.0 + nrm((DEPTH, MLA_Q_RANK), 0.02),
        'w_uq': nrm((DEPTH, MLA_Q_RANK, MLA_HEADS * MLA_QK), MLA_Q_RANK ** -0.5),
        'g_kva': 1.0 + nrm((DEPTH, MLA_KV_RANK), 0.02),
        'w_ukv': nrm((DEPTH, MLA_KV_RANK, MLA_HEADS * (MLA_NOPE + MLA_V)), MLA_KV_RANK ** -0.5),
        'g_qn': 1.0 + nrm((DEPTH, MLA_QK), 0.02),
        'g_kn': 1.0 + nrm((DEPTH, MLA_QK), 0.02),
        'hy_conv_w': nrm((DEPTH, HY_SHORT, 3 * HY_WIDTH), HY_SHORT ** -0.5),
        'hy_conv_b': nrm((DEPTH, 3 * HY_WIDTH), 0.02),
        'hy_w1': nrm((DEPTH, HY_EMB, HY_HIDDEN), HY_EMB ** -0.5),
        'hy_b1': nrm((DEPTH, HY_HIDDEN), 0.1),
        'hy_w2': nrm((DEPTH, HY_HIDDEN, HY_HIDDEN), HY_HIDDEN ** -0.5),
        'hy_b2': nrm((DEPTH, HY_HIDDEN), 0.1),
        'hy_w3': nrm((DEPTH, HY_HIDDEN, HY_DIRS * HY_WIDTH), HY_HIDDEN ** -0.5),
        'hy_b3': nrm((DEPTH, HY_DIRS * HY_WIDTH), 0.02),
        'hy_freq': 1.0 + nrm((DEPTH, HY_HIDDEN), 0.1),
        'hy_decay': decay_base[None, None, :] + nrm((DEPTH, HY_DIRS, HY_WIDTH), 0.1),
        'hy_bias': nrm((DEPTH, HY_WIDTH), 0.5),
        'gm_norm_g': 1.0 + nrm((DEPTH, GM_WIDTH), 0.02),
        'gm_ws': nrm((DEPTH, GM_GROUPS, GM_CHUNK, GM_CHUNK), GM_CHUNK ** -0.5),
        'gm_bs': 1.0 + nrm((DEPTH, GM_GROUPS, GM_CHUNK), 0.02),
        'w_pa': nrm((DEPTH, MLA_WIDTH, D), MLA_WIDTH ** -0.5),
        'w_pb': nrm((DEPTH, HY_WIDTH, D), HY_WIDTH ** -0.5),
        'w_pc': nrm((DEPTH, GM_WIDTH, D), GM_WIDTH ** -0.5),
        'w_out': nrm((DEPTH, D, D), D ** -0.5),
        'router_w': nrm((D, N_EXPERTS), D ** -0.5),
        'router_b': nrm((N_EXPERTS,), 0.01),
        'moe_w_gate': nrm((DEPTH, N_EXPERTS, D, D_EXPERT), D ** -0.5),
        'moe_w_up': nrm((DEPTH, N_EXPERTS, D, D_EXPERT), D ** -0.5),
        'moe_w_down': nrm((DEPTH, N_EXPERTS, D_EXPERT, D), D_EXPERT ** -0.5),
    }


def reference(x, c, ctx, c_ctx, w_mod, b_mod, norm1_g, norm2_g, w_in, g_qa, w_uq, g_kva, w_ukv, g_qn, g_kn,
              hy_conv_w, hy_conv_b, hy_w1, hy_b1, hy_w2, hy_b2, hy_w3, hy_b3, hy_freq, hy_decay, hy_bias,
              gm_norm_g, gm_ws, gm_bs, w_pa, w_pb, w_pc, w_out, router_w, router_b,
              moe_w_gate, moe_w_up, moe_w_down):
    ROWS = x.shape[1] // GRID_W
    rope = axial_rope_tables(ROWS)
    silu_c = jax.nn.silu(c)
    silu_cc = jax.nn.silu(c_ctx)[None]
    xc = ctx
    for l in range(DEPTH):
        last = l == DEPTH - 1
        lp = dict(w_in=w_in[l], g_qa=g_qa[l], w_uq=w_uq[l], g_kva=g_kva[l], w_ukv=w_ukv[l],
                  g_qn=g_qn[l], g_kn=g_kn[l], hy_conv_w=hy_conv_w[l], hy_conv_b=hy_conv_b[l],
                  hy_w1=hy_w1[l], hy_b1=hy_b1[l], hy_w2=hy_w2[l], hy_b2=hy_b2[l], hy_w3=hy_w3[l],
                  hy_b3=hy_b3[l], hy_freq=hy_freq[l], hy_decay=hy_decay[l], hy_bias=hy_bias[l],
                  gm_norm_g=gm_norm_g[l], gm_ws=gm_ws[l], gm_bs=gm_bs[l],
                  w_pa=w_pa[l], w_pb=w_pb[l], w_pc=w_pc[l], w_out=w_out[l])
        mod = jnp.split((silu_c @ w_mod[l] + b_mod[l])[:, None, :], N_MOD, axis=-1)
        n_ctx_mod = 2 if last else N_MOD
        mod_c = jnp.split((silu_cc @ w_mod[l][:, :n_ctx_mod * D_MODEL]
                           + b_mod[l][:n_ctx_mod * D_MODEL])[:, None, :], n_ctx_mod, axis=-1)

        hc = modulate(xc, norm1_g[l], mod_c[0], mod_c[1])
        if last:
            k_c, v_c = context_kv(hc, lp)
        else:
            y_c, k_c, v_c = token_mixer(hc, lp, None, None)
            xc = xc + mod_c[2] * y_c
            xc = xc + mod_c[5] * moe_ffn(modulate(xc, norm2_g[l], mod_c[3], mod_c[4]),
                                         router_w, router_b, moe_w_gate[l], moe_w_up[l], moe_w_down[l])

        h = modulate(x, norm1_g[l], mod[0], mod[1])
        y, _, _ = token_mixer(h, lp, rope, (k_c, v_c))
        x = x + mod[2] * y
        x = x + mod[5] * moe_ffn(modulate(x, norm2_g[l], mod[3], mod[4]),
                                 router_w, router_b, moe_w_gate[l], moe_w_up[l], moe_w_down[l])
    return x
```

```python
import functools
import math

import jax
import jax.numpy as jnp
from jax import lax
from jax.experimental import pallas as pl
from jax.experimental.pallas import tpu as pltpu

F32 = jnp.float32
BF16 = jnp.bfloat16
HIGHEST = lax.Precision.HIGHEST

D_MODEL = 1024
GRID_W = 64
EPS = 1e-6
N_MOD = 6

HEADS = 8
Q_RANK = 384
KV_RANK = 256
NOPE = 64
ROPE = 32
QK = NOPE + ROPE
VDIM = 64
HEAD_PAD = 128
ROPE_THETA = 10000.0
V_ROWS = 80

HY_W = 256
HY_BANDS = 16
GM_W = 256
GM_CHUNK = 128
GM_GROUPS = 4

OFF_Q = 0
OFF_KV = OFF_Q + Q_RANK
OFF_KR = OFF_KV + KV_RANK
OFF_HY = OFF_KR + ROPE
OFF_GM = OFF_HY + 3 * HY_W
OFF_GT = OFF_GM + 2 * GM_W

P_Q = 0
P_KV = P_Q + Q_RANK
P_KR = P_KV + KV_RANK
P_HY = P_KR + HEAD_PAD
P_GM = P_HY + 3 * HY_W
P_GT = P_GM + 2 * GM_W
P_W = P_GT + 3 * D_MODEL

N_EXPERTS = 16
N_GROUPS = 4
EXP_PER_GROUP = 4
D_EXPERT = 512

DFT_N2 = 256

VMEM_LIMIT = 56 * 1024 * 1024
NEG_BIG = -1e30
LOG2E = 1.4426950408889634


def _cparams(sem):
    return pltpu.CompilerParams(dimension_semantics=sem, vmem_limit_bytes=VMEM_LIMIT)


def _rms(x):
    return x * lax.rsqrt(jnp.mean(x * x, axis=-1, keepdims=True) + EPS)


def _nt_dot(a, b):
    return lax.dot_general(a, b, (((1,), (1,)), ((), ())), preferred_element_type=F32)


def _modvec_kernel(c_ref, w_ref, b_ref, o_ref):
    cv = c_ref[...]
    s = cv * jax.nn.sigmoid(cv)
    o_ref[0] = jnp.dot(s, w_ref[0], preferred_element_type=F32, precision=HIGHEST) + b_ref[0]


def _modvec(cvecs, w_mod, b_mod):
    depth = w_mod.shape[0]
    tn = 1536
    return pl.pallas_call(
        _modvec_kernel,
        out_shape=jax.ShapeDtypeStruct((depth, 8, N_MOD * D_MODEL), F32),
        grid=(depth, N_MOD * D_MODEL // tn),
        in_specs=[pl.BlockSpec((8, D_MODEL), lambda l, j: (0, 0)),
                  pl.BlockSpec((1, D_MODEL, tn), lambda l, j: (l, 0, j)),
                  pl.BlockSpec((1, 1, tn), lambda l, j: (l, 0, j))],
        out_specs=pl.BlockSpec((1, 8, tn), lambda l, j: (l, 0, j)),
        compiler_params=_cparams(("arbitrary", "arbitrary")),
        name="modvec",
    )(cvecs, w_mod, b_mod.reshape(depth, 1, N_MOD * D_MODEL))


def _head_norm_rope(xh, gain, rope):
    ms = jnp.sum(xh * xh, axis=-1, keepdims=True) * (1.0 / QK)
    xh = xh * lax.rsqrt(ms + EPS) * gain
    if rope is not None:
        rc, rs1, rs2 = rope
        xh = xh * rc + pltpu.roll(xh, ROPE // 2, 1) * rs1 + pltpu.roll(xh, HEAD_PAD - ROPE // 2, 1) * rs2
    return xh


def _premix_kernel(*refs, use_rope, tm):
    if use_rope:
        (x_ref, shift_ref, scale_ref, g1_ref, win_ref, gqa_ref, wuq_ref, gqn_ref, gkva_ref, wk_ref, wvt_ref,
         gkn_ref, gmg_ref, gmw_ref, gmb_ref, rc_ref, rs1_ref, rs2_ref,
         q_ref, k_ref, vt_ref, m_ref, hy_ref, gt_ref) = refs
        rope = (rc_ref[...], rs1_ref[...], rs2_ref[...])
    else:
        (x_ref, shift_ref, scale_ref, g1_ref, win_ref, gqa_ref, wuq_ref, gqn_ref, gkva_ref, wk_ref, wvt_ref,
         gkn_ref, gmg_ref, gmw_ref, gmb_ref,
         q_ref, k_ref, vt_ref, m_ref, hy_ref, gt_ref) = refs
        rope = None

    x = x_ref[...]
    h = _rms(x) * g1_ref[...]
    h = h * (1.0 + scale_ref[0]) + shift_ref[0]
    hb = h.astype(BF16)

    def proj(lo, width):
        return jnp.dot(hb, win_ref[:, lo:lo + width], preferred_element_type=F32)

    qa = (_rms(proj(P_Q, Q_RANK)) * gqa_ref[...]).astype(BF16)
    q = jnp.dot(qa, wuq_ref[...], preferred_element_type=F32)
    qscale = QK ** -0.5 * LOG2E
    for hh in range(HEADS):
        qh = _head_norm_rope(q[:, hh * HEAD_PAD:(hh + 1) * HEAD_PAD], gqn_ref[...], rope)
        q_ref[:, hh * HEAD_PAD:(hh + 1) * HEAD_PAD] = (qh * qscale).astype(BF16)

    kva = (_rms(proj(P_KV, KV_RANK)) * gkva_ref[...]).astype(BF16)
    kr = pltpu.roll(proj(P_KR, HEAD_PAD), NOPE, 1)
    kn = jnp.dot(kva, wk_ref[...], preferred_element_type=F32)
    for hh in range(HEADS):
        kh = _head_norm_rope(kn[:, hh * HEAD_PAD:(hh + 1) * HEAD_PAD] + kr, gkn_ref[...], rope)
        k_ref[:, hh * HEAD_PAD:(hh + 1) * HEAD_PAD] = kh.astype(BF16)
    vt = _nt_dot(wvt_ref[...], kva)
    row = lax.broadcasted_iota(jnp.int32, (V_ROWS - VDIM, tm), 0)
    ones_rows = jnp.where(row == 0, 1.0, 0.0).astype(BF16)
    for hh in range(HEADS):
        vt_ref[0, 0, hh * V_ROWS:hh * V_ROWS + VDIM, :] = vt[hh * VDIM:(hh + 1) * VDIM].astype(BF16)
        vt_ref[0, 0, hh * V_ROWS + VDIM:(hh + 1) * V_ROWS, :] = ones_rows

    gg = jax.nn.gelu(proj(P_GM, 2 * GM_W), approximate=True)
    gu = gg[:, :GM_W]
    gv = (_rms(gg[:, GM_W:]) * gmg_ref[...]).astype(BF16)
    grp = lax.broadcasted_iota(jnp.int32, (GM_CHUNK, GM_W), 1) // (GM_W // GM_GROUPS)
    for ci in range(tm // GM_CHUNK):
        vc = gv[ci * GM_CHUNK:(ci + 1) * GM_CHUNK]
        s = jnp.zeros((GM_CHUNK, GM_W), F32)
        for g in range(GM_GROUPS):
            sg = jnp.dot(gmw_ref[g], vc, preferred_element_type=F32)
            s = jnp.where(grp == g, sg, s)
        m_ref[ci * GM_CHUNK:(ci + 1) * GM_CHUNK, :] = (
            gu[ci * GM_CHUNK:(ci + 1) * GM_CHUNK] * (s + gmb_ref[...])).astype(BF16)

    hy_ref[...] = proj(P_HY, 3 * HY_W).astype(BF16)
    for j in range(3):
        gt_ref[:, j * D_MODEL:(j + 1) * D_MODEL] = jax.nn.sigmoid(
            proj(P_GT + j * D_MODEL, D_MODEL)).astype(BF16)


def _premix(x, shift, scale, lw, rope_tabs, seq, tm):
    n = x.shape[0]
    tiles_per_seq = seq // tm
    batch = n // seq
    use_rope = rope_tabs is not None
    const = lambda i: (0, 0)
    in_specs = [
        pl.BlockSpec((tm, D_MODEL), lambda i: (i, 0)),
        pl.BlockSpec((1, 1, D_MODEL), lambda i: (i // tiles_per_seq, 0, 0)),
        pl.BlockSpec((1, 1, D_MODEL), lambda i: (i // tiles_per_seq, 0, 0)),
        pl.BlockSpec((1, D_MODEL), const),
        pl.BlockSpec((D_MODEL, P_W), const, pipeline_mode=pl.Buffered(1)),
        pl.BlockSpec((1, Q_RANK), const),
        pl.BlockSpec((Q_RANK, HEADS * HEAD_PAD), const),
        pl.BlockSpec((1, HEAD_PAD), const),
        pl.BlockSpec((1, KV_RANK), const),
        pl.BlockSpec((KV_RANK, HEADS * HEAD_PAD), const),
        pl.BlockSpec((HEADS * VDIM, KV_RANK), const),
        pl.BlockSpec((1, HEAD_PAD), const),
        pl.BlockSpec((1, GM_W), const),
        pl.BlockSpec((GM_GROUPS, GM_CHUNK, GM_CHUNK), lambda i: (0, 0, 0)),
        pl.BlockSpec((GM_CHUNK, GM_W), const),
    ]
    args = [x, shift, scale, lw['g1'], lw['w_in'], lw['g_qa'], lw['w_uq'], lw['g_qn'], lw['g_kva'], lw['w_k'],
            lw['w_vt'], lw['g_kn'], lw['gm_g'], lw['gm_w'], lw['gm_b']]
    if use_rope:
        in_specs += [pl.BlockSpec((tm, HEAD_PAD), lambda i: (i % tiles_per_seq, 0))] * 3
        args += list(rope_tabs)
    out_shape = (
        jax.ShapeDtypeStruct((n, HEADS * HEAD_PAD), BF16),
        jax.ShapeDtypeStruct((n, HEADS * HEAD_PAD), BF16),
        jax.ShapeDtypeStruct((batch, tiles_per_seq, HEADS * V_ROWS, tm), BF16),
        jax.ShapeDtypeStruct((n, GM_W), BF16),
        jax.ShapeDtypeStruct((n, 3 * HY_W), BF16),
        jax.ShapeDtypeStruct((n, 3 * D_MODEL), BF16),
    )
    out_specs = (
        pl.BlockSpec((tm, HEADS * HEAD_PAD), lambda i: (i, 0)),
        pl.BlockSpec((tm, HEADS * HEAD_PAD), lambda i: (i, 0)),
        pl.BlockSpec((1, 1, HEADS * V_ROWS, tm), lambda i: (i // tiles_per_seq, i % tiles_per_seq, 0, 0)),
        pl.BlockSpec((tm, GM_W), lambda i: (i, 0)),
        pl.BlockSpec((tm, 3 * HY_W), lambda i: (i, 0)),
        pl.BlockSpec((tm, 3 * D_MODEL), lambda i: (i, 0)),
    )
    return pl.pallas_call(
        functools.partial(_premix_kernel, use_rope=use_rope, tm=tm),
        out_shape=out_shape, grid=(n // tm,), in_specs=in_specs, out_specs=out_specs,
        compiler_params=_cparams(("arbitrary",)), name="premix",
    )(*args)


def _attn_kernel(*refs, n_chunks, has_ctx, tq):
    if has_ctx:
        q_ref, k_ref, vt_ref, kc_ref, vtc_ref, o_ref = refs
    else:
        q_ref, k_ref, vt_ref, o_ref = refs
    tk = vt_ref.shape[-1]

    def step(hh, kc, vtc, m, acc):
        s = _nt_dot(kc, q_ref[:, hh * HEAD_PAD:(hh + 1) * HEAD_PAD])
        m_new = jnp.maximum(m, jnp.max(s, axis=0, keepdims=True))
        p = jnp.exp2(s - m_new).astype(BF16)
        alpha = jnp.exp2(m - m_new)
        return m_new, acc * alpha + jnp.dot(vtc, p, preferred_element_type=F32)

    def body(i, carry):
        out = []
        for hh in range(2):
            m, acc = carry[hh]
            start = pl.multiple_of(i * tk, tk)
            kc = k_ref[pl.ds(start, tk), hh * HEAD_PAD:(hh + 1) * HEAD_PAD]
            vtc = vt_ref[0, i, hh * V_ROWS:(hh + 1) * V_ROWS, :]
            out.append(step(hh, kc, vtc, m, acc))
        return tuple(out)

    init = tuple((jnp.full((1, tq), NEG_BIG, F32), jnp.zeros((V_ROWS, tq), F32)) for _ in range(2))
    carry = lax.fori_loop(0, n_chunks, body, init)
    outs = []
    for hh in range(2):
        m, acc = carry[hh]
        if has_ctx:
            m, acc = step(hh, kc_ref[:, hh * HEAD_PAD:(hh + 1) * HEAD_PAD],
                          vtc_ref[0, 0, hh * V_ROWS:(hh + 1) * V_ROWS, :], m, acc)
        outs.append(acc[:VDIM] / acc[VDIM:VDIM + 1])
    o_ref[...] = jnp.concatenate(outs, axis=0).T.astype(BF16)


def _attention(q, k, vt, ctx_kv, seq, tq):
    n = q.shape[0]
    batch = n // seq
    n_chunks, tk = vt.shape[1], vt.shape[3]
    q_tiles = seq // tq
    has_ctx = ctx_kv is not None
    in_specs = [
        pl.BlockSpec((tq, 2 * HEAD_PAD), lambda b, j, i: (b * q_tiles + i, j)),
        pl.BlockSpec((seq, 2 * HEAD_PAD), lambda b, j, i: (b, j)),
        pl.BlockSpec((1, n_chunks, 2 * V_ROWS, tk), lambda b, j, i: (b, 0, j, 0)),
    ]
    args = [q, k, vt]
    if has_ctx:
        kc, vtc = ctx_kv
        lc = vtc.shape[3]
        in_specs += [pl.BlockSpec((lc, 2 * HEAD_PAD), lambda b, j, i: (b, j)),
                     pl.BlockSpec((1, 1, 2 * V_ROWS, lc), lambda b, j, i: (b, 0, j, 0))]
        args += [kc, vtc]
    return pl.pallas_call(
        functools.partial(_attn_kernel, n_chunks=n_chunks, has_ctx=has_ctx, tq=tq),
        out_shape=jax.ShapeDtypeStruct((n, HEADS * VDIM), BF16),
        grid=(batch, HEADS // 2, q_tiles),
        in_specs=in_specs,
        out_specs=pl.BlockSpec((tq, 2 * VDIM), lambda b, j, i: (b * q_tiles + i, j)),
        compiler_params=_cparams(("arbitrary", "arbitrary", "arbitrary")),
        name="attn",
    )(*args)


def _hy_filter_kernel(z_ref, w1_ref, b1_ref, w2_ref, b2_ref, w3_ref, b3_ref, fr_ref, dec_ref, k_ref, ss_ref):
    z = z_ref[...]
    fr = fr_ref[...]
    hdn = jnp.sin(fr * (jnp.dot(z, w1_ref[...], preferred_element_type=F32, precision=HIGHEST) + b1_ref[...]))
    hdn = jnp.sin(fr * (jnp.dot(hdn, w2_ref[...], preferred_element_type=F32, precision=HIGHEST) + b2_ref[...]))
    k = jnp.dot(hdn, w3_ref[...], preferred_element_type=F32, precision=HIGHEST) + b3_ref[...]
    k = k * jnp.exp(-z[:, 0:1] * jnp.abs(dec_ref[...]))
    k_ref[...] = k
    ss_ref[0] = jnp.sum(k * k, axis=0, keepdims=True)


def _hy_filters(z, lw, tr):
    seq, emb = z.shape
    nblk = seq // tr
    hid = lw['hy_w2'].shape[0]
    const = lambda i: (0, 0)
    return pl.pallas_call(
        _hy_filter_kernel,
        out_shape=(jax.ShapeDtypeStruct((seq, 2 * HY_W), F32), jax.ShapeDtypeStruct((nblk, 1, 2 * HY_W), F32)),
        grid=(nblk,),
        in_specs=[pl.BlockSpec((tr, emb), lambda i: (i, 0)),
                  pl.BlockSpec((emb, hid), const), pl.BlockSpec((1, hid), const),
                  pl.BlockSpec((hid, hid), const), pl.BlockSpec((1, hid), const),
                  pl.BlockSpec((hid, 2 * HY_W), const), pl.BlockSpec((1, 2 * HY_W), const),
                  pl.BlockSpec((1, hid), const), pl.BlockSpec((1, 2 * HY_W), const)],
        out_specs=(pl.BlockSpec((tr, 2 * HY_W), lambda i: (i, 0)),
                   pl.BlockSpec((1, 1, 2 * HY_W), lambda i: (i, 0, 0))),
        compiler_params=_cparams(("arbitrary",)), name="hy_filter",
    )(z, lw['hy_w1'], lw['hy_b1'], lw['hy_w2'], lw['hy_b2'], lw['hy_w3'], lw['hy_b3'], lw['hy_freq'],
      lw['hy_decay'])


def _hy_conv3_kernel(p_ref, prev_ref, next_ref, w_ref, b_ref, x0_ref, u_ref, *, tiles_per_seq, tr):
    i = pl.program_id(0)
    p = p_ref[...].astype(F32)
    first = (i % tiles_per_seq) == 0
    last = (i % tiles_per_seq) == tiles_per_seq - 1
    prev_row = jnp.where(first, 0.0, prev_ref[...].astype(F32)[15:16, :])
    next_row = jnp.where(last, 0.0, next_ref[...].astype(F32)[0:1, :])
    row = lax.broadcasted_iota(jnp.int32, p.shape, 0)
    p_prev = jnp.where(row == 0, prev_row, pltpu.roll(p, 1, 0))
    p_next = jnp.where(row == tr - 1, next_row, pltpu.roll(p, tr - 1, 0))
    z = b_ref[...] + p_prev * w_ref[0:1, :] + p * w_ref[1:2, :] + p_next * w_ref[2:3, :]
    x0_ref[...] = z[:, :HY_W].astype(BF16)
    u_ref[...] = (z[:, 2 * HY_W:] * z[:, HY_W:2 * HY_W]).astype(BF16)


def _hy_conv3(phy, conv_w, conv_b, seq, tr):
    n = phy.shape[0]
    tiles_per_seq = seq // tr
    hb = tr // 16
    nhb = n // 16
    return pl.pallas_call(
        functools.partial(_hy_conv3_kernel, tiles_per_seq=tiles_per_seq, tr=tr),
        out_shape=(jax.ShapeDtypeStruct((n, HY_W), BF16), jax.ShapeDtypeStruct((n, HY_W), BF16)),
        grid=(n // tr,),
        in_specs=[pl.BlockSpec((tr, 3 * HY_W), lambda i: (i, 0)),
                  pl.BlockSpec((16, 3 * HY_W), lambda i: (jnp.maximum(i * hb - 1, 0), 0)),
                  pl.BlockSpec((16, 3 * HY_W), lambda i: (jnp.minimum((i + 1) * hb, nhb - 1), 0)),
                  pl.BlockSpec((3, 3 * HY_W), lambda i: (0, 0)),
                  pl.BlockSpec((1, 3 * HY_W), lambda i: (0, 0))],
        out_specs=(pl.BlockSpec((tr, HY_W), lambda i: (i, 0)), pl.BlockSpec((tr, HY_W), lambda i: (i, 0))),
        compiler_params=_cparams(("arbitrary",)), name="hy_conv3",
    )(phy, phy, phy, conv_w, conv_b)


def _dft_outer_kernel(wr_ref, wi_ref, u_ref, ar_ref, ai_ref):
    u = u_ref[0]
    ar_ref[0] = jnp.dot(wr_ref[...], u, preferred_element_type=F32).astype(BF16)
    ai_ref[0] = jnp.dot(wi_ref[...], u, preferred_element_type=F32).astype(BF16)


def _dft_outer(wr, wi, u, tc):
    nb, kk, cols = u.shape
    n1 = wr.shape[0]
    return pl.pallas_call(
        _dft_outer_kernel,
        out_shape=(jax.ShapeDtypeStruct((nb, n1, cols), BF16),) * 2,
        grid=(nb, cols // tc),
        in_specs=[pl.BlockSpec((n1, kk), lambda b, j: (0, 0)), pl.BlockSpec((n1, kk), lambda b, j: (0, 0)),
                  pl.BlockSpec((1, kk, tc), lambda b, j: (b, 0, j))],
        out_specs=(pl.BlockSpec((1, n1, tc), lambda b, j: (b, 0, j)),) * 2,
        compiler_params=_cparams(("arbitrary", "arbitrary")), name="dft_outer",
    )(wr, wi, u)


def _cdot(mr, mi, xr, xi):
    rr = jnp.dot(mr, xr, preferred_element_type=F32) - jnp.dot(mi, xi, preferred_element_type=F32)
    ri = jnp.dot(mr, xi, preferred_element_type=F32) + jnp.dot(mi, xr, preferred_element_type=F32)
    return rr, ri


def _dft_inner_fwd_kernel(mr_ref, mi_ref, ar_ref, ai_ref, fr_ref, fi_ref):
    fr, fi = _cdot(mr_ref[0], mi_ref[0], ar_ref[0, 0], ai_ref[0, 0])
    fr_ref[0] = fr.astype(BF16)
    fi_ref[0] = fi.astype(BF16)


def _dft_inner_fwd(mr, mi, ar, ai):
    n1, n2, _ = mr.shape
    ch = ar.shape[-1]
    mspec = pl.BlockSpec((1, n2, n2), lambda k: (k, 0, 0))
    aspec = pl.BlockSpec((1, 1, n2, ch), lambda k: (0, k, 0, 0))
    ospec = pl.BlockSpec((1, n2, ch), lambda k: (k, 0, 0))
    return pl.pallas_call(
        _dft_inner_fwd_kernel,
        out_shape=(jax.ShapeDtypeStruct((n1, n2, ch), BF16),) * 2,
        grid=(n1,), in_specs=[mspec, mspec, aspec, aspec], out_specs=(ospec, ospec),
        compiler_params=_cparams(("arbitrary",)), name="dft_inner_fwd",
    )(mr, mi, ar, ai)


def _dft_inner_conv_kernel(mr_ref, mi_ref, tr_ref, ti_ref, ar_ref, ai_ref, fr_ref, fi_ref, br_ref, bi_ref, *, nb):
    fr = fr_ref[0].astype(F32)
    fi = fi_ref[0].astype(F32)
    for b in range(nb):
        ur, ui = _cdot(mr_ref[0], mi_ref[0], ar_ref[b, 0], ai_ref[b, 0])
        yr = (ur * fr - ui * fi).astype(BF16)
        yi = (ur * fi + ui * fr).astype(BF16)
        br, bi = _cdot(tr_ref[0], ti_ref[0], yr, yi)
        br_ref[b, 0] = br.astype(BF16)
        bi_ref[b, 0] = bi.astype(BF16)


def _dft_inner_conv(mr, mi, tr, ti, ar, ai, fr, fi):
    n1, n2, _ = mr.shape
    nb, _, _, ch = ar.shape
    mspec = pl.BlockSpec((1, n2, n2), lambda k: (k, 0, 0))
    aspec = pl.BlockSpec((nb, 1, n2, ch), lambda k: (0, k, 0, 0))
    fspec = pl.BlockSpec((1, n2, ch), lambda k: (k, 0, 0))
    return pl.pallas_call(
        functools.partial(_dft_inner_conv_kernel, nb=nb),
        out_shape=(jax.ShapeDtypeStruct((nb, n1, n2, ch), BF16),) * 2,
        grid=(n1,), in_specs=[mspec, mspec, mspec, mspec, aspec, aspec, fspec, fspec],
        out_specs=(aspec, aspec),
        compiler_params=_cparams(("arbitrary",)), name="dft_inner_conv",
    )(mr, mi, tr, ti, ar, ai, fr, fi)


def _filter_scale(ss_ref):
    ss = jnp.sum(ss_ref[...], axis=0)
    return lax.rsqrt(ss[:, :HY_W] + ss[:, HY_W:] + EPS)


def _hy_final_kernel(cr_ref, ci_ref, br_ref, bi_ref, x0_ref, u_ref, ss_ref, bias_ref, o_ref, *, reps):
    y = (jnp.dot(cr_ref[...], br_ref[0], preferred_element_type=F32)
         + jnp.dot(ci_ref[...], bi_ref[0], preferred_element_type=F32))
    scale = jnp.tile(_filter_scale(ss_ref), (1, reps))
    bias = jnp.tile(bias_ref[...], (1, reps))
    u = u_ref[0].astype(F32)
    o_ref[0] = (x0_ref[0].astype(F32) * (y * scale + u * bias)).astype(BF16)


def _hy_final(cr, ci, br, bi, x0, u, ss, bias, tc):
    nb, n1, cols = br.shape
    n1h = cr.shape[0]
    nblk = ss.shape[0]
    cspec = pl.BlockSpec((n1h, n1), lambda b, j: (0, 0))
    bspec = pl.BlockSpec((1, n1, tc), lambda b, j: (b, 0, j))
    xspec = pl.BlockSpec((1, n1h, tc), lambda b, j: (b, 0, j))
    return pl.pallas_call(
        functools.partial(_hy_final_kernel, reps=tc // HY_W),
        out_shape=jax.ShapeDtypeStruct((nb, n1h, cols), BF16),
        grid=(nb, cols // tc),
        in_specs=[cspec, cspec, bspec, bspec, xspec, xspec,
                  pl.BlockSpec((nblk, 1, 2 * HY_W), lambda b, j: (0, 0, 0)),
                  pl.BlockSpec((1, HY_W), lambda b, j: (0, 0))],
        out_specs=xspec,
        compiler_params=_cparams(("arbitrary", "arbitrary")), name="hy_final",
    )(cr, ci, br, bi, x0, u, ss, bias)


def _hy_direct_kernel(kk_ref, x0_ref, u_ref, ss_ref, bias_ref, o_ref, *, seq):
    u = u_ref[0].astype(F32)
    y = jnp.zeros((seq, HY_W), F32)
    for j in range(seq):
        y = y + kk_ref[seq - j:2 * seq - j, :] * u[j:j + 1, :]
    o_ref[0] = (x0_ref[0].astype(F32) * (y * _filter_scale(ss_ref) + u * bias_ref[...])).astype(BF16)


def _hy_direct(kk, x0, u, ss, bias):
    nb, seq, _ = u.shape
    nblk = ss.shape[0]
    xspec = pl.BlockSpec((1, seq, HY_W), lambda b: (b, 0, 0))
    return pl.pallas_call(
        functools.partial(_hy_direct_kernel, seq=seq),
        out_shape=jax.ShapeDtypeStruct((nb, seq, HY_W), BF16),
        grid=(nb,),
        in_specs=[pl.BlockSpec((2 * seq, HY_W), lambda b: (0, 0)), xspec, xspec,
                  pl.BlockSpec((nblk, 1, 2 * HY_W), lambda b: (0, 0, 0)),
                  pl.BlockSpec((1, HY_W), lambda b: (0, 0))],
        out_specs=xspec,
        compiler_params=_cparams(("arbitrary",)), name="hy_direct",
    )(kk, x0, u, ss, bias)


def _hy_embedding(seq):
    t = jnp.arange(seq, dtype=F32)
    t_unit = t / max(seq - 1, 1)
    bands = jnp.linspace(1e-4, HY_BANDS - 1, HY_BANDS, dtype=F32)
    ang = (2 * jnp.pi / seq) * t[:, None] * bands[None, :]
    return jnp.concatenate([t_unit[:, None], jnp.cos(ang), -jnp.sin(ang)], axis=-1)


def _dft_tables(seq):
    n = 2 * seq
    n2 = DFT_N2
    n1 = n // n2
    a = jnp.arange(n1, dtype=jnp.int32)
    th1 = (2 * jnp.pi / n1) * ((a[:, None] * a[None, :]) % n1).astype(F32)
    w1r, w1i = jnp.cos(th1), -jnp.sin(th1)
    k1 = jnp.arange(n1, dtype=jnp.int32)[:, None, None]
    k2 = jnp.arange(n2, dtype=jnp.int32)[None, :, None]
    t2 = jnp.arange(n2, dtype=jnp.int32)[None, None, :]
    th = (2 * jnp.pi / n) * ((t2 * (k1 + n1 * k2)) % n).astype(F32)
    mr, mi = jnp.cos(th), -jnp.sin(th)
    return dict(
        w1r=w1r.astype(BF16), w1i=w1i.astype(BF16),
        mr=mr.astype(BF16), mi=mi.astype(BF16),
        tr=jnp.swapaxes(mr, 1, 2).astype(BF16), ti=jnp.swapaxes(-mi, 1, 2).astype(BF16),
        cr=(w1r[:n1 // 2] / n).astype(BF16), ci=(w1i[:n1 // 2] / n).astype(BF16))


def _hyena_long(phy, lw, filt, seq, tabs):
    n = phy.shape[0]
    nb = n // seq
    n2 = DFT_N2
    n1 = 2 * seq // n2
    kf, ss = filt
    x0, u = _hy_conv3(phy, lw['hy_conv_w'], lw['hy_conv_b'], seq, min(seq, 1024))
    f = jnp.concatenate([kf[:, :HY_W], jnp.zeros((1, HY_W), F32), kf[:0:-1, HY_W:]], axis=0).astype(BF16)
    tc = min(n2 * HY_W, 8192)
    far, fai = _dft_outer(tabs['w1r'], tabs['w1i'], f.reshape(1, n1, n2 * HY_W), tc)
    fr, fi = _dft_inner_fwd(tabs['mr'], tabs['mi'], far.reshape(1, n1, n2, HY_W), fai.reshape(1, n1, n2, HY_W))
    ar, ai = _dft_outer(tabs['w1r'][:, :n1 // 2], tabs['w1i'][:, :n1 // 2], u.reshape(nb, n1 // 2, n2 * HY_W), tc)
    br, bi = _dft_inner_conv(tabs['mr'], tabs['mi'], tabs['tr'], tabs['ti'],
                             ar.reshape(nb, n1, n2, HY_W), ai.reshape(nb, n1, n2, HY_W), fr, fi)
    out = _hy_final(tabs['cr'], tabs['ci'], br.reshape(nb, n1, n2 * HY_W), bi.reshape(nb, n1, n2 * HY_W),
                    x0.reshape(nb, n1 // 2, n2 * HY_W), u.reshape(nb, n1 // 2, n2 * HY_W), ss, lw['hy_bias'], tc)
    return out.reshape(n, HY_W)


def _hyena_short(phy, lw, filt, seq):
    n = phy.shape[0]
    nb = n // seq
    kf, ss = filt
    x0, u = _hy_conv3(phy, lw['hy_conv_w'], lw['hy_conv_b'], seq, seq)
    kk = jnp.concatenate([jnp.zeros((1, HY_W), F32), kf[:0:-1, HY_W:], kf[:, :HY_W]], axis=0)
    out = _hy_direct(kk, x0.reshape(nb, seq, HY_W), u.reshape(nb, seq, HY_W), ss, lw['hy_bias'])
    return out.reshape(n, HY_W)


def _pair_top2_sum(a, b, c, d):
    return jnp.maximum(jnp.maximum(jnp.maximum(a + b, a + c), jnp.maximum(a + d, b + c)),
                       jnp.maximum(b + d, c + d))


def _route(logits_t, rb):
    aff = jax.nn.sigmoid(logits_t)
    sel = aff + rb
    rows = [sel[e:e + 1, :] for e in range(N_EXPERTS)]
    affr = [aff[e:e + 1, :] for e in range(N_EXPERTS)]
    best, bidx = None, None
    for g in range(N_GROUPS):
        gs = _pair_top2_sum(*rows[g * EXP_PER_GROUP:(g + 1) * EXP_PER_GROUP])
        if g == 0:
            best, bidx = gs, jnp.zeros(gs.shape, jnp.int32)
        else:
            upd = gs > best
            bidx = jnp.where(upd, g, bidx)
            best = jnp.where(upd, gs, best)
    vals = [jnp.where(bidx == e // EXP_PER_GROUP, rows[e], -jnp.inf) for e in range(N_EXPERTS)]
    m1, i1 = vals[0], jnp.zeros(best.shape, jnp.int32)
    for e in range(1, N_EXPERTS):
        upd = vals[e] > m1
        i1 = jnp.where(upd, e, i1)
        m1 = jnp.where(upd, vals[e], m1)
    m2, i2 = jnp.full(best.shape, -jnp.inf, F32), jnp.zeros(best.shape, jnp.int32)
    for e in range(N_EXPERTS):
        cand = jnp.where(i1 == e, -jnp.inf, vals[e])
        upd = cand > m2
        i2 = jnp.where(upd, e, i2)
        m2 = jnp.where(upd, cand, m2)
    a1 = sum(jnp.where(i1 == e, affr[e], 0.0) for e in range(N_EXPERTS))
    a2 = sum(jnp.where(i2 == e, affr[e], 0.0) for e in range(N_EXPERTS))
    inv = 1.0 / (a1 + a2)
    return jnp.concatenate(
        [jnp.where(i1 == e, a1 * inv, 0.0) + jnp.where(i2 == e, a2 * inv, 0.0) for e in range(N_EXPERTS)], axis=0)


def _merge_kernel(a_ref, b_ref, m_ref, gt_ref, x_ref, gate_ref, shift_ref, scale_ref, g2_ref,
                  wpa_ref, wpb_ref, wpc_ref, wout_ref, rwt_ref, rb_ref, xo_ref, h2_ref, comb_ref):
    y = gt_ref[:, 0:D_MODEL].astype(F32) * jnp.dot(a_ref[...], wpa_ref[...], preferred_element_type=F32)
    y = y + gt_ref[:, D_MODEL:2 * D_MODEL].astype(F32) * jnp.dot(b_ref[...], wpb_ref[...],
                                                                 preferred_element_type=F32)
    y = y + gt_ref[:, 2 * D_MODEL:].astype(F32) * jnp.dot(m_ref[...], wpc_ref[...], preferred_element_type=F32)
    y2 = jnp.dot(y.astype(BF16), wout_ref[...], preferred_element_type=F32)
    xn = x_ref[...] + gate_ref[0] * y2
    xo_ref[...] = xn
    h2 = _rms(xn) * g2_ref[...]
    h2 = h2 * (1.0 + scale_ref[0]) + shift_ref[0]
    h2_ref[...] = h2.astype(BF16)
    logits_t = lax.dot_general(rwt_ref[...], h2, (((1,), (1,)), ((), ())), preferred_element_type=F32,
                               precision=HIGHEST)
    comb_ref[...] = _route(logits_t, rb_ref[...])


def _merge(a, b, m, gt, x, gate, shift, scale, lw, rw_t, rb, seq, tm):
    n = x.shape[0]
    tiles_per_seq = seq // tm
    const = lambda i: (0, 0)
    row = lambda w: pl.BlockSpec((tm, w), lambda i: (i, 0))
    vec = pl.BlockSpec((1, 1, D_MODEL), lambda i: (i // tiles_per_seq, 0, 0))
    return pl.pallas_call(
        _merge_kernel,
        out_shape=(jax.ShapeDtypeStruct((n, D_MODEL), F32), jax.ShapeDtypeStruct((n, D_MODEL), BF16),
                   jax.ShapeDtypeStruct((N_EXPERTS, n), F32)),
        grid=(n // tm,),
        in_specs=[row(HEADS * VDIM), row(HY_W), row(GM_W), row(3 * D_MODEL), row(D_MODEL), vec, vec, vec,
                  pl.BlockSpec((1, D_MODEL), const),
                  pl.BlockSpec((HEADS * VDIM, D_MODEL), const), pl.BlockSpec((HY_W, D_MODEL), const),
                  pl.BlockSpec((GM_W, D_MODEL), const), pl.BlockSpec((D_MODEL, D_MODEL), const),
                  pl.BlockSpec((N_EXPERTS, D_MODEL), const), pl.BlockSpec((N_EXPERTS, 1), const)],
        out_specs=(row(D_MODEL), row(D_MODEL), pl.BlockSpec((N_EXPERTS, tm), lambda i: (0, i))),
        compiler_params=_cparams(("arbitrary",)), name="merge",
    )(a, b, m, gt, x, gate, shift, scale, lw['g2'], lw['w_pa'], lw['w_pb'], lw['w_pc'], lw['w_out'], rw_t, rb)


def _moe_kernel(h_ref, comb_ref, x_ref, gate_ref, wgu_ref, wd_ref, o_ref, acc_ref):
    e = pl.program_id(1)

    @pl.when(e == 0)
    def _():
        acc_ref[...] = jnp.zeros_like(acc_ref)

    gu = jnp.dot(h_ref[...], wgu_ref[0], preferred_element_type=F32)
    g = gu[:, :D_EXPERT]
    hid = (g * jax.nn.sigmoid(g) * gu[:, D_EXPERT:]).astype(BF16)
    y = jnp.dot(hid, wd_ref[0], preferred_element_type=F32)
    lane = lax.broadcasted_iota(jnp.int32, comb_ref.shape, 1)
    col = jnp.sum(jnp.where(lane == e, comb_ref[...], 0.0), axis=-1, keepdims=True)
    acc_ref[...] += col * y

    @pl.when(e == N_EXPERTS - 1)
    def _():
        o_ref[...] = x_ref[...] + gate_ref[0] * acc_ref[...]


def _moe(h2, comb, x, gate, lw, seq, tm):
    n = x.shape[0]
    tiles_per_seq = seq // tm
    return pl.pallas_call(
        _moe_kernel,
        out_shape=jax.ShapeDtypeStruct((n, D_MODEL), F32),
        grid=(n // tm, N_EXPERTS),
        in_specs=[pl.BlockSpec((tm, D_MODEL), lambda i, e: (i, 0)),
                  pl.BlockSpec((tm, N_EXPERTS), lambda i, e: (i, 0)),
                  pl.BlockSpec((tm, D_MODEL), lambda i, e: (i, 0)),
                  pl.BlockSpec((1, 1, D_MODEL), lambda i, e: (i // tiles_per_seq, 0, 0)),
                  pl.BlockSpec((1, D_MODEL, 2 * D_EXPERT), lambda i, e: (e, 0, 0)),
                  pl.BlockSpec((1, D_EXPERT, D_MODEL), lambda i, e: (e, 0, 0))],
        out_specs=pl.BlockSpec((tm, D_MODEL), lambda i, e: (i, 0)),
        scratch_shapes=[pltpu.VMEM((tm, D_MODEL), F32)],
        compiler_params=_cparams(("arbitrary", "arbitrary")), name="moe",
    )(h2, comb, x, gate, lw['w_gu'], lw['w_d'])


def _head_perm():
    rope_idx = list(range(NOPE, QK, 2)) + list(range(NOPE + 1, QK, 2))
    return jnp.array(list(range(NOPE)) + rope_idx, dtype=jnp.int32)


def _layer_weights(p, l):
    perm = _head_perm()
    row = lambda v: v.reshape(1, -1)
    w_in = p['w_in'][l]
    kr_perm = jnp.array(list(range(0, ROPE, 2)) + list(range(1, ROPE, 2)), dtype=jnp.int32)
    w_kr = jnp.pad(w_in[:, OFF_KR:OFF_HY][:, kr_perm], ((0, 0), (0, HEAD_PAD - ROPE)))
    w_in2 = jnp.concatenate([w_in[:, OFF_Q:OFF_KR], w_kr, w_in[:, OFF_HY:]], axis=1).astype(BF16)
    w_uq = p['w_uq'][l].reshape(Q_RANK, HEADS, QK)[:, :, perm]
    w_uq = jnp.pad(w_uq, ((0, 0), (0, 0), (0, HEAD_PAD - QK))).reshape(Q_RANK, HEADS * HEAD_PAD).astype(BF16)
    w_ukv = p['w_ukv'][l].reshape(KV_RANK, HEADS, NOPE + VDIM)
    w_k = jnp.pad(w_ukv[:, :, :NOPE], ((0, 0), (0, 0), (0, HEAD_PAD - NOPE)))
    w_k = w_k.reshape(KV_RANK, HEADS * HEAD_PAD).astype(BF16)
    w_vt = w_ukv[:, :, NOPE:].reshape(KV_RANK, HEADS * VDIM).T.astype(BF16)
    pad_gain = lambda g: jnp.pad(g[perm], (0, HEAD_PAD - QK)).reshape(1, HEAD_PAD)
    gm_b = jnp.repeat(p['gm_bs'][l].T, GM_W // GM_GROUPS, axis=1)
    return dict(
        g1=row(p['norm1_g'][l]), g2=row(p['norm2_g'][l]), w_in=w_in2,
        g_qa=row(p['g_qa'][l]), w_uq=w_uq, g_qn=pad_gain(p['g_qn'][l]),
        g_kva=row(p['g_kva'][l]), w_k=w_k, w_vt=w_vt, g_kn=pad_gain(p['g_kn'][l]),
        gm_g=row(p['gm_norm_g'][l]), gm_w=p['gm_ws'][l].astype(BF16), gm_b=gm_b,
        hy_conv_w=p['hy_conv_w'][l], hy_conv_b=row(p['hy_conv_b'][l]),
        hy_w1=p['hy_w1'][l], hy_b1=row(p['hy_b1'][l]), hy_w2=p['hy_w2'][l], hy_b2=row(p['hy_b2'][l]),
        hy_w3=p['hy_w3'][l], hy_b3=row(p['hy_b3'][l]), hy_freq=row(p['hy_freq'][l]),
        hy_decay=p['hy_decay'][l].reshape(1, 2 * HY_W), hy_bias=row(p['hy_bias'][l]),
        w_pa=p['w_pa'][l].astype(BF16), w_pb=p['w_pb'][l].astype(BF16), w_pc=p['w_pc'][l].astype(BF16),
        w_out=p['w_out'][l].astype(BF16),
        w_gu=jnp.concatenate([p['moe_w_gate'][l], p['moe_w_up'][l]], axis=-1).astype(BF16),
        w_d=p['moe_w_down'][l].astype(BF16))


def _rope_tables(seq):
    rows = seq // GRID_W
    row = jnp.repeat(jnp.arange(rows, dtype=F32), GRID_W)
    col = jnp.tile(jnp.arange(GRID_W, dtype=F32), rows)
    n_freq = ROPE // 4
    inv = ROPE_THETA ** (-jnp.arange(n_freq, dtype=F32) / n_freq)
    ang = jnp.concatenate([row[:, None] * inv, col[:, None] * inv], axis=-1)
    c, s = jnp.cos(ang), jnp.sin(ang)
    z = lambda w: jnp.zeros((seq, w), F32)
    rc = jnp.concatenate([jnp.ones((seq, NOPE), F32), c, c, z(HEAD_PAD - QK)], axis=1)
    rs1 = jnp.concatenate([z(NOPE), z(ROPE // 2), s, z(HEAD_PAD - QK)], axis=1)
    rs2 = jnp.concatenate([z(NOPE), -s, z(ROPE // 2), z(HEAD_PAD - QK)], axis=1)
    return rc, rs1, rs2


def _mixer_and_ffn(x, mods, lw, rw_t, rb, seq, tm, rope_tabs, ctx_kv, filt, dft_tabs, moe_tm):
    shift1, scale1, gate1, shift2, scale2, gate2 = mods
    q, k, vt, m, phy, gt = _premix(x, shift1, scale1, lw, rope_tabs, seq, tm)
    a = _attention(q, k, vt, ctx_kv, seq, min(seq, 512))
    if dft_tabs is not None:
        b = _hyena_long(phy, lw, filt, seq, dft_tabs)
    else:
        b = _hyena_short(phy, lw, filt, seq)
    xn, h2, comb_t = _merge(a, b, m, gt, x, gate1, shift2, scale2, lw, rw_t, rb, seq, tm)
    return _moe(h2, comb_t.T, xn, gate2, lw, seq, moe_tm)


def _forward(p):
    x, ctx = p['x'], p['ctx']
    batch, seq, _ = x.shape
    lc = ctx.shape[1]
    depth = p['w_mod'].shape[0]

    cvecs = jnp.concatenate([p['c'], p['c_ctx'][None], jnp.zeros((8 - batch - 1, D_MODEL), F32)], axis=0)
    mod_all = _modvec(cvecs, p['w_mod'], p['b_mod'])
    rw_t = p['router_w'].T
    rb = p['router_b'].reshape(N_EXPERTS, 1)
    rope_tabs = _rope_tables(seq)
    dft_tabs = _dft_tables(seq)
    z_lat, z_ctx = _hy_embedding(seq), _hy_embedding(lc)

    xl = x.reshape(batch * seq, D_MODEL)
    xc = ctx.reshape(batch * lc, D_MODEL)
    tm_lat = min(seq, 512)
    for l in range(depth):
        lw = _layer_weights(p, l)
        mod = mod_all[l].reshape(8, N_MOD, D_MODEL)
        mods_lat = [mod[:batch, j].reshape(batch, 1, D_MODEL) for j in range(N_MOD)]
        mods_ctx = [jnp.broadcast_to(mod[batch, j].reshape(1, 1, D_MODEL), (batch, 1, D_MODEL))
                    for j in range(N_MOD)]
        if l == depth - 1:
            _, k_c, vt_c, _, _, _ = _premix(xc, mods_ctx[0], mods_ctx[1], lw, None, lc, lc)
        else:
            q_c, k_c, vt_c, m_c, phy_c, gt_c = _premix(xc, mods_ctx[0], mods_ctx[1], lw, None, lc, lc)
            a_c = _attention(q_c, k_c, vt_c, None, lc, lc)
            b_c = _hyena_short(phy_c, lw, _hy_filters(z_ctx, lw, lc), lc)
            xn_c, h2_c, comb_c = _merge(a_c, b_c, m_c, gt_c, xc, mods_ctx[2], mods_ctx[3], mods_ctx[4], lw,
                                        rw_t, rb, lc, lc)
            xc = _moe(h2_c, comb_c.T, xn_c, mods_ctx[5], lw, lc, lc)
        filt = _hy_filters(z_lat, lw, min(seq, 2048))
        xl = _mixer_and_ffn(xl, mods_lat, lw, rw_t, rb, seq, tm_lat, rope_tabs, (k_c, vt_c), filt, dft_tabs,
                            min(seq, 1024))
    return xl.reshape(batch, seq, D_MODEL)


def kernel(x, c, ctx, c_ctx, w_mod, b_mod, norm1_g, norm2_g, w_in, g_qa, w_uq, g_kva, w_ukv, g_qn, g_kn,
           hy_conv_w, hy_conv_b, hy_w1, hy_b1, hy_w2, hy_b2, hy_w3, hy_b3, hy_freq, hy_decay, hy_bias,
           gm_norm_g, gm_ws, gm_bs, w_pa, w_pb, w_pc, w_out, router_w, router_b,
           moe_w_gate, moe_w_up, moe_w_down):
    return _forward(dict(
        x=x, c=c, ctx=ctx, c_ctx=c_ctx, w_mod=w_mod, b_mod=b_mod, norm1_g=norm1_g, norm2_g=norm2_g, w_in=w_in,
        g_qa=g_qa, w_uq=w_uq, g_kva=g_kva, w_ukv=w_ukv, g_qn=g_qn, g_kn=g_kn, hy_conv_w=hy_conv_w,
        hy_conv_b=hy_conv_b, hy_w1=hy_w1, hy_b1=hy_b1, hy_w2=hy_w2, hy_b2=hy_b2, hy_w3=hy_w3, hy_b3=hy_b3,
        hy_freq=hy_freq, hy_decay=hy_decay, hy_bias=hy_bias, gm_norm_g=gm_norm_g, gm_ws=gm_ws, gm_bs=gm_bs,
        w_pa=w_pa, w_pb=w_pb, w_pc=w_pc, w_out=w_out, router_w=router_w, router_b=router_b,
        moe_w_gate=moe_w_gate, moe_w_up=moe_w_up, moe_w_down=moe_w_down))
```

```python
import functools
import math

import jax
import jax.numpy as jnp
from jax import lax
from jax.experimental import pallas as pl
from jax.experimental.pallas import tpu as pltpu

F32 = jnp.float32
BF16 = jnp.bfloat16
HIGHEST = lax.Precision.HIGHEST

D_MODEL = 1024
GRID_W = 64
EPS = 1e-6
N_MOD = 6

HEADS = 8
Q_RANK = 384
KV_RANK = 256
NOPE = 64
ROPE = 32
QK = NOPE + ROPE
VDIM = 64
HEAD_PAD = 128
ROPE_THETA = 10000.0
V_ROWS = 80

HY_W = 256
HY_BANDS = 16
GM_W = 256
GM_CHUNK = 128
GM_GROUPS = 4

OFF_Q = 0
OFF_KV = OFF_Q + Q_RANK
OFF_KR = OFF_KV + KV_RANK
OFF_HY = OFF_KR + ROPE
OFF_GM = OFF_HY + 3 * HY_W
OFF_GT = OFF_GM + 2 * GM_W

P_Q = 0
P_KV = P_Q + Q_RANK
P_KR = P_KV + KV_RANK
P_HY = P_KR + HEAD_PAD
P_GM = P_HY + 3 * HY_W
P_GT = P_GM + 2 * GM_W
P_W = P_GT + 3 * D_MODEL

N_EXPERTS = 16
N_GROUPS = 4
EXP_PER_GROUP = 4
D_EXPERT = 512

DFT_N2 = 256

VMEM_LIMIT = 56 * 1024 * 1024
ATTN_TQ = 512
ATTN_TK = 2048
ATTN_BOUND_MAX = 50.0
NEG_BIG = -1e30
LOG2E = 1.4426950408889634
QSCALE = QK ** -0.5 * LOG2E


def _cparams(sem):
    return pltpu.CompilerParams(dimension_semantics=sem, vmem_limit_bytes=VMEM_LIMIT)


def _rms(x):
    return x * lax.rsqrt(jnp.mean(x * x, axis=-1, keepdims=True) + EPS)


def _nt_dot(a, b):
    return lax.dot_general(a, b, (((1,), (1,)), ((), ())), preferred_element_type=F32)


def _modvec_kernel(c_ref, w_ref, b_ref, o_ref):
    cv = c_ref[...]
    s = cv * jax.nn.sigmoid(cv)
    o_ref[0] = jnp.dot(s, w_ref[0], preferred_element_type=F32, precision=HIGHEST) + b_ref[0]


def _modvec(cvecs, w_mod, b_mod):
    depth = w_mod.shape[0]
    tn = 1536
    return pl.pallas_call(
        _modvec_kernel,
        out_shape=jax.ShapeDtypeStruct((depth, 8, N_MOD * D_MODEL), F32),
        grid=(depth, N_MOD * D_MODEL // tn),
        in_specs=[pl.BlockSpec((8, D_MODEL), lambda l, j: (0, 0)),
                  pl.BlockSpec((1, D_MODEL, tn), lambda l, j: (l, 0, j)),
                  pl.BlockSpec((1, 1, tn), lambda l, j: (l, 0, j))],
        out_specs=pl.BlockSpec((1, 8, tn), lambda l, j: (l, 0, j)),
        compiler_params=_cparams(("arbitrary", "arbitrary")),
        name="modvec",
    )(cvecs, w_mod, b_mod.reshape(depth, 1, N_MOD * D_MODEL))


def _head_norm_rope(xh, gain, rope):
    ms = jnp.sum(xh * xh, axis=-1, keepdims=True) * (1.0 / QK)
    xh = xh * lax.rsqrt(ms + EPS) * gain
    if rope is not None:
        rc, rs1, rs2 = rope
        xh = xh * rc + pltpu.roll(xh, ROPE // 2, 1) * rs1 + pltpu.roll(xh, HEAD_PAD - ROPE // 2, 1) * rs2
    return xh


def _premix_kernel(*refs, use_rope, tm):
    if use_rope:
        (x_ref, shift_ref, scale_ref, g1_ref, win_ref, gqa_ref, wuq_ref, gqn_ref, gkva_ref, wk_ref, wvt_ref,
         gkn_ref, gmg_ref, gmw_ref, gmb_ref, qpad_ref, kpad_ref, rc_ref, rs1_ref, rs2_ref,
         q_ref, k_ref, vt_ref, m_ref, hy_ref, gt_ref) = refs
        rope = (rc_ref[...], rs1_ref[...], rs2_ref[...])
    else:
        (x_ref, shift_ref, scale_ref, g1_ref, win_ref, gqa_ref, wuq_ref, gqn_ref, gkva_ref, wk_ref, wvt_ref,
         gkn_ref, gmg_ref, gmw_ref, gmb_ref, qpad_ref, kpad_ref,
         q_ref, k_ref, vt_ref, m_ref, hy_ref, gt_ref) = refs
        rope = None

    x = x_ref[...]
    h = _rms(x) * g1_ref[...]
    h = h * (1.0 + scale_ref[0]) + shift_ref[0]
    hb = h.astype(BF16)

    def proj(lo, width):
        return jnp.dot(hb, win_ref[:, lo:lo + width], preferred_element_type=F32)

    qa = (_rms(proj(P_Q, Q_RANK)) * gqa_ref[...]).astype(BF16)
    q = jnp.dot(qa, wuq_ref[...], preferred_element_type=F32)
    for hh in range(HEADS):
        qh = _head_norm_rope(q[:, hh * HEAD_PAD:(hh + 1) * HEAD_PAD], gqn_ref[...], rope)
        q_ref[:, hh * HEAD_PAD:(hh + 1) * HEAD_PAD] = (qh * QSCALE + qpad_ref[...]).astype(BF16)

    kva = (_rms(proj(P_KV, KV_RANK)) * gkva_ref[...]).astype(BF16)
    kr = pltpu.roll(proj(P_KR, HEAD_PAD), NOPE, 1)
    kn = jnp.dot(kva, wk_ref[...], preferred_element_type=F32)
    for hh in range(HEADS):
        kh = _head_norm_rope(kn[:, hh * HEAD_PAD:(hh + 1) * HEAD_PAD] + kr, gkn_ref[...], rope)
        k_ref[:, hh * HEAD_PAD:(hh + 1) * HEAD_PAD] = (kh + kpad_ref[...]).astype(BF16)
    vt = _nt_dot(wvt_ref[...], kva)
    row = lax.broadcasted_iota(jnp.int32, (V_ROWS - VDIM, tm), 0)
    ones_rows = jnp.where(row == 0, 1.0, 0.0).astype(BF16)
    for hh in range(HEADS):
        vt_ref[0, 0, hh * V_ROWS:hh * V_ROWS + VDIM, :] = vt[hh * VDIM:(hh + 1) * VDIM].astype(BF16)
        vt_ref[0, 0, hh * V_ROWS + VDIM:(hh + 1) * V_ROWS, :] = ones_rows

    gg = jax.nn.gelu(proj(P_GM, 2 * GM_W), approximate=True)
    gu = gg[:, :GM_W]
    gv = (_rms(gg[:, GM_W:]) * gmg_ref[...]).astype(BF16)
    grp = lax.broadcasted_iota(jnp.int32, (GM_CHUNK, GM_W), 1) // (GM_W // GM_GROUPS)
    for ci in range(tm // GM_CHUNK):
        vc = gv[ci * GM_CHUNK:(ci + 1) * GM_CHUNK]
        s = jnp.zeros((GM_CHUNK, GM_W), F32)
        for g in range(GM_GROUPS):
            sg = jnp.dot(gmw_ref[g], vc, preferred_element_type=F32)
            s = jnp.where(grp == g, sg, s)
        m_ref[ci * GM_CHUNK:(ci + 1) * GM_CHUNK, :] = (
            gu[ci * GM_CHUNK:(ci + 1) * GM_CHUNK] * (s + gmb_ref[...])).astype(BF16)

    hy_ref[...] = proj(P_HY, 3 * HY_W).astype(BF16)
    for j in range(3):
        gt_ref[:, j * D_MODEL:(j + 1) * D_MODEL] = jax.nn.sigmoid(
            proj(P_GT + j * D_MODEL, D_MODEL)).astype(BF16)


def _premix(x, shift, scale, lw, rope_tabs, seq, tm):
    n = x.shape[0]
    tiles_per_seq = seq // tm
    batch = n // seq
    use_rope = rope_tabs is not None
    const = lambda i: (0, 0)
    in_specs = [
        pl.BlockSpec((tm, D_MODEL), lambda i: (i, 0)),
        pl.BlockSpec((1, 1, D_MODEL), lambda i: (i // tiles_per_seq, 0, 0)),
        pl.BlockSpec((1, 1, D_MODEL), lambda i: (i // tiles_per_seq, 0, 0)),
        pl.BlockSpec((1, D_MODEL), const),
        pl.BlockSpec((D_MODEL, P_W), const, pipeline_mode=pl.Buffered(1)),
        pl.BlockSpec((1, Q_RANK), const),
        pl.BlockSpec((Q_RANK, HEADS * HEAD_PAD), const),
        pl.BlockSpec((1, HEAD_PAD), const),
        pl.BlockSpec((1, KV_RANK), const),
        pl.BlockSpec((KV_RANK, HEADS * HEAD_PAD), const),
        pl.BlockSpec((HEADS * VDIM, KV_RANK), const),
        pl.BlockSpec((1, HEAD_PAD), const),
        pl.BlockSpec((1, GM_W), const),
        pl.BlockSpec((GM_GROUPS, GM_CHUNK, GM_CHUNK), lambda i: (0, 0, 0)),
        pl.BlockSpec((GM_CHUNK, GM_W), const),
        pl.BlockSpec((1, HEAD_PAD), const),
        pl.BlockSpec((1, HEAD_PAD), const),
    ]
    args = [x, shift, scale, lw['g1'], lw['w_in'], lw['g_qa'], lw['w_uq'], lw['g_qn'], lw['g_kva'], lw['w_k'],
            lw['w_vt'], lw['g_kn'], lw['gm_g'], lw['gm_w'], lw['gm_b'], lw['q_pad'], lw['k_pad']]
    tkv = min(seq, ATTN_TK)
    sub = tkv // tm
    if use_rope:
        in_specs += [pl.BlockSpec((tm, HEAD_PAD), lambda i: (i % tiles_per_seq, 0))] * 3
        args += list(rope_tabs)
    out_shape = (
        jax.ShapeDtypeStruct((n, HEADS * HEAD_PAD), BF16),
        jax.ShapeDtypeStruct((n, HEADS * HEAD_PAD), BF16),
        jax.ShapeDtypeStruct((batch, seq // tkv, HEADS * V_ROWS, tkv), BF16),
        jax.ShapeDtypeStruct((n, GM_W), BF16),
        jax.ShapeDtypeStruct((n, 3 * HY_W), BF16),
        jax.ShapeDtypeStruct((n, 3 * D_MODEL), BF16),
    )
    out_specs = (
        pl.BlockSpec((tm, HEADS * HEAD_PAD), lambda i: (i, 0)),
        pl.BlockSpec((tm, HEADS * HEAD_PAD), lambda i: (i, 0)),
        pl.BlockSpec((1, 1, HEADS * V_ROWS, tm),
                     lambda i: (i // tiles_per_seq, (i % tiles_per_seq) // sub, 0, (i % tiles_per_seq) % sub)),
        pl.BlockSpec((tm, GM_W), lambda i: (i, 0)),
        pl.BlockSpec((tm, 3 * HY_W), lambda i: (i, 0)),
        pl.BlockSpec((tm, 3 * D_MODEL), lambda i: (i, 0)),
    )
    return pl.pallas_call(
        functools.partial(_premix_kernel, use_rope=use_rope, tm=tm),
        out_shape=out_shape, grid=(n // tm,), in_specs=in_specs, out_specs=out_specs,
        compiler_params=_cparams(("arbitrary",)), name="premix",
    )(*args)


def _attn_kernel(*refs, n_chunks, has_ctx, tq):
    if has_ctx:
        q_ref, k_ref, vt_ref, kc_ref, vtc_ref, o_ref = refs
    else:
        q_ref, k_ref, vt_ref, o_ref = refs
    tk = vt_ref.shape[-1]

    def step(hh, kc, vtc, m, acc):
        s = _nt_dot(kc, q_ref[:, hh * HEAD_PAD:(hh + 1) * HEAD_PAD])
        m_new = jnp.maximum(m, jnp.max(s, axis=0, keepdims=True))
        p = jnp.exp2(s - m_new).astype(BF16)
        alpha = jnp.exp2(m - m_new)
        return m_new, acc * alpha + jnp.dot(vtc, p, preferred_element_type=F32)

    def body(i, carry):
        out = []
        for hh in range(2):
            m, acc = carry[hh]
            start = pl.multiple_of(i * tk, tk)
            kc = k_ref[pl.ds(start, tk), hh * HEAD_PAD:(hh + 1) * HEAD_PAD]
            vtc = vt_ref[0, i, hh * V_ROWS:(hh + 1) * V_ROWS, :]
            out.append(step(hh, kc, vtc, m, acc))
        return tuple(out)

    init = tuple((jnp.full((1, tq), NEG_BIG, F32), jnp.zeros((V_ROWS, tq), F32)) for _ in range(2))
    carry = lax.fori_loop(0, n_chunks, body, init)
    outs = []
    for hh in range(2):
        m, acc = carry[hh]
        if has_ctx:
            m, acc = step(hh, kc_ref[:, hh * HEAD_PAD:(hh + 1) * HEAD_PAD],
                          vtc_ref[0, 0, hh * V_ROWS:(hh + 1) * V_ROWS, :], m, acc)
        outs.append(acc[:VDIM] / acc[VDIM:VDIM + 1])
    o_ref[...] = jnp.concatenate(outs, axis=0).T.astype(BF16)


def _attn_bounded_kernel(*refs, n_chunks, has_ctx, tq):
    if has_ctx:
        q_ref, k_ref, vt_ref, kc_ref, vtc_ref, o_ref = refs
    else:
        q_ref, k_ref, vt_ref, o_ref = refs
    tk = vt_ref.shape[-1]

    def step(hh, kc, vtc, acc):
        s = _nt_dot(kc, q_ref[:, hh * HEAD_PAD:(hh + 1) * HEAD_PAD])
        return acc + jnp.dot(vtc, jnp.exp2(s).astype(BF16), preferred_element_type=F32)

    def body(i, carry):
        start = pl.multiple_of(i * tk, tk)
        return tuple(step(hh, k_ref[pl.ds(start, tk), hh * HEAD_PAD:(hh + 1) * HEAD_PAD],
                          vt_ref[0, i, hh * V_ROWS:(hh + 1) * V_ROWS, :], carry[hh]) for hh in range(2))

    carry = lax.fori_loop(0, n_chunks, body, tuple(jnp.zeros((V_ROWS, tq), F32) for _ in range(2)))
    outs = []
    for hh in range(2):
        acc = carry[hh]
        if has_ctx:
            acc = step(hh, kc_ref[:, hh * HEAD_PAD:(hh + 1) * HEAD_PAD],
                       vtc_ref[0, 0, hh * V_ROWS:(hh + 1) * V_ROWS, :], acc)
        outs.append(acc[:VDIM] / acc[VDIM:VDIM + 1])
    o_ref[...] = jnp.concatenate(outs, axis=0).T.astype(BF16)


def _attention(q, k, vt, ctx_kv, seq, tq, bound):
    n = q.shape[0]
    batch = n // seq
    n_chunks, tk = vt.shape[1], vt.shape[3]
    q_tiles = seq // tq
    has_ctx = ctx_kv is not None
    in_specs = [
        pl.BlockSpec((tq, 2 * HEAD_PAD), lambda b, j, i: (b * q_tiles + i, j)),
        pl.BlockSpec((seq, 2 * HEAD_PAD), lambda b, j, i: (b, j)),
        pl.BlockSpec((1, n_chunks, 2 * V_ROWS, tk), lambda b, j, i: (b, 0, j, 0)),
    ]
    args = [q, k, vt]
    if has_ctx:
        kc, vtc = ctx_kv
        lc = vtc.shape[3]
        in_specs += [pl.BlockSpec((lc, 2 * HEAD_PAD), lambda b, j, i: (b, j)),
                     pl.BlockSpec((1, 1, 2 * V_ROWS, lc), lambda b, j, i: (b, 0, j, 0))]
        args += [kc, vtc]
    def call(body, name):
        return pl.pallas_call(
            functools.partial(body, n_chunks=n_chunks, has_ctx=has_ctx, tq=tq),
            out_shape=jax.ShapeDtypeStruct((n, HEADS * VDIM), BF16),
            grid=(batch, HEADS // 2, q_tiles),
            in_specs=in_specs,
            out_specs=pl.BlockSpec((tq, 2 * VDIM), lambda b, j, i: (b * q_tiles + i, j)),
            compiler_params=_cparams(("arbitrary", "arbitrary", "arbitrary")),
            name=name,
        )(*args)

    return lax.cond(bound < ATTN_BOUND_MAX,
                    lambda: call(_attn_bounded_kernel, "attn_bounded"),
                    lambda: call(_attn_kernel, "attn_online"))


def _hy_filter_kernel(z_ref, w1_ref, b1_ref, w2_ref, b2_ref, w3_ref, b3_ref, fr_ref, dec_ref, k_ref, ss_ref):
    z = z_ref[...]
    fr = fr_ref[...]
    hdn = jnp.sin(fr * (jnp.dot(z, w1_ref[...], preferred_element_type=F32, precision=HIGHEST) + b1_ref[...]))
    hdn = jnp.sin(fr * (jnp.dot(hdn, w2_ref[...], preferred_element_type=F32, precision=HIGHEST) + b2_ref[...]))
    k = jnp.dot(hdn, w3_ref[...], preferred_element_type=F32, precision=HIGHEST) + b3_ref[...]
    k = k * jnp.exp(-z[:, 0:1] * jnp.abs(dec_ref[...]))
    k_ref[...] = k
    ss_ref[0] = jnp.sum(k * k, axis=0, keepdims=True)


def _hy_filters(z, lw, tr):
    seq, emb = z.shape
    nblk = seq // tr
    hid = lw['hy_w2'].shape[0]
    const = lambda i: (0, 0)
    return pl.pallas_call(
        _hy_filter_kernel,
        out_shape=(jax.ShapeDtypeStruct((seq, 2 * HY_W), F32), jax.ShapeDtypeStruct((nblk, 1, 2 * HY_W), F32)),
        grid=(nblk,),
        in_specs=[pl.BlockSpec((tr, emb), lambda i: (i, 0)),
                  pl.BlockSpec((emb, hid), const), pl.BlockSpec((1, hid), const),
                  pl.BlockSpec((hid, hid), const), pl.BlockSpec((1, hid), const),
                  pl.BlockSpec((hid, 2 * HY_W), const), pl.BlockSpec((1, 2 * HY_W), const),
                  pl.BlockSpec((1, hid), const), pl.BlockSpec((1, 2 * HY_W), const)],
        out_specs=(pl.BlockSpec((tr, 2 * HY_W), lambda i: (i, 0)),
                   pl.BlockSpec((1, 1, 2 * HY_W), lambda i: (i, 0, 0))),
        compiler_params=_cparams(("arbitrary",)), name="hy_filter",
    )(z, lw['hy_w1'], lw['hy_b1'], lw['hy_w2'], lw['hy_b2'], lw['hy_w3'], lw['hy_b3'], lw['hy_freq'],
      lw['hy_decay'])


def _hy_conv3_kernel(p_ref, prev_ref, next_ref, w_ref, b_ref, x0_ref, u_ref, *, tiles_per_seq, tr):
    i = pl.program_id(0)
    p = p_ref[...].astype(F32)
    first = (i % tiles_per_seq) == 0
    last = (i % tiles_per_seq) == tiles_per_seq - 1
    prev_row = jnp.where(first, 0.0, prev_ref[...].astype(F32)[15:16, :])
    next_row = jnp.where(last, 0.0, next_ref[...].astype(F32)[0:1, :])
    row = lax.broadcasted_iota(jnp.int32, p.shape, 0)
    p_prev = jnp.where(row == 0, prev_row, pltpu.roll(p, 1, 0))
    p_next = jnp.where(row == tr - 1, next_row, pltpu.roll(p, tr - 1, 0))
    z = b_ref[...] + p_prev * w_ref[0:1, :] + p * w_ref[1:2, :] + p_next * w_ref[2:3, :]
    x0_ref[...] = z[:, :HY_W].astype(BF16)
    u_ref[...] = (z[:, 2 * HY_W:] * z[:, HY_W:2 * HY_W]).astype(BF16)


def _hy_conv3(phy, conv_w, conv_b, seq, tr):
    n = phy.shape[0]
    tiles_per_seq = seq // tr
    hb = tr // 16
    nhb = n // 16
    return pl.pallas_call(
        functools.partial(_hy_conv3_kernel, tiles_per_seq=tiles_per_seq, tr=tr),
        out_shape=(jax.ShapeDtypeStruct((n, HY_W), BF16), jax.ShapeDtypeStruct((n, HY_W), BF16)),
        grid=(n // tr,),
        in_specs=[pl.BlockSpec((tr, 3 * HY_W), lambda i: (i, 0)),
                  pl.BlockSpec((16, 3 * HY_W), lambda i: (jnp.maximum(i * hb - 1, 0), 0)),
                  pl.BlockSpec((16, 3 * HY_W), lambda i: (jnp.minimum((i + 1) * hb, nhb - 1), 0)),
                  pl.BlockSpec((3, 3 * HY_W), lambda i: (0, 0)),
                  pl.BlockSpec((1, 3 * HY_W), lambda i: (0, 0))],
        out_specs=(pl.BlockSpec((tr, HY_W), lambda i: (i, 0)), pl.BlockSpec((tr, HY_W), lambda i: (i, 0))),
        compiler_params=_cparams(("arbitrary",)), name="hy_conv3",
    )(phy, phy, phy, conv_w, conv_b)


def _dft_outer_kernel(wr_ref, wi_ref, u_ref, ar_ref, ai_ref):
    u = u_ref[0]
    ar_ref[0] = jnp.dot(wr_ref[...], u, preferred_element_type=F32).astype(BF16)
    ai_ref[0] = jnp.dot(wi_ref[...], u, preferred_element_type=F32).astype(BF16)


def _dft_outer(wr, wi, u, tc):
    nb, kk, cols = u.shape
    n1 = wr.shape[0]
    return pl.pallas_call(
        _dft_outer_kernel,
        out_shape=(jax.ShapeDtypeStruct((nb, n1, cols), BF16),) * 2,
        grid=(nb, cols // tc),
        in_specs=[pl.BlockSpec((n1, kk), lambda b, j: (0, 0)), pl.BlockSpec((n1, kk), lambda b, j: (0, 0)),
                  pl.BlockSpec((1, kk, tc), lambda b, j: (b, 0, j))],
        out_specs=(pl.BlockSpec((1, n1, tc), lambda b, j: (b, 0, j)),) * 2,
        compiler_params=_cparams(("arbitrary", "arbitrary")), name="dft_outer",
    )(wr, wi, u)


def _cdot(mr, mi, xr, xi):
    rr = jnp.dot(mr, xr, preferred_element_type=F32) - jnp.dot(mi, xi, preferred_element_type=F32)
    ri = jnp.dot(mr, xi, preferred_element_type=F32) + jnp.dot(mi, xr, preferred_element_type=F32)
    return rr, ri


def _dft_inner_fwd_kernel(mr_ref, mi_ref, ar_ref, ai_ref, fr_ref, fi_ref):
    fr, fi = _cdot(mr_ref[0], mi_ref[0], ar_ref[0, 0], ai_ref[0, 0])
    fr_ref[0] = fr.astype(BF16)
    fi_ref[0] = fi.astype(BF16)


def _dft_inner_fwd(mr, mi, ar, ai):
    n1, n2, _ = mr.shape
    ch = ar.shape[-1]
    mspec = pl.BlockSpec((1, n2, n2), lambda k: (k, 0, 0))
    aspec = pl.BlockSpec((1, 1, n2, ch), lambda k: (0, k, 0, 0))
    ospec = pl.BlockSpec((1, n2, ch), lambda k: (k, 0, 0))
    return pl.pallas_call(
        _dft_inner_fwd_kernel,
        out_shape=(jax.ShapeDtypeStruct((n1, n2, ch), BF16),) * 2,
        grid=(n1,), in_specs=[mspec, mspec, aspec, aspec], out_specs=(ospec, ospec),
        compiler_params=_cparams(("arbitrary",)), name="dft_inner_fwd",
    )(mr, mi, ar, ai)


def _dft_inner_conv_kernel(mr_ref, mi_ref, tr_ref, ti_ref, ar_ref, ai_ref, fr_ref, fi_ref, br_ref, bi_ref, *, nb):
    fr = fr_ref[0].astype(F32)
    fi = fi_ref[0].astype(F32)
    for b in range(nb):
        ur, ui = _cdot(mr_ref[0], mi_ref[0], ar_ref[b, 0], ai_ref[b, 0])
        yr = (ur * fr - ui * fi).astype(BF16)
        yi = (ur * fi + ui * fr).astype(BF16)
        br, bi = _cdot(tr_ref[0], ti_ref[0], yr, yi)
        br_ref[b, 0] = br.astype(BF16)
        bi_ref[b, 0] = bi.astype(BF16)


def _dft_inner_conv(mr, mi, tr, ti, ar, ai, fr, fi):
    n1, n2, _ = mr.shape
    nb, _, _, ch = ar.shape
    mspec = pl.BlockSpec((1, n2, n2), lambda k: (k, 0, 0))
    aspec = pl.BlockSpec((nb, 1, n2, ch), lambda k: (0, k, 0, 0))
    fspec = pl.BlockSpec((1, n2, ch), lambda k: (k, 0, 0))
    return pl.pallas_call(
        functools.partial(_dft_inner_conv_kernel, nb=nb),
        out_shape=(jax.ShapeDtypeStruct((nb, n1, n2, ch), BF16),) * 2,
        grid=(n1,), in_specs=[mspec, mspec, mspec, mspec, aspec, aspec, fspec, fspec],
        out_specs=(aspec, aspec),
        compiler_params=_cparams(("arbitrary",)), name="dft_inner_conv",
    )(mr, mi, tr, ti, ar, ai, fr, fi)


def _filter_scale(ss_ref):
    ss = jnp.sum(ss_ref[...], axis=0)
    return lax.rsqrt(ss[:, :HY_W] + ss[:, HY_W:] + EPS)


def _hy_final_kernel(cr_ref, ci_ref, br_ref, bi_ref, x0_ref, u_ref, ss_ref, bias_ref, o_ref, *, reps):
    y = (jnp.dot(cr_ref[...], br_ref[0], preferred_element_type=F32)
         + jnp.dot(ci_ref[...], bi_ref[0], preferred_element_type=F32))
    scale = jnp.tile(_filter_scale(ss_ref), (1, reps))
    bias = jnp.tile(bias_ref[...], (1, reps))
    u = u_ref[0].astype(F32)
    o_ref[0] = (x0_ref[0].astype(F32) * (y * scale + u * bias)).astype(BF16)


def _hy_final(cr, ci, br, bi, x0, u, ss, bias, tc):
    nb, n1, cols = br.shape
    n1h = cr.shape[0]
    nblk = ss.shape[0]
    cspec = pl.BlockSpec((n1h, n1), lambda b, j: (0, 0))
    bspec = pl.BlockSpec((1, n1, tc), lambda b, j: (b, 0, j))
    xspec = pl.BlockSpec((1, n1h, tc), lambda b, j: (b, 0, j))
    return pl.pallas_call(
        functools.partial(_hy_final_kernel, reps=tc // HY_W),
        out_shape=jax.ShapeDtypeStruct((nb, n1h, cols), BF16),
        grid=(nb, cols // tc),
        in_specs=[cspec, cspec, bspec, bspec, xspec, xspec,
                  pl.BlockSpec((nblk, 1, 2 * HY_W), lambda b, j: (0, 0, 0)),
                  pl.BlockSpec((1, HY_W), lambda b, j: (0, 0))],
        out_specs=xspec,
        compiler_params=_cparams(("arbitrary", "arbitrary")), name="hy_final",
    )(cr, ci, br, bi, x0, u, ss, bias)


def _hy_direct_kernel(kk_ref, x0_ref, u_ref, ss_ref, bias_ref, o_ref, *, seq):
    u = u_ref[0].astype(F32)
    y = jnp.zeros((seq, HY_W), F32)
    for j in range(seq):
        y = y + kk_ref[seq - j:2 * seq - j, :] * u[j:j + 1, :]
    o_ref[0] = (x0_ref[0].astype(F32) * (y * _filter_scale(ss_ref) + u * bias_ref[...])).astype(BF16)


def _hy_direct(kk, x0, u, ss, bias):
    nb, seq, _ = u.shape
    nblk = ss.shape[0]
    xspec = pl.BlockSpec((1, seq, HY_W), lambda b: (b, 0, 0))
    return pl.pallas_call(
        functools.partial(_hy_direct_kernel, seq=seq),
        out_shape=jax.ShapeDtypeStruct((nb, seq, HY_W), BF16),
        grid=(nb,),
        in_specs=[pl.BlockSpec((2 * seq, HY_W), lambda b: (0, 0)), xspec, xspec,
                  pl.BlockSpec((nblk, 1, 2 * HY_W), lambda b: (0, 0, 0)),
                  pl.BlockSpec((1, HY_W), lambda b: (0, 0))],
        out_specs=xspec,
        compiler_params=_cparams(("arbitrary",)), name="hy_direct",
    )(kk, x0, u, ss, bias)


def _hy_embedding(seq):
    t = jnp.arange(seq, dtype=F32)
    t_unit = t / max(seq - 1, 1)
    bands = jnp.linspace(1e-4, HY_BANDS - 1, HY_BANDS, dtype=F32)
    ang = (2 * jnp.pi / seq) * t[:, None] * bands[None, :]
    return jnp.concatenate([t_unit[:, None], jnp.cos(ang), -jnp.sin(ang)], axis=-1)


def _dft_tables(seq):
    n = 2 * seq
    n2 = DFT_N2
    n1 = n // n2
    a = jnp.arange(n1, dtype=jnp.int32)
    th1 = (2 * jnp.pi / n1) * ((a[:, None] * a[None, :]) % n1).astype(F32)
    w1r, w1i = jnp.cos(th1), -jnp.sin(th1)
    k1 = jnp.arange(n1, dtype=jnp.int32)[:, None, None]
    k2 = jnp.arange(n2, dtype=jnp.int32)[None, :, None]
    t2 = jnp.arange(n2, dtype=jnp.int32)[None, None, :]
    th = (2 * jnp.pi / n) * ((t2 * (k1 + n1 * k2)) % n).astype(F32)
    mr, mi = jnp.cos(th), -jnp.sin(th)
    return dict(
        w1r=w1r.astype(BF16), w1i=w1i.astype(BF16),
        mr=mr.astype(BF16), mi=mi.astype(BF16),
        tr=jnp.swapaxes(mr, 1, 2).astype(BF16), ti=jnp.swapaxes(-mi, 1, 2).astype(BF16),
        cr=(w1r[:n1 // 2] / n).astype(BF16), ci=(w1i[:n1 // 2] / n).astype(BF16))


def _hyena_long(phy, lw, filt, seq, tabs):
    n = phy.shape[0]
    nb = n // seq
    n2 = DFT_N2
    n1 = 2 * seq // n2
    kf, ss = filt
    x0, u = _hy_conv3(phy, lw['hy_conv_w'], lw['hy_conv_b'], seq, min(seq, 1024))
    f = jnp.concatenate([kf[:, :HY_W], jnp.zeros((1, HY_W), F32), kf[:0:-1, HY_W:]], axis=0).astype(BF16)
    tc = min(n2 * HY_W, 8192)
    far, fai = _dft_outer(tabs['w1r'], tabs['w1i'], f.reshape(1, n1, n2 * HY_W), tc)
    fr, fi = _dft_inner_fwd(tabs['mr'], tabs['mi'], far.reshape(1, n1, n2, HY_W), fai.reshape(1, n1, n2, HY_W))
    ar, ai = _dft_outer(tabs['w1r'][:, :n1 // 2], tabs['w1i'][:, :n1 // 2], u.reshape(nb, n1 // 2, n2 * HY_W), tc)
    br, bi = _dft_inner_conv(tabs['mr'], tabs['mi'], tabs['tr'], tabs['ti'],
                             ar.reshape(nb, n1, n2, HY_W), ai.reshape(nb, n1, n2, HY_W), fr, fi)
    out = _hy_final(tabs['cr'], tabs['ci'], br.reshape(nb, n1, n2 * HY_W), bi.reshape(nb, n1, n2 * HY_W),
                    x0.reshape(nb, n1 // 2, n2 * HY_W), u.reshape(nb, n1 // 2, n2 * HY_W), ss, lw['hy_bias'], tc)
    return out.reshape(n, HY_W)


def _hyena_short(phy, lw, filt, seq):
    n = phy.shape[0]
    nb = n // seq
    kf, ss = filt
    x0, u = _hy_conv3(phy, lw['hy_conv_w'], lw['hy_conv_b'], seq, seq)
    kk = jnp.concatenate([jnp.zeros((1, HY_W), F32), kf[:0:-1, HY_W:], kf[:, :HY_W]], axis=0)
    out = _hy_direct(kk, x0.reshape(nb, seq, HY_W), u.reshape(nb, seq, HY_W), ss, lw['hy_bias'])
    return out.reshape(n, HY_W)


def _pair_top2_sum(a, b, c, d):
    return jnp.maximum(jnp.maximum(jnp.maximum(a + b, a + c), jnp.maximum(a + d, b + c)),
                       jnp.maximum(b + d, c + d))


def _route(logits_t, rb):
    aff = jax.nn.sigmoid(logits_t)
    sel = aff + rb
    rows = [sel[e:e + 1, :] for e in range(N_EXPERTS)]
    affr = [aff[e:e + 1, :] for e in range(N_EXPERTS)]
    best, bidx = None, None
    for g in range(N_GROUPS):
        gs = _pair_top2_sum(*rows[g * EXP_PER_GROUP:(g + 1) * EXP_PER_GROUP])
        if g == 0:
            best, bidx = gs, jnp.zeros(gs.shape, jnp.int32)
        else:
            upd = gs > best
            bidx = jnp.where(upd, g, bidx)
            best = jnp.where(upd, gs, best)
    vals = [jnp.where(bidx == e // EXP_PER_GROUP, rows[e], -jnp.inf) for e in range(N_EXPERTS)]
    m1, i1 = vals[0], jnp.zeros(best.shape, jnp.int32)
    for e in range(1, N_EXPERTS):
        upd = vals[e] > m1
        i1 = jnp.where(upd, e, i1)
        m1 = jnp.where(upd, vals[e], m1)
    m2, i2 = jnp.full(best.shape, -jnp.inf, F32), jnp.zeros(best.shape, jnp.int32)
    for e in range(N_EXPERTS):
        cand = jnp.where(i1 == e, -jnp.inf, vals[e])
        upd = cand > m2
        i2 = jnp.where(upd, e, i2)
        m2 = jnp.where(upd, cand, m2)
    a1 = sum(jnp.where(i1 == e, affr[e], 0.0) for e in range(N_EXPERTS))
    a2 = sum(jnp.where(i2 == e, affr[e], 0.0) for e in range(N_EXPERTS))
    inv = 1.0 / (a1 + a2)
    return jnp.concatenate(
        [jnp.where(i1 == e, a1 * inv, 0.0) + jnp.where(i2 == e, a2 * inv, 0.0) for e in range(N_EXPERTS)], axis=0)


def _merge_kernel(a_ref, b_ref, m_ref, gt_ref, x_ref, gate_ref, shift_ref, scale_ref, g2_ref,
                  wpa_ref, wpb_ref, wpc_ref, wout_ref, rwt_ref, rb_ref, xo_ref, h2_ref, comb_ref):
    y = gt_ref[:, 0:D_MODEL].astype(F32) * jnp.dot(a_ref[...], wpa_ref[...], preferred_element_type=F32)
    y = y + gt_ref[:, D_MODEL:2 * D_MODEL].astype(F32) * jnp.dot(b_ref[...], wpb_ref[...],
                                                                 preferred_element_type=F32)
    y = y + gt_ref[:, 2 * D_MODEL:].astype(F32) * jnp.dot(m_ref[...], wpc_ref[...], preferred_element_type=F32)
    y2 = jnp.dot(y.astype(BF16), wout_ref[...], preferred_element_type=F32)
    xn = x_ref[...] + gate_ref[0] * y2
    xo_ref[...] = xn
    h2 = _rms(xn) * g2_ref[...]
    h2 = h2 * (1.0 + scale_ref[0]) + shift_ref[0]
    h2_ref[...] = h2.astype(BF16)
    logits_t = lax.dot_general(rwt_ref[...], h2, (((1,), (1,)), ((), ())), preferred_element_type=F32,
                               precision=HIGHEST)
    comb_ref[...] = _route(logits_t, rb_ref[...])


def _merge(a, b, m, gt, x, gate, shift, scale, lw, rw_t, rb, seq, tm):
    n = x.shape[0]
    tiles_per_seq = seq // tm
    const = lambda i: (0, 0)
    row = lambda w: pl.BlockSpec((tm, w), lambda i: (i, 0))
    vec = pl.BlockSpec((1, 1, D_MODEL), lambda i: (i // tiles_per_seq, 0, 0))
    return pl.pallas_call(
        _merge_kernel,
        out_shape=(jax.ShapeDtypeStruct((n, D_MODEL), F32), jax.ShapeDtypeStruct((n, D_MODEL), BF16),
                   jax.ShapeDtypeStruct((N_EXPERTS, n), F32)),
        grid=(n // tm,),
        in_specs=[row(HEADS * VDIM), row(HY_W), row(GM_W), row(3 * D_MODEL), row(D_MODEL), vec, vec, vec,
                  pl.BlockSpec((1, D_MODEL), const),
                  pl.BlockSpec((HEADS * VDIM, D_MODEL), const), pl.BlockSpec((HY_W, D_MODEL), const),
                  pl.BlockSpec((GM_W, D_MODEL), const), pl.BlockSpec((D_MODEL, D_MODEL), const),
                  pl.BlockSpec((N_EXPERTS, D_MODEL), const), pl.BlockSpec((N_EXPERTS, 1), const)],
        out_specs=(row(D_MODEL), row(D_MODEL), pl.BlockSpec((N_EXPERTS, tm), lambda i: (0, i))),
        compiler_params=_cparams(("arbitrary",)), name="merge",
    )(a, b, m, gt, x, gate, shift, scale, lw['g2'], lw['w_pa'], lw['w_pb'], lw['w_pc'], lw['w_out'], rw_t, rb)


def _moe_kernel(h_ref, comb_ref, x_ref, gate_ref, wgu_ref, wd_ref, o_ref, acc_ref):
    e = pl.program_id(1)

    @pl.when(e == 0)
    def _():
        acc_ref[...] = jnp.zeros_like(acc_ref)

    gu = jnp.dot(h_ref[...], wgu_ref[0], preferred_element_type=F32)
    g = gu[:, :D_EXPERT]
    hid = (g * jax.nn.sigmoid(g) * gu[:, D_EXPERT:]).astype(BF16)
    y = jnp.dot(hid, wd_ref[0], preferred_element_type=F32)
    lane = lax.broadcasted_iota(jnp.int32, comb_ref.shape, 1)
    col = jnp.sum(jnp.where(lane == e, comb_ref[...], 0.0), axis=-1, keepdims=True)
    acc_ref[...] += col * y

    @pl.when(e == N_EXPERTS - 1)
    def _():
        o_ref[...] = x_ref[...] + gate_ref[0] * acc_ref[...]


def _moe(h2, comb, x, gate, lw, seq, tm):
    n = x.shape[0]
    tiles_per_seq = seq // tm
    return pl.pallas_call(
        _moe_kernel,
        out_shape=jax.ShapeDtypeStruct((n, D_MODEL), F32),
        grid=(n // tm, N_EXPERTS),
        in_specs=[pl.BlockSpec((tm, D_MODEL), lambda i, e: (i, 0)),
                  pl.BlockSpec((tm, N_EXPERTS), lambda i, e: (i, 0)),
                  pl.BlockSpec((tm, D_MODEL), lambda i, e: (i, 0)),
                  pl.BlockSpec((1, 1, D_MODEL), lambda i, e: (i // tiles_per_seq, 0, 0)),
                  pl.BlockSpec((1, D_MODEL, 2 * D_EXPERT), lambda i, e: (e, 0, 0)),
                  pl.BlockSpec((1, D_EXPERT, D_MODEL), lambda i, e: (e, 0, 0))],
        out_specs=pl.BlockSpec((tm, D_MODEL), lambda i, e: (i, 0)),
        scratch_shapes=[pltpu.VMEM((tm, D_MODEL), F32)],
        compiler_params=_cparams(("arbitrary", "arbitrary")), name="moe",
    )(h2, comb, x, gate, lw['w_gu'], lw['w_d'])


def _head_perm():
    rope_idx = list(range(NOPE, QK, 2)) + list(range(NOPE + 1, QK, 2))
    return jnp.array(list(range(NOPE)) + rope_idx, dtype=jnp.int32)


def _layer_weights(p, l):
    perm = _head_perm()
    row = lambda v: v.reshape(1, -1)
    w_in = p['w_in'][l]
    kr_perm = jnp.array(list(range(0, ROPE, 2)) + list(range(1, ROPE, 2)), dtype=jnp.int32)
    w_kr = jnp.pad(w_in[:, OFF_KR:OFF_HY][:, kr_perm], ((0, 0), (0, HEAD_PAD - ROPE)))
    w_in2 = jnp.concatenate([w_in[:, OFF_Q:OFF_KR], w_kr, w_in[:, OFF_HY:]], axis=1).astype(BF16)
    w_uq = p['w_uq'][l].reshape(Q_RANK, HEADS, QK)[:, :, perm]
    w_uq = jnp.pad(w_uq, ((0, 0), (0, 0), (0, HEAD_PAD - QK))).reshape(Q_RANK, HEADS * HEAD_PAD).astype(BF16)
    w_ukv = p['w_ukv'][l].reshape(KV_RANK, HEADS, NOPE + VDIM)
    w_k = jnp.pad(w_ukv[:, :, :NOPE], ((0, 0), (0, 0), (0, HEAD_PAD - NOPE)))
    w_k = w_k.reshape(KV_RANK, HEADS * HEAD_PAD).astype(BF16)
    w_vt = w_ukv[:, :, NOPE:].reshape(KV_RANK, HEADS * VDIM).T.astype(BF16)
    pad_gain = lambda g: jnp.pad(g[perm], (0, HEAD_PAD - QK)).reshape(1, HEAD_PAD)
    gm_b = jnp.repeat(p['gm_bs'][l].T, GM_W // GM_GROUPS, axis=1)
    bound = 1.02 * QSCALE * QK * jnp.max(jnp.abs(p['g_qn'][l])) * jnp.max(jnp.abs(p['g_kn'][l]))
    pad_lane = jnp.arange(HEAD_PAD) == QK
    return dict(
        attn_bound=bound,
        q_pad=jnp.where(pad_lane, -bound, 0.0).reshape(1, HEAD_PAD).astype(F32),
        k_pad=jnp.where(pad_lane, 1.0, 0.0).reshape(1, HEAD_PAD).astype(F32),
        g1=row(p['norm1_g'][l]), g2=row(p['norm2_g'][l]), w_in=w_in2,
        g_qa=row(p['g_qa'][l]), w_uq=w_uq, g_qn=pad_gain(p['g_qn'][l]),
        g_kva=row(p['g_kva'][l]), w_k=w_k, w_vt=w_vt, g_kn=pad_gain(p['g_kn'][l]),
        gm_g=row(p['gm_norm_g'][l]), gm_w=p['gm_ws'][l].astype(BF16), gm_b=gm_b,
        hy_conv_w=p['hy_conv_w'][l], hy_conv_b=row(p['hy_conv_b'][l]),
        hy_w1=p['hy_w1'][l], hy_b1=row(p['hy_b1'][l]), hy_w2=p['hy_w2'][l], hy_b2=row(p['hy_b2'][l]),
        hy_w3=p['hy_w3'][l], hy_b3=row(p['hy_b3'][l]), hy_freq=row(p['hy_freq'][l]),
        hy_decay=p['hy_decay'][l].reshape(1, 2 * HY_W), hy_bias=row(p['hy_bias'][l]),
        w_pa=p['w_pa'][l].astype(BF16), w_pb=p['w_pb'][l].astype(BF16), w_pc=p['w_pc'][l].astype(BF16),
        w_out=p['w_out'][l].astype(BF16),
        w_gu=jnp.concatenate([p['moe_w_gate'][l], p['moe_w_up'][l]], axis=-1).astype(BF16),
        w_d=p['moe_w_down'][l].astype(BF16))


def _rope_tables(seq):
    rows = seq // GRID_W
    row = jnp.repeat(jnp.arange(rows, dtype=F32), GRID_W)
    col = jnp.tile(jnp.arange(GRID_W, dtype=F32), rows)
    n_freq = ROPE // 4
    inv = ROPE_THETA ** (-jnp.arange(n_freq, dtype=F32) / n_freq)
    ang = jnp.concatenate([row[:, None] * inv, col[:, None] * inv], axis=-1)
    c, s = jnp.cos(ang), jnp.sin(ang)
    z = lambda w: jnp.zeros((seq, w), F32)
    rc = jnp.concatenate([jnp.ones((seq, NOPE), F32), c, c, z(HEAD_PAD - QK)], axis=1)
    rs1 = jnp.concatenate([z(NOPE), z(ROPE // 2), s, z(HEAD_PAD - QK)], axis=1)
    rs2 = jnp.concatenate([z(NOPE), -s, z(ROPE // 2), z(HEAD_PAD - QK)], axis=1)
    return rc, rs1, rs2


def _mixer_and_ffn(x, mods, lw, rw_t, rb, seq, tm, rope_tabs, ctx_kv, filt, dft_tabs, moe_tm):
    shift1, scale1, gate1, shift2, scale2, gate2 = mods
    q, k, vt, m, phy, gt = _premix(x, shift1, scale1, lw, rope_tabs, seq, tm)
    a = _attention(q, k, vt, ctx_kv, seq, min(seq, ATTN_TQ), lw['attn_bound'])
    if dft_tabs is not None:
        b = _hyena_long(phy, lw, filt, seq, dft_tabs)
    else:
        b = _hyena_short(phy, lw, filt, seq)
    xn, h2, comb_t = _merge(a, b, m, gt, x, gate1, shift2, scale2, lw, rw_t, rb, seq, tm)
    return _moe(h2, comb_t.T, xn, gate2, lw, seq, moe_tm)


def _forward(p):
    x, ctx = p['x'], p['ctx']
    batch, seq, _ = x.shape
    lc = ctx.shape[1]
    depth = p['w_mod'].shape[0]

    cvecs = jnp.concatenate([p['c'], p['c_ctx'][None], jnp.zeros((8 - batch - 1, D_MODEL), F32)], axis=0)
    mod_all = _modvec(cvecs, p['w_mod'], p['b_mod'])
    rw_t = p['router_w'].T
    rb = p['router_b'].reshape(N_EXPERTS, 1)
    rope_tabs = _rope_tables(seq)
    dft_tabs = _dft_tables(seq)
    z_lat, z_ctx = _hy_embedding(seq), _hy_embedding(lc)

    xl = x.reshape(batch * seq, D_MODEL)
    xc = ctx.reshape(batch * lc, D_MODEL)
    tm_lat = min(seq, 512)
    for l in range(depth):
        lw = _layer_weights(p, l)
        mod = mod_all[l].reshape(8, N_MOD, D_MODEL)
        mods_lat = [mod[:batch, j].reshape(batch, 1, D_MODEL) for j in range(N_MOD)]
        mods_ctx = [jnp.broadcast_to(mod[batch, j].reshape(1, 1, D_MODEL), (batch, 1, D_MODEL))
                    for j in range(N_MOD)]
        if l == depth - 1:
            _, k_c, vt_c, _, _, _ = _premix(xc, mods_ctx[0], mods_ctx[1], lw, None, lc, lc)
        else:
            q_c, k_c, vt_c, m_c, phy_c, gt_c = _premix(xc, mods_ctx[0], mods_ctx[1], lw, None, lc, lc)
            a_c = _attention(q_c, k_c, vt_c, None, lc, lc, lw['attn_bound'])
            b_c = _hyena_short(phy_c, lw, _hy_filters(z_ctx, lw, lc), lc)
            xn_c, h2_c, comb_c = _merge(a_c, b_c, m_c, gt_c, xc, mods_ctx[2], mods_ctx[3], mods_ctx[4], lw,
                                        rw_t, rb, lc, lc)
            xc = _moe(h2_c, comb_c.T, xn_c, mods_ctx[5], lw, lc, lc)
        filt = _hy_filters(z_lat, lw, min(seq, 2048))
        xl = _mixer_and_ffn(xl, mods_lat, lw, rw_t, rb, seq, tm_lat, rope_tabs, (k_c, vt_c), filt, dft_tabs,
                            min(seq, 1024))
    return xl.reshape(batch, seq, D_MODEL)


def kernel(x, c, ctx, c_ctx, w_mod, b_mod, norm1_g, norm2_g, w_in, g_qa, w_uq, g_kva, w_ukv, g_qn, g_kn,
           hy_conv_w, hy_conv_b, hy_w1, hy_b1, hy_w2, hy_b2, hy_w3, hy_b3, hy_freq, hy_decay, hy_bias,
           gm_norm_g, gm_ws, gm_bs, w_pa, w_pb, w_pc, w_out, router_w, router_b,
           moe_w_gate, moe_w_up, moe_w_down):
    return _forward(dict(
        x=x, c=c, ctx=ctx, c_ctx=c_ctx, w_mod=w_mod, b_mod=b_mod, norm1_g=norm1_g, norm2_g=norm2_g, w_in=w_in,
        g_qa=g_qa, w_uq=w_uq, g_kva=g_kva, w_ukv=w_ukv, g_qn=g_qn, g_kn=g_kn, hy_conv_w=hy_conv_w,
        hy_conv_b=hy_conv_b, hy_w1=hy_w1, hy_b1=hy_b1, hy_w2=hy_w2, hy_b2=hy_b2, hy_w3=hy_w3, hy_b3=hy_b3,
        hy_freq=hy_freq, hy_decay=hy_decay, hy_bias=hy_bias, gm_norm_g=gm_norm_g, gm_ws=gm_ws, gm_bs=gm_bs,
        w_pa=w_pa, w_pb=w_pb, w_pc=w_pc, w_out=w_out, router_w=router_w, router_b=router_b,
        moe_w_gate=moe_w_gate, moe_w_up=moe_w_up, moe_w_down=moe_w_down))
```

```python
import functools
import math

import jax
import jax.numpy as jnp
from jax import lax
from jax.experimental import pallas as pl
from jax.experimental.pallas import tpu as pltpu

F32 = jnp.float32
BF16 = jnp.bfloat16
HIGHEST = lax.Precision.HIGHEST

D_MODEL = 1024
GRID_W = 64
EPS = 1e-6
N_MOD = 6

HEADS = 8
Q_RANK = 384
KV_RANK = 256
NOPE = 64
ROPE = 32
QK = NOPE + ROPE
VDIM = 64
HEAD_PAD = 128
ROPE_THETA = 10000.0
V_ROWS = 80

HY_W = 256
HY_BANDS = 16
GM_W = 256
GM_CHUNK = 128
GM_GROUPS = 4

OFF_Q = 0
OFF_KV = OFF_Q + Q_RANK
OFF_KR = OFF_KV + KV_RANK
OFF_HY = OFF_KR + ROPE
OFF_GM = OFF_HY + 3 * HY_W
OFF_GT = OFF_GM + 2 * GM_W

P_Q = 0
P_KV = P_Q + Q_RANK
P_KR = P_KV + KV_RANK
P_HY = P_KR + HEAD_PAD
P_GM = P_HY + 3 * HY_W
P_GT = P_GM + 2 * GM_W
P_W = P_GT + 3 * D_MODEL

N_EXPERTS = 16
N_GROUPS = 4
EXP_PER_GROUP = 4
TOP_K = 2
D_EXPERT = 512
MOE_BM = 256
MOE_TC = 256

DFT_N2 = 256

VMEM_LIMIT = 56 * 1024 * 1024
ATTN_TQ = 512
ATTN_TK = 2048
ATTN_BOUND_MAX = 50.0
NEG_BIG = -1e30
LOG2E = 1.4426950408889634
QSCALE = QK ** -0.5 * LOG2E


def _cparams(sem):
    return pltpu.CompilerParams(dimension_semantics=sem, vmem_limit_bytes=VMEM_LIMIT)


def _rms(x):
    return x * lax.rsqrt(jnp.mean(x * x, axis=-1, keepdims=True) + EPS)


def _nt_dot(a, b):
    return lax.dot_general(a, b, (((1,), (1,)), ((), ())), preferred_element_type=F32)


def _modvec_kernel(c_ref, w_ref, b_ref, o_ref):
    cv = c_ref[...]
    s = cv * jax.nn.sigmoid(cv)
    o_ref[0] = jnp.dot(s, w_ref[0], preferred_element_type=F32, precision=HIGHEST) + b_ref[0]


def _modvec(cvecs, w_mod, b_mod):
    depth = w_mod.shape[0]
    tn = 1536
    return pl.pallas_call(
        _modvec_kernel,
        out_shape=jax.ShapeDtypeStruct((depth, 8, N_MOD * D_MODEL), F32),
        grid=(depth, N_MOD * D_MODEL // tn),
        in_specs=[pl.BlockSpec((8, D_MODEL), lambda l, j: (0, 0)),
                  pl.BlockSpec((1, D_MODEL, tn), lambda l, j: (l, 0, j)),
                  pl.BlockSpec((1, 1, tn), lambda l, j: (l, 0, j))],
        out_specs=pl.BlockSpec((1, 8, tn), lambda l, j: (l, 0, j)),
        compiler_params=_cparams(("arbitrary", "arbitrary")),
        name="modvec",
    )(cvecs, w_mod, b_mod.reshape(depth, 1, N_MOD * D_MODEL))


def _head_norm_rope(xh, gain, rope):
    ms = jnp.sum(xh * xh, axis=-1, keepdims=True) * (1.0 / QK)
    xh = xh * lax.rsqrt(ms + EPS) * gain
    if rope is not None:
        rc, rs1, rs2 = rope
        xh = xh * rc + pltpu.roll(xh, ROPE // 2, 1) * rs1 + pltpu.roll(xh, HEAD_PAD - ROPE // 2, 1) * rs2
    return xh


def _premix_kernel(*refs, use_rope, tm):
    if use_rope:
        (x_ref, shift_ref, scale_ref, g1_ref, win_ref, gqa_ref, wuq_ref, gqn_ref, gkva_ref, wk_ref, wvt_ref,
         gkn_ref, gmg_ref, gmw_ref, gmb_ref, qpad_ref, kpad_ref, rc_ref, rs1_ref, rs2_ref,
         q_ref, k_ref, vt_ref, m_ref, hy_ref, gt_ref) = refs
        rope = (rc_ref[...], rs1_ref[...], rs2_ref[...])
    else:
        (x_ref, shift_ref, scale_ref, g1_ref, win_ref, gqa_ref, wuq_ref, gqn_ref, gkva_ref, wk_ref, wvt_ref,
         gkn_ref, gmg_ref, gmw_ref, gmb_ref, qpad_ref, kpad_ref,
         q_ref, k_ref, vt_ref, m_ref, hy_ref, gt_ref) = refs
        rope = None

    x = x_ref[...]
    h = _rms(x) * g1_ref[...]
    h = h * (1.0 + scale_ref[0]) + shift_ref[0]
    hb = h.astype(BF16)

    def proj(lo, width):
        return jnp.dot(hb, win_ref[:, lo:lo + width], preferred_element_type=F32)

    qa = (_rms(proj(P_Q, Q_RANK)) * gqa_ref[...]).astype(BF16)
    q = jnp.dot(qa, wuq_ref[...], preferred_element_type=F32)
    for hh in range(HEADS):
        qh = _head_norm_rope(q[:, hh * HEAD_PAD:(hh + 1) * HEAD_PAD], gqn_ref[...], rope)
        q_ref[:, hh * HEAD_PAD:(hh + 1) * HEAD_PAD] = (qh * QSCALE + qpad_ref[...]).astype(BF16)

    kva = (_rms(proj(P_KV, KV_RANK)) * gkva_ref[...]).astype(BF16)
    kr = pltpu.roll(proj(P_KR, HEAD_PAD), NOPE, 1)
    kn = jnp.dot(kva, wk_ref[...], preferred_element_type=F32)
    for hh in range(HEADS):
        kh = _head_norm_rope(kn[:, hh * HEAD_PAD:(hh + 1) * HEAD_PAD] + kr, gkn_ref[...], rope)
        k_ref[:, hh * HEAD_PAD:(hh + 1) * HEAD_PAD] = (kh + kpad_ref[...]).astype(BF16)
    vt = _nt_dot(wvt_ref[...], kva)
    row = lax.broadcasted_iota(jnp.int32, (V_ROWS - VDIM, tm), 0)
    ones_rows = jnp.where(row == 0, 1.0, 0.0).astype(BF16)
    for hh in range(HEADS):
        vt_ref[0, 0, hh * V_ROWS:hh * V_ROWS + VDIM, :] = vt[hh * VDIM:(hh + 1) * VDIM].astype(BF16)
        vt_ref[0, 0, hh * V_ROWS + VDIM:(hh + 1) * V_ROWS, :] = ones_rows

    gg = jax.nn.gelu(proj(P_GM, 2 * GM_W), approximate=True)
    gu = gg[:, :GM_W]
    gv = (_rms(gg[:, GM_W:]) * gmg_ref[...]).astype(BF16)
    grp = lax.broadcasted_iota(jnp.int32, (GM_CHUNK, GM_W), 1) // (GM_W // GM_GROUPS)
    for ci in range(tm // GM_CHUNK):
        vc = gv[ci * GM_CHUNK:(ci + 1) * GM_CHUNK]
        s = jnp.zeros((GM_CHUNK, GM_W), F32)
        for g in range(GM_GROUPS):
            sg = jnp.dot(gmw_ref[g], vc, preferred_element_type=F32)
            s = jnp.where(grp == g, sg, s)
        m_ref[ci * GM_CHUNK:(ci + 1) * GM_CHUNK, :] = (
            gu[ci * GM_CHUNK:(ci + 1) * GM_CHUNK] * (s + gmb_ref[...])).astype(BF16)

    hy_ref[...] = proj(P_HY, 3 * HY_W).astype(BF16)
    for j in range(3):
        gt_ref[:, j * D_MODEL:(j + 1) * D_MODEL] = jax.nn.sigmoid(
            proj(P_GT + j * D_MODEL, D_MODEL)).astype(BF16)


def _premix(x, shift, scale, lw, rope_tabs, seq, tm):
    n = x.shape[0]
    tiles_per_seq = seq // tm
    batch = n // seq
    use_rope = rope_tabs is not None
    const = lambda i: (0, 0)
    in_specs = [
        pl.BlockSpec((tm, D_MODEL), lambda i: (i, 0)),
        pl.BlockSpec((1, 1, D_MODEL), lambda i: (i // tiles_per_seq, 0, 0)),
        pl.BlockSpec((1, 1, D_MODEL), lambda i: (i // tiles_per_seq, 0, 0)),
        pl.BlockSpec((1, D_MODEL), const),
        pl.BlockSpec((D_MODEL, P_W), const, pipeline_mode=pl.Buffered(1)),
        pl.BlockSpec((1, Q_RANK), const),
        pl.BlockSpec((Q_RANK, HEADS * HEAD_PAD), const),
        pl.BlockSpec((1, HEAD_PAD), const),
        pl.BlockSpec((1, KV_RANK), const),
        pl.BlockSpec((KV_RANK, HEADS * HEAD_PAD), const),
        pl.BlockSpec((HEADS * VDIM, KV_RANK), const),
        pl.BlockSpec((1, HEAD_PAD), const),
        pl.BlockSpec((1, GM_W), const),
        pl.BlockSpec((GM_GROUPS, GM_CHUNK, GM_CHUNK), lambda i: (0, 0, 0)),
        pl.BlockSpec((GM_CHUNK, GM_W), const),
        pl.BlockSpec((1, HEAD_PAD), const),
        pl.BlockSpec((1, HEAD_PAD), const),
    ]
    args = [x, shift, scale, lw['g1'], lw['w_in'], lw['g_qa'], lw['w_uq'], lw['g_qn'], lw['g_kva'], lw['w_k'],
            lw['w_vt'], lw['g_kn'], lw['gm_g'], lw['gm_w'], lw['gm_b'], lw['q_pad'], lw['k_pad']]
    tkv = min(seq, ATTN_TK)
    sub = tkv // tm
    if use_rope:
        in_specs += [pl.BlockSpec((tm, HEAD_PAD), lambda i: (i % tiles_per_seq, 0))] * 3
        args += list(rope_tabs)
    out_shape = (
        jax.ShapeDtypeStruct((n, HEADS * HEAD_PAD), BF16),
        jax.ShapeDtypeStruct((n, HEADS * HEAD_PAD), BF16),
        jax.ShapeDtypeStruct((batch, seq // tkv, HEADS * V_ROWS, tkv), BF16),
        jax.ShapeDtypeStruct((n, GM_W), BF16),
        jax.ShapeDtypeStruct((n, 3 * HY_W), BF16),
        jax.ShapeDtypeStruct((n, 3 * D_MODEL), BF16),
    )
    out_specs = (
        pl.BlockSpec((tm, HEADS * HEAD_PAD), lambda i: (i, 0)),
        pl.BlockSpec((tm, HEADS * HEAD_PAD), lambda i: (i, 0)),
        pl.BlockSpec((1, 1, HEADS * V_ROWS, tm),
                     lambda i: (i // tiles_per_seq, (i % tiles_per_seq) // sub, 0, (i % tiles_per_seq) % sub)),
        pl.BlockSpec((tm, GM_W), lambda i: (i, 0)),
        pl.BlockSpec((tm, 3 * HY_W), lambda i: (i, 0)),
        pl.BlockSpec((tm, 3 * D_MODEL), lambda i: (i, 0)),
    )
    return pl.pallas_call(
        functools.partial(_premix_kernel, use_rope=use_rope, tm=tm),
        out_shape=out_shape, grid=(n // tm,), in_specs=in_specs, out_specs=out_specs,
        compiler_params=_cparams(("arbitrary",)), name="premix",
    )(*args)


def _attn_kernel(*refs, n_chunks, has_ctx, tq):
    if has_ctx:
        q_ref, k_ref, vt_ref, kc_ref, vtc_ref, o_ref = refs
    else:
        q_ref, k_ref, vt_ref, o_ref = refs
    tk = vt_ref.shape[-1]

    def step(hh, kc, vtc, m, acc):
        s = _nt_dot(kc, q_ref[:, hh * HEAD_PAD:(hh + 1) * HEAD_PAD])
        m_new = jnp.maximum(m, jnp.max(s, axis=0, keepdims=True))
        p = jnp.exp2(s - m_new).astype(BF16)
        alpha = jnp.exp2(m - m_new)
        return m_new, acc * alpha + jnp.dot(vtc, p, preferred_element_type=F32)

    def body(i, carry):
        out = []
        for hh in range(2):
            m, acc = carry[hh]
            start = pl.multiple_of(i * tk, tk)
            kc = k_ref[pl.ds(start, tk), hh * HEAD_PAD:(hh + 1) * HEAD_PAD]
            vtc = vt_ref[0, i, hh * V_ROWS:(hh + 1) * V_ROWS, :]
            out.append(step(hh, kc, vtc, m, acc))
        return tuple(out)

    init = tuple((jnp.full((1, tq), NEG_BIG, F32), jnp.zeros((V_ROWS, tq), F32)) for _ in range(2))
    carry = lax.fori_loop(0, n_chunks, body, init)
    outs = []
    for hh in range(2):
        m, acc = carry[hh]
        if has_ctx:
            m, acc = step(hh, kc_ref[:, hh * HEAD_PAD:(hh + 1) * HEAD_PAD],
                          vtc_ref[0, 0, hh * V_ROWS:(hh + 1) * V_ROWS, :], m, acc)
        outs.append(acc[:VDIM] / acc[VDIM:VDIM + 1])
    o_ref[...] = jnp.concatenate(outs, axis=0).T.astype(BF16)


def _attn_bounded_kernel(*refs, n_chunks, has_ctx, tq):
    if has_ctx:
        q_ref, k_ref, vt_ref, kc_ref, vtc_ref, o_ref = refs
    else:
        q_ref, k_ref, vt_ref, o_ref = refs
    tk = vt_ref.shape[-1]

    def step(hh, kc, vtc, acc):
        s = _nt_dot(kc, q_ref[:, hh * HEAD_PAD:(hh + 1) * HEAD_PAD])
        return acc + jnp.dot(vtc, jnp.exp2(s).astype(BF16), preferred_element_type=F32)

    def body(i, carry):
        start = pl.multiple_of(i * tk, tk)
        return tuple(step(hh, k_ref[pl.ds(start, tk), hh * HEAD_PAD:(hh + 1) * HEAD_PAD],
                          vt_ref[0, i, hh * V_ROWS:(hh + 1) * V_ROWS, :], carry[hh]) for hh in range(2))

    carry = lax.fori_loop(0, n_chunks, body, tuple(jnp.zeros((V_ROWS, tq), F32) for _ in range(2)))
    outs = []
    for hh in range(2):
        acc = carry[hh]
        if has_ctx:
            acc = step(hh, kc_ref[:, hh * HEAD_PAD:(hh + 1) * HEAD_PAD],
                       vtc_ref[0, 0, hh * V_ROWS:(hh + 1) * V_ROWS, :], acc)
        outs.append(acc[:VDIM] / acc[VDIM:VDIM + 1])
    o_ref[...] = jnp.concatenate(outs, axis=0).T.astype(BF16)


def _attention(q, k, vt, ctx_kv, seq, tq, bound):
    n = q.shape[0]
    batch = n // seq
    n_chunks, tk = vt.shape[1], vt.shape[3]
    q_tiles = seq // tq
    has_ctx = ctx_kv is not None
    in_specs = [
        pl.BlockSpec((tq, 2 * HEAD_PAD), lambda b, j, i: (b * q_tiles + i, j)),
        pl.BlockSpec((seq, 2 * HEAD_PAD), lambda b, j, i: (b, j)),
        pl.BlockSpec((1, n_chunks, 2 * V_ROWS, tk), lambda b, j, i: (b, 0, j, 0)),
    ]
    args = [q, k, vt]
    if has_ctx:
        kc, vtc = ctx_kv
        lc = vtc.shape[3]
        in_specs += [pl.BlockSpec((lc, 2 * HEAD_PAD), lambda b, j, i: (b, j)),
                     pl.BlockSpec((1, 1, 2 * V_ROWS, lc), lambda b, j, i: (b, 0, j, 0))]
        args += [kc, vtc]
    def call(body, name):
        return pl.pallas_call(
            functools.partial(body, n_chunks=n_chunks, has_ctx=has_ctx, tq=tq),
            out_shape=jax.ShapeDtypeStruct((n, HEADS * VDIM), BF16),
            grid=(batch, HEADS // 2, q_tiles),
            in_specs=in_specs,
            out_specs=pl.BlockSpec((tq, 2 * VDIM), lambda b, j, i: (b * q_tiles + i, j)),
            compiler_params=_cparams(("arbitrary", "arbitrary", "arbitrary")),
            name=name,
        )(*args)

    return lax.cond(bound < ATTN_BOUND_MAX,
                    lambda: call(_attn_bounded_kernel, "attn_bounded"),
                    lambda: call(_attn_kernel, "attn_online"))


def _hy_filter_kernel(z_ref, w1_ref, b1_ref, w2_ref, b2_ref, w3_ref, b3_ref, fr_ref, dec_ref, k_ref, ss_ref):
    z = z_ref[...]
    fr = fr_ref[...]
    hdn = jnp.sin(fr * (jnp.dot(z, w1_ref[...], preferred_element_type=F32, precision=HIGHEST) + b1_ref[...]))
    hdn = jnp.sin(fr * (jnp.dot(hdn, w2_ref[...], preferred_element_type=F32, precision=HIGHEST) + b2_ref[...]))
    k = jnp.dot(hdn, w3_ref[...], preferred_element_type=F32, precision=HIGHEST) + b3_ref[...]
    k = k * jnp.exp(-z[:, 0:1] * jnp.abs(dec_ref[...]))
    k_ref[...] = k
    ss_ref[0] = jnp.sum(k * k, axis=0, keepdims=True)


def _hy_filters(z, lw, tr):
    seq, emb = z.shape
    nblk = seq // tr
    hid = lw['hy_w2'].shape[0]
    const = lambda i: (0, 0)
    return pl.pallas_call(
        _hy_filter_kernel,
        out_shape=(jax.ShapeDtypeStruct((seq, 2 * HY_W), F32), jax.ShapeDtypeStruct((nblk, 1, 2 * HY_W), F32)),
        grid=(nblk,),
        in_specs=[pl.BlockSpec((tr, emb), lambda i: (i, 0)),
                  pl.BlockSpec((emb, hid), const), pl.BlockSpec((1, hid), const),
                  pl.BlockSpec((hid, hid), const), pl.BlockSpec((1, hid), const),
                  pl.BlockSpec((hid, 2 * HY_W), const), pl.BlockSpec((1, 2 * HY_W), const),
                  pl.BlockSpec((1, hid), const), pl.BlockSpec((1, 2 * HY_W), const)],
        out_specs=(pl.BlockSpec((tr, 2 * HY_W), lambda i: (i, 0)),
                   pl.BlockSpec((1, 1, 2 * HY_W), lambda i: (i, 0, 0))),
        compiler_params=_cparams(("arbitrary",)), name="hy_filter",
    )(z, lw['hy_w1'], lw['hy_b1'], lw['hy_w2'], lw['hy_b2'], lw['hy_w3'], lw['hy_b3'], lw['hy_freq'],
      lw['hy_decay'])


def _hy_conv3_kernel(p_ref, prev_ref, next_ref, w_ref, b_ref, x0_ref, u_ref, *, tiles_per_seq, tr):
    i = pl.program_id(0)
    p = p_ref[...].astype(F32)
    first = (i % tiles_per_seq) == 0
    last = (i % tiles_per_seq) == tiles_per_seq - 1
    prev_row = jnp.where(first, 0.0, prev_ref[...].astype(F32)[15:16, :])
    next_row = jnp.where(last, 0.0, next_ref[...].astype(F32)[0:1, :])
    row = lax.broadcasted_iota(jnp.int32, p.shape, 0)
    p_prev = jnp.where(row == 0, prev_row, pltpu.roll(p, 1, 0))
    p_next = jnp.where(row == tr - 1, next_row, pltpu.roll(p, tr - 1, 0))
    z = b_ref[...] + p_prev * w_ref[0:1, :] + p * w_ref[1:2, :] + p_next * w_ref[2:3, :]
    x0_ref[...] = z[:, :HY_W].astype(BF16)
    u_ref[...] = (z[:, 2 * HY_W:] * z[:, HY_W:2 * HY_W]).astype(BF16)


def _hy_conv3(phy, conv_w, conv_b, seq, tr):
    n = phy.shape[0]
    tiles_per_seq = seq // tr
    hb = tr // 16
    nhb = n // 16
    return pl.pallas_call(
        functools.partial(_hy_conv3_kernel, tiles_per_seq=tiles_per_seq, tr=tr),
        out_shape=(jax.ShapeDtypeStruct((n, HY_W), BF16), jax.ShapeDtypeStruct((n, HY_W), BF16)),
        grid=(n // tr,),
        in_specs=[pl.BlockSpec((tr, 3 * HY_W), lambda i: (i, 0)),
                  pl.BlockSpec((16, 3 * HY_W), lambda i: (jnp.maximum(i * hb - 1, 0), 0)),
                  pl.BlockSpec((16, 3 * HY_W), lambda i: (jnp.minimum((i + 1) * hb, nhb - 1), 0)),
                  pl.BlockSpec((3, 3 * HY_W), lambda i: (0, 0)),
                  pl.BlockSpec((1, 3 * HY_W), lambda i: (0, 0))],
        out_specs=(pl.BlockSpec((tr, HY_W), lambda i: (i, 0)), pl.BlockSpec((tr, HY_W), lambda i: (i, 0))),
        compiler_params=_cparams(("arbitrary",)), name="hy_conv3",
    )(phy, phy, phy, conv_w, conv_b)


def _dft_outer_kernel(wr_ref, wi_ref, u_ref, ar_ref, ai_ref):
    u = u_ref[0]
    ar_ref[0] = jnp.dot(wr_ref[...], u, preferred_element_type=F32).astype(BF16)
    ai_ref[0] = jnp.dot(wi_ref[...], u, preferred_element_type=F32).astype(BF16)


def _dft_outer(wr, wi, u, tc):
    nb, kk, cols = u.shape
    n1 = wr.shape[0]
    return pl.pallas_call(
        _dft_outer_kernel,
        out_shape=(jax.ShapeDtypeStruct((nb, n1, cols), BF16),) * 2,
        grid=(nb, cols // tc),
        in_specs=[pl.BlockSpec((n1, kk), lambda b, j: (0, 0)), pl.BlockSpec((n1, kk), lambda b, j: (0, 0)),
                  pl.BlockSpec((1, kk, tc), lambda b, j: (b, 0, j))],
        out_specs=(pl.BlockSpec((1, n1, tc), lambda b, j: (b, 0, j)),) * 2,
        compiler_params=_cparams(("arbitrary", "arbitrary")), name="dft_outer",
    )(wr, wi, u)


def _cdot(mr, mi, xr, xi):
    rr = jnp.dot(mr, xr, preferred_element_type=F32) - jnp.dot(mi, xi, preferred_element_type=F32)
    ri = jnp.dot(mr, xi, preferred_element_type=F32) + jnp.dot(mi, xr, preferred_element_type=F32)
    return rr, ri


def _dft_inner_fwd_kernel(mr_ref, mi_ref, ar_ref, ai_ref, fr_ref, fi_ref):
    fr, fi = _cdot(mr_ref[0], mi_ref[0], ar_ref[0, 0], ai_ref[0, 0])
    fr_ref[0] = fr.astype(BF16)
    fi_ref[0] = fi.astype(BF16)


def _dft_inner_fwd(mr, mi, ar, ai):
    n1, n2, _ = mr.shape
    ch = ar.shape[-1]
    mspec = pl.BlockSpec((1, n2, n2), lambda k: (k, 0, 0))
    aspec = pl.BlockSpec((1, 1, n2, ch), lambda k: (0, k, 0, 0))
    ospec = pl.BlockSpec((1, n2, ch), lambda k: (k, 0, 0))
    return pl.pallas_call(
        _dft_inner_fwd_kernel,
        out_shape=(jax.ShapeDtypeStruct((n1, n2, ch), BF16),) * 2,
        grid=(n1,), in_specs=[mspec, mspec, aspec, aspec], out_specs=(ospec, ospec),
        compiler_params=_cparams(("arbitrary",)), name="dft_inner_fwd",
    )(mr, mi, ar, ai)


def _dft_inner_conv_kernel(mr_ref, mi_ref, tr_ref, ti_ref, ar_ref, ai_ref, fr_ref, fi_ref, br_ref, bi_ref, *, nb):
    fr = fr_ref[0].astype(F32)
    fi = fi_ref[0].astype(F32)
    for b in range(nb):
        ur, ui = _cdot(mr_ref[0], mi_ref[0], ar_ref[b, 0], ai_ref[b, 0])
        yr = (ur * fr - ui * fi).astype(BF16)
        yi = (ur * fi + ui * fr).astype(BF16)
        br, bi = _cdot(tr_ref[0], ti_ref[0], yr, yi)
        br_ref[b, 0] = br.astype(BF16)
        bi_ref[b, 0] = bi.astype(BF16)


def _dft_inner_conv(mr, mi, tr, ti, ar, ai, fr, fi):
    n1, n2, _ = mr.shape
    nb, _, _, ch = ar.shape
    mspec = pl.BlockSpec((1, n2, n2), lambda k: (k, 0, 0))
    aspec = pl.BlockSpec((nb, 1, n2, ch), lambda k: (0, k, 0, 0))
    fspec = pl.BlockSpec((1, n2, ch), lambda k: (k, 0, 0))
    return pl.pallas_call(
        functools.partial(_dft_inner_conv_kernel, nb=nb),
        out_shape=(jax.ShapeDtypeStruct((nb, n1, n2, ch), BF16),) * 2,
        grid=(n1,), in_specs=[mspec, mspec, mspec, mspec, aspec, aspec, fspec, fspec],
        out_specs=(aspec, aspec),
        compiler_params=_cparams(("arbitrary",)), name="dft_inner_conv",
    )(mr, mi, tr, ti, ar, ai, fr, fi)


def _filter_scale(ss_ref):
    ss = jnp.sum(ss_ref[...], axis=0)
    return lax.rsqrt(ss[:, :HY_W] + ss[:, HY_W:] + EPS)


def _hy_final_kernel(cr_ref, ci_ref, br_ref, bi_ref, x0_ref, u_ref, ss_ref, bias_ref, o_ref, *, reps):
    y = (jnp.dot(cr_ref[...], br_ref[0], preferred_element_type=F32)
         + jnp.dot(ci_ref[...], bi_ref[0], preferred_element_type=F32))
    scale = jnp.tile(_filter_scale(ss_ref), (1, reps))
    bias = jnp.tile(bias_ref[...], (1, reps))
    u = u_ref[0].astype(F32)
    o_ref[0] = (x0_ref[0].astype(F32) * (y * scale + u * bias)).astype(BF16)


def _hy_final(cr, ci, br, bi, x0, u, ss, bias, tc):
    nb, n1, cols = br.shape
    n1h = cr.shape[0]
    nblk = ss.shape[0]
    cspec = pl.BlockSpec((n1h, n1), lambda b, j: (0, 0))
    bspec = pl.BlockSpec((1, n1, tc), lambda b, j: (b, 0, j))
    xspec = pl.BlockSpec((1, n1h, tc), lambda b, j: (b, 0, j))
    return pl.pallas_call(
        functools.partial(_hy_final_kernel, reps=tc // HY_W),
        out_shape=jax.ShapeDtypeStruct((nb, n1h, cols), BF16),
        grid=(nb, cols // tc),
        in_specs=[cspec, cspec, bspec, bspec, xspec, xspec,
                  pl.BlockSpec((nblk, 1, 2 * HY_W), lambda b, j: (0, 0, 0)),
                  pl.BlockSpec((1, HY_W), lambda b, j: (0, 0))],
        out_specs=xspec,
        compiler_params=_cparams(("arbitrary", "arbitrary")), name="hy_final",
    )(cr, ci, br, bi, x0, u, ss, bias)


def _hy_direct_kernel(kk_ref, x0_ref, u_ref, ss_ref, bias_ref, o_ref, *, seq):
    u = u_ref[0].astype(F32)
    y = jnp.zeros((seq, HY_W), F32)
    for j in range(seq):
        y = y + kk_ref[seq - j:2 * seq - j, :] * u[j:j + 1, :]
    o_ref[0] = (x0_ref[0].astype(F32) * (y * _filter_scale(ss_ref) + u * bias_ref[...])).astype(BF16)


def _hy_direct(kk, x0, u, ss, bias):
    nb, seq, _ = u.shape
    nblk = ss.shape[0]
    xspec = pl.BlockSpec((1, seq, HY_W), lambda b: (b, 0, 0))
    return pl.pallas_call(
        functools.partial(_hy_direct_kernel, seq=seq),
        out_shape=jax.ShapeDtypeStruct((nb, seq, HY_W), BF16),
        grid=(nb,),
        in_specs=[pl.BlockSpec((2 * seq, HY_W), lambda b: (0, 0)), xspec, xspec,
                  pl.BlockSpec((nblk, 1, 2 * HY_W), lambda b: (0, 0, 0)),
                  pl.BlockSpec((1, HY_W), lambda b: (0, 0))],
        out_specs=xspec,
        compiler_params=_cparams(("arbitrary",)), name="hy_direct",
    )(kk, x0, u, ss, bias)


def _hy_embedding(seq):
    t = jnp.arange(seq, dtype=F32)
    t_unit = t / max(seq - 1, 1)
    bands = jnp.linspace(1e-4, HY_BANDS - 1, HY_BANDS, dtype=F32)
    ang = (2 * jnp.pi / seq) * t[:, None] * bands[None, :]
    return jnp.concatenate([t_unit[:, None], jnp.cos(ang), -jnp.sin(ang)], axis=-1)


def _dft_tables(seq):
    n = 2 * seq
    n2 = DFT_N2
    n1 = n // n2
    a = jnp.arange(n1, dtype=jnp.int32)
    th1 = (2 * jnp.pi / n1) * ((a[:, None] * a[None, :]) % n1).astype(F32)
    w1r, w1i = jnp.cos(th1), -jnp.sin(th1)
    k1 = jnp.arange(n1, dtype=jnp.int32)[:, None, None]
    k2 = jnp.arange(n2, dtype=jnp.int32)[None, :, None]
    t2 = jnp.arange(n2, dtype=jnp.int32)[None, None, :]
    th = (2 * jnp.pi / n) * ((t2 * (k1 + n1 * k2)) % n).astype(F32)
    mr, mi = jnp.cos(th), -jnp.sin(th)
    return dict(
        w1r=w1r.astype(BF16), w1i=w1i.astype(BF16),
        mr=mr.astype(BF16), mi=mi.astype(BF16),
        tr=jnp.swapaxes(mr, 1, 2).astype(BF16), ti=jnp.swapaxes(-mi, 1, 2).astype(BF16),
        cr=(w1r[:n1 // 2] / n).astype(BF16), ci=(w1i[:n1 // 2] / n).astype(BF16))


def _hyena_long(phy, lw, filt, seq, tabs):
    n = phy.shape[0]
    nb = n // seq
    n2 = DFT_N2
    n1 = 2 * seq // n2
    kf, ss = filt
    x0, u = _hy_conv3(phy, lw['hy_conv_w'], lw['hy_conv_b'], seq, min(seq, 1024))
    f = jnp.concatenate([kf[:, :HY_W], jnp.zeros((1, HY_W), F32), kf[:0:-1, HY_W:]], axis=0).astype(BF16)
    tc = min(n2 * HY_W, 8192)
    far, fai = _dft_outer(tabs['w1r'], tabs['w1i'], f.reshape(1, n1, n2 * HY_W), tc)
    fr, fi = _dft_inner_fwd(tabs['mr'], tabs['mi'], far.reshape(1, n1, n2, HY_W), fai.reshape(1, n1, n2, HY_W))
    ar, ai = _dft_outer(tabs['w1r'][:, :n1 // 2], tabs['w1i'][:, :n1 // 2], u.reshape(nb, n1 // 2, n2 * HY_W), tc)
    br, bi = _dft_inner_conv(tabs['mr'], tabs['mi'], tabs['tr'], tabs['ti'],
                             ar.reshape(nb, n1, n2, HY_W), ai.reshape(nb, n1, n2, HY_W), fr, fi)
    out = _hy_final(tabs['cr'], tabs['ci'], br.reshape(nb, n1, n2 * HY_W), bi.reshape(nb, n1, n2 * HY_W),
                    x0.reshape(nb, n1 // 2, n2 * HY_W), u.reshape(nb, n1 // 2, n2 * HY_W), ss, lw['hy_bias'], tc)
    return out.reshape(n, HY_W)


def _hyena_short(phy, lw, filt, seq):
    n = phy.shape[0]
    nb = n // seq
    kf, ss = filt
    x0, u = _hy_conv3(phy, lw['hy_conv_w'], lw['hy_conv_b'], seq, seq)
    kk = jnp.concatenate([jnp.zeros((1, HY_W), F32), kf[:0:-1, HY_W:], kf[:, :HY_W]], axis=0)
    out = _hy_direct(kk, x0.reshape(nb, seq, HY_W), u.reshape(nb, seq, HY_W), ss, lw['hy_bias'])
    return out.reshape(n, HY_W)


def _pair_top2_sum(a, b, c, d):
    return jnp.maximum(jnp.maximum(jnp.maximum(a + b, a + c), jnp.maximum(a + d, b + c)),
                       jnp.maximum(b + d, c + d))


def _route(logits_t, rb):
    aff = jax.nn.sigmoid(logits_t)
    sel = aff + rb
    rows = [sel[e:e + 1, :] for e in range(N_EXPERTS)]
    affr = [aff[e:e + 1, :] for e in range(N_EXPERTS)]
    best, bidx = None, None
    for g in range(N_GROUPS):
        gs = _pair_top2_sum(*rows[g * EXP_PER_GROUP:(g + 1) * EXP_PER_GROUP])
        if g == 0:
            best, bidx = gs, jnp.zeros(gs.shape, jnp.int32)
        else:
            upd = gs > best
            bidx = jnp.where(upd, g, bidx)
            best = jnp.where(upd, gs, best)
    vals = [jnp.where(bidx == e // EXP_PER_GROUP, rows[e], -jnp.inf) for e in range(N_EXPERTS)]
    m1, i1 = vals[0], jnp.zeros(best.shape, jnp.int32)
    for e in range(1, N_EXPERTS):
        upd = vals[e] > m1
        i1 = jnp.where(upd, e, i1)
        m1 = jnp.where(upd, vals[e], m1)
    m2, i2 = jnp.full(best.shape, -jnp.inf, F32), jnp.zeros(best.shape, jnp.int32)
    for e in range(N_EXPERTS):
        cand = jnp.where(i1 == e, -jnp.inf, vals[e])
        upd = cand > m2
        i2 = jnp.where(upd, e, i2)
        m2 = jnp.where(upd, cand, m2)
    a1 = sum(jnp.where(i1 == e, affr[e], 0.0) for e in range(N_EXPERTS))
    a2 = sum(jnp.where(i2 == e, affr[e], 0.0) for e in range(N_EXPERTS))
    inv = 1.0 / (a1 + a2)
    return jnp.concatenate([i1, i2], axis=0), jnp.concatenate([a1 * inv, a2 * inv], axis=0)


def _merge_kernel(a_ref, b_ref, m_ref, gt_ref, x_ref, gate_ref, shift_ref, scale_ref, g2_ref,
                  wpa_ref, wpb_ref, wpc_ref, wout_ref, rwt_ref, rb_ref, xo_ref, h2_ref, idx_ref, wts_ref):
    y = gt_ref[:, 0:D_MODEL].astype(F32) * jnp.dot(a_ref[...], wpa_ref[...], preferred_element_type=F32)
    y = y + gt_ref[:, D_MODEL:2 * D_MODEL].astype(F32) * jnp.dot(b_ref[...], wpb_ref[...],
                                                                 preferred_element_type=F32)
    y = y + gt_ref[:, 2 * D_MODEL:].astype(F32) * jnp.dot(m_ref[...], wpc_ref[...], preferred_element_type=F32)
    y2 = jnp.dot(y.astype(BF16), wout_ref[...], preferred_element_type=F32)
    xn = x_ref[...] + gate_ref[0] * y2
    xo_ref[...] = xn
    h2 = _rms(xn) * g2_ref[...]
    h2 = h2 * (1.0 + scale_ref[0]) + shift_ref[0]
    h2_ref[...] = h2
    logits_t = lax.dot_general(rwt_ref[...], h2, (((1,), (1,)), ((), ())), preferred_element_type=F32,
                               precision=HIGHEST)
    idx_ref[...], wts_ref[...] = _route(logits_t, rb_ref[...])


def _merge(a, b, m, gt, x, gate, shift, scale, lw, rw_t, rb, seq, tm):
    n = x.shape[0]
    tiles_per_seq = seq // tm
    const = lambda i: (0, 0)
    row = lambda w: pl.BlockSpec((tm, w), lambda i: (i, 0))
    vec = pl.BlockSpec((1, 1, D_MODEL), lambda i: (i // tiles_per_seq, 0, 0))
    return pl.pallas_call(
        _merge_kernel,
        out_shape=(jax.ShapeDtypeStruct((n, D_MODEL), F32), jax.ShapeDtypeStruct((n, D_MODEL), F32),
                   jax.ShapeDtypeStruct((TOP_K, n), jnp.int32), jax.ShapeDtypeStruct((TOP_K, n), F32)),
        grid=(n // tm,),
        in_specs=[row(HEADS * VDIM), row(HY_W), row(GM_W), row(3 * D_MODEL), row(D_MODEL), vec, vec, vec,
                  pl.BlockSpec((1, D_MODEL), const),
                  pl.BlockSpec((HEADS * VDIM, D_MODEL), const), pl.BlockSpec((HY_W, D_MODEL), const),
                  pl.BlockSpec((GM_W, D_MODEL), const), pl.BlockSpec((D_MODEL, D_MODEL), const),
                  pl.BlockSpec((N_EXPERTS, D_MODEL), const), pl.BlockSpec((N_EXPERTS, 1), const)],
        out_specs=(row(D_MODEL), row(D_MODEL), pl.BlockSpec((TOP_K, tm), lambda i: (0, i)),
                   pl.BlockSpec((TOP_K, tm), lambda i: (0, i))),
        compiler_params=_cparams(("arbitrary",)), name="merge",
    )(a, b, m, gt, x, gate, shift, scale, lw['g2'], lw['w_pa'], lw['w_pb'], lw['w_pc'], lw['w_out'], rw_t, rb)


def _rows_copy(src_hbm, dst, sem, idx_ref, rows):
    return [pltpu.make_async_copy(src_hbm.at[pl.ds(idx_ref[0, 0, r], 1)], dst.at[pl.ds(r, 1)], sem)
            for r in range(rows)]


def _rows_wait(src_hbm, dst, sem, rows):
    pltpu.make_async_copy(src_hbm.at[pl.ds(0, rows)], dst, sem).wait()


def _gather_pipeline(src_hbm, cur_ref, nxt_ref, buf, sem, rows):
    i = pl.program_id(0)
    slot = i % 2

    @pl.when(i == 0)
    def _():
        for cp in _rows_copy(src_hbm, buf.at[0], sem.at[0], cur_ref, rows):
            cp.start()

    _rows_wait(src_hbm, buf.at[slot], sem.at[slot], rows)
    for cp in _rows_copy(src_hbm, buf.at[1 - slot], sem.at[1 - slot], nxt_ref, rows):
        cp.start()
    return slot


def _gather_drain(src_hbm, buf, sem, rows):
    i = pl.program_id(0)

    @pl.when(i == pl.num_programs(0) - 1)
    def _():
        _rows_wait(src_hbm, buf.at[1 - i % 2], sem.at[1 - i % 2], rows)


def _moe_ffn_kernel(be_ref, cur_ref, nxt_ref, h_hbm, wgu_ref, wd_ref, y_ref, buf, sem):
    del be_ref
    slot = _gather_pipeline(h_hbm, cur_ref, nxt_ref, buf, sem, MOE_BM)
    gu = jnp.dot(buf[slot].astype(BF16), wgu_ref[0], preferred_element_type=F32)
    g = gu[:, :D_EXPERT]
    hid = (g * jax.nn.sigmoid(g) * gu[:, D_EXPERT:]).astype(BF16)
    y_ref[...] = jnp.dot(hid, wd_ref[0], preferred_element_type=F32)
    _gather_drain(h_hbm, buf, sem, MOE_BM)


def _moe_ffn(blk_expert, tok_sorted, h2, lw):
    nblk = blk_expert.shape[0]
    idx_spec = lambda off: pl.BlockSpec((1, 1, MOE_BM), lambda i, be: (i + off, 0, 0), memory_space=pltpu.SMEM)
    return pl.pallas_call(
        _moe_ffn_kernel,
        out_shape=jax.ShapeDtypeStruct((nblk * MOE_BM, D_MODEL), F32),
        grid_spec=pltpu.PrefetchScalarGridSpec(
            num_scalar_prefetch=1, grid=(nblk,),
            in_specs=[idx_spec(0), idx_spec(1), pl.BlockSpec(memory_space=pl.ANY),
                      pl.BlockSpec((1, D_MODEL, 2 * D_EXPERT), lambda i, be: (be[i], 0, 0)),
                      pl.BlockSpec((1, D_EXPERT, D_MODEL), lambda i, be: (be[i], 0, 0))],
            out_specs=pl.BlockSpec((MOE_BM, D_MODEL), lambda i, be: (i, 0)),
            scratch_shapes=[pltpu.VMEM((2, MOE_BM, D_MODEL), F32), pltpu.SemaphoreType.DMA((2,))]),
        compiler_params=_cparams(("arbitrary",)), name="moe_ffn",
    )(blk_expert, tok_sorted, tok_sorted, h2, lw['w_gu'], lw['w_d'])


def _moe_combine_kernel(cur_ref, nxt_ref, y_hbm, x_ref, w_ref, gate_ref, o_ref, buf, sem, *, tc):
    slot = _gather_pipeline(y_hbm, cur_ref, nxt_ref, buf, sem, TOP_K * tc)
    w = w_ref[...]
    y = w[:, 0:1] * buf[slot, 0:tc] + w[:, 1:2] * buf[slot, tc:2 * tc]
    o_ref[...] = x_ref[...] + gate_ref[0] * y
    _gather_drain(y_hbm, buf, sem, TOP_K * tc)


def _moe_combine(pos_tiles, y_sorted, x, w_cols, gate, seq, tc):
    n = x.shape[0]
    tiles_per_seq = seq // tc
    idx_spec = lambda off: pl.BlockSpec((1, 1, TOP_K * tc), lambda i: (i + off, 0, 0), memory_space=pltpu.SMEM)
    return pl.pallas_call(
        functools.partial(_moe_combine_kernel, tc=tc),
        out_shape=jax.ShapeDtypeStruct((n, D_MODEL), F32),
        grid=(n // tc,),
        in_specs=[idx_spec(0), idx_spec(1), pl.BlockSpec(memory_space=pl.ANY),
                  pl.BlockSpec((tc, D_MODEL), lambda i: (i, 0)),
                  pl.BlockSpec((tc, TOP_K), lambda i: (i, 0)),
                  pl.BlockSpec((1, 1, D_MODEL), lambda i: (i // tiles_per_seq, 0, 0))],
        out_specs=pl.BlockSpec((tc, D_MODEL), lambda i: (i, 0)),
        scratch_shapes=[pltpu.VMEM((2, TOP_K * tc, D_MODEL), F32), pltpu.SemaphoreType.DMA((2,))],
        compiler_params=_cparams(("arbitrary",)), name="moe_combine",
    )(pos_tiles, pos_tiles, y_sorted, x, w_cols, gate)


def _moe(h2, idx, wts, x, gate, lw, seq):
    n = x.shape[0]
    n_slots = TOP_K * n
    nblk = n_slots // MOE_BM + N_EXPERTS
    e_flat = idx.reshape(n_slots)
    onehot = (e_flat[:, None] == jnp.arange(N_EXPERTS, dtype=jnp.int32)[None, :]).astype(jnp.int32)
    csum = jnp.cumsum(onehot, axis=0)
    counts = csum[-1]
    rank = jnp.sum(csum * onehot, axis=1) - 1
    seg_len = (counts + MOE_BM - 1) // MOE_BM * MOE_BM
    seg_end = jnp.cumsum(seg_len)
    pos = jnp.sum(onehot * (seg_end - seg_len)[None, :], axis=1) + rank
    tok = jnp.arange(n_slots, dtype=jnp.int32) % n
    tok_sorted = jnp.zeros(((nblk + 1) * MOE_BM,), jnp.int32).at[pos].set(tok)
    blk_expert = jnp.minimum(
        jnp.searchsorted(seg_end, jnp.arange(nblk, dtype=jnp.int32) * MOE_BM, side='right'),
        N_EXPERTS - 1).astype(jnp.int32)
    y_sorted = _moe_ffn(blk_expert, tok_sorted.reshape(nblk + 1, 1, MOE_BM), h2, lw)
    tc = min(seq, MOE_TC)
    pos_tiles = pos.reshape(TOP_K, n // tc, tc).transpose(1, 0, 2).reshape(n // tc, 1, TOP_K * tc)
    pos_tiles = jnp.concatenate([pos_tiles, jnp.zeros((1, 1, TOP_K * tc), jnp.int32)], axis=0)
    return _moe_combine(pos_tiles, y_sorted, x, wts.T, gate, seq, tc)


def _head_perm():
    rope_idx = list(range(NOPE, QK, 2)) + list(range(NOPE + 1, QK, 2))
    return jnp.array(list(range(NOPE)) + rope_idx, dtype=jnp.int32)


def _layer_weights(p, l):
    perm = _head_perm()
    row = lambda v: v.reshape(1, -1)
    w_in = p['w_in'][l]
    kr_perm = jnp.array(list(range(0, ROPE, 2)) + list(range(1, ROPE, 2)), dtype=jnp.int32)
    w_kr = jnp.pad(w_in[:, OFF_KR:OFF_HY][:, kr_perm], ((0, 0), (0, HEAD_PAD - ROPE)))
    w_in2 = jnp.concatenate([w_in[:, OFF_Q:OFF_KR], w_kr, w_in[:, OFF_HY:]], axis=1).astype(BF16)
    w_uq = p['w_uq'][l].reshape(Q_RANK, HEADS, QK)[:, :, perm]
    w_uq = jnp.pad(w_uq, ((0, 0), (0, 0), (0, HEAD_PAD - QK))).reshape(Q_RANK, HEADS * HEAD_PAD).astype(BF16)
    w_ukv = p['w_ukv'][l].reshape(KV_RANK, HEADS, NOPE + VDIM)
    w_k = jnp.pad(w_ukv[:, :, :NOPE], ((0, 0), (0, 0), (0, HEAD_PAD - NOPE)))
    w_k = w_k.reshape(KV_RANK, HEADS * HEAD_PAD).astype(BF16)
    w_vt = w_ukv[:, :, NOPE:].reshape(KV_RANK, HEADS * VDIM).T.astype(BF16)
    pad_gain = lambda g: jnp.pad(g[perm], (0, HEAD_PAD - QK)).reshape(1, HEAD_PAD)
    gm_b = jnp.repeat(p['gm_bs'][l].T, GM_W // GM_GROUPS, axis=1)
    bound = 1.02 * QSCALE * QK * jnp.max(jnp.abs(p['g_qn'][l])) * jnp.max(jnp.abs(p['g_kn'][l]))
    pad_lane = jnp.arange(HEAD_PAD) == QK
    return dict(
        attn_bound=bound,
        q_pad=jnp.where(pad_lane, -bound, 0.0).reshape(1, HEAD_PAD).astype(F32),
        k_pad=jnp.where(pad_lane, 1.0, 0.0).reshape(1, HEAD_PAD).astype(F32),
        g1=row(p['norm1_g'][l]), g2=row(p['norm2_g'][l]), w_in=w_in2,
        g_qa=row(p['g_qa'][l]), w_uq=w_uq, g_qn=pad_gain(p['g_qn'][l]),
        g_kva=row(p['g_kva'][l]), w_k=w_k, w_vt=w_vt, g_kn=pad_gain(p['g_kn'][l]),
        gm_g=row(p['gm_norm_g'][l]), gm_w=p['gm_ws'][l].astype(BF16), gm_b=gm_b,
        hy_conv_w=p['hy_conv_w'][l], hy_conv_b=row(p['hy_conv_b'][l]),
        hy_w1=p['hy_w1'][l], hy_b1=row(p['hy_b1'][l]), hy_w2=p['hy_w2'][l], hy_b2=row(p['hy_b2'][l]),
        hy_w3=p['hy_w3'][l], hy_b3=row(p['hy_b3'][l]), hy_freq=row(p['hy_freq'][l]),
        hy_decay=p['hy_decay'][l].reshape(1, 2 * HY_W), hy_bias=row(p['hy_bias'][l]),
        w_pa=p['w_pa'][l].astype(BF16), w_pb=p['w_pb'][l].astype(BF16), w_pc=p['w_pc'][l].astype(BF16),
        w_out=p['w_out'][l].astype(BF16),
        w_gu=jnp.concatenate([p['moe_w_gate'][l], p['moe_w_up'][l]], axis=-1).astype(BF16),
        w_d=p['moe_w_down'][l].astype(BF16))


def _rope_tables(seq):
    rows = seq // GRID_W
    row = jnp.repeat(jnp.arange(rows, dtype=F32), GRID_W)
    col = jnp.tile(jnp.arange(GRID_W, dtype=F32), rows)
    n_freq = ROPE // 4
    inv = ROPE_THETA ** (-jnp.arange(n_freq, dtype=F32) / n_freq)
    ang = jnp.concatenate([row[:, None] * inv, col[:, None] * inv], axis=-1)
    c, s = jnp.cos(ang), jnp.sin(ang)
    z = lambda w: jnp.zeros((seq, w), F32)
    rc = jnp.concatenate([jnp.ones((seq, NOPE), F32), c, c, z(HEAD_PAD - QK)], axis=1)
    rs1 = jnp.concatenate([z(NOPE), z(ROPE // 2), s, z(HEAD_PAD - QK)], axis=1)
    rs2 = jnp.concatenate([z(NOPE), -s, z(ROPE // 2), z(HEAD_PAD - QK)], axis=1)
    return rc, rs1, rs2


def _mixer_and_ffn(x, mods, lw, rw_t, rb, seq, tm, rope_tabs, ctx_kv, filt, dft_tabs):
    shift1, scale1, gate1, shift2, scale2, gate2 = mods
    q, k, vt, m, phy, gt = _premix(x, shift1, scale1, lw, rope_tabs, seq, tm)
    a = _attention(q, k, vt, ctx_kv, seq, min(seq, ATTN_TQ), lw['attn_bound'])
    if dft_tabs is not None:
        b = _hyena_long(phy, lw, filt, seq, dft_tabs)
    else:
        b = _hyena_short(phy, lw, filt, seq)
    xn, h2, idx, wts = _merge(a, b, m, gt, x, gate1, shift2, scale2, lw, rw_t, rb, seq, tm)
    return _moe(h2, idx, wts, xn, gate2, lw, seq), k, vt


def _forward(p):
    x, ctx = p['x'], p['ctx']
    batch, seq, _ = x.shape
    lc = ctx.shape[1]
    depth = p['w_mod'].shape[0]

    cvecs = jnp.concatenate([p['c'], p['c_ctx'][None], jnp.zeros((8 - batch - 1, D_MODEL), F32)], axis=0)
    mod_all = _modvec(cvecs, p['w_mod'], p['b_mod'])
    rw_t = p['router_w'].T
    rb = p['router_b'].reshape(N_EXPERTS, 1)
    rope_tabs = _rope_tables(seq)
    dft_tabs = _dft_tables(seq)
    z_lat, z_ctx = _hy_embedding(seq), _hy_embedding(lc)

    xl = x.reshape(batch * seq, D_MODEL)
    xc = ctx.reshape(batch * lc, D_MODEL)
    tm_lat = min(seq, 512)
    for l in range(depth):
        lw = _layer_weights(p, l)
        mod = mod_all[l].reshape(8, N_MOD, D_MODEL)
        mods_lat = [mod[:batch, j].reshape(batch, 1, D_MODEL) for j in range(N_MOD)]
        mods_ctx = [jnp.broadcast_to(mod[batch, j].reshape(1, 1, D_MODEL), (batch, 1, D_MODEL))
                    for j in range(N_MOD)]
        if l == depth - 1:
            _, k_c, vt_c, _, _, _ = _premix(xc, mods_ctx[0], mods_ctx[1], lw, None, lc, lc)
        else:
            xc, k_c, vt_c = _mixer_and_ffn(xc, mods_ctx, lw, rw_t, rb, lc, lc, None, None,
                                           _hy_filters(z_ctx, lw, lc), None)
        filt = _hy_filters(z_lat, lw, min(seq, 2048))
        xl, _, _ = _mixer_and_ffn(xl, mods_lat, lw, rw_t, rb, seq, tm_lat, rope_tabs, (k_c, vt_c), filt, dft_tabs)
    return xl.reshape(batch, seq, D_MODEL)


def kernel(x, c, ctx, c_ctx, w_mod, b_mod, norm1_g, norm2_g, w_in, g_qa, w_uq, g_kva, w_ukv, g_qn, g_kn,
           hy_conv_w, hy_conv_b, hy_w1, hy_b1, hy_w2, hy_b2, hy_w3, hy_b3, hy_freq, hy_decay, hy_bias,
           gm_norm_g, gm_ws, gm_bs, w_pa, w_pb, w_pc, w_out, router_w, router_b,
           moe_w_gate, moe_w_up, moe_w_down):
    return _forward(dict(
        x=x, c=c, ctx=ctx, c_ctx=c_ctx, w_mod=w_mod, b_mod=b_mod, norm1_g=norm1_g, norm2_g=norm2_g, w_in=w_in,
        g_qa=g_qa, w_uq=w_uq, g_kva=g_kva, w_ukv=w_ukv, g_qn=g_qn, g_kn=g_kn, hy_conv_w=hy_conv_w,
        hy_conv_b=hy_conv_b, hy_w1=hy_w1, hy_b1=hy_b1, hy_w2=hy_w2, hy_b2=hy_b2, hy_w3=hy_w3, hy_b3=hy_b3,
        hy_freq=hy_freq, hy_decay=hy_decay, hy_bias=hy_bias, gm_norm_g=gm_norm_g, gm_ws=gm_ws, gm_bs=gm_bs,
        w_pa=w_pa, w_pb=w_pb, w_pc=w_pc, w_out=w_out, router_w=router_w, router_b=router_b,
        moe_w_gate=moe_w_gate, moe_w_up=moe_w_up, moe_w_down=moe_w_down))
```

```python
import functools
import math

import jax
import jax.numpy as jnp
from jax import lax
from jax.experimental import pallas as pl
from jax.experimental.pallas import tpu as pltpu

F32 = jnp.float32
BF16 = jnp.bfloat16
HIGHEST = lax.Precision.HIGHEST

D_MODEL = 1024
GRID_W = 64
EPS = 1e-6
N_MOD = 6

HEADS = 8
Q_RANK = 384
KV_RANK = 256
NOPE = 64
ROPE = 32
QK = NOPE + ROPE
VDIM = 64
HEAD_PAD = 128
ROPE_THETA = 10000.0
V_ROWS = 80

HY_W = 256
HY_BANDS = 16
GM_W = 256
GM_CHUNK = 128
GM_GROUPS = 4

OFF_Q = 0
OFF_KV = OFF_Q + Q_RANK
OFF_KR = OFF_KV + KV_RANK
OFF_HY = OFF_KR + ROPE
OFF_GM = OFF_HY + 3 * HY_W
OFF_GT = OFF_GM + 2 * GM_W

P_Q = 0
P_KV = P_Q + Q_RANK
P_KR = P_KV + KV_RANK
P_HY = P_KR + HEAD_PAD
P_GM = P_HY + 3 * HY_W
P_GT = P_GM + 2 * GM_W
P_W = P_GT + 3 * D_MODEL

N_EXPERTS = 16
N_GROUPS = 4
EXP_PER_GROUP = 4
TOP_K = 2
D_EXPERT = 512
MOE_BM = 256
MOE_TC = 256

DFT_N2 = 256

VMEM_LIMIT = 56 * 1024 * 1024
ATTN_TQ = 512
ATTN_TK = 2048
ATTN_BOUND_MAX = 50.0
NEG_BIG = -1e30
LOG2E = 1.4426950408889634
QSCALE = QK ** -0.5 * LOG2E


def _cparams(sem):
    return pltpu.CompilerParams(dimension_semantics=sem, vmem_limit_bytes=VMEM_LIMIT)


def _rms(x):
    return x * lax.rsqrt(jnp.mean(x * x, axis=-1, keepdims=True) + EPS)


def _nt_dot(a, b):
    return lax.dot_general(a, b, (((1,), (1,)), ((), ())), preferred_element_type=F32)


def _modvec_kernel(c_ref, w_ref, b_ref, o_ref):
    cv = c_ref[...]
    s = cv * jax.nn.sigmoid(cv)
    o_ref[0] = jnp.dot(s, w_ref[0], preferred_element_type=F32, precision=HIGHEST) + b_ref[0]


def _modvec(cvecs, w_mod, b_mod):
    depth = w_mod.shape[0]
    tn = 1536
    return pl.pallas_call(
        _modvec_kernel,
        out_shape=jax.ShapeDtypeStruct((depth, 8, N_MOD * D_MODEL), F32),
        grid=(depth, N_MOD * D_MODEL // tn),
        in_specs=[pl.BlockSpec((8, D_MODEL), lambda l, j: (0, 0)),
                  pl.BlockSpec((1, D_MODEL, tn), lambda l, j: (l, 0, j)),
                  pl.BlockSpec((1, 1, tn), lambda l, j: (l, 0, j))],
        out_specs=pl.BlockSpec((1, 8, tn), lambda l, j: (l, 0, j)),
        compiler_params=_cparams(("arbitrary", "arbitrary")),
        name="modvec",
    )(cvecs, w_mod, b_mod.reshape(depth, 1, N_MOD * D_MODEL))


def _head_norm_rope(xh, gain, rope):
    ms = jnp.sum(xh * xh, axis=-1, keepdims=True) * (1.0 / QK)
    xh = xh * lax.rsqrt(ms + EPS) * gain
    if rope is not None:
        rc, rs1, rs2 = rope
        xh = xh * rc + pltpu.roll(xh, ROPE // 2, 1) * rs1 + pltpu.roll(xh, HEAD_PAD - ROPE // 2, 1) * rs2
    return xh


def _premix_kernel(*refs, use_rope, tm):
    if use_rope:
        (x_ref, shift_ref, scale_ref, g1_ref, win_ref, gqa_ref, wuq_ref, gqn_ref, gkva_ref, wk_ref, wvt_ref,
         gkn_ref, gmg_ref, gmw_ref, gmb_ref, qpad_ref, kpad_ref, rc_ref, rs1_ref, rs2_ref,
         q_ref, k_ref, vt_ref, m_ref, hy_ref, gt_ref) = refs
        rope = (rc_ref[...], rs1_ref[...], rs2_ref[...])
    else:
        (x_ref, shift_ref, scale_ref, g1_ref, win_ref, gqa_ref, wuq_ref, gqn_ref, gkva_ref, wk_ref, wvt_ref,
         gkn_ref, gmg_ref, gmw_ref, gmb_ref, qpad_ref, kpad_ref,
         q_ref, k_ref, vt_ref, m_ref, hy_ref, gt_ref) = refs
        rope = None

    x = x_ref[...]
    h = _rms(x) * g1_ref[...]
    h = h * (1.0 + scale_ref[0]) + shift_ref[0]
    hb = h.astype(BF16)

    def proj(lo, width):
        return jnp.dot(hb, win_ref[:, lo:lo + width], preferred_element_type=F32)

    qa = (_rms(proj(P_Q, Q_RANK)) * gqa_ref[...]).astype(BF16)
    q = jnp.dot(qa, wuq_ref[...], preferred_element_type=F32)
    for hh in range(HEADS):
        qh = _head_norm_rope(q[:, hh * HEAD_PAD:(hh + 1) * HEAD_PAD], gqn_ref[...], rope)
        q_ref[:, hh * HEAD_PAD:(hh + 1) * HEAD_PAD] = (qh * QSCALE + qpad_ref[...]).astype(BF16)

    kva = (_rms(proj(P_KV, KV_RANK)) * gkva_ref[...]).astype(BF16)
    kr = pltpu.roll(proj(P_KR, HEAD_PAD), NOPE, 1)
    kn = jnp.dot(kva, wk_ref[...], preferred_element_type=F32)
    for hh in range(HEADS):
        kh = _head_norm_rope(kn[:, hh * HEAD_PAD:(hh + 1) * HEAD_PAD] + kr, gkn_ref[...], rope)
        k_ref[:, hh * HEAD_PAD:(hh + 1) * HEAD_PAD] = (kh + kpad_ref[...]).astype(BF16)
    vt = _nt_dot(wvt_ref[...], kva)
    row = lax.broadcasted_iota(jnp.int32, (V_ROWS - VDIM, tm), 0)
    ones_rows = jnp.where(row == 0, 1.0, 0.0).astype(BF16)
    for hh in range(HEADS):
        vt_ref[0, 0, hh * V_ROWS:hh * V_ROWS + VDIM, :] = vt[hh * VDIM:(hh + 1) * VDIM].astype(BF16)
        vt_ref[0, 0, hh * V_ROWS + VDIM:(hh + 1) * V_ROWS, :] = ones_rows

    gg = jax.nn.gelu(proj(P_GM, 2 * GM_W), approximate=True)
    gu = gg[:, :GM_W]
    gv = (_rms(gg[:, GM_W:]) * gmg_ref[...]).astype(BF16)
    grp = lax.broadcasted_iota(jnp.int32, (GM_CHUNK, GM_W), 1) // (GM_W // GM_GROUPS)
    for ci in range(tm // GM_CHUNK):
        vc = gv[ci * GM_CHUNK:(ci + 1) * GM_CHUNK]
        s = jnp.zeros((GM_CHUNK, GM_W), F32)
        for g in range(GM_GROUPS):
            sg = jnp.dot(gmw_ref[g], vc, preferred_element_type=F32)
            s = jnp.where(grp == g, sg, s)
        m_ref[ci * GM_CHUNK:(ci + 1) * GM_CHUNK, :] = (
            gu[ci * GM_CHUNK:(ci + 1) * GM_CHUNK] * (s + gmb_ref[...])).astype(BF16)

    hy_ref[...] = proj(P_HY, 3 * HY_W).astype(BF16)
    for j in range(3):
        gt_ref[:, j * D_MODEL:(j + 1) * D_MODEL] = jax.nn.sigmoid(
            proj(P_GT + j * D_MODEL, D_MODEL)).astype(BF16)


def _premix(x, shift, scale, lw, rope_tabs, seq, tm):
    n = x.shape[0]
    tiles_per_seq = seq // tm
    batch = n // seq
    use_rope = rope_tabs is not None
    const = lambda i: (0, 0)
    in_specs = [
        pl.BlockSpec((tm, D_MODEL), lambda i: (i, 0)),
        pl.BlockSpec((1, 1, D_MODEL), lambda i: (i // tiles_per_seq, 0, 0)),
        pl.BlockSpec((1, 1, D_MODEL), lambda i: (i // tiles_per_seq, 0, 0)),
        pl.BlockSpec((1, D_MODEL), const),
        pl.BlockSpec((D_MODEL, P_W), const, pipeline_mode=pl.Buffered(1)),
        pl.BlockSpec((1, Q_RANK), const),
        pl.BlockSpec((Q_RANK, HEADS * HEAD_PAD), const),
        pl.BlockSpec((1, HEAD_PAD), const),
        pl.BlockSpec((1, KV_RANK), const),
        pl.BlockSpec((KV_RANK, HEADS * HEAD_PAD), const),
        pl.BlockSpec((HEADS * VDIM, KV_RANK), const),
        pl.BlockSpec((1, HEAD_PAD), const),
        pl.BlockSpec((1, GM_W), const),
        pl.BlockSpec((GM_GROUPS, GM_CHUNK, GM_CHUNK), lambda i: (0, 0, 0)),
        pl.BlockSpec((GM_CHUNK, GM_W), const),
        pl.BlockSpec((1, HEAD_PAD), const),
        pl.BlockSpec((1, HEAD_PAD), const),
    ]
    args = [x, shift, scale, lw['g1'], lw['w_in'], lw['g_qa'], lw['w_uq'], lw['g_qn'], lw['g_kva'], lw['w_k'],
            lw['w_vt'], lw['g_kn'], lw['gm_g'], lw['gm_w'], lw['gm_b'], lw['q_pad'], lw['k_pad']]
    tkv = min(seq, ATTN_TK)
    sub = tkv // tm
    if use_rope:
        in_specs += [pl.BlockSpec((tm, HEAD_PAD), lambda i: (i % tiles_per_seq, 0))] * 3
        args += list(rope_tabs)
    out_shape = (
        jax.ShapeDtypeStruct((n, HEADS * HEAD_PAD), BF16),
        jax.ShapeDtypeStruct((n, HEADS * HEAD_PAD), BF16),
        jax.ShapeDtypeStruct((batch, seq // tkv, HEADS * V_ROWS, tkv), BF16),
        jax.ShapeDtypeStruct((n, GM_W), BF16),
        jax.ShapeDtypeStruct((n, 3 * HY_W), BF16),
        jax.ShapeDtypeStruct((n, 3 * D_MODEL), BF16),
    )
    out_specs = (
        pl.BlockSpec((tm, HEADS * HEAD_PAD), lambda i: (i, 0)),
        pl.BlockSpec((tm, HEADS * HEAD_PAD), lambda i: (i, 0)),
        pl.BlockSpec((1, 1, HEADS * V_ROWS, tm),
                     lambda i: (i // tiles_per_seq, (i % tiles_per_seq) // sub, 0, (i % tiles_per_seq) % sub)),
        pl.BlockSpec((tm, GM_W), lambda i: (i, 0)),
        pl.BlockSpec((tm, 3 * HY_W), lambda i: (i, 0)),
        pl.BlockSpec((tm, 3 * D_MODEL), lambda i: (i, 0)),
    )
    return pl.pallas_call(
        functools.partial(_premix_kernel, use_rope=use_rope, tm=tm),
        out_shape=out_shape, grid=(n // tm,), in_specs=in_specs, out_specs=out_specs,
        compiler_params=_cparams(("arbitrary",)), name="premix",
    )(*args)


def _attn_kernel(*refs, n_chunks, has_ctx, tq):
    if has_ctx:
        q_ref, k_ref, vt_ref, kc_ref, vtc_ref, o_ref = refs
    else:
        q_ref, k_ref, vt_ref, o_ref = refs
    tk = vt_ref.shape[-1]

    def step(hh, kc, vtc, m, acc):
        s = _nt_dot(kc, q_ref[:, hh * HEAD_PAD:(hh + 1) * HEAD_PAD])
        m_new = jnp.maximum(m, jnp.max(s, axis=0, keepdims=True))
        p = jnp.exp2(s - m_new).astype(BF16)
        alpha = jnp.exp2(m - m_new)
        return m_new, acc * alpha + jnp.dot(vtc, p, preferred_element_type=F32)

    def body(i, carry):
        out = []
        for hh in range(2):
            m, acc = carry[hh]
            start = pl.multiple_of(i * tk, tk)
            kc = k_ref[pl.ds(start, tk), hh * HEAD_PAD:(hh + 1) * HEAD_PAD]
            vtc = vt_ref[0, i, hh * V_ROWS:(hh + 1) * V_ROWS, :]
            out.append(step(hh, kc, vtc, m, acc))
        return tuple(out)

    init = tuple((jnp.full((1, tq), NEG_BIG, F32), jnp.zeros((V_ROWS, tq), F32)) for _ in range(2))
    carry = lax.fori_loop(0, n_chunks, body, init)
    outs = []
    for hh in range(2):
        m, acc = carry[hh]
        if has_ctx:
            m, acc = step(hh, kc_ref[:, hh * HEAD_PAD:(hh + 1) * HEAD_PAD],
                          vtc_ref[0, 0, hh * V_ROWS:(hh + 1) * V_ROWS, :], m, acc)
        outs.append(acc[:VDIM] / acc[VDIM:VDIM + 1])
    o_ref[...] = jnp.concatenate(outs, axis=0).T.astype(BF16)


def _attn_bounded_kernel(*refs, n_chunks, has_ctx, tq):
    if has_ctx:
        q_ref, k_ref, vt_ref, kc_ref, vtc_ref, o_ref = refs
    else:
        q_ref, k_ref, vt_ref, o_ref = refs
    tk = vt_ref.shape[-1]

    def step(hh, kc, vtc, acc):
        s = _nt_dot(kc, q_ref[:, hh * HEAD_PAD:(hh + 1) * HEAD_PAD])
        return acc + jnp.dot(vtc, jnp.exp2(s).astype(BF16), preferred_element_type=F32)

    def body(i, carry):
        start = pl.multiple_of(i * tk, tk)
        return tuple(step(hh, k_ref[pl.ds(start, tk), hh * HEAD_PAD:(hh + 1) * HEAD_PAD],
                          vt_ref[0, i, hh * V_ROWS:(hh + 1) * V_ROWS, :], carry[hh]) for hh in range(2))

    carry = lax.fori_loop(0, n_chunks, body, tuple(jnp.zeros((V_ROWS, tq), F32) for _ in range(2)))
    outs = []
    for hh in range(2):
        acc = carry[hh]
        if has_ctx:
            acc = step(hh, kc_ref[:, hh * HEAD_PAD:(hh + 1) * HEAD_PAD],
                       vtc_ref[0, 0, hh * V_ROWS:(hh + 1) * V_ROWS, :], acc)
        outs.append(acc[:VDIM] / acc[VDIM:VDIM + 1])
    o_ref[...] = jnp.concatenate(outs, axis=0).T.astype(BF16)


def _attention(q, k, vt, ctx_kv, seq, tq, bound):
    n = q.shape[0]
    batch = n // seq
    n_chunks, tk = vt.shape[1], vt.shape[3]
    q_tiles = seq // tq
    has_ctx = ctx_kv is not None
    in_specs = [
        pl.BlockSpec((tq, 2 * HEAD_PAD), lambda b, j, i: (b * q_tiles + i, j)),
        pl.BlockSpec((seq, 2 * HEAD_PAD), lambda b, j, i: (b, j)),
        pl.BlockSpec((1, n_chunks, 2 * V_ROWS, tk), lambda b, j, i: (b, 0, j, 0)),
    ]
    args = [q, k, vt]
    if has_ctx:
        kc, vtc = ctx_kv
        lc = vtc.shape[3]
        in_specs += [pl.BlockSpec((lc, 2 * HEAD_PAD), lambda b, j, i: (b, j)),
                     pl.BlockSpec((1, 1, 2 * V_ROWS, lc), lambda b, j, i: (b, 0, j, 0))]
        args += [kc, vtc]
    def call(body, name):
        return pl.pallas_call(
            functools.partial(body, n_chunks=n_chunks, has_ctx=has_ctx, tq=tq),
            out_shape=jax.ShapeDtypeStruct((n, HEADS * VDIM), BF16),
            grid=(batch, HEADS // 2, q_tiles),
            in_specs=in_specs,
            out_specs=pl.BlockSpec((tq, 2 * VDIM), lambda b, j, i: (b * q_tiles + i, j)),
            compiler_params=_cparams(("arbitrary", "arbitrary", "arbitrary")),
            name=name,
        )(*args)

    return lax.cond(bound < ATTN_BOUND_MAX,
                    lambda: call(_attn_bounded_kernel, "attn_bounded"),
                    lambda: call(_attn_kernel, "attn_online"))


def _hy_filter_kernel(z_ref, w1_ref, b1_ref, w2_ref, b2_ref, w3_ref, b3_ref, fr_ref, dec_ref, k_ref, ss_ref):
    z = z_ref[...]
    fr = fr_ref[...]
    hdn = jnp.sin(fr * (jnp.dot(z, w1_ref[...], preferred_element_type=F32, precision=HIGHEST) + b1_ref[...]))
    hdn = jnp.sin(fr * (jnp.dot(hdn, w2_ref[...], preferred_element_type=F32, precision=HIGHEST) + b2_ref[...]))
    k = jnp.dot(hdn, w3_ref[...], preferred_element_type=F32, precision=HIGHEST) + b3_ref[...]
    k = k * jnp.exp(-z[:, 0:1] * jnp.abs(dec_ref[...]))
    k_ref[...] = k
    ss_ref[0] = jnp.sum(k * k, axis=0, keepdims=True)


def _hy_filters(z, lw, tr):
    seq, emb = z.shape
    nblk = seq // tr
    hid = lw['hy_w2'].shape[0]
    const = lambda i: (0, 0)
    return pl.pallas_call(
        _hy_filter_kernel,
        out_shape=(jax.ShapeDtypeStruct((seq, 2 * HY_W), F32), jax.ShapeDtypeStruct((nblk, 1, 2 * HY_W), F32)),
        grid=(nblk,),
        in_specs=[pl.BlockSpec((tr, emb), lambda i: (i, 0)),
                  pl.BlockSpec((emb, hid), const), pl.BlockSpec((1, hid), const),
                  pl.BlockSpec((hid, hid), const), pl.BlockSpec((1, hid), const),
                  pl.BlockSpec((hid, 2 * HY_W), const), pl.BlockSpec((1, 2 * HY_W), const),
                  pl.BlockSpec((1, hid), const), pl.BlockSpec((1, 2 * HY_W), const)],
        out_specs=(pl.BlockSpec((tr, 2 * HY_W), lambda i: (i, 0)),
                   pl.BlockSpec((1, 1, 2 * HY_W), lambda i: (i, 0, 0))),
        compiler_params=_cparams(("arbitrary",)), name="hy_filter",
    )(z, lw['hy_w1'], lw['hy_b1'], lw['hy_w2'], lw['hy_b2'], lw['hy_w3'], lw['hy_b3'], lw['hy_freq'],
      lw['hy_decay'])


def _hy_conv3_kernel(p_ref, prev_ref, next_ref, w_ref, b_ref, x0_ref, u_ref, *, tiles_per_seq, tr):
    i = pl.program_id(0)
    p = p_ref[...].astype(F32)
    first = (i % tiles_per_seq) == 0
    last = (i % tiles_per_seq) == tiles_per_seq - 1
    prev_row = jnp.where(first, 0.0, prev_ref[...].astype(F32)[15:16, :])
    next_row = jnp.where(last, 0.0, next_ref[...].astype(F32)[0:1, :])
    row = lax.broadcasted_iota(jnp.int32, p.shape, 0)
    p_prev = jnp.where(row == 0, prev_row, pltpu.roll(p, 1, 0))
    p_next = jnp.where(row == tr - 1, next_row, pltpu.roll(p, tr - 1, 0))
    z = b_ref[...] + p_prev * w_ref[0:1, :] + p * w_ref[1:2, :] + p_next * w_ref[2:3, :]
    x0_ref[...] = z[:, :HY_W].astype(BF16)
    u_ref[...] = (z[:, 2 * HY_W:] * z[:, HY_W:2 * HY_W]).astype(BF16)


def _hy_conv3(phy, conv_w, conv_b, seq, tr):
    n = phy.shape[0]
    tiles_per_seq = seq // tr
    hb = tr // 16
    nhb = n // 16
    return pl.pallas_call(
        functools.partial(_hy_conv3_kernel, tiles_per_seq=tiles_per_seq, tr=tr),
        out_shape=(jax.ShapeDtypeStruct((n, HY_W), BF16), jax.ShapeDtypeStruct((n, HY_W), BF16)),
        grid=(n // tr,),
        in_specs=[pl.BlockSpec((tr, 3 * HY_W), lambda i: (i, 0)),
                  pl.BlockSpec((16, 3 * HY_W), lambda i: (jnp.maximum(i * hb - 1, 0), 0)),
                  pl.BlockSpec((16, 3 * HY_W), lambda i: (jnp.minimum((i + 1) * hb, nhb - 1), 0)),
                  pl.BlockSpec((3, 3 * HY_W), lambda i: (0, 0)),
                  pl.BlockSpec((1, 3 * HY_W), lambda i: (0, 0))],
        out_specs=(pl.BlockSpec((tr, HY_W), lambda i: (i, 0)), pl.BlockSpec((tr, HY_W), lambda i: (i, 0))),
        compiler_params=_cparams(("arbitrary",)), name="hy_conv3",
    )(phy, phy, phy, conv_w, conv_b)


def _dft_outer_kernel(wr_ref, wi_ref, u_ref, ar_ref, ai_ref):
    u = u_ref[0]
    ar_ref[0] = jnp.dot(wr_ref[...], u, preferred_element_type=F32).astype(BF16)
    ai_ref[0] = jnp.dot(wi_ref[...], u, preferred_element_type=F32).astype(BF16)


def _dft_outer(wr, wi, u, tc):
    nb, kk, cols = u.shape
    n1 = wr.shape[0]
    return pl.pallas_call(
        _dft_outer_kernel,
        out_shape=(jax.ShapeDtypeStruct((nb, n1, cols), BF16),) * 2,
        grid=(nb, cols // tc),
        in_specs=[pl.BlockSpec((n1, kk), lambda b, j: (0, 0)), pl.BlockSpec((n1, kk), lambda b, j: (0, 0)),
                  pl.BlockSpec((1, kk, tc), lambda b, j: (b, 0, j))],
        out_specs=(pl.BlockSpec((1, n1, tc), lambda b, j: (b, 0, j)),) * 2,
        compiler_params=_cparams(("arbitrary", "arbitrary")), name="dft_outer",
    )(wr, wi, u)


def _cdot(mr, mi, xr, xi):
    rr = jnp.dot(mr, xr, preferred_element_type=F32) - jnp.dot(mi, xi, preferred_element_type=F32)
    ri = jnp.dot(mr, xi, preferred_element_type=F32) + jnp.dot(mi, xr, preferred_element_type=F32)
    return rr, ri


def _dft_inner_fwd_kernel(mr_ref, mi_ref, ar_ref, ai_ref, fr_ref, fi_ref):
    fr, fi = _cdot(mr_ref[0], mi_ref[0], ar_ref[0, 0], ai_ref[0, 0])
    fr_ref[0] = fr.astype(BF16)
    fi_ref[0] = fi.astype(BF16)


def _dft_inner_fwd(mr, mi, ar, ai):
    n1, n2, _ = mr.shape
    ch = ar.shape[-1]
    mspec = pl.BlockSpec((1, n2, n2), lambda k: (k, 0, 0))
    aspec = pl.BlockSpec((1, 1, n2, ch), lambda k: (0, k, 0, 0))
    ospec = pl.BlockSpec((1, n2, ch), lambda k: (k, 0, 0))
    return pl.pallas_call(
        _dft_inner_fwd_kernel,
        out_shape=(jax.ShapeDtypeStruct((n1, n2, ch), BF16),) * 2,
        grid=(n1,), in_specs=[mspec, mspec, aspec, aspec], out_specs=(ospec, ospec),
        compiler_params=_cparams(("arbitrary",)), name="dft_inner_fwd",
    )(mr, mi, ar, ai)


def _dft_inner_conv_kernel(mr_ref, mi_ref, tr_ref, ti_ref, ar_ref, ai_ref, fr_ref, fi_ref, br_ref, bi_ref, *, nb):
    fr = fr_ref[0].astype(F32)
    fi = fi_ref[0].astype(F32)
    for b in range(nb):
        ur, ui = _cdot(mr_ref[0], mi_ref[0], ar_ref[b, 0], ai_ref[b, 0])
        yr = (ur * fr - ui * fi).astype(BF16)
        yi = (ur * fi + ui * fr).astype(BF16)
        br, bi = _cdot(tr_ref[0], ti_ref[0], yr, yi)
        br_ref[b, 0] = br.astype(BF16)
        bi_ref[b, 0] = bi.astype(BF16)


def _dft_inner_conv(mr, mi, tr, ti, ar, ai, fr, fi):
    n1, n2, _ = mr.shape
    nb, _, _, ch = ar.shape
    mspec = pl.BlockSpec((1, n2, n2), lambda k: (k, 0, 0))
    aspec = pl.BlockSpec((nb, 1, n2, ch), lambda k: (0, k, 0, 0))
    fspec = pl.BlockSpec((1, n2, ch), lambda k: (k, 0, 0))
    return pl.pallas_call(
        functools.partial(_dft_inner_conv_kernel, nb=nb),
        out_shape=(jax.ShapeDtypeStruct((nb, n1, n2, ch), BF16),) * 2,
        grid=(n1,), in_specs=[mspec, mspec, mspec, mspec, aspec, aspec, fspec, fspec],
        out_specs=(aspec, aspec),
        compiler_params=_cparams(("arbitrary",)), name="dft_inner_conv",
    )(mr, mi, tr, ti, ar, ai, fr, fi)


def _filter_scale(ss_ref):
    ss = jnp.sum(ss_ref[...], axis=0)
    return lax.rsqrt(ss[:, :HY_W] + ss[:, HY_W:] + EPS)


def _hy_final_kernel(cr_ref, ci_ref, br_ref, bi_ref, x0_ref, u_ref, ss_ref, bias_ref, o_ref, *, reps):
    y = (jnp.dot(cr_ref[...], br_ref[0], preferred_element_type=F32)
         + jnp.dot(ci_ref[...], bi_ref[0], preferred_element_type=F32))
    scale = jnp.tile(_filter_scale(ss_ref), (1, reps))
    bias = jnp.tile(bias_ref[...], (1, reps))
    u = u_ref[0].astype(F32)
    o_ref[0] = (x0_ref[0].astype(F32) * (y * scale + u * bias)).astype(BF16)


def _hy_final(cr, ci, br, bi, x0, u, ss, bias, tc):
    nb, n1, cols = br.shape
    n1h = cr.shape[0]
    nblk = ss.shape[0]
    cspec = pl.BlockSpec((n1h, n1), lambda b, j: (0, 0))
    bspec = pl.BlockSpec((1, n1, tc), lambda b, j: (b, 0, j))
    xspec = pl.BlockSpec((1, n1h, tc), lambda b, j: (b, 0, j))
    return pl.pallas_call(
        functools.partial(_hy_final_kernel, reps=tc // HY_W),
        out_shape=jax.ShapeDtypeStruct((nb, n1h, cols), BF16),
        grid=(nb, cols // tc),
        in_specs=[cspec, cspec, bspec, bspec, xspec, xspec,
                  pl.BlockSpec((nblk, 1, 2 * HY_W), lambda b, j: (0, 0, 0)),
                  pl.BlockSpec((1, HY_W), lambda b, j: (0, 0))],
        out_specs=xspec,
        compiler_params=_cparams(("arbitrary", "arbitrary")), name="hy_final",
    )(cr, ci, br, bi, x0, u, ss, bias)


def _hy_direct_kernel(kk_ref, x0_ref, u_ref, ss_ref, bias_ref, o_ref, *, seq):
    u = u_ref[0].astype(F32)
    y = jnp.zeros((seq, HY_W), F32)
    for j in range(seq):
        y = y + kk_ref[seq - j:2 * seq - j, :] * u[j:j + 1, :]
    o_ref[0] = (x0_ref[0].astype(F32) * (y * _filter_scale(ss_ref) + u * bias_ref[...])).astype(BF16)


def _hy_direct(kk, x0, u, ss, bias):
    nb, seq, _ = u.shape
    nblk = ss.shape[0]
    xspec = pl.BlockSpec((1, seq, HY_W), lambda b: (b, 0, 0))
    return pl.pallas_call(
        functools.partial(_hy_direct_kernel, seq=seq),
        out_shape=jax.ShapeDtypeStruct((nb, seq, HY_W), BF16),
        grid=(nb,),
        in_specs=[pl.BlockSpec((2 * seq, HY_W), lambda b: (0, 0)), xspec, xspec,
                  pl.BlockSpec((nblk, 1, 2 * HY_W), lambda b: (0, 0, 0)),
                  pl.BlockSpec((1, HY_W), lambda b: (0, 0))],
        out_specs=xspec,
        compiler_params=_cparams(("arbitrary",)), name="hy_direct",
    )(kk, x0, u, ss, bias)


def _hy_embedding(seq):
    t = jnp.arange(seq, dtype=F32)
    t_unit = t / max(seq - 1, 1)
    bands = jnp.linspace(1e-4, HY_BANDS - 1, HY_BANDS, dtype=F32)
    ang = (2 * jnp.pi / seq) * t[:, None] * bands[None, :]
    return jnp.concatenate([t_unit[:, None], jnp.cos(ang), -jnp.sin(ang)], axis=-1)


def _dft_tables(seq):
    n = 2 * seq
    n2 = DFT_N2
    n1 = n // n2
    a = jnp.arange(n1, dtype=jnp.int32)
    th1 = (2 * jnp.pi / n1) * ((a[:, None] * a[None, :]) % n1).astype(F32)
    w1r, w1i = jnp.cos(th1), -jnp.sin(th1)
    k1 = jnp.arange(n1, dtype=jnp.int32)[:, None, None]
    k2 = jnp.arange(n2, dtype=jnp.int32)[None, :, None]
    t2 = jnp.arange(n2, dtype=jnp.int32)[None, None, :]
    th = (2 * jnp.pi / n) * ((t2 * (k1 + n1 * k2)) % n).astype(F32)
    mr, mi = jnp.cos(th), -jnp.sin(th)
    return dict(
        w1r=w1r.astype(BF16), w1i=w1i.astype(BF16),
        mr=mr.astype(BF16), mi=mi.astype(BF16),
        tr=jnp.swapaxes(mr, 1, 2).astype(BF16), ti=jnp.swapaxes(-mi, 1, 2).astype(BF16),
        cr=(w1r[:n1 // 2] / n).astype(BF16), ci=(w1i[:n1 // 2] / n).astype(BF16))


def _hyena_long(phy, lw, filt, seq, tabs):
    n = phy.shape[0]
    nb = n // seq
    n2 = DFT_N2
    n1 = 2 * seq // n2
    kf, ss = filt
    x0, u = _hy_conv3(phy, lw['hy_conv_w'], lw['hy_conv_b'], seq, min(seq, 1024))
    f = jnp.concatenate([kf[:, :HY_W], jnp.zeros((1, HY_W), F32), kf[:0:-1, HY_W:]], axis=0).astype(BF16)
    tc = min(n2 * HY_W, 8192)
    far, fai = _dft_outer(tabs['w1r'], tabs['w1i'], f.reshape(1, n1, n2 * HY_W), tc)
    fr, fi = _dft_inner_fwd(tabs['mr'], tabs['mi'], far.reshape(1, n1, n2, HY_W), fai.reshape(1, n1, n2, HY_W))
    ar, ai = _dft_outer(tabs['w1r'][:, :n1 // 2], tabs['w1i'][:, :n1 // 2], u.reshape(nb, n1 // 2, n2 * HY_W), tc)
    br, bi = _dft_inner_conv(tabs['mr'], tabs['mi'], tabs['tr'], tabs['ti'],
                             ar.reshape(nb, n1, n2, HY_W), ai.reshape(nb, n1, n2, HY_W), fr, fi)
    out = _hy_final(tabs['cr'], tabs['ci'], br.reshape(nb, n1, n2 * HY_W), bi.reshape(nb, n1, n2 * HY_W),
                    x0.reshape(nb, n1 // 2, n2 * HY_W), u.reshape(nb, n1 // 2, n2 * HY_W), ss, lw['hy_bias'], tc)
    return out.reshape(n, HY_W)


def _hyena_short(phy, lw, filt, seq):
    n = phy.shape[0]
    nb = n // seq
    kf, ss = filt
    x0, u = _hy_conv3(phy, lw['hy_conv_w'], lw['hy_conv_b'], seq, seq)
    kk = jnp.concatenate([jnp.zeros((1, HY_W), F32), kf[:0:-1, HY_W:], kf[:, :HY_W]], axis=0)
    out = _hy_direct(kk, x0.reshape(nb, seq, HY_W), u.reshape(nb, seq, HY_W), ss, lw['hy_bias'])
    return out.reshape(n, HY_W)


def _pair_top2_sum(a, b, c, d):
    return jnp.maximum(jnp.maximum(jnp.maximum(a + b, a + c), jnp.maximum(a + d, b + c)),
                       jnp.maximum(b + d, c + d))


def _route(logits_t, rb):
    aff = jax.nn.sigmoid(logits_t)
    sel = aff + rb
    rows = [sel[e:e + 1, :] for e in range(N_EXPERTS)]
    affr = [aff[e:e + 1, :] for e in range(N_EXPERTS)]
    best, bidx = None, None
    for g in range(N_GROUPS):
        gs = _pair_top2_sum(*rows[g * EXP_PER_GROUP:(g + 1) * EXP_PER_GROUP])
        if g == 0:
            best, bidx = gs, jnp.zeros(gs.shape, jnp.int32)
        else:
            upd = gs > best
            bidx = jnp.where(upd, g, bidx)
            best = jnp.where(upd, gs, best)
    vals = [jnp.where(bidx == e // EXP_PER_GROUP, rows[e], -jnp.inf) for e in range(N_EXPERTS)]
    m1, i1 = vals[0], jnp.zeros(best.shape, jnp.int32)
    for e in range(1, N_EXPERTS):
        upd = vals[e] > m1
        i1 = jnp.where(upd, e, i1)
        m1 = jnp.where(upd, vals[e], m1)
    m2, i2 = jnp.full(best.shape, -jnp.inf, F32), jnp.zeros(best.shape, jnp.int32)
    for e in range(N_EXPERTS):
        cand = jnp.where(i1 == e, -jnp.inf, vals[e])
        upd = cand > m2
        i2 = jnp.where(upd, e, i2)
        m2 = jnp.where(upd, cand, m2)
    a1 = sum(jnp.where(i1 == e, affr[e], 0.0) for e in range(N_EXPERTS))
    a2 = sum(jnp.where(i2 == e, affr[e], 0.0) for e in range(N_EXPERTS))
    inv = 1.0 / (a1 + a2)
    return jnp.concatenate([i1, i2], axis=0), jnp.concatenate([a1 * inv, a2 * inv], axis=0)


def _merge_kernel(a_ref, b_ref, m_ref, gt_ref, x_ref, gate_ref, shift_ref, scale_ref, g2_ref,
                  wpa_ref, wpb_ref, wpc_ref, wout_ref, rwt_ref, rb_ref, xo_ref, h2_ref, idx_ref, wts_ref):
    y = gt_ref[:, 0:D_MODEL].astype(F32) * jnp.dot(a_ref[...], wpa_ref[...], preferred_element_type=F32)
    y = y + gt_ref[:, D_MODEL:2 * D_MODEL].astype(F32) * jnp.dot(b_ref[...], wpb_ref[...],
                                                                 preferred_element_type=F32)
    y = y + gt_ref[:, 2 * D_MODEL:].astype(F32) * jnp.dot(m_ref[...], wpc_ref[...], preferred_element_type=F32)
    y2 = jnp.dot(y.astype(BF16), wout_ref[...], preferred_element_type=F32)
    xn = x_ref[...] + gate_ref[0] * y2
    xo_ref[...] = xn
    h2 = _rms(xn) * g2_ref[...]
    h2 = h2 * (1.0 + scale_ref[0]) + shift_ref[0]
    h2_ref[...] = h2
    logits_t = lax.dot_general(rwt_ref[...], h2, (((1,), (1,)), ((), ())), preferred_element_type=F32,
                               precision=HIGHEST)
    idx_ref[...], wts_ref[...] = _route(logits_t, rb_ref[...])


def _merge(a, b, m, gt, x, gate, shift, scale, lw, rw_t, rb, seq, tm):
    n = x.shape[0]
    tiles_per_seq = seq // tm
    const = lambda i: (0, 0)
    row = lambda w: pl.BlockSpec((tm, w), lambda i: (i, 0))
    vec = pl.BlockSpec((1, 1, D_MODEL), lambda i: (i // tiles_per_seq, 0, 0))
    return pl.pallas_call(
        _merge_kernel,
        out_shape=(jax.ShapeDtypeStruct((n, D_MODEL), F32), jax.ShapeDtypeStruct((n, D_MODEL), F32),
                   jax.ShapeDtypeStruct((TOP_K, n), jnp.int32), jax.ShapeDtypeStruct((TOP_K, n), F32)),
        grid=(n // tm,),
        in_specs=[row(HEADS * VDIM), row(HY_W), row(GM_W), row(3 * D_MODEL), row(D_MODEL), vec, vec, vec,
                  pl.BlockSpec((1, D_MODEL), const),
                  pl.BlockSpec((HEADS * VDIM, D_MODEL), const), pl.BlockSpec((HY_W, D_MODEL), const),
                  pl.BlockSpec((GM_W, D_MODEL), const), pl.BlockSpec((D_MODEL, D_MODEL), const),
                  pl.BlockSpec((N_EXPERTS, D_MODEL), const), pl.BlockSpec((N_EXPERTS, 1), const)],
        out_specs=(row(D_MODEL), row(D_MODEL), pl.BlockSpec((TOP_K, tm), lambda i: (0, i)),
                   pl.BlockSpec((TOP_K, tm), lambda i: (0, i))),
        compiler_params=_cparams(("arbitrary",)), name="merge",
    )(a, b, m, gt, x, gate, shift, scale, lw['g2'], lw['w_pa'], lw['w_pb'], lw['w_pc'], lw['w_out'], rw_t, rb)


def _rows_copy(src_hbm, dst, sem, idx_ref, rows):
    return [pltpu.make_async_copy(src_hbm.at[pl.ds(idx_ref[0, 0, r], 1)], dst.at[pl.ds(r, 1)], sem)
            for r in range(rows)]


def _rows_wait(src_hbm, dst, sem, rows):
    pltpu.make_async_copy(src_hbm.at[pl.ds(0, rows)], dst, sem).wait()


def _gather_pipeline(src_hbm, cur_ref, nxt_ref, buf, sem, rows):
    i = pl.program_id(0)
    slot = i % 2

    @pl.when(i == 0)
    def _():
        for cp in _rows_copy(src_hbm, buf.at[0], sem.at[0], cur_ref, rows):
            cp.start()

    _rows_wait(src_hbm, buf.at[slot], sem.at[slot], rows)
    for cp in _rows_copy(src_hbm, buf.at[1 - slot], sem.at[1 - slot], nxt_ref, rows):
        cp.start()
    return slot


def _gather_drain(src_hbm, buf, sem, rows):
    i = pl.program_id(0)

    @pl.when(i == pl.num_programs(0) - 1)
    def _():
        _rows_wait(src_hbm, buf.at[1 - i % 2], sem.at[1 - i % 2], rows)


def _moe_dispatch_kernel(pos_ref, h_ref, xs_in_hbm, xs_hbm, sem, *, tc):
    del xs_in_hbm
    for r in range(tc):
        pltpu.make_async_copy(h_ref.at[pl.ds(r, 1)], xs_hbm.at[pl.ds(pos_ref[0, 0, r], 1)], sem).start()
    pltpu.make_async_copy(h_ref, xs_hbm.at[pl.ds(0, tc)], sem).wait()


def _moe_dispatch(pos_tiles, h2, rows, tc):
    n = h2.shape[0]
    return pl.pallas_call(
        functools.partial(_moe_dispatch_kernel, tc=tc),
        out_shape=jax.ShapeDtypeStruct((rows, D_MODEL), F32),
        grid=(n // tc,),
        in_specs=[pl.BlockSpec((1, 1, tc), lambda i: (i, 0, 0), memory_space=pltpu.SMEM),
                  pl.BlockSpec((tc, D_MODEL), lambda i: (i, 0)),
                  pl.BlockSpec(memory_space=pl.ANY)],
        out_specs=pl.BlockSpec(memory_space=pl.ANY),
        scratch_shapes=[pltpu.SemaphoreType.DMA],
        input_output_aliases={2: 0},
        compiler_params=_cparams(("arbitrary",)), name="moe_dispatch",
    )(pos_tiles, h2, jnp.zeros((rows, D_MODEL), F32))


def _moe_ffn_kernel(ea_ref, eb_ref, x_ref, wgua_ref, wda_ref, wgub_ref, wdb_ref, y_ref):
    del ea_ref, eb_ref
    xb = x_ref[...].astype(BF16)
    for j, (wgu_ref, wd_ref) in enumerate(((wgua_ref, wda_ref), (wgub_ref, wdb_ref))):
        gu = jnp.dot(xb, wgu_ref[0], preferred_element_type=F32)
        g = gu[:, :D_EXPERT]
        hid = (g * jax.nn.sigmoid(g) * gu[:, D_EXPERT:]).astype(BF16)
        y_ref[:, j * D_MODEL:(j + 1) * D_MODEL] = jnp.dot(hid, wd_ref[0], preferred_element_type=F32)


def _moe_ffn(blk_ea, blk_eb, x_sorted, lw):
    nblk = blk_ea.shape[0]
    wgu_spec = lambda sel: pl.BlockSpec((1, D_MODEL, 2 * D_EXPERT), lambda i, ea, eb: (sel(ea, eb)[i], 0, 0))
    wd_spec = lambda sel: pl.BlockSpec((1, D_EXPERT, D_MODEL), lambda i, ea, eb: (sel(ea, eb)[i], 0, 0))
    first, second = (lambda ea, eb: ea), (lambda ea, eb: eb)
    return pl.pallas_call(
        _moe_ffn_kernel,
        out_shape=jax.ShapeDtypeStruct((nblk * MOE_BM, TOP_K * D_MODEL), F32),
        grid_spec=pltpu.PrefetchScalarGridSpec(
            num_scalar_prefetch=2, grid=(nblk,),
            in_specs=[pl.BlockSpec((MOE_BM, D_MODEL), lambda i, ea, eb: (i, 0)),
                      wgu_spec(first), wd_spec(first), wgu_spec(second), wd_spec(second)],
            out_specs=pl.BlockSpec((MOE_BM, TOP_K * D_MODEL), lambda i, ea, eb: (i, 0))),
        compiler_params=_cparams(("arbitrary",)), name="moe_ffn",
    )(blk_ea, blk_eb, x_sorted, lw['w_gu'], lw['w_d'], lw['w_gu'], lw['w_d'])


def _moe_combine_kernel(cur_ref, nxt_ref, y_hbm, x_ref, w_ref, gate_ref, o_ref, buf, sem, *, tc):
    slot = _gather_pipeline(y_hbm, cur_ref, nxt_ref, buf, sem, tc)
    w = w_ref[...]
    y = w[:, 0:1] * buf[slot, :, 0:D_MODEL] + w[:, 1:2] * buf[slot, :, D_MODEL:]
    o_ref[...] = x_ref[...] + gate_ref[0] * y
    _gather_drain(y_hbm, buf, sem, tc)


def _moe_combine(pos_tiles, y_sorted, x, w_cols, gate, seq, tc):
    n = x.shape[0]
    tiles_per_seq = seq // tc
    idx_spec = lambda off: pl.BlockSpec((1, 1, tc), lambda i: (i + off, 0, 0), memory_space=pltpu.SMEM)
    return pl.pallas_call(
        functools.partial(_moe_combine_kernel, tc=tc),
        out_shape=jax.ShapeDtypeStruct((n, D_MODEL), F32),
        grid=(n // tc,),
        in_specs=[idx_spec(0), idx_spec(1), pl.BlockSpec(memory_space=pl.ANY),
                  pl.BlockSpec((tc, D_MODEL), lambda i: (i, 0)),
                  pl.BlockSpec((tc, TOP_K), lambda i: (i, 0)),
                  pl.BlockSpec((1, 1, D_MODEL), lambda i: (i // tiles_per_seq, 0, 0))],
        out_specs=pl.BlockSpec((tc, D_MODEL), lambda i: (i, 0)),
        scratch_shapes=[pltpu.VMEM((2, tc, TOP_K * D_MODEL), F32), pltpu.SemaphoreType.DMA((2,))],
        compiler_params=_cparams(("arbitrary",)), name="moe_combine",
    )(pos_tiles, pos_tiles, y_sorted, x, w_cols, gate)


def _pair_tables():
    pairs = [(a, b) for a in range(EXP_PER_GROUP) for b in range(a + 1, EXP_PER_GROUP)]
    ea = [g * EXP_PER_GROUP + a for g in range(N_GROUPS) for a, _ in pairs]
    eb = [g * EXP_PER_GROUP + b for g in range(N_GROUPS) for _, b in pairs]
    return jnp.array(ea, jnp.int32), jnp.array(eb, jnp.int32)


def _moe(h2, idx, wts, x, gate, lw, seq):
    n = x.shape[0]
    tab_a, tab_b = _pair_tables()
    n_cls = tab_a.shape[0]
    nblk = n // MOE_BM + n_cls
    swap = idx[0] > idx[1]
    e_lo, e_hi = jnp.minimum(idx[0], idx[1]), jnp.maximum(idx[0], idx[1])
    w_cols = jnp.stack([jnp.where(swap, wts[1], wts[0]), jnp.where(swap, wts[0], wts[1])], axis=1)
    onehot = ((e_lo[:, None] == tab_a[None, :]) & (e_hi[:, None] == tab_b[None, :])).astype(jnp.int32)
    csum = jnp.cumsum(onehot, axis=0)
    counts = csum[-1]
    rank = jnp.sum(csum * onehot, axis=1) - 1
    seg_len = (counts + MOE_BM - 1) // MOE_BM * MOE_BM
    seg_end = jnp.cumsum(seg_len)
    pos = jnp.sum(onehot * (seg_end - seg_len)[None, :], axis=1) + rank
    blk_cls = jnp.minimum(jnp.searchsorted(seg_end, jnp.arange(nblk, dtype=jnp.int32) * MOE_BM, side='right'),
                          n_cls - 1)
    tc = min(seq, MOE_TC)
    pos_tiles = pos.reshape(n // tc, 1, tc)
    x_sorted = _moe_dispatch(pos_tiles, h2, nblk * MOE_BM, tc)
    y_sorted = _moe_ffn(tab_a[blk_cls], tab_b[blk_cls], x_sorted, lw)
    pos_next = jnp.concatenate([pos_tiles, jnp.zeros((1, 1, tc), jnp.int32)], axis=0)
    return _moe_combine(pos_next, y_sorted, x, w_cols, gate, seq, tc)


def _head_perm():
    rope_idx = list(range(NOPE, QK, 2)) + list(range(NOPE + 1, QK, 2))
    return jnp.array(list(range(NOPE)) + rope_idx, dtype=jnp.int32)


def _layer_weights(p, l):
    perm = _head_perm()
    row = lambda v: v.reshape(1, -1)
    w_in = p['w_in'][l]
    kr_perm = jnp.array(list(range(0, ROPE, 2)) + list(range(1, ROPE, 2)), dtype=jnp.int32)
    w_kr = jnp.pad(w_in[:, OFF_KR:OFF_HY][:, kr_perm], ((0, 0), (0, HEAD_PAD - ROPE)))
    w_in2 = jnp.concatenate([w_in[:, OFF_Q:OFF_KR], w_kr, w_in[:, OFF_HY:]], axis=1).astype(BF16)
    w_uq = p['w_uq'][l].reshape(Q_RANK, HEADS, QK)[:, :, perm]
    w_uq = jnp.pad(w_uq, ((0, 0), (0, 0), (0, HEAD_PAD - QK))).reshape(Q_RANK, HEADS * HEAD_PAD).astype(BF16)
    w_ukv = p['w_ukv'][l].reshape(KV_RANK, HEADS, NOPE + VDIM)
    w_k = jnp.pad(w_ukv[:, :, :NOPE], ((0, 0), (0, 0), (0, HEAD_PAD - NOPE)))
    w_k = w_k.reshape(KV_RANK, HEADS * HEAD_PAD).astype(BF16)
    w_vt = w_ukv[:, :, NOPE:].reshape(KV_RANK, HEADS * VDIM).T.astype(BF16)
    pad_gain = lambda g: jnp.pad(g[perm], (0, HEAD_PAD - QK)).reshape(1, HEAD_PAD)
    gm_b = jnp.repeat(p['gm_bs'][l].T, GM_W // GM_GROUPS, axis=1)
    bound = 1.02 * QSCALE * QK * jnp.max(jnp.abs(p['g_qn'][l])) * jnp.max(jnp.abs(p['g_kn'][l]))
    pad_lane = jnp.arange(HEAD_PAD) == QK
    return dict(
        attn_bound=bound,
        q_pad=jnp.where(pad_lane, -bound, 0.0).reshape(1, HEAD_PAD).astype(F32),
        k_pad=jnp.where(pad_lane, 1.0, 0.0).reshape(1, HEAD_PAD).astype(F32),
        g1=row(p['norm1_g'][l]), g2=row(p['norm2_g'][l]), w_in=w_in2,
        g_qa=row(p['g_qa'][l]), w_uq=w_uq, g_qn=pad_gain(p['g_qn'][l]),
        g_kva=row(p['g_kva'][l]), w_k=w_k, w_vt=w_vt, g_kn=pad_gain(p['g_kn'][l]),
        gm_g=row(p['gm_norm_g'][l]), gm_w=p['gm_ws'][l].astype(BF16), gm_b=gm_b,
        hy_conv_w=p['hy_conv_w'][l], hy_conv_b=row(p['hy_conv_b'][l]),
        hy_w1=p['hy_w1'][l], hy_b1=row(p['hy_b1'][l]), hy_w2=p['hy_w2'][l], hy_b2=row(p['hy_b2'][l]),
        hy_w3=p['hy_w3'][l], hy_b3=row(p['hy_b3'][l]), hy_freq=row(p['hy_freq'][l]),
        hy_decay=p['hy_decay'][l].reshape(1, 2 * HY_W), hy_bias=row(p['hy_bias'][l]),
        w_pa=p['w_pa'][l].astype(BF16), w_pb=p['w_pb'][l].astype(BF16), w_pc=p['w_pc'][l].astype(BF16),
        w_out=p['w_out'][l].astype(BF16),
        w_gu=jnp.concatenate([p['moe_w_gate'][l], p['moe_w_up'][l]], axis=-1).astype(BF16),
        w_d=p['moe_w_down'][l].astype(BF16))


def _rope_tables(seq):
    rows = seq // GRID_W
    row = jnp.repeat(jnp.arange(rows, dtype=F32), GRID_W)
    col = jnp.tile(jnp.arange(GRID_W, dtype=F32), rows)
    n_freq = ROPE // 4
    inv = ROPE_THETA ** (-jnp.arange(n_freq, dtype=F32) / n_freq)
    ang = jnp.concatenate([row[:, None] * inv, col[:, None] * inv], axis=-1)
    c, s = jnp.cos(ang), jnp.sin(ang)
    z = lambda w: jnp.zeros((seq, w), F32)
    rc = jnp.concatenate([jnp.ones((seq, NOPE), F32), c, c, z(HEAD_PAD - QK)], axis=1)
    rs1 = jnp.concatenate([z(NOPE), z(ROPE // 2), s, z(HEAD_PAD - QK)], axis=1)
    rs2 = jnp.concatenate([z(NOPE), -s, z(ROPE // 2), z(HEAD_PAD - QK)], axis=1)
    return rc, rs1, rs2


def _mixer_and_ffn(x, mods, lw, rw_t, rb, seq, tm, rope_tabs, ctx_kv, filt, dft_tabs):
    shift1, scale1, gate1, shift2, scale2, gate2 = mods
    q, k, vt, m, phy, gt = _premix(x, shift1, scale1, lw, rope_tabs, seq, tm)
    a = _attention(q, k, vt, ctx_kv, seq, min(seq, ATTN_TQ), lw['attn_bound'])
    if dft_tabs is not None:
        b = _hyena_long(phy, lw, filt, seq, dft_tabs)
    else:
        b = _hyena_short(phy, lw, filt, seq)
    xn, h2, idx, wts = _merge(a, b, m, gt, x, gate1, shift2, scale2, lw, rw_t, rb, seq, tm)
    return _moe(h2, idx, wts, xn, gate2, lw, seq), k, vt


def _forward(p):
    x, ctx = p['x'], p['ctx']
    batch, seq, _ = x.shape
    lc = ctx.shape[1]
    depth = p['w_mod'].shape[0]

    cvecs = jnp.concatenate([p['c'], p['c_ctx'][None], jnp.zeros((8 - batch - 1, D_MODEL), F32)], axis=0)
    mod_all = _modvec(cvecs, p['w_mod'], p['b_mod'])
    rw_t = p['router_w'].T
    rb = p['router_b'].reshape(N_EXPERTS, 1)
    rope_tabs = _rope_tables(seq)
    dft_tabs = _dft_tables(seq)
    z_lat, z_ctx = _hy_embedding(seq), _hy_embedding(lc)

    xl = x.reshape(batch * seq, D_MODEL)
    xc = ctx.reshape(batch * lc, D_MODEL)
    tm_lat = min(seq, 512)
    for l in range(depth):
        lw = _layer_weights(p, l)
        mod = mod_all[l].reshape(8, N_MOD, D_MODEL)
        mods_lat = [mod[:batch, j].reshape(batch, 1, D_MODEL) for j in range(N_MOD)]
        mods_ctx = [jnp.broadcast_to(mod[batch, j].reshape(1, 1, D_MODEL), (batch, 1, D_MODEL))
                    for j in range(N_MOD)]
        if l == depth - 1:
            _, k_c, vt_c, _, _, _ = _premix(xc, mods_ctx[0], mods_ctx[1], lw, None, lc, lc)
        else:
            xc, k_c, vt_c = _mixer_and_ffn(xc, mods_ctx, lw, rw_t, rb, lc, lc, None, None,
                                           _hy_filters(z_ctx, lw, lc), None)
        filt = _hy_filters(z_lat, lw, min(seq, 2048))
        xl, _, _ = _mixer_and_ffn(xl, mods_lat, lw, rw_t, rb, seq, tm_lat, rope_tabs, (k_c, vt_c), filt, dft_tabs)
    return xl.reshape(batch, seq, D_MODEL)


def kernel(x, c, ctx, c_ctx, w_mod, b_mod, norm1_g, norm2_g, w_in, g_qa, w_uq, g_kva, w_ukv, g_qn, g_kn,
           hy_conv_w, hy_conv_b, hy_w1, hy_b1, hy_w2, hy_b2, hy_w3, hy_b3, hy_freq, hy_decay, hy_bias,
           gm_norm_g, gm_ws, gm_bs, w_pa, w_pb, w_pc, w_out, router_w, router_b,
           moe_w_gate, moe_w_up, moe_w_down):
    return _forward(dict(
        x=x, c=c, ctx=ctx, c_ctx=c_ctx, w_mod=w_mod, b_mod=b_mod, norm1_g=norm1_g, norm2_g=norm2_g, w_in=w_in,
        g_qa=g_qa, w_uq=w_uq, g_kva=g_kva, w_ukv=w_ukv, g_qn=g_qn, g_kn=g_kn, hy_conv_w=hy_conv_w,
        hy_conv_b=hy_conv_b, hy_w1=hy_w1, hy_b1=hy_b1, hy_w2=hy_w2, hy_b2=hy_b2, hy_w3=hy_w3, hy_b3=hy_b3,
        hy_freq=hy_freq, hy_decay=hy_decay, hy_bias=hy_bias, gm_norm_g=gm_norm_g, gm_ws=gm_ws, gm_bs=gm_bs,
        w_pa=w_pa, w_pb=w_pb, w_pc=w_pc, w_out=w_out, router_w=router_w, router_b=router_b,
        moe_w_gate=moe_w_gate, moe_w_up=moe_w_up, moe_w_down=moe_w_down))
```

```python
import functools
import math

import jax
import jax.numpy as jnp
from jax import lax
from jax.experimental import pallas as pl
from jax.experimental.pallas import tpu as pltpu

F32 = jnp.float32
BF16 = jnp.bfloat16
HIGHEST = lax.Precision.HIGHEST

D_MODEL = 1024
GRID_W = 64
EPS = 1e-6
N_MOD = 6

HEADS = 8
Q_RANK = 384
KV_RANK = 256
NOPE = 64
ROPE = 32
QK = NOPE + ROPE
VDIM = 64
HEAD_PAD = 128
ROPE_THETA = 10000.0
V_ROWS = 80

HY_W = 256
HY_BANDS = 16
GM_W = 256
GM_CHUNK = 128
GM_GROUPS = 4

OFF_Q = 0
OFF_KV = OFF_Q + Q_RANK
OFF_KR = OFF_KV + KV_RANK
OFF_HY = OFF_KR + ROPE
OFF_GM = OFF_HY + 3 * HY_W
OFF_GT = OFF_GM + 2 * GM_W

P_Q = 0
P_KV = P_Q + Q_RANK
P_KR = P_KV + KV_RANK
P_HY = P_KR + HEAD_PAD
P_GM = P_HY + 3 * HY_W
P_GT = P_GM + 2 * GM_W
P_W = P_GT + 3 * D_MODEL

N_EXPERTS = 16
N_GROUPS = 4
EXP_PER_GROUP = 4
TOP_K = 2
D_EXPERT = 512
MOE_BM = 256
MOE_TC = 256

DFT_N2 = 256

VMEM_LIMIT = 56 * 1024 * 1024
ATTN_TQ = 512
ATTN_TK = 2048
ATTN_BOUND_MAX = 50.0
NEG_BIG = -1e30
LOG2E = 1.4426950408889634
QSCALE = QK ** -0.5 * LOG2E


def _cparams(sem):
    return pltpu.CompilerParams(dimension_semantics=sem, vmem_limit_bytes=VMEM_LIMIT)


def _rms(x):
    return x * lax.rsqrt(jnp.mean(x * x, axis=-1, keepdims=True) + EPS)


def _nt_dot(a, b):
    return lax.dot_general(a, b, (((1,), (1,)), ((), ())), preferred_element_type=F32)


def _modvec_kernel(c_ref, w_ref, b_ref, o_ref):
    cv = c_ref[...]
    s = cv * jax.nn.sigmoid(cv)
    o_ref[0] = jnp.dot(s, w_ref[0], preferred_element_type=F32, precision=HIGHEST) + b_ref[0]


def _modvec(cvecs, w_mod, b_mod):
    depth = w_mod.shape[0]
    tn = 1536
    return pl.pallas_call(
        _modvec_kernel,
        out_shape=jax.ShapeDtypeStruct((depth, 8, N_MOD * D_MODEL), F32),
        grid=(depth, N_MOD * D_MODEL // tn),
        in_specs=[pl.BlockSpec((8, D_MODEL), lambda l, j: (0, 0)),
                  pl.BlockSpec((1, D_MODEL, tn), lambda l, j: (l, 0, j)),
                  pl.BlockSpec((1, 1, tn), lambda l, j: (l, 0, j))],
        out_specs=pl.BlockSpec((1, 8, tn), lambda l, j: (l, 0, j)),
        compiler_params=_cparams(("arbitrary", "arbitrary")),
        name="modvec",
    )(cvecs, w_mod, b_mod.reshape(depth, 1, N_MOD * D_MODEL))


def _head_norm_rope(xh, gain, rope):
    ms = jnp.sum(xh * xh, axis=-1, keepdims=True) * (1.0 / QK)
    xh = xh * lax.rsqrt(ms + EPS) * gain
    if rope is not None:
        rc, rs1, rs2 = rope
        xh = xh * rc + pltpu.roll(xh, ROPE // 2, 1) * rs1 + pltpu.roll(xh, HEAD_PAD - ROPE // 2, 1) * rs2
    return xh


def _premix_kernel(*refs, use_rope, tm):
    if use_rope:
        (x_ref, shift_ref, scale_ref, g1_ref, win_ref, gqa_ref, wuq_ref, gqn_ref, gkva_ref, wk_ref, wvt_ref,
         gkn_ref, gmg_ref, gmw_ref, gmb_ref, qpad_ref, kpad_ref, rc_ref, rs1_ref, rs2_ref,
         q_ref, k_ref, vt_ref, m_ref, hy_ref, gt_ref) = refs
        rope = (rc_ref[...], rs1_ref[...], rs2_ref[...])
    else:
        (x_ref, shift_ref, scale_ref, g1_ref, win_ref, gqa_ref, wuq_ref, gqn_ref, gkva_ref, wk_ref, wvt_ref,
         gkn_ref, gmg_ref, gmw_ref, gmb_ref, qpad_ref, kpad_ref,
         q_ref, k_ref, vt_ref, m_ref, hy_ref, gt_ref) = refs
        rope = None

    x = x_ref[...]
    h = _rms(x) * g1_ref[...]
    h = h * (1.0 + scale_ref[0]) + shift_ref[0]
    hb = h.astype(BF16)

    def proj(lo, width):
        return jnp.dot(hb, win_ref[:, lo:lo + width], preferred_element_type=F32)

    qa = (_rms(proj(P_Q, Q_RANK)) * gqa_ref[...]).astype(BF16)
    q = jnp.dot(qa, wuq_ref[...], preferred_element_type=F32)
    for hh in range(HEADS):
        qh = _head_norm_rope(q[:, hh * HEAD_PAD:(hh + 1) * HEAD_PAD], gqn_ref[...], rope)
        q_ref[:, hh * HEAD_PAD:(hh + 1) * HEAD_PAD] = (qh * QSCALE + qpad_ref[...]).astype(BF16)

    kva = (_rms(proj(P_KV, KV_RANK)) * gkva_ref[...]).astype(BF16)
    kr = pltpu.roll(proj(P_KR, HEAD_PAD), NOPE, 1)
    kn = jnp.dot(kva, wk_ref[...], preferred_element_type=F32)
    for hh in range(HEADS):
        kh = _head_norm_rope(kn[:, hh * HEAD_PAD:(hh + 1) * HEAD_PAD] + kr, gkn_ref[...], rope)
        k_ref[:, hh * HEAD_PAD:(hh + 1) * HEAD_PAD] = (kh + kpad_ref[...]).astype(BF16)
    vt = _nt_dot(wvt_ref[...], kva)
    row = lax.broadcasted_iota(jnp.int32, (V_ROWS - VDIM, tm), 0)
    ones_rows = jnp.where(row == 0, 1.0, 0.0).astype(BF16)
    for hh in range(HEADS):
        vt_ref[0, 0, hh * V_ROWS:hh * V_ROWS + VDIM, :] = vt[hh * VDIM:(hh + 1) * VDIM].astype(BF16)
        vt_ref[0, 0, hh * V_ROWS + VDIM:(hh + 1) * V_ROWS, :] = ones_rows

    gg = jax.nn.gelu(proj(P_GM, 2 * GM_W), approximate=True)
    gu = gg[:, :GM_W]
    gv = (_rms(gg[:, GM_W:]) * gmg_ref[...]).astype(BF16)
    grp = lax.broadcasted_iota(jnp.int32, (GM_CHUNK, GM_W), 1) // (GM_W // GM_GROUPS)
    for ci in range(tm // GM_CHUNK):
        vc = gv[ci * GM_CHUNK:(ci + 1) * GM_CHUNK]
        s = jnp.zeros((GM_CHUNK, GM_W), F32)
        for g in range(GM_GROUPS):
            sg = jnp.dot(gmw_ref[g], vc, preferred_element_type=F32)
            s = jnp.where(grp == g, sg, s)
        m_ref[ci * GM_CHUNK:(ci + 1) * GM_CHUNK, :] = (
            gu[ci * GM_CHUNK:(ci + 1) * GM_CHUNK] * (s + gmb_ref[...])).astype(BF16)

    hy_ref[...] = proj(P_HY, 3 * HY_W).astype(BF16)
    for j in range(3):
        gt_ref[:, j * D_MODEL:(j + 1) * D_MODEL] = jax.nn.sigmoid(
            proj(P_GT + j * D_MODEL, D_MODEL)).astype(BF16)


def _premix(x, shift, scale, lw, rope_tabs, seq, tm):
    n = x.shape[0]
    tiles_per_seq = seq // tm
    batch = n // seq
    use_rope = rope_tabs is not None
    const = lambda i: (0, 0)
    in_specs = [
        pl.BlockSpec((tm, D_MODEL), lambda i: (i, 0)),
        pl.BlockSpec((1, 1, D_MODEL), lambda i: (i // tiles_per_seq, 0, 0)),
        pl.BlockSpec((1, 1, D_MODEL), lambda i: (i // tiles_per_seq, 0, 0)),
        pl.BlockSpec((1, D_MODEL), const),
        pl.BlockSpec((D_MODEL, P_W), const, pipeline_mode=pl.Buffered(1)),
        pl.BlockSpec((1, Q_RANK), const),
        pl.BlockSpec((Q_RANK, HEADS * HEAD_PAD), const),
        pl.BlockSpec((1, HEAD_PAD), const),
        pl.BlockSpec((1, KV_RANK), const),
        pl.BlockSpec((KV_RANK, HEADS * HEAD_PAD), const),
        pl.BlockSpec((HEADS * VDIM, KV_RANK), const),
        pl.BlockSpec((1, HEAD_PAD), const),
        pl.BlockSpec((1, GM_W), const),
        pl.BlockSpec((GM_GROUPS, GM_CHUNK, GM_CHUNK), lambda i: (0, 0, 0)),
        pl.BlockSpec((GM_CHUNK, GM_W), const),
        pl.BlockSpec((1, HEAD_PAD), const),
        pl.BlockSpec((1, HEAD_PAD), const),
    ]
    args = [x, shift, scale, lw['g1'], lw['w_in'], lw['g_qa'], lw['w_uq'], lw['g_qn'], lw['g_kva'], lw['w_k'],
            lw['w_vt'], lw['g_kn'], lw['gm_g'], lw['gm_w'], lw['gm_b'], lw['q_pad'], lw['k_pad']]
    tkv = min(seq, ATTN_TK)
    sub = tkv // tm
    if use_rope:
        in_specs += [pl.BlockSpec((tm, HEAD_PAD), lambda i: (i % tiles_per_seq, 0))] * 3
        args += list(rope_tabs)
    out_shape = (
        jax.ShapeDtypeStruct((n, HEADS * HEAD_PAD), BF16),
        jax.ShapeDtypeStruct((n, HEADS * HEAD_PAD), BF16),
        jax.ShapeDtypeStruct((batch, seq // tkv, HEADS * V_ROWS, tkv), BF16),
        jax.ShapeDtypeStruct((n, GM_W), BF16),
        jax.ShapeDtypeStruct((n, 3 * HY_W), BF16),
        jax.ShapeDtypeStruct((n, 3 * D_MODEL), BF16),
    )
    out_specs = (
        pl.BlockSpec((tm, HEADS * HEAD_PAD), lambda i: (i, 0)),
        pl.BlockSpec((tm, HEADS * HEAD_PAD), lambda i: (i, 0)),
        pl.BlockSpec((1, 1, HEADS * V_ROWS, tm),
                     lambda i: (i // tiles_per_seq, (i % tiles_per_seq) // sub, 0, (i % tiles_per_seq) % sub)),
        pl.BlockSpec((tm, GM_W), lambda i: (i, 0)),
        pl.BlockSpec((tm, 3 * HY_W), lambda i: (i, 0)),
        pl.BlockSpec((tm, 3 * D_MODEL), lambda i: (i, 0)),
    )
    return pl.pallas_call(
        functools.partial(_premix_kernel, use_rope=use_rope, tm=tm),
        out_shape=out_shape, grid=(n // tm,), in_specs=in_specs, out_specs=out_specs,
        compiler_params=_cparams(("arbitrary",)), name="premix",
    )(*args)


def _attn_kernel(*refs, n_chunks, has_ctx, tq):
    if has_ctx:
        q_ref, k_ref, vt_ref, kc_ref, vtc_ref, o_ref = refs
    else:
        q_ref, k_ref, vt_ref, o_ref = refs
    tk = vt_ref.shape[-1]

    def step(hh, kc, vtc, m, acc):
        s = _nt_dot(kc, q_ref[:, hh * HEAD_PAD:(hh + 1) * HEAD_PAD])
        m_new = jnp.maximum(m, jnp.max(s, axis=0, keepdims=True))
        p = jnp.exp2(s - m_new).astype(BF16)
        alpha = jnp.exp2(m - m_new)
        return m_new, acc * alpha + jnp.dot(vtc, p, preferred_element_type=F32)

    def body(i, carry):
        out = []
        for hh in range(2):
            m, acc = carry[hh]
            start = pl.multiple_of(i * tk, tk)
            kc = k_ref[pl.ds(start, tk), hh * HEAD_PAD:(hh + 1) * HEAD_PAD]
            vtc = vt_ref[0, i, hh * V_ROWS:(hh + 1) * V_ROWS, :]
            out.append(step(hh, kc, vtc, m, acc))
        return tuple(out)

    init = tuple((jnp.full((1, tq), NEG_BIG, F32), jnp.zeros((V_ROWS, tq), F32)) for _ in range(2))
    carry = lax.fori_loop(0, n_chunks, body, init)
    outs = []
    for hh in range(2):
        m, acc = carry[hh]
        if has_ctx:
            m, acc = step(hh, kc_ref[:, hh * HEAD_PAD:(hh + 1) * HEAD_PAD],
                          vtc_ref[0, 0, hh * V_ROWS:(hh + 1) * V_ROWS, :], m, acc)
        outs.append(acc[:VDIM] / acc[VDIM:VDIM + 1])
    o_ref[...] = jnp.concatenate(outs, axis=0).T.astype(BF16)


def _attn_bounded_kernel(*refs, n_chunks, has_ctx, tq):
    if has_ctx:
        q_ref, k_ref, vt_ref, kc_ref, vtc_ref, o_ref = refs
    else:
        q_ref, k_ref, vt_ref, o_ref = refs
    tk = vt_ref.shape[-1]

    def step(hh, kc, vtc, acc):
        s = _nt_dot(kc, q_ref[:, hh * HEAD_PAD:(hh + 1) * HEAD_PAD])
        return acc + jnp.dot(vtc, jnp.exp2(s).astype(BF16), preferred_element_type=F32)

    def body(i, carry):
        start = pl.multiple_of(i * tk, tk)
        return tuple(step(hh, k_ref[pl.ds(start, tk), hh * HEAD_PAD:(hh + 1) * HEAD_PAD],
                          vt_ref[0, i, hh * V_ROWS:(hh + 1) * V_ROWS, :], carry[hh]) for hh in range(2))

    carry = lax.fori_loop(0, n_chunks, body, tuple(jnp.zeros((V_ROWS, tq), F32) for _ in range(2)), unroll=True)
    outs = []
    for hh in range(2):
        acc = carry[hh]
        if has_ctx:
            acc = step(hh, kc_ref[:, hh * HEAD_PAD:(hh + 1) * HEAD_PAD],
                       vtc_ref[0, 0, hh * V_ROWS:(hh + 1) * V_ROWS, :], acc)
        outs.append(acc[:VDIM] / acc[VDIM:VDIM + 1])
    o_ref[...] = jnp.concatenate(outs, axis=0).T.astype(BF16)


def _attention(q, k, vt, ctx_kv, seq, tq, bound):
    n = q.shape[0]
    batch = n // seq
    n_chunks, tk = vt.shape[1], vt.shape[3]
    q_tiles = seq // tq
    has_ctx = ctx_kv is not None
    in_specs = [
        pl.BlockSpec((tq, 2 * HEAD_PAD), lambda b, j, i: (b * q_tiles + i, j)),
        pl.BlockSpec((seq, 2 * HEAD_PAD), lambda b, j, i: (b, j)),
        pl.BlockSpec((1, n_chunks, 2 * V_ROWS, tk), lambda b, j, i: (b, 0, j, 0)),
    ]
    args = [q, k, vt]
    if has_ctx:
        kc, vtc = ctx_kv
        lc = vtc.shape[3]
        in_specs += [pl.BlockSpec((lc, 2 * HEAD_PAD), lambda b, j, i: (b, j)),
                     pl.BlockSpec((1, 1, 2 * V_ROWS, lc), lambda b, j, i: (b, 0, j, 0))]
        args += [kc, vtc]
    def call(body, name):
        return pl.pallas_call(
            functools.partial(body, n_chunks=n_chunks, has_ctx=has_ctx, tq=tq),
            out_shape=jax.ShapeDtypeStruct((n, HEADS * VDIM), BF16),
            grid=(batch, HEADS // 2, q_tiles),
            in_specs=in_specs,
            out_specs=pl.BlockSpec((tq, 2 * VDIM), lambda b, j, i: (b * q_tiles + i, j)),
            compiler_params=_cparams(("arbitrary", "arbitrary", "arbitrary")),
            name=name,
        )(*args)

    return lax.cond(bound < ATTN_BOUND_MAX,
                    lambda: call(_attn_bounded_kernel, "attn_bounded"),
                    lambda: call(_attn_kernel, "attn_online"))


def _hy_filter_kernel(zf_ref, zr_ref, w1_ref, b1_ref, w2_ref, b2_ref, w3_ref, b3_ref, fr_ref, dec_ref,
                      f_ref, ss_ref):
    fr = fr_ref[...]

    def taps(z, d):
        hdn = jnp.sin(fr * (jnp.dot(z, w1_ref[...], preferred_element_type=F32, precision=HIGHEST) + b1_ref[...]))
        hdn = jnp.sin(fr * (jnp.dot(hdn, w2_ref[...], preferred_element_type=F32, precision=HIGHEST)
                            + b2_ref[...]))
        cols = slice(d * HY_W, (d + 1) * HY_W)
        k = jnp.dot(hdn, w3_ref[:, cols], preferred_element_type=F32, precision=HIGHEST) + b3_ref[:, cols]
        return k * jnp.exp(-z[:, 0:1] * jnp.abs(dec_ref[:, cols]))

    zf = zf_ref[...]
    k0 = taps(zf, 0)
    k1 = taps(zr_ref[...], 1)
    first_block = pl.program_id(0) == 0
    row = lax.broadcasted_iota(jnp.int32, k1.shape, 0)
    k1 = jnp.where(first_block & (row == 0), 0.0, k1)
    f_ref[0] = k0.astype(BF16)
    f_ref[1] = k1.astype(BF16)
    k1_0 = taps(zf[0:8], 1)[0:1]
    ss1 = jnp.sum(k1 * k1, axis=0, keepdims=True) + jnp.where(first_block, k1_0 * k1_0, 0.0)
    ss_ref[0] = jnp.concatenate([jnp.sum(k0 * k0, axis=0, keepdims=True), ss1], axis=1)


def _hy_filters(z, lw, tr):
    seq, emb = z.shape
    nblk = seq // tr
    hid = lw['hy_w2'].shape[0]
    const = lambda i: (0, 0)
    z_rev = jnp.concatenate([z[0:1], z[:0:-1]], axis=0)
    return pl.pallas_call(
        _hy_filter_kernel,
        out_shape=(jax.ShapeDtypeStruct((2, seq, HY_W), BF16), jax.ShapeDtypeStruct((nblk, 1, 2 * HY_W), F32)),
        grid=(nblk,),
        in_specs=[pl.BlockSpec((tr, emb), lambda i: (i, 0)), pl.BlockSpec((tr, emb), lambda i: (i, 0)),
                  pl.BlockSpec((emb, hid), const), pl.BlockSpec((1, hid), const),
                  pl.BlockSpec((hid, hid), const), pl.BlockSpec((1, hid), const),
                  pl.BlockSpec((hid, 2 * HY_W), const), pl.BlockSpec((1, 2 * HY_W), const),
                  pl.BlockSpec((1, hid), const), pl.BlockSpec((1, 2 * HY_W), const)],
        out_specs=(pl.BlockSpec((2, tr, HY_W), lambda i: (0, i, 0)),
                   pl.BlockSpec((1, 1, 2 * HY_W), lambda i: (i, 0, 0))),
        compiler_params=_cparams(("arbitrary",)), name="hy_filter",
    )(z, z_rev, lw['hy_w1'], lw['hy_b1'], lw['hy_w2'], lw['hy_b2'], lw['hy_w3'], lw['hy_b3'], lw['hy_freq'],
      lw['hy_decay'])


def _hy_conv3_kernel(p_ref, prev_ref, next_ref, w_ref, b_ref, x0_ref, u_ref, *, tiles_per_seq, tr):
    i = pl.program_id(0)
    p = p_ref[...].astype(F32)
    first = (i % tiles_per_seq) == 0
    last = (i % tiles_per_seq) == tiles_per_seq - 1
    prev_row = jnp.where(first, 0.0, prev_ref[...].astype(F32)[15:16, :])
    next_row = jnp.where(last, 0.0, next_ref[...].astype(F32)[0:1, :])
    row = lax.broadcasted_iota(jnp.int32, p.shape, 0)
    p_prev = jnp.where(row == 0, prev_row, pltpu.roll(p, 1, 0))
    p_next = jnp.where(row == tr - 1, next_row, pltpu.roll(p, tr - 1, 0))
    z = b_ref[...] + p_prev * w_ref[0:1, :] + p * w_ref[1:2, :] + p_next * w_ref[2:3, :]
    x0_ref[...] = z[:, :HY_W].astype(BF16)
    u_ref[...] = (z[:, 2 * HY_W:] * z[:, HY_W:2 * HY_W]).astype(BF16)


def _hy_conv3(phy, conv_w, conv_b, seq, tr):
    n = phy.shape[0]
    tiles_per_seq = seq // tr
    hb = tr // 16
    nhb = n // 16
    return pl.pallas_call(
        functools.partial(_hy_conv3_kernel, tiles_per_seq=tiles_per_seq, tr=tr),
        out_shape=(jax.ShapeDtypeStruct((n, HY_W), BF16), jax.ShapeDtypeStruct((n, HY_W), BF16)),
        grid=(n // tr,),
        in_specs=[pl.BlockSpec((tr, 3 * HY_W), lambda i: (i, 0)),
                  pl.BlockSpec((16, 3 * HY_W), lambda i: (jnp.maximum(i * hb - 1, 0), 0)),
                  pl.BlockSpec((16, 3 * HY_W), lambda i: (jnp.minimum((i + 1) * hb, nhb - 1), 0)),
                  pl.BlockSpec((3, 3 * HY_W), lambda i: (0, 0)),
                  pl.BlockSpec((1, 3 * HY_W), lambda i: (0, 0))],
        out_specs=(pl.BlockSpec((tr, HY_W), lambda i: (i, 0)), pl.BlockSpec((tr, HY_W), lambda i: (i, 0))),
        compiler_params=_cparams(("arbitrary",)), name="hy_conv3",
    )(phy, phy, phy, conv_w, conv_b)


def _dft_outer_kernel(wr_ref, wi_ref, u_ref, ar_ref, ai_ref):
    u = u_ref[0]
    ar_ref[0] = jnp.dot(wr_ref[...], u, preferred_element_type=F32).astype(BF16)
    ai_ref[0] = jnp.dot(wi_ref[...], u, preferred_element_type=F32).astype(BF16)


def _dft_outer(wr, wi, u, tc):
    nb, kk, cols = u.shape
    n1 = wr.shape[0]
    return pl.pallas_call(
        _dft_outer_kernel,
        out_shape=(jax.ShapeDtypeStruct((nb, n1, cols), BF16),) * 2,
        grid=(nb, cols // tc),
        in_specs=[pl.BlockSpec((n1, kk), lambda b, j: (0, 0)), pl.BlockSpec((n1, kk), lambda b, j: (0, 0)),
                  pl.BlockSpec((1, kk, tc), lambda b, j: (b, 0, j))],
        out_specs=(pl.BlockSpec((1, n1, tc), lambda b, j: (b, 0, j)),) * 2,
        compiler_params=_cparams(("arbitrary", "arbitrary")), name="dft_outer",
    )(wr, wi, u)


def _cdot(mr, mi, xr, xi):
    rr = jnp.dot(mr, xr, preferred_element_type=F32) - jnp.dot(mi, xi, preferred_element_type=F32)
    ri = jnp.dot(mr, xi, preferred_element_type=F32) + jnp.dot(mi, xr, preferred_element_type=F32)
    return rr, ri


def _dft_inner_fwd_kernel(mr_ref, mi_ref, ar_ref, ai_ref, fr_ref, fi_ref):
    fr, fi = _cdot(mr_ref[0], mi_ref[0], ar_ref[0, 0], ai_ref[0, 0])
    fr_ref[0] = fr.astype(BF16)
    fi_ref[0] = fi.astype(BF16)


def _dft_inner_fwd(mr, mi, ar, ai):
    n1, n2, _ = mr.shape
    ch = ar.shape[-1]
    mspec = pl.BlockSpec((1, n2, n2), lambda k: (k, 0, 0))
    aspec = pl.BlockSpec((1, 1, n2, ch), lambda k: (0, k, 0, 0))
    ospec = pl.BlockSpec((1, n2, ch), lambda k: (k, 0, 0))
    return pl.pallas_call(
        _dft_inner_fwd_kernel,
        out_shape=(jax.ShapeDtypeStruct((n1, n2, ch), BF16),) * 2,
        grid=(n1,), in_specs=[mspec, mspec, aspec, aspec], out_specs=(ospec, ospec),
        compiler_params=_cparams(("arbitrary",)), name="dft_inner_fwd",
    )(mr, mi, ar, ai)


def _dft_inner_conv_kernel(mr_ref, mi_ref, tr_ref, ti_ref, ar_ref, ai_ref, fr_ref, fi_ref, br_ref, bi_ref, *, nb):
    fr = fr_ref[0].astype(F32)
    fi = fi_ref[0].astype(F32)
    for b in range(nb):
        ur, ui = _cdot(mr_ref[0], mi_ref[0], ar_ref[b, 0], ai_ref[b, 0])
        yr = (ur * fr - ui * fi).astype(BF16)
        yi = (ur * fi + ui * fr).astype(BF16)
        br, bi = _cdot(tr_ref[0], ti_ref[0], yr, yi)
        br_ref[b, 0] = br.astype(BF16)
        bi_ref[b, 0] = bi.astype(BF16)


def _dft_inner_conv(mr, mi, tr, ti, ar, ai, fr, fi):
    n1, n2, _ = mr.shape
    nb, _, _, ch = ar.shape
    mspec = pl.BlockSpec((1, n2, n2), lambda k: (k, 0, 0))
    aspec = pl.BlockSpec((nb, 1, n2, ch), lambda k: (0, k, 0, 0))
    fspec = pl.BlockSpec((1, n2, ch), lambda k: (k, 0, 0))
    return pl.pallas_call(
        functools.partial(_dft_inner_conv_kernel, nb=nb),
        out_shape=(jax.ShapeDtypeStruct((nb, n1, n2, ch), BF16),) * 2,
        grid=(n1,), in_specs=[mspec, mspec, mspec, mspec, aspec, aspec, fspec, fspec],
        out_specs=(aspec, aspec),
        compiler_params=_cparams(("arbitrary",)), name="dft_inner_conv",
    )(mr, mi, tr, ti, ar, ai, fr, fi)


def _filter_scale(ss_ref):
    ss = jnp.sum(ss_ref[...], axis=0)
    return lax.rsqrt(ss[:, :HY_W] + ss[:, HY_W:] + EPS)


def _hy_final_kernel(cr_ref, ci_ref, br_ref, bi_ref, x0_ref, u_ref, ss_ref, bias_ref, o_ref, *, reps):
    y = (jnp.dot(cr_ref[...], br_ref[0], preferred_element_type=F32)
         + jnp.dot(ci_ref[...], bi_ref[0], preferred_element_type=F32))
    scale = jnp.tile(_filter_scale(ss_ref), (1, reps))
    bias = jnp.tile(bias_ref[...], (1, reps))
    u = u_ref[0].astype(F32)
    o_ref[0] = (x0_ref[0].astype(F32) * (y * scale + u * bias)).astype(BF16)


def _hy_final(cr, ci, br, bi, x0, u, ss, bias, tc):
    nb, n1, cols = br.shape
    n1h = cr.shape[0]
    nblk = ss.shape[0]
    cspec = pl.BlockSpec((n1h, n1), lambda b, j: (0, 0))
    bspec = pl.BlockSpec((1, n1, tc), lambda b, j: (b, 0, j))
    xspec = pl.BlockSpec((1, n1h, tc), lambda b, j: (b, 0, j))
    return pl.pallas_call(
        functools.partial(_hy_final_kernel, reps=tc // HY_W),
        out_shape=jax.ShapeDtypeStruct((nb, n1h, cols), BF16),
        grid=(nb, cols // tc),
        in_specs=[cspec, cspec, bspec, bspec, xspec, xspec,
                  pl.BlockSpec((nblk, 1, 2 * HY_W), lambda b, j: (0, 0, 0)),
                  pl.BlockSpec((1, HY_W), lambda b, j: (0, 0))],
        out_specs=xspec,
        compiler_params=_cparams(("arbitrary", "arbitrary")), name="hy_final",
    )(cr, ci, br, bi, x0, u, ss, bias)


def _hy_direct_kernel(f_ref, x0_ref, u_ref, ss_ref, bias_ref, o_ref, kk_ref, *, seq):
    kk_ref[0:seq, :] = f_ref[1].astype(F32)
    kk_ref[seq:2 * seq, :] = f_ref[0].astype(F32)
    u = u_ref[0].astype(F32)
    y = jnp.zeros((seq, HY_W), F32)
    for j in range(seq):
        y = y + kk_ref[seq - j:2 * seq - j, :] * u[j:j + 1, :]
    o_ref[0] = (x0_ref[0].astype(F32) * (y * _filter_scale(ss_ref) + u * bias_ref[...])).astype(BF16)


def _hy_direct(f, x0, u, ss, bias):
    nb, seq, _ = u.shape
    nblk = ss.shape[0]
    xspec = pl.BlockSpec((1, seq, HY_W), lambda b: (b, 0, 0))
    return pl.pallas_call(
        functools.partial(_hy_direct_kernel, seq=seq),
        out_shape=jax.ShapeDtypeStruct((nb, seq, HY_W), BF16),
        grid=(nb,),
        in_specs=[pl.BlockSpec((2, seq, HY_W), lambda b: (0, 0, 0)), xspec, xspec,
                  pl.BlockSpec((nblk, 1, 2 * HY_W), lambda b: (0, 0, 0)),
                  pl.BlockSpec((1, HY_W), lambda b: (0, 0))],
        out_specs=xspec,
        scratch_shapes=[pltpu.VMEM((2 * seq, HY_W), F32)],
        compiler_params=_cparams(("arbitrary",)), name="hy_direct",
    )(f, x0, u, ss, bias)


def _hy_embedding(seq):
    t = jnp.arange(seq, dtype=F32)
    t_unit = t / max(seq - 1, 1)
    bands = jnp.linspace(1e-4, HY_BANDS - 1, HY_BANDS, dtype=F32)
    ang = (2 * jnp.pi / seq) * t[:, None] * bands[None, :]
    return jnp.concatenate([t_unit[:, None], jnp.cos(ang), -jnp.sin(ang)], axis=-1)


def _dft_tables(seq):
    n = 2 * seq
    n2 = DFT_N2
    n1 = n // n2
    a = jnp.arange(n1, dtype=jnp.int32)
    th1 = (2 * jnp.pi / n1) * ((a[:, None] * a[None, :]) % n1).astype(F32)
    w1r, w1i = jnp.cos(th1), -jnp.sin(th1)
    b = jnp.arange(n2, dtype=jnp.int32)
    tha = (2 * jnp.pi / n) * (a[:, None] * b[None, :]).astype(F32)
    thb = (2 * jnp.pi / n2) * ((b[:, None] * b[None, :]) % n2).astype(F32)
    ar, ai = jnp.cos(tha), -jnp.sin(tha)
    br, bi = jnp.cos(thb), -jnp.sin(thb)
    mr = ar[:, None, :] * br[None] - ai[:, None, :] * bi[None]
    mi = ar[:, None, :] * bi[None] + ai[:, None, :] * br[None]
    tr = ar[:, :, None] * br[None] - ai[:, :, None] * bi[None]
    ti = -(ar[:, :, None] * bi[None] + ai[:, :, None] * br[None])
    return dict(
        w1r=w1r.astype(BF16), w1i=w1i.astype(BF16),
        mr=mr.astype(BF16), mi=mi.astype(BF16), tr=tr.astype(BF16), ti=ti.astype(BF16),
        cr=(w1r[:n1 // 2] / n).astype(BF16), ci=(w1i[:n1 // 2] / n).astype(BF16))


def _hyena_long(phy, lw, filt, seq, tabs):
    n = phy.shape[0]
    nb = n // seq
    n2 = DFT_N2
    n1 = 2 * seq // n2
    f, ss = filt
    x0, u = _hy_conv3(phy, lw['hy_conv_w'], lw['hy_conv_b'], seq, min(seq, 1024))
    tc = min(n2 * HY_W, 8192)
    far, fai = _dft_outer(tabs['w1r'], tabs['w1i'], f.reshape(1, n1, n2 * HY_W), tc)
    fr, fi = _dft_inner_fwd(tabs['mr'], tabs['mi'], far.reshape(1, n1, n2, HY_W), fai.reshape(1, n1, n2, HY_W))
    ar, ai = _dft_outer(tabs['w1r'][:, :n1 // 2], tabs['w1i'][:, :n1 // 2], u.reshape(nb, n1 // 2, n2 * HY_W), tc)
    br, bi = _dft_inner_conv(tabs['mr'], tabs['mi'], tabs['tr'], tabs['ti'],
                             ar.reshape(nb, n1, n2, HY_W), ai.reshape(nb, n1, n2, HY_W), fr, fi)
    out = _hy_final(tabs['cr'], tabs['ci'], br.reshape(nb, n1, n2 * HY_W), bi.reshape(nb, n1, n2 * HY_W),
                    x0.reshape(nb, n1 // 2, n2 * HY_W), u.reshape(nb, n1 // 2, n2 * HY_W), ss, lw['hy_bias'], tc)
    return out.reshape(n, HY_W)


def _hyena_short(phy, lw, filt, seq):
    n = phy.shape[0]
    nb = n // seq
    f, ss = filt
    x0, u = _hy_conv3(phy, lw['hy_conv_w'], lw['hy_conv_b'], seq, seq)
    out = _hy_direct(f, x0.reshape(nb, seq, HY_W), u.reshape(nb, seq, HY_W), ss, lw['hy_bias'])
    return out.reshape(n, HY_W)


def _pair_top2_sum(a, b, c, d):
    return jnp.maximum(jnp.maximum(jnp.maximum(a + b, a + c), jnp.maximum(a + d, b + c)),
                       jnp.maximum(b + d, c + d))


def _route(logits_t, rb):
    aff = jax.nn.sigmoid(logits_t)
    sel = aff + rb
    rows = [sel[e:e + 1, :] for e in range(N_EXPERTS)]
    affr = [aff[e:e + 1, :] for e in range(N_EXPERTS)]
    best, bidx = None, None
    for g in range(N_GROUPS):
        gs = _pair_top2_sum(*rows[g * EXP_PER_GROUP:(g + 1) * EXP_PER_GROUP])
        if g == 0:
            best, bidx = gs, jnp.zeros(gs.shape, jnp.int32)
        else:
            upd = gs > best
            bidx = jnp.where(upd, g, bidx)
            best = jnp.where(upd, gs, best)
    vals = [jnp.where(bidx == e // EXP_PER_GROUP, rows[e], -jnp.inf) for e in range(N_EXPERTS)]
    m1, i1 = vals[0], jnp.zeros(best.shape, jnp.int32)
    for e in range(1, N_EXPERTS):
        upd = vals[e] > m1
        i1 = jnp.where(upd, e, i1)
        m1 = jnp.where(upd, vals[e], m1)
    m2, i2 = jnp.full(best.shape, -jnp.inf, F32), jnp.zeros(best.shape, jnp.int32)
    for e in range(N_EXPERTS):
        cand = jnp.where(i1 == e, -jnp.inf, vals[e])
        upd = cand > m2
        i2 = jnp.where(upd, e, i2)
        m2 = jnp.where(upd, cand, m2)
    a1 = sum(jnp.where(i1 == e, affr[e], 0.0) for e in range(N_EXPERTS))
    a2 = sum(jnp.where(i2 == e, affr[e], 0.0) for e in range(N_EXPERTS))
    inv = 1.0 / (a1 + a2)
    return jnp.concatenate([i1, i2], axis=0), jnp.concatenate([a1 * inv, a2 * inv], axis=0)


def _merge_kernel(a_ref, b_ref, m_ref, gt_ref, x_ref, gate_ref, shift_ref, scale_ref, g2_ref,
                  wpa_ref, wpb_ref, wpc_ref, wout_ref, rwt_ref, rb_ref, xo_ref, h2_ref, idx_ref, wts_ref):
    y = gt_ref[:, 0:D_MODEL].astype(F32) * jnp.dot(a_ref[...], wpa_ref[...], preferred_element_type=F32)
    y = y + gt_ref[:, D_MODEL:2 * D_MODEL].astype(F32) * jnp.dot(b_ref[...], wpb_ref[...],
                                                                 preferred_element_type=F32)
    y = y + gt_ref[:, 2 * D_MODEL:].astype(F32) * jnp.dot(m_ref[...], wpc_ref[...], preferred_element_type=F32)
    y2 = jnp.dot(y.astype(BF16), wout_ref[...], preferred_element_type=F32)
    xn = x_ref[...] + gate_ref[0] * y2
    xo_ref[...] = xn
    h2 = _rms(xn) * g2_ref[...]
    h2 = h2 * (1.0 + scale_ref[0]) + shift_ref[0]
    h2_ref[...] = h2
    logits_t = lax.dot_general(rwt_ref[...], h2, (((1,), (1,)), ((), ())), preferred_element_type=F32,
                               precision=HIGHEST)
    idx_ref[...], wts_ref[...] = _route(logits_t, rb_ref[...])


def _merge(a, b, m, gt, x, gate, shift, scale, lw, rw_t, rb, seq, tm):
    n = x.shape[0]
    tiles_per_seq = seq // tm
    const = lambda i: (0, 0)
    row = lambda w: pl.BlockSpec((tm, w), lambda i: (i, 0))
    vec = pl.BlockSpec((1, 1, D_MODEL), lambda i: (i // tiles_per_seq, 0, 0))
    return pl.pallas_call(
        _merge_kernel,
        out_shape=(jax.ShapeDtypeStruct((n, D_MODEL), F32), jax.ShapeDtypeStruct((n, D_MODEL), F32),
                   jax.ShapeDtypeStruct((TOP_K, n), jnp.int32), jax.ShapeDtypeStruct((TOP_K, n), F32)),
        grid=(n // tm,),
        in_specs=[row(HEADS * VDIM), row(HY_W), row(GM_W), row(3 * D_MODEL), row(D_MODEL), vec, vec, vec,
                  pl.BlockSpec((1, D_MODEL), const),
                  pl.BlockSpec((HEADS * VDIM, D_MODEL), const), pl.BlockSpec((HY_W, D_MODEL), const),
                  pl.BlockSpec((GM_W, D_MODEL), const), pl.BlockSpec((D_MODEL, D_MODEL), const),
                  pl.BlockSpec((N_EXPERTS, D_MODEL), const), pl.BlockSpec((N_EXPERTS, 1), const)],
        out_specs=(row(D_MODEL), row(D_MODEL), pl.BlockSpec((TOP_K, tm), lambda i: (0, i)),
                   pl.BlockSpec((TOP_K, tm), lambda i: (0, i))),
        compiler_params=_cparams(("arbitrary",)), name="merge",
    )(a, b, m, gt, x, gate, shift, scale, lw['g2'], lw['w_pa'], lw['w_pb'], lw['w_pc'], lw['w_out'], rw_t, rb)


def _rows_copy(src_hbm, dst, sem, idx_ref, rows):
    return [pltpu.make_async_copy(src_hbm.at[pl.ds(idx_ref[0, 0, r], 1)], dst.at[pl.ds(r, 1)], sem)
            for r in range(rows)]


def _rows_wait(src_hbm, dst, sem, rows):
    pltpu.make_async_copy(src_hbm.at[pl.ds(0, rows)], dst, sem).wait()


def _gather_pipeline(src_hbm, cur_ref, nxt_ref, buf, sem, rows):
    i = pl.program_id(0)
    slot = i % 2

    @pl.when(i == 0)
    def _():
        for cp in _rows_copy(src_hbm, buf.at[0], sem.at[0], cur_ref, rows):
            cp.start()

    _rows_wait(src_hbm, buf.at[slot], sem.at[slot], rows)
    for cp in _rows_copy(src_hbm, buf.at[1 - slot], sem.at[1 - slot], nxt_ref, rows):
        cp.start()
    return slot


def _gather_drain(src_hbm, buf, sem, rows):
    i = pl.program_id(0)

    @pl.when(i == pl.num_programs(0) - 1)
    def _():
        _rows_wait(src_hbm, buf.at[1 - i % 2], sem.at[1 - i % 2], rows)


def _moe_dispatch_kernel(pos_ref, h_ref, xs_in_hbm, xs_hbm, sem, *, tc):
    del xs_in_hbm
    for r in range(tc):
        pltpu.make_async_copy(h_ref.at[pl.ds(r, 1)], xs_hbm.at[pl.ds(pos_ref[0, 0, r], 1)], sem).start()
    pltpu.make_async_copy(h_ref, xs_hbm.at[pl.ds(0, tc)], sem).wait()


def _moe_dispatch(pos_tiles, h2, rows, tc):
    n = h2.shape[0]
    return pl.pallas_call(
        functools.partial(_moe_dispatch_kernel, tc=tc),
        out_shape=jax.ShapeDtypeStruct((rows, D_MODEL), F32),
        grid=(n // tc,),
        in_specs=[pl.BlockSpec((1, 1, tc), lambda i: (i, 0, 0), memory_space=pltpu.SMEM),
                  pl.BlockSpec((tc, D_MODEL), lambda i: (i, 0)),
                  pl.BlockSpec(memory_space=pl.ANY)],
        out_specs=pl.BlockSpec(memory_space=pl.ANY),
        scratch_shapes=[pltpu.SemaphoreType.DMA],
        input_output_aliases={2: 0},
        compiler_params=_cparams(("arbitrary",)), name="moe_dispatch",
    )(pos_tiles, h2, jnp.zeros((rows, D_MODEL), F32))


def _moe_ffn_kernel(ea_ref, eb_ref, x_ref, wgua_ref, wda_ref, wgub_ref, wdb_ref, y_ref):
    del ea_ref, eb_ref
    xb = x_ref[...].astype(BF16)
    for j, (wgu_ref, wd_ref) in enumerate(((wgua_ref, wda_ref), (wgub_ref, wdb_ref))):
        gu = jnp.dot(xb, wgu_ref[0], preferred_element_type=F32)
        g = gu[:, :D_EXPERT]
        hid = (g * jax.nn.sigmoid(g) * gu[:, D_EXPERT:]).astype(BF16)
        y_ref[:, j * D_MODEL:(j + 1) * D_MODEL] = jnp.dot(hid, wd_ref[0], preferred_element_type=F32)


def _moe_ffn(blk_ea, blk_eb, x_sorted, lw):
    nblk = blk_ea.shape[0]
    wgu_spec = lambda sel: pl.BlockSpec((1, D_MODEL, 2 * D_EXPERT), lambda i, ea, eb: (sel(ea, eb)[i], 0, 0))
    wd_spec = lambda sel: pl.BlockSpec((1, D_EXPERT, D_MODEL), lambda i, ea, eb: (sel(ea, eb)[i], 0, 0))
    first, second = (lambda ea, eb: ea), (lambda ea, eb: eb)
    return pl.pallas_call(
        _moe_ffn_kernel,
        out_shape=jax.ShapeDtypeStruct((nblk * MOE_BM, TOP_K * D_MODEL), F32),
        grid_spec=pltpu.PrefetchScalarGridSpec(
            num_scalar_prefetch=2, grid=(nblk,),
            in_specs=[pl.BlockSpec((MOE_BM, D_MODEL), lambda i, ea, eb: (i, 0)),
                      wgu_spec(first), wd_spec(first), wgu_spec(second), wd_spec(second)],
            out_specs=pl.BlockSpec((MOE_BM, TOP_K * D_MODEL), lambda i, ea, eb: (i, 0))),
        compiler_params=_cparams(("arbitrary",)), name="moe_ffn",
    )(blk_ea, blk_eb, x_sorted, lw['w_gu'], lw['w_d'], lw['w_gu'], lw['w_d'])


def _moe_combine_kernel(cur_ref, nxt_ref, y_hbm, x_ref, w_ref, gate_ref, o_ref, buf, sem, *, tc):
    slot = _gather_pipeline(y_hbm, cur_ref, nxt_ref, buf, sem, tc)
    w = w_ref[...]
    y = w[:, 0:1] * buf[slot, :, 0:D_MODEL] + w[:, 1:2] * buf[slot, :, D_MODEL:]
    o_ref[...] = x_ref[...] + gate_ref[0] * y
    _gather_drain(y_hbm, buf, sem, tc)


def _moe_combine(pos_tiles, y_sorted, x, w_cols, gate, seq, tc):
    n = x.shape[0]
    tiles_per_seq = seq // tc
    idx_spec = lambda off: pl.BlockSpec((1, 1, tc), lambda i: (i + off, 0, 0), memory_space=pltpu.SMEM)
    return pl.pallas_call(
        functools.partial(_moe_combine_kernel, tc=tc),
        out_shape=jax.ShapeDtypeStruct((n, D_MODEL), F32),
        grid=(n // tc,),
        in_specs=[idx_spec(0), idx_spec(1), pl.BlockSpec(memory_space=pl.ANY),
                  pl.BlockSpec((tc, D_MODEL), lambda i: (i, 0)),
                  pl.BlockSpec((tc, TOP_K), lambda i: (i, 0)),
                  pl.BlockSpec((1, 1, D_MODEL), lambda i: (i // tiles_per_seq, 0, 0))],
        out_specs=pl.BlockSpec((tc, D_MODEL), lambda i: (i, 0)),
        scratch_shapes=[pltpu.VMEM((2, tc, TOP_K * D_MODEL), F32), pltpu.SemaphoreType.DMA((2,))],
        compiler_params=_cparams(("arbitrary",)), name="moe_combine",
    )(pos_tiles, pos_tiles, y_sorted, x, w_cols, gate)


def _cumsum_rows(onehot):
    n, c = onehot.shape
    blk = math.gcd(n, 256)
    x = onehot.astype(F32).reshape(n // blk, blk, c)
    tri = (jnp.arange(blk)[:, None] >= jnp.arange(blk)[None, :]).astype(F32)
    within = jnp.einsum('ij,bjc->bic', tri, x, precision=HIGHEST)
    nb = n // blk
    before = (jnp.arange(nb)[:, None] > jnp.arange(nb)[None, :]).astype(F32)
    offset = jnp.dot(before, within[:, -1, :], precision=HIGHEST)
    return (within + offset[:, None, :]).reshape(n, c).astype(jnp.int32)


def _pair_tables():
    pairs = [(a, b) for a in range(EXP_PER_GROUP) for b in range(a + 1, EXP_PER_GROUP)]
    ea = [g * EXP_PER_GROUP + a for g in range(N_GROUPS) for a, _ in pairs]
    eb = [g * EXP_PER_GROUP + b for g in range(N_GROUPS) for _, b in pairs]
    return jnp.array(ea, jnp.int32), jnp.array(eb, jnp.int32)


def _moe(h2, idx, wts, x, gate, lw, seq):
    n = x.shape[0]
    tab_a, tab_b = _pair_tables()
    n_cls = tab_a.shape[0]
    nblk = n // MOE_BM + n_cls
    swap = idx[0] > idx[1]
    e_lo, e_hi = jnp.minimum(idx[0], idx[1]), jnp.maximum(idx[0], idx[1])
    w_cols = jnp.stack([jnp.where(swap, wts[1], wts[0]), jnp.where(swap, wts[0], wts[1])], axis=1)
    onehot = ((e_lo[:, None] == tab_a[None, :]) & (e_hi[:, None] == tab_b[None, :])).astype(jnp.int32)
    csum = _cumsum_rows(onehot)
    counts = csum[-1]
    rank = jnp.sum(csum * onehot, axis=1) - 1
    seg_len = (counts + MOE_BM - 1) // MOE_BM * MOE_BM
    seg_end = jnp.cumsum(seg_len)
    pos = jnp.sum(onehot * (seg_end - seg_len)[None, :], axis=1) + rank
    blk_cls = jnp.minimum(jnp.searchsorted(seg_end, jnp.arange(nblk, dtype=jnp.int32) * MOE_BM, side='right'),
                          n_cls - 1)
    tc = min(seq, MOE_TC)
    pos_tiles = pos.reshape(n // tc, 1, tc)
    x_sorted = _moe_dispatch(pos_tiles, h2, nblk * MOE_BM, tc)
    y_sorted = _moe_ffn(tab_a[blk_cls], tab_b[blk_cls], x_sorted, lw)
    pos_next = jnp.concatenate([pos_tiles, jnp.zeros((1, 1, tc), jnp.int32)], axis=0)
    return _moe_combine(pos_next, y_sorted, x, w_cols, gate, seq, tc)


def _head_perm():
    rope_idx = list(range(NOPE, QK, 2)) + list(range(NOPE + 1, QK, 2))
    return jnp.array(list(range(NOPE)) + rope_idx, dtype=jnp.int32)


def _layer_weights(p, l):
    perm = _head_perm()
    row = lambda v: v.reshape(1, -1)
    w_in = p['w_in'][l]
    kr_perm = jnp.array(list(range(0, ROPE, 2)) + list(range(1, ROPE, 2)), dtype=jnp.int32)
    w_kr = jnp.pad(w_in[:, OFF_KR:OFF_HY][:, kr_perm], ((0, 0), (0, HEAD_PAD - ROPE)))
    w_in2 = jnp.concatenate([w_in[:, OFF_Q:OFF_KR], w_kr, w_in[:, OFF_HY:]], axis=1).astype(BF16)
    w_uq = p['w_uq'][l].reshape(Q_RANK, HEADS, QK)[:, :, perm]
    w_uq = jnp.pad(w_uq, ((0, 0), (0, 0), (0, HEAD_PAD - QK))).reshape(Q_RANK, HEADS * HEAD_PAD).astype(BF16)
    w_ukv = p['w_ukv'][l].reshape(KV_RANK, HEADS, NOPE + VDIM)
    w_k = jnp.pad(w_ukv[:, :, :NOPE], ((0, 0), (0, 0), (0, HEAD_PAD - NOPE)))
    w_k = w_k.reshape(KV_RANK, HEADS * HEAD_PAD).astype(BF16)
    w_vt = w_ukv[:, :, NOPE:].reshape(KV_RANK, HEADS * VDIM).T.astype(BF16)
    pad_gain = lambda g: jnp.pad(g[perm], (0, HEAD_PAD - QK)).reshape(1, HEAD_PAD)
    gm_b = jnp.repeat(p['gm_bs'][l].T, GM_W // GM_GROUPS, axis=1)
    bound = 1.02 * QSCALE * QK * jnp.max(jnp.abs(p['g_qn'][l])) * jnp.max(jnp.abs(p['g_kn'][l]))
    pad_lane = jnp.arange(HEAD_PAD) == QK
    return dict(
        attn_bound=bound,
        q_pad=jnp.where(pad_lane, -bound, 0.0).reshape(1, HEAD_PAD).astype(F32),
        k_pad=jnp.where(pad_lane, 1.0, 0.0).reshape(1, HEAD_PAD).astype(F32),
        g1=row(p['norm1_g'][l]), g2=row(p['norm2_g'][l]), w_in=w_in2,
        g_qa=row(p['g_qa'][l]), w_uq=w_uq, g_qn=pad_gain(p['g_qn'][l]),
        g_kva=row(p['g_kva'][l]), w_k=w_k, w_vt=w_vt, g_kn=pad_gain(p['g_kn'][l]),
        gm_g=row(p['gm_norm_g'][l]), gm_w=p['gm_ws'][l].astype(BF16), gm_b=gm_b,
        hy_conv_w=p['hy_conv_w'][l], hy_conv_b=row(p['hy_conv_b'][l]),
        hy_w1=p['hy_w1'][l], hy_b1=row(p['hy_b1'][l]), hy_w2=p['hy_w2'][l], hy_b2=row(p['hy_b2'][l]),
        hy_w3=p['hy_w3'][l], hy_b3=row(p['hy_b3'][l]), hy_freq=row(p['hy_freq'][l]),
        hy_decay=p['hy_decay'][l].reshape(1, 2 * HY_W), hy_bias=row(p['hy_bias'][l]),
        w_pa=p['w_pa'][l].astype(BF16), w_pb=p['w_pb'][l].astype(BF16), w_pc=p['w_pc'][l].astype(BF16),
        w_out=p['w_out'][l].astype(BF16),
        w_gu=jnp.concatenate([p['moe_w_gate'][l], p['moe_w_up'][l]], axis=-1).astype(BF16),
        w_d=p['moe_w_down'][l].astype(BF16))


def _rope_tables(seq):
    rows = seq // GRID_W
    row = jnp.repeat(jnp.arange(rows, dtype=F32), GRID_W)
    col = jnp.tile(jnp.arange(GRID_W, dtype=F32), rows)
    n_freq = ROPE // 4
    inv = ROPE_THETA ** (-jnp.arange(n_freq, dtype=F32) / n_freq)
    ang = jnp.concatenate([row[:, None] * inv, col[:, None] * inv], axis=-1)
    c, s = jnp.cos(ang), jnp.sin(ang)
    z = lambda w: jnp.zeros((seq, w), F32)
    rc = jnp.concatenate([jnp.ones((seq, NOPE), F32), c, c, z(HEAD_PAD - QK)], axis=1)
    rs1 = jnp.concatenate([z(NOPE), z(ROPE // 2), s, z(HEAD_PAD - QK)], axis=1)
    rs2 = jnp.concatenate([z(NOPE), -s, z(ROPE // 2), z(HEAD_PAD - QK)], axis=1)
    return rc, rs1, rs2


def _mixer_and_ffn(x, mods, lw, rw_t, rb, seq, tm, rope_tabs, ctx_kv, filt, dft_tabs):
    shift1, scale1, gate1, shift2, scale2, gate2 = mods
    q, k, vt, m, phy, gt = _premix(x, shift1, scale1, lw, rope_tabs, seq, tm)
    a = _attention(q, k, vt, ctx_kv, seq, min(seq, ATTN_TQ), lw['attn_bound'])
    if dft_tabs is not None:
        b = _hyena_long(phy, lw, filt, seq, dft_tabs)
    else:
        b = _hyena_short(phy, lw, filt, seq)
    xn, h2, idx, wts = _merge(a, b, m, gt, x, gate1, shift2, scale2, lw, rw_t, rb, seq, tm)
    return _moe(h2, idx, wts, xn, gate2, lw, seq), k, vt


def _forward(p):
    x, ctx = p['x'], p['ctx']
    batch, seq, _ = x.shape
    lc = ctx.shape[1]
    depth = p['w_mod'].shape[0]

    cvecs = jnp.concatenate([p['c'], p['c_ctx'][None], jnp.zeros((8 - batch - 1, D_MODEL), F32)], axis=0)
    mod_all = _modvec(cvecs, p['w_mod'], p['b_mod'])
    rw_t = p['router_w'].T
    rb = p['router_b'].reshape(N_EXPERTS, 1)
    rope_tabs = _rope_tables(seq)
    dft_tabs = _dft_tables(seq)
    z_lat, z_ctx = _hy_embedding(seq), _hy_embedding(lc)

    xl = x.reshape(batch * seq, D_MODEL)
    xc = ctx.reshape(batch * lc, D_MODEL)
    tm_lat = min(seq, 512)
    for l in range(depth):
        lw = _layer_weights(p, l)
        mod = mod_all[l].reshape(8, N_MOD, D_MODEL)
        mods_lat = [mod[:batch, j].reshape(batch, 1, D_MODEL) for j in range(N_MOD)]
        mods_ctx = [jnp.broadcast_to(mod[batch, j].reshape(1, 1, D_MODEL), (batch, 1, D_MODEL))
                    for j in range(N_MOD)]
        if l == depth - 1:
            _, k_c, vt_c, _, _, _ = _premix(xc, mods_ctx[0], mods_ctx[1], lw, None, lc, lc)
        else:
            xc, k_c, vt_c = _mixer_and_ffn(xc, mods_ctx, lw, rw_t, rb, lc, lc, None, None,
                                           _hy_filters(z_ctx, lw, lc), None)
        filt = _hy_filters(z_lat, lw, min(seq, 2048))
        xl, _, _ = _mixer_and_ffn(xl, mods_lat, lw, rw_t, rb, seq, tm_lat, rope_tabs, (k_c, vt_c), filt, dft_tabs)
    return xl.reshape(batch, seq, D_MODEL)


def kernel(x, c, ctx, c_ctx, w_mod, b_mod, norm1_g, norm2_g, w_in, g_qa, w_uq, g_kva, w_ukv, g_qn, g_kn,
           hy_conv_w, hy_conv_b, hy_w1, hy_b1, hy_w2, hy_b2, hy_w3, hy_b3, hy_freq, hy_decay, hy_bias,
           gm_norm_g, gm_ws, gm_bs, w_pa, w_pb, w_pc, w_out, router_w, router_b,
           moe_w_gate, moe_w_up, moe_w_down):
    return _forward(dict(
        x=x, c=c, ctx=ctx, c_ctx=c_ctx, w_mod=w_mod, b_mod=b_mod, norm1_g=norm1_g, norm2_g=norm2_g, w_in=w_in,
        g_qa=g_qa, w_uq=w_uq, g_kva=g_kva, w_ukv=w_ukv, g_qn=g_qn, g_kn=g_kn, hy_conv_w=hy_conv_w,
        hy_conv_b=hy_conv_b, hy_w1=hy_w1, hy_b1=hy_b1, hy_w2=hy_w2, hy_b2=hy_b2, hy_w3=hy_w3, hy_b3=hy_b3,
        hy_freq=hy_freq, hy_decay=hy_decay, hy_bias=hy_bias, gm_norm_g=gm_norm_g, gm_ws=gm_ws, gm_bs=gm_bs,
        w_pa=w_pa, w_pb=w_pb, w_pc=w_pc, w_out=w_out, router_w=router_w, router_b=router_b,
        moe_w_gate=moe_w_gate, moe_w_up=moe_w_up, moe_w_down=moe_w_down))
```

```python
import functools
import math

import jax
import jax.numpy as jnp
from jax import lax
from jax.experimental import pallas as pl
from jax.experimental.pallas import tpu as pltpu

F32 = jnp.float32
BF16 = jnp.bfloat16
HIGHEST = lax.Precision.HIGHEST

D_MODEL = 1024
GRID_W = 64
EPS = 1e-6
N_MOD = 6

HEADS = 8
Q_RANK = 384
KV_RANK = 256
NOPE = 64
ROPE = 32
QK = NOPE + ROPE
VDIM = 64
HEAD_PAD = 128
ROPE_THETA = 10000.0
V_ROWS = 80

HY_W = 256
HY_BANDS = 16
GM_W = 256
GM_CHUNK = 128
GM_GROUPS = 4

OFF_Q = 0
OFF_KV = OFF_Q + Q_RANK
OFF_KR = OFF_KV + KV_RANK
OFF_HY = OFF_KR + ROPE
OFF_GM = OFF_HY + 3 * HY_W
OFF_GT = OFF_GM + 2 * GM_W

P_Q = 0
P_KV = P_Q + Q_RANK
P_KR = P_KV + KV_RANK
P_HY = P_KR + HEAD_PAD
P_GM = P_HY + 3 * HY_W
P_GT = P_GM + 2 * GM_W
P_W = P_GT + 3 * D_MODEL

N_EXPERTS = 16
N_GROUPS = 4
EXP_PER_GROUP = 4
TOP_K = 2
D_EXPERT = 512
MOE_BM = 256
MOE_TC = 256

DFT_N2 = 256

VMEM_LIMIT = 56 * 1024 * 1024
ATTN_TQ = 512
ATTN_TK = 2048
ATTN_BOUND_MAX = 50.0
NEG_BIG = -1e30
LOG2E = 1.4426950408889634
QSCALE = QK ** -0.5 * LOG2E


def _cparams(sem):
    return pltpu.CompilerParams(dimension_semantics=sem, vmem_limit_bytes=VMEM_LIMIT)


def _rms(x):
    return x * lax.rsqrt(jnp.mean(x * x, axis=-1, keepdims=True) + EPS)


def _sigmoid(x):
    return 0.5 * jnp.tanh(0.5 * x) + 0.5


def _nt_dot(a, b):
    return lax.dot_general(a, b, (((1,), (1,)), ((), ())), preferred_element_type=F32)


def _modvec_kernel(c_ref, w_ref, b_ref, o_ref):
    cv = c_ref[...]
    s = cv * jax.nn.sigmoid(cv)
    o_ref[0] = jnp.dot(s, w_ref[0], preferred_element_type=F32, precision=HIGHEST) + b_ref[0]


def _modvec(cvecs, w_mod, b_mod):
    depth = w_mod.shape[0]
    tn = 1536
    return pl.pallas_call(
        _modvec_kernel,
        out_shape=jax.ShapeDtypeStruct((depth, 8, N_MOD * D_MODEL), F32),
        grid=(depth, N_MOD * D_MODEL // tn),
        in_specs=[pl.BlockSpec((8, D_MODEL), lambda l, j: (0, 0)),
                  pl.BlockSpec((1, D_MODEL, tn), lambda l, j: (l, 0, j)),
                  pl.BlockSpec((1, 1, tn), lambda l, j: (l, 0, j))],
        out_specs=pl.BlockSpec((1, 8, tn), lambda l, j: (l, 0, j)),
        compiler_params=_cparams(("arbitrary", "arbitrary")),
        name="modvec",
    )(cvecs, w_mod, b_mod.reshape(depth, 1, N_MOD * D_MODEL))


def _head_norm_rope(xh, gain, rope):
    ms = jnp.sum(xh * xh, axis=-1, keepdims=True) * (1.0 / QK)
    xh = xh * lax.rsqrt(ms + EPS) * gain
    if rope is not None:
        rc, rs1, rs2 = rope
        xh = xh * rc + pltpu.roll(xh, ROPE // 2, 1) * rs1 + pltpu.roll(xh, HEAD_PAD - ROPE // 2, 1) * rs2
    return xh


def _premix_kernel(*refs, use_rope, tm):
    if use_rope:
        (x_ref, shift_ref, scale_ref, g1_ref, win_ref, gqa_ref, wuq_ref, gqn_ref, gkva_ref, wk_ref, wvt_ref,
         gkn_ref, gmg_ref, gmw_ref, gmb_ref, qpad_ref, kpad_ref, rc_ref, rs1_ref, rs2_ref,
         q_ref, k_ref, vt_ref, m_ref, hy_ref, gt_ref) = refs
        rope = (rc_ref[...], rs1_ref[...], rs2_ref[...])
    else:
        (x_ref, shift_ref, scale_ref, g1_ref, win_ref, gqa_ref, wuq_ref, gqn_ref, gkva_ref, wk_ref, wvt_ref,
         gkn_ref, gmg_ref, gmw_ref, gmb_ref, qpad_ref, kpad_ref,
         q_ref, k_ref, vt_ref, m_ref, hy_ref, gt_ref) = refs
        rope = None

    x = x_ref[...]
    h = _rms(x) * g1_ref[...]
    h = h * (1.0 + scale_ref[0]) + shift_ref[0]
    hb = h.astype(BF16)

    def proj(lo, width):
        return jnp.dot(hb, win_ref[:, lo:lo + width], preferred_element_type=F32)

    qa = (_rms(proj(P_Q, Q_RANK)) * gqa_ref[...]).astype(BF16)
    q = jnp.dot(qa, wuq_ref[...], preferred_element_type=F32)
    for hh in range(HEADS):
        qh = _head_norm_rope(q[:, hh * HEAD_PAD:(hh + 1) * HEAD_PAD], gqn_ref[...], rope)
        q_ref[:, hh * HEAD_PAD:(hh + 1) * HEAD_PAD] = (qh * QSCALE + qpad_ref[...]).astype(BF16)

    kva = (_rms(proj(P_KV, KV_RANK)) * gkva_ref[...]).astype(BF16)
    kr = pltpu.roll(proj(P_KR, HEAD_PAD), NOPE, 1)
    kn = jnp.dot(kva, wk_ref[...], preferred_element_type=F32)
    for hh in range(HEADS):
        kh = _head_norm_rope(kn[:, hh * HEAD_PAD:(hh + 1) * HEAD_PAD] + kr, gkn_ref[...], rope)
        k_ref[:, hh * HEAD_PAD:(hh + 1) * HEAD_PAD] = (kh + kpad_ref[...]).astype(BF16)
    vt = _nt_dot(wvt_ref[...], kva)
    row = lax.broadcasted_iota(jnp.int32, (V_ROWS - VDIM, tm), 0)
    ones_rows = jnp.where(row == 0, 1.0, 0.0).astype(BF16)
    for hh in range(HEADS):
        vt_ref[0, 0, hh * V_ROWS:hh * V_ROWS + VDIM, :] = vt[hh * VDIM:(hh + 1) * VDIM].astype(BF16)
        vt_ref[0, 0, hh * V_ROWS + VDIM:(hh + 1) * V_ROWS, :] = ones_rows

    gg = jax.nn.gelu(proj(P_GM, 2 * GM_W), approximate=True)
    gu = gg[:, :GM_W]
    gv = (_rms(gg[:, GM_W:]) * gmg_ref[...]).astype(BF16)
    grp = lax.broadcasted_iota(jnp.int32, (GM_CHUNK, GM_W), 1) // (GM_W // GM_GROUPS)
    for ci in range(tm // GM_CHUNK):
        vc = gv[ci * GM_CHUNK:(ci + 1) * GM_CHUNK]
        s = jnp.zeros((GM_CHUNK, GM_W), F32)
        for g in range(GM_GROUPS):
            sg = jnp.dot(gmw_ref[g], vc, preferred_element_type=F32)
            s = jnp.where(grp == g, sg, s)
        m_ref[ci * GM_CHUNK:(ci + 1) * GM_CHUNK, :] = (
            gu[ci * GM_CHUNK:(ci + 1) * GM_CHUNK] * (s + gmb_ref[...])).astype(BF16)

    hy_ref[...] = proj(P_HY, 3 * HY_W).astype(BF16)
    for j in range(3):
        gt_ref[:, j * D_MODEL:(j + 1) * D_MODEL] = _sigmoid(proj(P_GT + j * D_MODEL, D_MODEL)).astype(BF16)


def _premix(x, shift, scale, lw, rope_tabs, seq, tm):
    n = x.shape[0]
    tiles_per_seq = seq // tm
    batch = n // seq
    use_rope = rope_tabs is not None
    const = lambda i: (0, 0)
    in_specs = [
        pl.BlockSpec((tm, D_MODEL), lambda i: (i, 0)),
        pl.BlockSpec((1, 1, D_MODEL), lambda i: (i // tiles_per_seq, 0, 0)),
        pl.BlockSpec((1, 1, D_MODEL), lambda i: (i // tiles_per_seq, 0, 0)),
        pl.BlockSpec((1, D_MODEL), const),
        pl.BlockSpec((D_MODEL, P_W), const, pipeline_mode=pl.Buffered(1)),
        pl.BlockSpec((1, Q_RANK), const),
        pl.BlockSpec((Q_RANK, HEADS * HEAD_PAD), const),
        pl.BlockSpec((1, HEAD_PAD), const),
        pl.BlockSpec((1, KV_RANK), const),
        pl.BlockSpec((KV_RANK, HEADS * HEAD_PAD), const),
        pl.BlockSpec((HEADS * VDIM, KV_RANK), const),
        pl.BlockSpec((1, HEAD_PAD), const),
        pl.BlockSpec((1, GM_W), const),
        pl.BlockSpec((GM_GROUPS, GM_CHUNK, GM_CHUNK), lambda i: (0, 0, 0)),
        pl.BlockSpec((GM_CHUNK, GM_W), const),
        pl.BlockSpec((1, HEAD_PAD), const),
        pl.BlockSpec((1, HEAD_PAD), const),
    ]
    args = [x, shift, scale, lw['g1'], lw['w_in'], lw['g_qa'], lw['w_uq'], lw['g_qn'], lw['g_kva'], lw['w_k'],
            lw['w_vt'], lw['g_kn'], lw['gm_g'], lw['gm_w'], lw['gm_b'], lw['q_pad'], lw['k_pad']]
    tkv = min(seq, ATTN_TK)
    sub = tkv // tm
    if use_rope:
        in_specs += [pl.BlockSpec((tm, HEAD_PAD), lambda i: (i % tiles_per_seq, 0))] * 3
        args += list(rope_tabs)
    out_shape = (
        jax.ShapeDtypeStruct((n, HEADS * HEAD_PAD), BF16),
        jax.ShapeDtypeStruct((n, HEADS * HEAD_PAD), BF16),
        jax.ShapeDtypeStruct((batch, seq // tkv, HEADS * V_ROWS, tkv), BF16),
        jax.ShapeDtypeStruct((n, GM_W), BF16),
        jax.ShapeDtypeStruct((n, 3 * HY_W), BF16),
        jax.ShapeDtypeStruct((n, 3 * D_MODEL), BF16),
    )
    out_specs = (
        pl.BlockSpec((tm, HEADS * HEAD_PAD), lambda i: (i, 0)),
        pl.BlockSpec((tm, HEADS * HEAD_PAD), lambda i: (i, 0)),
        pl.BlockSpec((1, 1, HEADS * V_ROWS, tm),
                     lambda i: (i // tiles_per_seq, (i % tiles_per_seq) // sub, 0, (i % tiles_per_seq) % sub)),
        pl.BlockSpec((tm, GM_W), lambda i: (i, 0)),
        pl.BlockSpec((tm, 3 * HY_W), lambda i: (i, 0)),
        pl.BlockSpec((tm, 3 * D_MODEL), lambda i: (i, 0)),
    )
    return pl.pallas_call(
        functools.partial(_premix_kernel, use_rope=use_rope, tm=tm),
        out_shape=out_shape, grid=(n // tm,), in_specs=in_specs, out_specs=out_specs,
        compiler_params=_cparams(("arbitrary",)), name="premix",
    )(*args)


def _attn_kernel(*refs, n_chunks, has_ctx, tq):
    if has_ctx:
        q_ref, k_ref, vt_ref, kc_ref, vtc_ref, o_ref = refs
    else:
        q_ref, k_ref, vt_ref, o_ref = refs
    tk = vt_ref.shape[-1]

    def step(hh, kc, vtc, m, acc):
        s = _nt_dot(kc, q_ref[:, hh * HEAD_PAD:(hh + 1) * HEAD_PAD])
        m_new = jnp.maximum(m, jnp.max(s, axis=0, keepdims=True))
        p = jnp.exp2(s - m_new).astype(BF16)
        alpha = jnp.exp2(m - m_new)
        return m_new, acc * alpha + jnp.dot(vtc, p, preferred_element_type=F32)

    def body(i, carry):
        out = []
        for hh in range(2):
            m, acc = carry[hh]
            start = pl.multiple_of(i * tk, tk)
            kc = k_ref[pl.ds(start, tk), hh * HEAD_PAD:(hh + 1) * HEAD_PAD]
            vtc = vt_ref[0, i, hh * V_ROWS:(hh + 1) * V_ROWS, :]
            out.append(step(hh, kc, vtc, m, acc))
        return tuple(out)

    init = tuple((jnp.full((1, tq), NEG_BIG, F32), jnp.zeros((V_ROWS, tq), F32)) for _ in range(2))
    carry = lax.fori_loop(0, n_chunks, body, init)
    outs = []
    for hh in range(2):
        m, acc = carry[hh]
        if has_ctx:
            m, acc = step(hh, kc_ref[:, hh * HEAD_PAD:(hh + 1) * HEAD_PAD],
                          vtc_ref[0, 0, hh * V_ROWS:(hh + 1) * V_ROWS, :], m, acc)
        outs.append(acc[:VDIM] / acc[VDIM:VDIM + 1])
    o_ref[...] = jnp.concatenate(outs, axis=0).T.astype(BF16)


def _attn_bounded_kernel(*refs, n_chunks, has_ctx, tq):
    if has_ctx:
        q_ref, k_ref, vt_ref, kc_ref, vtc_ref, o_ref = refs
    else:
        q_ref, k_ref, vt_ref, o_ref = refs
    tk = vt_ref.shape[-1]

    def step(hh, kc, vtc, acc):
        s = _nt_dot(kc, q_ref[:, hh * HEAD_PAD:(hh + 1) * HEAD_PAD])
        return acc + jnp.dot(vtc, jnp.exp2(s).astype(BF16), preferred_element_type=F32)

    def body(i, carry):
        start = pl.multiple_of(i * tk, tk)
        return tuple(step(hh, k_ref[pl.ds(start, tk), hh * HEAD_PAD:(hh + 1) * HEAD_PAD],
                          vt_ref[0, i, hh * V_ROWS:(hh + 1) * V_ROWS, :], carry[hh]) for hh in range(2))

    carry = lax.fori_loop(0, n_chunks, body, tuple(jnp.zeros((V_ROWS, tq), F32) for _ in range(2)), unroll=True)
    outs = []
    for hh in range(2):
        acc = carry[hh]
        if has_ctx:
            acc = step(hh, kc_ref[:, hh * HEAD_PAD:(hh + 1) * HEAD_PAD],
                       vtc_ref[0, 0, hh * V_ROWS:(hh + 1) * V_ROWS, :], acc)
        outs.append(acc[:VDIM] / acc[VDIM:VDIM + 1])
    o_ref[...] = jnp.concatenate(outs, axis=0).T.astype(BF16)


def _attention(q, k, vt, ctx_kv, seq, tq, bound):
    n = q.shape[0]
    batch = n // seq
    n_chunks, tk = vt.shape[1], vt.shape[3]
    q_tiles = seq // tq
    has_ctx = ctx_kv is not None
    in_specs = [
        pl.BlockSpec((tq, 2 * HEAD_PAD), lambda b, j, i: (b * q_tiles + i, j)),
        pl.BlockSpec((seq, 2 * HEAD_PAD), lambda b, j, i: (b, j)),
        pl.BlockSpec((1, n_chunks, 2 * V_ROWS, tk), lambda b, j, i: (b, 0, j, 0)),
    ]
    args = [q, k, vt]
    if has_ctx:
        kc, vtc = ctx_kv
        lc = vtc.shape[3]
        in_specs += [pl.BlockSpec((lc, 2 * HEAD_PAD), lambda b, j, i: (b, j)),
                     pl.BlockSpec((1, 1, 2 * V_ROWS, lc), lambda b, j, i: (b, 0, j, 0))]
        args += [kc, vtc]
    def call(body, name):
        return pl.pallas_call(
            functools.partial(body, n_chunks=n_chunks, has_ctx=has_ctx, tq=tq),
            out_shape=jax.ShapeDtypeStruct((n, HEADS * VDIM), BF16),
            grid=(batch, HEADS // 2, q_tiles),
            in_specs=in_specs,
            out_specs=pl.BlockSpec((tq, 2 * VDIM), lambda b, j, i: (b * q_tiles + i, j)),
            compiler_params=_cparams(("arbitrary", "arbitrary", "arbitrary")),
            name=name,
        )(*args)

    return lax.cond(bound < ATTN_BOUND_MAX,
                    lambda: call(_attn_bounded_kernel, "attn_bounded"),
                    lambda: call(_attn_kernel, "attn_online"))


def _hy_filter_kernel(zz_ref, w1_ref, b1_ref, w2_ref, b2_ref, w3_ref, b3_ref, fr_ref, dec_ref, f_ref, ss_ref, *, emb):
    fr = fr_ref[...]
    dec = jnp.abs(dec_ref[...])
    zz = zz_ref[...]
    hdn = jnp.sin(fr * (jnp.dot(zz, w1_ref[...], preferred_element_type=F32, precision=HIGHEST) + b1_ref[...]))
    hdn = jnp.sin(fr * (jnp.dot(hdn, w2_ref[...], preferred_element_type=F32, precision=HIGHEST) + b2_ref[...]))
    k = jnp.dot(hdn, w3_ref[...], preferred_element_type=F32, precision=HIGHEST) + b3_ref[...]
    col = lax.broadcasted_iota(jnp.int32, k.shape, 1)
    k = k * jnp.exp(-jnp.where(col < HY_W, zz[:, 0:1], zz[:, emb:emb + 1]) * dec)
    first_block = pl.program_id(0) == 0
    row = lax.broadcasted_iota(jnp.int32, k.shape, 0)
    k = jnp.where(first_block & (row == 0) & (col >= HY_W), 0.0, k)
    f_ref[0] = k[:, :HY_W].astype(BF16)
    f_ref[1] = k[:, HY_W:].astype(BF16)
    hid = hdn.shape[1] // 2
    k_0 = jnp.dot(pltpu.roll(hdn[0:8], hid, 1), w3_ref[...], preferred_element_type=F32,
                  precision=HIGHEST) + b3_ref[...]
    k_0 = k_0[0:1] * jnp.exp(-zz[0:1, 0:1] * dec)
    extra = jnp.where(first_block & (col[0:1] >= HY_W), k_0 * k_0, 0.0)
    ss_ref[0] = jnp.sum(k * k, axis=0, keepdims=True) + extra


def _hy_filters(z, lw, tr):
    seq, emb = z.shape
    nblk = seq // tr
    hid2 = lw['hy_w2p'].shape[0]
    const = lambda i: (0, 0)
    zz = jnp.concatenate([z, jnp.concatenate([z[0:1], z[:0:-1]], axis=0)], axis=1)
    return pl.pallas_call(
        functools.partial(_hy_filter_kernel, emb=emb),
        out_shape=(jax.ShapeDtypeStruct((2, seq, HY_W), BF16), jax.ShapeDtypeStruct((nblk, 1, 2 * HY_W), F32)),
        grid=(nblk,),
        in_specs=[pl.BlockSpec((tr, 2 * emb), lambda i: (i, 0)),
                  pl.BlockSpec((2 * emb, hid2), const), pl.BlockSpec((1, hid2), const),
                  pl.BlockSpec((hid2, hid2), const), pl.BlockSpec((1, hid2), const),
                  pl.BlockSpec((hid2, 2 * HY_W), const), pl.BlockSpec((1, 2 * HY_W), const),
                  pl.BlockSpec((1, hid2), const), pl.BlockSpec((1, 2 * HY_W), const)],
        out_specs=(pl.BlockSpec((2, tr, HY_W), lambda i: (0, i, 0)),
                   pl.BlockSpec((1, 1, 2 * HY_W), lambda i: (i, 0, 0))),
        compiler_params=_cparams(("arbitrary",)), name="hy_filter",
    )(zz, lw['hy_w1p'], lw['hy_b1p'], lw['hy_w2p'], lw['hy_b2p'], lw['hy_w3p'], lw['hy_b3'], lw['hy_freqp'],
      lw['hy_decay'])


def _hy_conv3_kernel(p_ref, prev_ref, next_ref, w_ref, b_ref, x0_ref, u_ref, *, tiles_per_seq, tr):
    i = pl.program_id(0)
    p = p_ref[...].astype(F32)
    first = (i % tiles_per_seq) == 0
    last = (i % tiles_per_seq) == tiles_per_seq - 1
    prev_row = jnp.where(first, 0.0, prev_ref[...].astype(F32)[15:16, :])
    next_row = jnp.where(last, 0.0, next_ref[...].astype(F32)[0:1, :])
    row = lax.broadcasted_iota(jnp.int32, p.shape, 0)
    p_prev = jnp.where(row == 0, prev_row, pltpu.roll(p, 1, 0))
    p_next = jnp.where(row == tr - 1, next_row, pltpu.roll(p, tr - 1, 0))
    z = b_ref[...] + p_prev * w_ref[0:1, :] + p * w_ref[1:2, :] + p_next * w_ref[2:3, :]
    x0_ref[...] = z[:, :HY_W].astype(BF16)
    u_ref[...] = (z[:, 2 * HY_W:] * z[:, HY_W:2 * HY_W]).astype(BF16)


def _hy_conv3(phy, conv_w, conv_b, seq, tr):
    n = phy.shape[0]
    tiles_per_seq = seq // tr
    hb = tr // 16
    nhb = n // 16
    return pl.pallas_call(
        functools.partial(_hy_conv3_kernel, tiles_per_seq=tiles_per_seq, tr=tr),
        out_shape=(jax.ShapeDtypeStruct((n, HY_W), BF16), jax.ShapeDtypeStruct((n, HY_W), BF16)),
        grid=(n // tr,),
        in_specs=[pl.BlockSpec((tr, 3 * HY_W), lambda i: (i, 0)),
                  pl.BlockSpec((16, 3 * HY_W), lambda i: (jnp.maximum(i * hb - 1, 0), 0)),
                  pl.BlockSpec((16, 3 * HY_W), lambda i: (jnp.minimum((i + 1) * hb, nhb - 1), 0)),
                  pl.BlockSpec((3, 3 * HY_W), lambda i: (0, 0)),
                  pl.BlockSpec((1, 3 * HY_W), lambda i: (0, 0))],
        out_specs=(pl.BlockSpec((tr, HY_W), lambda i: (i, 0)), pl.BlockSpec((tr, HY_W), lambda i: (i, 0))),
        compiler_params=_cparams(("arbitrary",)), name="hy_conv3",
    )(phy, phy, phy, conv_w, conv_b)


def _dft_outer_kernel(wr_ref, wi_ref, u_ref, ar_ref, ai_ref):
    u = u_ref[0]
    ar_ref[0] = jnp.dot(wr_ref[...], u, preferred_element_type=F32).astype(BF16)
    ai_ref[0] = jnp.dot(wi_ref[...], u, preferred_element_type=F32).astype(BF16)


def _dft_outer(wr, wi, u, tc):
    nb, kk, cols = u.shape
    n1 = wr.shape[0]
    return pl.pallas_call(
        _dft_outer_kernel,
        out_shape=(jax.ShapeDtypeStruct((nb, n1, cols), BF16),) * 2,
        grid=(nb, cols // tc),
        in_specs=[pl.BlockSpec((n1, kk), lambda b, j: (0, 0)), pl.BlockSpec((n1, kk), lambda b, j: (0, 0)),
                  pl.BlockSpec((1, kk, tc), lambda b, j: (b, 0, j))],
        out_specs=(pl.BlockSpec((1, n1, tc), lambda b, j: (b, 0, j)),) * 2,
        compiler_params=_cparams(("arbitrary", "arbitrary")), name="dft_outer",
    )(wr, wi, u)


def _cdot(mr, mi, xr, xi):
    rr = jnp.dot(mr, xr, preferred_element_type=F32) - jnp.dot(mi, xi, preferred_element_type=F32)
    ri = jnp.dot(mr, xi, preferred_element_type=F32) + jnp.dot(mi, xr, preferred_element_type=F32)
    return rr, ri


def _dft_inner_fwd_kernel(mr_ref, mi_ref, ar_ref, ai_ref, fr_ref, fi_ref):
    fr, fi = _cdot(mr_ref[0], mi_ref[0], ar_ref[0, 0], ai_ref[0, 0])
    fr_ref[0] = fr.astype(BF16)
    fi_ref[0] = fi.astype(BF16)


def _dft_inner_fwd(mr, mi, ar, ai):
    n1, n2, _ = mr.shape
    ch = ar.shape[-1]
    mspec = pl.BlockSpec((1, n2, n2), lambda k: (k, 0, 0))
    aspec = pl.BlockSpec((1, 1, n2, ch), lambda k: (0, k, 0, 0))
    ospec = pl.BlockSpec((1, n2, ch), lambda k: (k, 0, 0))
    return pl.pallas_call(
        _dft_inner_fwd_kernel,
        out_shape=(jax.ShapeDtypeStruct((n1, n2, ch), BF16),) * 2,
        grid=(n1,), in_specs=[mspec, mspec, aspec, aspec], out_specs=(ospec, ospec),
        compiler_params=_cparams(("arbitrary",)), name="dft_inner_fwd",
    )(mr, mi, ar, ai)


def _dft_inner_conv_kernel(mr_ref, mi_ref, tr_ref, ti_ref, ar_ref, ai_ref, fr_ref, fi_ref, br_ref, bi_ref, *, nb):
    fr = fr_ref[0].astype(F32)
    fi = fi_ref[0].astype(F32)
    for b in range(nb):
        ur, ui = _cdot(mr_ref[0], mi_ref[0], ar_ref[b, 0], ai_ref[b, 0])
        yr = (ur * fr - ui * fi).astype(BF16)
        yi = (ur * fi + ui * fr).astype(BF16)
        br, bi = _cdot(tr_ref[0], ti_ref[0], yr, yi)
        br_ref[b, 0] = br.astype(BF16)
        bi_ref[b, 0] = bi.astype(BF16)


def _dft_inner_conv(mr, mi, tr, ti, ar, ai, fr, fi):
    n1, n2, _ = mr.shape
    nb, _, _, ch = ar.shape
    mspec = pl.BlockSpec((1, n2, n2), lambda k: (k, 0, 0))
    aspec = pl.BlockSpec((nb, 1, n2, ch), lambda k: (0, k, 0, 0))
    fspec = pl.BlockSpec((1, n2, ch), lambda k: (k, 0, 0))
    return pl.pallas_call(
        functools.partial(_dft_inner_conv_kernel, nb=nb),
        out_shape=(jax.ShapeDtypeStruct((nb, n1, n2, ch), BF16),) * 2,
        grid=(n1,), in_specs=[mspec, mspec, mspec, mspec, aspec, aspec, fspec, fspec],
        out_specs=(aspec, aspec),
        compiler_params=_cparams(("arbitrary",)), name="dft_inner_conv",
    )(mr, mi, tr, ti, ar, ai, fr, fi)


def _filter_scale(ss_ref):
    ss = jnp.sum(ss_ref[...], axis=0)
    return lax.rsqrt(ss[:, :HY_W] + ss[:, HY_W:] + EPS)


def _hy_final_kernel(cr_ref, ci_ref, br_ref, bi_ref, x0_ref, u_ref, ss_ref, bias_ref, o_ref, *, reps):
    y = (jnp.dot(cr_ref[...], br_ref[0], preferred_element_type=F32)
         + jnp.dot(ci_ref[...], bi_ref[0], preferred_element_type=F32))
    scale = jnp.tile(_filter_scale(ss_ref), (1, reps))
    bias = jnp.tile(bias_ref[...], (1, reps))
    u = u_ref[0].astype(F32)
    o_ref[0] = (x0_ref[0].astype(F32) * (y * scale + u * bias)).astype(BF16)


def _hy_final(cr, ci, br, bi, x0, u, ss, bias, tc):
    nb, n1, cols = br.shape
    n1h = cr.shape[0]
    nblk = ss.shape[0]
    cspec = pl.BlockSpec((n1h, n1), lambda b, j: (0, 0))
    bspec = pl.BlockSpec((1, n1, tc), lambda b, j: (b, 0, j))
    xspec = pl.BlockSpec((1, n1h, tc), lambda b, j: (b, 0, j))
    return pl.pallas_call(
        functools.partial(_hy_final_kernel, reps=tc // HY_W),
        out_shape=jax.ShapeDtypeStruct((nb, n1h, cols), BF16),
        grid=(nb, cols // tc),
        in_specs=[cspec, cspec, bspec, bspec, xspec, xspec,
                  pl.BlockSpec((nblk, 1, 2 * HY_W), lambda b, j: (0, 0, 0)),
                  pl.BlockSpec((1, HY_W), lambda b, j: (0, 0))],
        out_specs=xspec,
        compiler_params=_cparams(("arbitrary", "arbitrary")), name="hy_final",
    )(cr, ci, br, bi, x0, u, ss, bias)


def _hy_direct_kernel(f_ref, x0_ref, u_ref, ss_ref, bias_ref, o_ref, kk_ref, *, seq):
    kk_ref[0:seq, :] = f_ref[1].astype(F32)
    kk_ref[seq:2 * seq, :] = f_ref[0].astype(F32)
    u = u_ref[0].astype(F32)
    y = jnp.zeros((seq, HY_W), F32)
    for j in range(seq):
        y = y + kk_ref[seq - j:2 * seq - j, :] * u[j:j + 1, :]
    o_ref[0] = (x0_ref[0].astype(F32) * (y * _filter_scale(ss_ref) + u * bias_ref[...])).astype(BF16)


def _hy_direct(f, x0, u, ss, bias):
    nb, seq, _ = u.shape
    nblk = ss.shape[0]
    xspec = pl.BlockSpec((1, seq, HY_W), lambda b: (b, 0, 0))
    return pl.pallas_call(
        functools.partial(_hy_direct_kernel, seq=seq),
        out_shape=jax.ShapeDtypeStruct((nb, seq, HY_W), BF16),
        grid=(nb,),
        in_specs=[pl.BlockSpec((2, seq, HY_W), lambda b: (0, 0, 0)), xspec, xspec,
                  pl.BlockSpec((nblk, 1, 2 * HY_W), lambda b: (0, 0, 0)),
                  pl.BlockSpec((1, HY_W), lambda b: (0, 0))],
        out_specs=xspec,
        scratch_shapes=[pltpu.VMEM((2 * seq, HY_W), F32)],
        compiler_params=_cparams(("arbitrary",)), name="hy_direct",
    )(f, x0, u, ss, bias)


def _hy_embedding(seq):
    t = jnp.arange(seq, dtype=F32)
    t_unit = t / max(seq - 1, 1)
    bands = jnp.linspace(1e-4, HY_BANDS - 1, HY_BANDS, dtype=F32)
    ang = (2 * jnp.pi / seq) * t[:, None] * bands[None, :]
    return jnp.concatenate([t_unit[:, None], jnp.cos(ang), -jnp.sin(ang)], axis=-1)


def _dft_tables(seq):
    n = 2 * seq
    n2 = DFT_N2
    n1 = n // n2
    a = jnp.arange(n1, dtype=jnp.int32)
    th1 = (2 * jnp.pi / n1) * ((a[:, None] * a[None, :]) % n1).astype(F32)
    w1r, w1i = jnp.cos(th1), -jnp.sin(th1)
    b = jnp.arange(n2, dtype=jnp.int32)
    tha = (2 * jnp.pi / n) * (a[:, None] * b[None, :]).astype(F32)
    thb = (2 * jnp.pi / n2) * ((b[:, None] * b[None, :]) % n2).astype(F32)
    ar, ai = jnp.cos(tha), -jnp.sin(tha)
    br, bi = jnp.cos(thb), -jnp.sin(thb)
    mr = ar[:, None, :] * br[None] - ai[:, None, :] * bi[None]
    mi = ar[:, None, :] * bi[None] + ai[:, None, :] * br[None]
    tr = ar[:, :, None] * br[None] - ai[:, :, None] * bi[None]
    ti = -(ar[:, :, None] * bi[None] + ai[:, :, None] * br[None])
    return dict(
        w1r=w1r.astype(BF16), w1i=w1i.astype(BF16),
        mr=mr.astype(BF16), mi=mi.astype(BF16), tr=tr.astype(BF16), ti=ti.astype(BF16),
        cr=(w1r[:n1 // 2] / n).astype(BF16), ci=(w1i[:n1 // 2] / n).astype(BF16))


def _hyena_long(phy, lw, filt, seq, tabs):
    n = phy.shape[0]
    nb = n // seq
    n2 = DFT_N2
    n1 = 2 * seq // n2
    f, ss = filt
    x0, u = _hy_conv3(phy, lw['hy_conv_w'], lw['hy_conv_b'], seq, min(seq, 1024))
    tc = min(n2 * HY_W, 8192)
    far, fai = _dft_outer(tabs['w1r'], tabs['w1i'], f.reshape(1, n1, n2 * HY_W), tc)
    fr, fi = _dft_inner_fwd(tabs['mr'], tabs['mi'], far.reshape(1, n1, n2, HY_W), fai.reshape(1, n1, n2, HY_W))
    ar, ai = _dft_outer(tabs['w1r'][:, :n1 // 2], tabs['w1i'][:, :n1 // 2], u.reshape(nb, n1 // 2, n2 * HY_W), tc)
    br, bi = _dft_inner_conv(tabs['mr'], tabs['mi'], tabs['tr'], tabs['ti'],
                             ar.reshape(nb, n1, n2, HY_W), ai.reshape(nb, n1, n2, HY_W), fr, fi)
    out = _hy_final(tabs['cr'], tabs['ci'], br.reshape(nb, n1, n2 * HY_W), bi.reshape(nb, n1, n2 * HY_W),
                    x0.reshape(nb, n1 // 2, n2 * HY_W), u.reshape(nb, n1 // 2, n2 * HY_W), ss, lw['hy_bias'], tc)
    return out.reshape(n, HY_W)


def _hyena_short(phy, lw, filt, seq):
    n = phy.shape[0]
    nb = n // seq
    f, ss = filt
    x0, u = _hy_conv3(phy, lw['hy_conv_w'], lw['hy_conv_b'], seq, seq)
    out = _hy_direct(f, x0.reshape(nb, seq, HY_W), u.reshape(nb, seq, HY_W), ss, lw['hy_bias'])
    return out.reshape(n, HY_W)


def _pair_top2_sum(a, b, c, d):
    return jnp.maximum(jnp.maximum(jnp.maximum(a + b, a + c), jnp.maximum(a + d, b + c)),
                       jnp.maximum(b + d, c + d))


def _route(logits_t, rb):
    aff = jax.nn.sigmoid(logits_t)
    sel = aff + rb
    rows = [sel[e:e + 1, :] for e in range(N_EXPERTS)]
    affr = [aff[e:e + 1, :] for e in range(N_EXPERTS)]
    best, bidx = None, None
    for g in range(N_GROUPS):
        gs = _pair_top2_sum(*rows[g * EXP_PER_GROUP:(g + 1) * EXP_PER_GROUP])
        if g == 0:
            best, bidx = gs, jnp.zeros(gs.shape, jnp.int32)
        else:
            upd = gs > best
            bidx = jnp.where(upd, g, bidx)
            best = jnp.where(upd, gs, best)
    vals = [jnp.where(bidx == e // EXP_PER_GROUP, rows[e], -jnp.inf) for e in range(N_EXPERTS)]
    m1, i1 = vals[0], jnp.zeros(best.shape, jnp.int32)
    for e in range(1, N_EXPERTS):
        upd = vals[e] > m1
        i1 = jnp.where(upd, e, i1)
        m1 = jnp.where(upd, vals[e], m1)
    m2, i2 = jnp.full(best.shape, -jnp.inf, F32), jnp.zeros(best.shape, jnp.int32)
    for e in range(N_EXPERTS):
        cand = jnp.where(i1 == e, -jnp.inf, vals[e])
        upd = cand > m2
        i2 = jnp.where(upd, e, i2)
        m2 = jnp.where(upd, cand, m2)
    a1 = sum(jnp.where(i1 == e, affr[e], 0.0) for e in range(N_EXPERTS))
    a2 = sum(jnp.where(i2 == e, affr[e], 0.0) for e in range(N_EXPERTS))
    inv = 1.0 / (a1 + a2)
    return jnp.concatenate([i1, i2], axis=0), jnp.concatenate([a1 * inv, a2 * inv], axis=0)


def _merge_kernel(a_ref, b_ref, m_ref, gt_ref, x_ref, gate_ref, shift_ref, scale_ref, g2_ref,
                  wpa_ref, wpb_ref, wpc_ref, wout_ref, rwt_ref, rb_ref, xo_ref, h2_ref, idx_ref, wts_ref):
    y = gt_ref[:, 0:D_MODEL].astype(F32) * jnp.dot(a_ref[...], wpa_ref[...], preferred_element_type=F32)
    y = y + gt_ref[:, D_MODEL:2 * D_MODEL].astype(F32) * jnp.dot(b_ref[...], wpb_ref[...],
                                                                 preferred_element_type=F32)
    y = y + gt_ref[:, 2 * D_MODEL:].astype(F32) * jnp.dot(m_ref[...], wpc_ref[...], preferred_element_type=F32)
    y2 = jnp.dot(y.astype(BF16), wout_ref[...], preferred_element_type=F32)
    xn = x_ref[...] + gate_ref[0] * y2
    xo_ref[...] = xn
    h2 = _rms(xn) * g2_ref[...]
    h2 = h2 * (1.0 + scale_ref[0]) + shift_ref[0]
    h2_ref[...] = h2
    logits_t = lax.dot_general(rwt_ref[...], h2, (((1,), (1,)), ((), ())), preferred_element_type=F32,
                               precision=HIGHEST)
    idx_ref[...], wts_ref[...] = _route(logits_t, rb_ref[...])


def _merge(a, b, m, gt, x, gate, shift, scale, lw, rw_t, rb, seq, tm):
    n = x.shape[0]
    tiles_per_seq = seq // tm
    const = lambda i: (0, 0)
    row = lambda w: pl.BlockSpec((tm, w), lambda i: (i, 0))
    vec = pl.BlockSpec((1, 1, D_MODEL), lambda i: (i // tiles_per_seq, 0, 0))
    return pl.pallas_call(
        _merge_kernel,
        out_shape=(jax.ShapeDtypeStruct((n, D_MODEL), F32), jax.ShapeDtypeStruct((n, D_MODEL), F32),
                   jax.ShapeDtypeStruct((TOP_K, n), jnp.int32), jax.ShapeDtypeStruct((TOP_K, n), F32)),
        grid=(n // tm,),
        in_specs=[row(HEADS * VDIM), row(HY_W), row(GM_W), row(3 * D_MODEL), row(D_MODEL), vec, vec, vec,
                  pl.BlockSpec((1, D_MODEL), const),
                  pl.BlockSpec((HEADS * VDIM, D_MODEL), const), pl.BlockSpec((HY_W, D_MODEL), const),
                  pl.BlockSpec((GM_W, D_MODEL), const), pl.BlockSpec((D_MODEL, D_MODEL), const),
                  pl.BlockSpec((N_EXPERTS, D_MODEL), const), pl.BlockSpec((N_EXPERTS, 1), const)],
        out_specs=(row(D_MODEL), row(D_MODEL), pl.BlockSpec((TOP_K, tm), lambda i: (0, i)),
                   pl.BlockSpec((TOP_K, tm), lambda i: (0, i))),
        compiler_params=_cparams(("arbitrary",)), name="merge",
    )(a, b, m, gt, x, gate, shift, scale, lw['g2'], lw['w_pa'], lw['w_pb'], lw['w_pc'], lw['w_out'], rw_t, rb)


def _rows_copy(src_hbm, dst, sem, idx_ref, rows):
    return [pltpu.make_async_copy(src_hbm.at[pl.ds(idx_ref[0, 0, r], 1)], dst.at[pl.ds(r, 1)], sem)
            for r in range(rows)]


def _rows_wait(src_hbm, dst, sem, rows):
    pltpu.make_async_copy(src_hbm.at[pl.ds(0, rows)], dst, sem).wait()


def _gather_pipeline(src_hbm, cur_ref, nxt_ref, buf, sem, rows):
    i = pl.program_id(0)
    slot = i % 2

    @pl.when(i == 0)
    def _():
        for cp in _rows_copy(src_hbm, buf.at[0], sem.at[0], cur_ref, rows):
            cp.start()

    _rows_wait(src_hbm, buf.at[slot], sem.at[slot], rows)
    for cp in _rows_copy(src_hbm, buf.at[1 - slot], sem.at[1 - slot], nxt_ref, rows):
        cp.start()
    return slot


def _gather_drain(src_hbm, buf, sem, rows):
    i = pl.program_id(0)

    @pl.when(i == pl.num_programs(0) - 1)
    def _():
        _rows_wait(src_hbm, buf.at[1 - i % 2], sem.at[1 - i % 2], rows)


def _moe_dispatch_kernel(pos_ref, h_ref, xs_in_hbm, xs_hbm, sem, *, tc):
    del xs_in_hbm
    for r in range(tc):
        pltpu.make_async_copy(h_ref.at[pl.ds(r, 1)], xs_hbm.at[pl.ds(pos_ref[0, 0, r], 1)], sem).start()
    pltpu.make_async_copy(h_ref, xs_hbm.at[pl.ds(0, tc)], sem).wait()


def _moe_dispatch(pos_tiles, h2, rows, tc):
    n = h2.shape[0]
    return pl.pallas_call(
        functools.partial(_moe_dispatch_kernel, tc=tc),
        out_shape=jax.ShapeDtypeStruct((rows, D_MODEL), F32),
        grid=(n // tc,),
        in_specs=[pl.BlockSpec((1, 1, tc), lambda i: (i, 0, 0), memory_space=pltpu.SMEM),
                  pl.BlockSpec((tc, D_MODEL), lambda i: (i, 0)),
                  pl.BlockSpec(memory_space=pl.ANY)],
        out_specs=pl.BlockSpec(memory_space=pl.ANY),
        scratch_shapes=[pltpu.SemaphoreType.DMA],
        input_output_aliases={2: 0},
        compiler_params=_cparams(("arbitrary",)), name="moe_dispatch",
    )(pos_tiles, h2, jnp.zeros((rows, D_MODEL), F32))


def _moe_ffn_kernel(ea_ref, eb_ref, x_ref, wgua_ref, wda_ref, wgub_ref, wdb_ref, y_ref):
    del ea_ref, eb_ref
    xb = x_ref[...].astype(BF16)
    for j, (wgu_ref, wd_ref) in enumerate(((wgua_ref, wda_ref), (wgub_ref, wdb_ref))):
        gu = jnp.dot(xb, wgu_ref[0], preferred_element_type=F32)
        g = gu[:, :D_EXPERT]
        hid = (g * _sigmoid(g) * gu[:, D_EXPERT:]).astype(BF16)
        y_ref[:, j * D_MODEL:(j + 1) * D_MODEL] = jnp.dot(hid, wd_ref[0], preferred_element_type=F32)


def _moe_ffn(blk_ea, blk_eb, x_sorted, lw):
    nblk = blk_ea.shape[0]
    wgu_spec = lambda sel: pl.BlockSpec((1, D_MODEL, 2 * D_EXPERT), lambda i, ea, eb: (sel(ea, eb)[i], 0, 0))
    wd_spec = lambda sel: pl.BlockSpec((1, D_EXPERT, D_MODEL), lambda i, ea, eb: (sel(ea, eb)[i], 0, 0))
    first, second = (lambda ea, eb: ea), (lambda ea, eb: eb)
    return pl.pallas_call(
        _moe_ffn_kernel,
        out_shape=jax.ShapeDtypeStruct((nblk * MOE_BM, TOP_K * D_MODEL), F32),
        grid_spec=pltpu.PrefetchScalarGridSpec(
            num_scalar_prefetch=2, grid=(nblk,),
            in_specs=[pl.BlockSpec((MOE_BM, D_MODEL), lambda i, ea, eb: (i, 0)),
                      wgu_spec(first), wd_spec(first), wgu_spec(second), wd_spec(second)],
            out_specs=pl.BlockSpec((MOE_BM, TOP_K * D_MODEL), lambda i, ea, eb: (i, 0))),
        compiler_params=_cparams(("arbitrary",)), name="moe_ffn",
    )(blk_ea, blk_eb, x_sorted, lw['w_gu'], lw['w_d'], lw['w_gu'], lw['w_d'])


def _moe_combine_kernel(cur_ref, nxt_ref, y_hbm, x_ref, w_ref, gate_ref, o_ref, buf, sem, *, tc):
    slot = _gather_pipeline(y_hbm, cur_ref, nxt_ref, buf, sem, tc)
    w = w_ref[...]
    y = w[:, 0:1] * buf[slot, :, 0:D_MODEL] + w[:, 1:2] * buf[slot, :, D_MODEL:]
    o_ref[...] = x_ref[...] + gate_ref[0] * y
    _gather_drain(y_hbm, buf, sem, tc)


def _moe_combine(pos_tiles, y_sorted, x, w_cols, gate, seq, tc):
    n = x.shape[0]
    tiles_per_seq = seq // tc
    idx_spec = lambda off: pl.BlockSpec((1, 1, tc), lambda i: (i + off, 0, 0), memory_space=pltpu.SMEM)
    return pl.pallas_call(
        functools.partial(_moe_combine_kernel, tc=tc),
        out_shape=jax.ShapeDtypeStruct((n, D_MODEL), F32),
        grid=(n // tc,),
        in_specs=[idx_spec(0), idx_spec(1), pl.BlockSpec(memory_space=pl.ANY),
                  pl.BlockSpec((tc, D_MODEL), lambda i: (i, 0)),
                  pl.BlockSpec((tc, TOP_K), lambda i: (i, 0)),
                  pl.BlockSpec((1, 1, D_MODEL), lambda i: (i // tiles_per_seq, 0, 0))],
        out_specs=pl.BlockSpec((tc, D_MODEL), lambda i: (i, 0)),
        scratch_shapes=[pltpu.VMEM((2, tc, TOP_K * D_MODEL), F32), pltpu.SemaphoreType.DMA((2,))],
        compiler_params=_cparams(("arbitrary",)), name="moe_combine",
    )(pos_tiles, pos_tiles, y_sorted, x, w_cols, gate)


def _cumsum_rows(onehot):
    n, c = onehot.shape
    blk = math.gcd(n, 256)
    x = onehot.astype(F32).reshape(n // blk, blk, c)
    tri = (jnp.arange(blk)[:, None] >= jnp.arange(blk)[None, :]).astype(F32)
    within = jnp.einsum('ij,bjc->bic', tri, x, precision=HIGHEST)
    nb = n // blk
    before = (jnp.arange(nb)[:, None] > jnp.arange(nb)[None, :]).astype(F32)
    offset = jnp.dot(before, within[:, -1, :], precision=HIGHEST)
    return (within + offset[:, None, :]).reshape(n, c).astype(jnp.int32)


def _pair_tables():
    pairs = [(a, b) for a in range(EXP_PER_GROUP) for b in range(a + 1, EXP_PER_GROUP)]
    ea = [g * EXP_PER_GROUP + a for g in range(N_GROUPS) for a, _ in pairs]
    eb = [g * EXP_PER_GROUP + b for g in range(N_GROUPS) for _, b in pairs]
    return jnp.array(ea, jnp.int32), jnp.array(eb, jnp.int32)


def _moe(h2, idx, wts, x, gate, lw, seq):
    n = x.shape[0]
    tab_a, tab_b = _pair_tables()
    n_cls = tab_a.shape[0]
    nblk = n // MOE_BM + n_cls
    swap = idx[0] > idx[1]
    e_lo, e_hi = jnp.minimum(idx[0], idx[1]), jnp.maximum(idx[0], idx[1])
    w_cols = jnp.stack([jnp.where(swap, wts[1], wts[0]), jnp.where(swap, wts[0], wts[1])], axis=1)
    onehot = ((e_lo[:, None] == tab_a[None, :]) & (e_hi[:, None] == tab_b[None, :])).astype(jnp.int32)
    csum = _cumsum_rows(onehot)
    counts = csum[-1]
    rank = jnp.sum(csum * onehot, axis=1) - 1
    seg_len = (counts + MOE_BM - 1) // MOE_BM * MOE_BM
    seg_end = jnp.cumsum(seg_len)
    pos = jnp.sum(onehot * (seg_end - seg_len)[None, :], axis=1) + rank
    blk_start = jnp.arange(nblk, dtype=jnp.int32) * MOE_BM
    blk_cls = jnp.minimum(jnp.sum((seg_end[None, :] <= blk_start[:, None]).astype(jnp.int32), axis=1), n_cls - 1)
    tc = min(seq, MOE_TC)
    pos_tiles = pos.reshape(n // tc, 1, tc)
    x_sorted = _moe_dispatch(pos_tiles, h2, nblk * MOE_BM, tc)
    y_sorted = _moe_ffn(tab_a[blk_cls], tab_b[blk_cls], x_sorted, lw)
    pos_next = jnp.concatenate([pos_tiles, jnp.zeros((1, 1, tc), jnp.int32)], axis=0)
    return _moe_combine(pos_next, y_sorted, x, w_cols, gate, seq, tc)


def _head_perm():
    rope_idx = list(range(NOPE, QK, 2)) + list(range(NOPE + 1, QK, 2))
    return jnp.array(list(range(NOPE)) + rope_idx, dtype=jnp.int32)


def _block_diag(a, b):
    return jnp.concatenate([jnp.pad(a, ((0, 0), (0, b.shape[1]))), jnp.pad(b, ((0, 0), (a.shape[1], 0)))], axis=0)


def _layer_weights(p, l):
    perm = _head_perm()
    row = lambda v: v.reshape(1, -1)
    w_in = p['w_in'][l]
    kr_perm = jnp.array(list(range(0, ROPE, 2)) + list(range(1, ROPE, 2)), dtype=jnp.int32)
    w_kr = jnp.pad(w_in[:, OFF_KR:OFF_HY][:, kr_perm], ((0, 0), (0, HEAD_PAD - ROPE)))
    w_in2 = jnp.concatenate([w_in[:, OFF_Q:OFF_KR], w_kr, w_in[:, OFF_HY:]], axis=1).astype(BF16)
    w_uq = p['w_uq'][l].reshape(Q_RANK, HEADS, QK)[:, :, perm]
    w_uq = jnp.pad(w_uq, ((0, 0), (0, 0), (0, HEAD_PAD - QK))).reshape(Q_RANK, HEADS * HEAD_PAD).astype(BF16)
    w_ukv = p['w_ukv'][l].reshape(KV_RANK, HEADS, NOPE + VDIM)
    w_k = jnp.pad(w_ukv[:, :, :NOPE], ((0, 0), (0, 0), (0, HEAD_PAD - NOPE)))
    w_k = w_k.reshape(KV_RANK, HEADS * HEAD_PAD).astype(BF16)
    w_vt = w_ukv[:, :, NOPE:].reshape(KV_RANK, HEADS * VDIM).T.astype(BF16)
    pad_gain = lambda g: jnp.pad(g[perm], (0, HEAD_PAD - QK)).reshape(1, HEAD_PAD)
    gm_b = jnp.repeat(p['gm_bs'][l].T, GM_W // GM_GROUPS, axis=1)
    bound = 1.02 * QSCALE * QK * jnp.max(jnp.abs(p['g_qn'][l])) * jnp.max(jnp.abs(p['g_kn'][l]))
    pad_lane = jnp.arange(HEAD_PAD) == QK
    return dict(
        attn_bound=bound,
        q_pad=jnp.where(pad_lane, -bound, 0.0).reshape(1, HEAD_PAD).astype(F32),
        k_pad=jnp.where(pad_lane, 1.0, 0.0).reshape(1, HEAD_PAD).astype(F32),
        g1=row(p['norm1_g'][l]), g2=row(p['norm2_g'][l]), w_in=w_in2,
        g_qa=row(p['g_qa'][l]), w_uq=w_uq, g_qn=pad_gain(p['g_qn'][l]),
        g_kva=row(p['g_kva'][l]), w_k=w_k, w_vt=w_vt, g_kn=pad_gain(p['g_kn'][l]),
        gm_g=row(p['gm_norm_g'][l]), gm_w=p['gm_ws'][l].astype(BF16), gm_b=gm_b,
        hy_conv_w=p['hy_conv_w'][l], hy_conv_b=row(p['hy_conv_b'][l]),
        hy_w1p=_block_diag(p['hy_w1'][l], p['hy_w1'][l]), hy_b1p=row(jnp.tile(p['hy_b1'][l], 2)),
        hy_w2p=_block_diag(p['hy_w2'][l], p['hy_w2'][l]), hy_b2p=row(jnp.tile(p['hy_b2'][l], 2)),
        hy_w3p=_block_diag(p['hy_w3'][l][:, :HY_W], p['hy_w3'][l][:, HY_W:]), hy_b3=row(p['hy_b3'][l]),
        hy_freqp=row(jnp.tile(p['hy_freq'][l], 2)),
        hy_decay=p['hy_decay'][l].reshape(1, 2 * HY_W), hy_bias=row(p['hy_bias'][l]),
        w_pa=p['w_pa'][l].astype(BF16), w_pb=p['w_pb'][l].astype(BF16), w_pc=p['w_pc'][l].astype(BF16),
        w_out=p['w_out'][l].astype(BF16),
        w_gu=jnp.concatenate([p['moe_w_gate'][l], p['moe_w_up'][l]], axis=-1).astype(BF16),
        w_d=p['moe_w_down'][l].astype(BF16))


def _rope_tables(seq):
    rows = seq // GRID_W
    row = jnp.repeat(jnp.arange(rows, dtype=F32), GRID_W)
    col = jnp.tile(jnp.arange(GRID_W, dtype=F32), rows)
    n_freq = ROPE // 4
    inv = ROPE_THETA ** (-jnp.arange(n_freq, dtype=F32) / n_freq)
    ang = jnp.concatenate([row[:, None] * inv, col[:, None] * inv], axis=-1)
    c, s = jnp.cos(ang), jnp.sin(ang)
    z = lambda w: jnp.zeros((seq, w), F32)
    rc = jnp.concatenate([jnp.ones((seq, NOPE), F32), c, c, z(HEAD_PAD - QK)], axis=1)
    rs1 = jnp.concatenate([z(NOPE), z(ROPE // 2), s, z(HEAD_PAD - QK)], axis=1)
    rs2 = jnp.concatenate([z(NOPE), -s, z(ROPE // 2), z(HEAD_PAD - QK)], axis=1)
    return rc, rs1, rs2


def _mixer_and_ffn(x, mods, lw, rw_t, rb, seq, tm, rope_tabs, ctx_kv, filt, dft_tabs):
    shift1, scale1, gate1, shift2, scale2, gate2 = mods
    q, k, vt, m, phy, gt = _premix(x, shift1, scale1, lw, rope_tabs, seq, tm)
    a = _attention(q, k, vt, ctx_kv, seq, min(seq, ATTN_TQ), lw['attn_bound'])
    if dft_tabs is not None:
        b = _hyena_long(phy, lw, filt, seq, dft_tabs)
    else:
        b = _hyena_short(phy, lw, filt, seq)
    xn, h2, idx, wts = _merge(a, b, m, gt, x, gate1, shift2, scale2, lw, rw_t, rb, seq, tm)
    return _moe(h2, idx, wts, xn, gate2, lw, seq), k, vt


def _forward(p):
    x, ctx = p['x'], p['ctx']
    batch, seq, _ = x.shape
    lc = ctx.shape[1]
    depth = p['w_mod'].shape[0]

    cvecs = jnp.concatenate([p['c'], p['c_ctx'][None], jnp.zeros((8 - batch - 1, D_MODEL), F32)], axis=0)
    mod_all = _modvec(cvecs, p['w_mod'], p['b_mod'])
    rw_t = p['router_w'].T
    rb = p['router_b'].reshape(N_EXPERTS, 1)
    rope_tabs = _rope_tables(seq)
    dft_tabs = _dft_tables(seq)
    z_lat, z_ctx = _hy_embedding(seq), _hy_embedding(lc)

    xl = x.reshape(batch * seq, D_MODEL)
    xc = ctx.reshape(batch * lc, D_MODEL)
    tm_lat = min(seq, 512)
    for l in range(depth):
        lw = _layer_weights(p, l)
        mod = mod_all[l].reshape(8, N_MOD, D_MODEL)
        mods_lat = [mod[:batch, j].reshape(batch, 1, D_MODEL) for j in range(N_MOD)]
        mods_ctx = [jnp.broadcast_to(mod[batch, j].reshape(1, 1, D_MODEL), (batch, 1, D_MODEL))
                    for j in range(N_MOD)]
        if l == depth - 1:
            _, k_c, vt_c, _, _, _ = _premix(xc, mods_ctx[0], mods_ctx[1], lw, None, lc, lc)
        else:
            xc, k_c, vt_c = _mixer_and_ffn(xc, mods_ctx, lw, rw_t, rb, lc, lc, None, None,
                                           _hy_filters(z_ctx, lw, lc), None)
        filt = _hy_filters(z_lat, lw, min(seq, 2048))
        xl, _, _ = _mixer_and_ffn(xl, mods_lat, lw, rw_t, rb, seq, tm_lat, rope_tabs, (k_c, vt_c), filt, dft_tabs)
    return xl.reshape(batch, seq, D_MODEL)


def kernel(x, c, ctx, c_ctx, w_mod, b_mod, norm1_g, norm2_g, w_in, g_qa, w_uq, g_kva, w_ukv, g_qn, g_kn,
           hy_conv_w, hy_conv_b, hy_w1, hy_b1, hy_w2, hy_b2, hy_w3, hy_b3, hy_freq, hy_decay, hy_bias,
           gm_norm_g, gm_ws, gm_bs, w_pa, w_pb, w_pc, w_out, router_w, router_b,
           moe_w_gate, moe_w_up, moe_w_down):
    return _forward(dict(
        x=x, c=c, ctx=ctx, c_ctx=c_ctx, w_mod=w_mod, b_mod=b_mod, norm1_g=norm1_g, norm2_g=norm2_g, w_in=w_in,
        g_qa=g_qa, w_uq=w_uq, g_kva=g_kva, w_ukv=w_ukv, g_qn=g_qn, g_kn=g_kn, hy_conv_w=hy_conv_w,
        hy_conv_b=hy_conv_b, hy_w1=hy_w1, hy_b1=hy_b1, hy_w2=hy_w2, hy_b2=hy_b2, hy_w3=hy_w3, hy_b3=hy_b3,
        hy_freq=hy_freq, hy_decay=hy_decay, hy_bias=hy_bias, gm_norm_g=gm_norm_g, gm_ws=gm_ws, gm_bs=gm_bs,
        w_pa=w_pa, w_pb=w_pb, w_pc=w_pc, w_out=w_out, router_w=router_w, router_b=router_b,
        moe_w_gate=moe_w_gate, moe_w_up=moe_w_up, moe_w_down=moe_w_down))
```

```python
import functools
import math

import jax
import jax.numpy as jnp
from jax import lax
from jax.experimental import pallas as pl
from jax.experimental.pallas import tpu as pltpu

F32 = jnp.float32
BF16 = jnp.bfloat16
HIGHEST = lax.Precision.HIGHEST

D_MODEL = 1024
GRID_W = 64
EPS = 1e-6
N_MOD = 6

HEADS = 8
Q_RANK = 384
KV_RANK = 256
NOPE = 64
ROPE = 32
QK = NOPE + ROPE
VDIM = 64
HEAD_PAD = 128
ROPE_THETA = 10000.0
V_ROWS = 80

HY_W = 256
HY_BANDS = 16
GM_W = 256
GM_CHUNK = 128
GM_GROUPS = 4

OFF_Q = 0
OFF_KV = OFF_Q + Q_RANK
OFF_KR = OFF_KV + KV_RANK
OFF_HY = OFF_KR + ROPE
OFF_GM = OFF_HY + 3 * HY_W
OFF_GT = OFF_GM + 2 * GM_W

P_Q = 0
P_KV = P_Q + Q_RANK
P_KR = P_KV + KV_RANK
P_HY = P_KR + HEAD_PAD
P_GM = P_HY + 3 * HY_W
P_GT = P_GM + 2 * GM_W
P_W = P_GT + 3 * D_MODEL

N_EXPERTS = 16
N_GROUPS = 4
EXP_PER_GROUP = 4
TOP_K = 2
D_EXPERT = 512
MOE_BM = 256
MOE_TC = 512

DFT_N2 = 256
DFT_K1_STEP = 2
DFT_COL_TILE = 8192

TOKEN_TILE = 512
HY_CONV_TILE = 1024
HY_FILTER_TILE = 2048

VMEM_LIMIT = 56 * 1024 * 1024
ATTN_TQ = 512
ATTN_TK = 2048
ATTN_BOUND_MAX = 50.0
NEG_BIG = -1e30
LOG2E = 1.4426950408889634
QSCALE = QK ** -0.5 * LOG2E


def _cparams(sem):
    return pltpu.CompilerParams(dimension_semantics=sem, vmem_limit_bytes=VMEM_LIMIT)


def _rms(x):
    return x * lax.rsqrt(jnp.mean(x * x, axis=-1, keepdims=True) + EPS)


def _sigmoid(x):
    return 0.5 * jnp.tanh(0.5 * x) + 0.5


def _nt_dot(a, b):
    return lax.dot_general(a, b, (((1,), (1,)), ((), ())), preferred_element_type=F32)


def _modvec_kernel(c_ref, w_ref, b_ref, o_ref):
    cv = c_ref[...]
    s = cv * jax.nn.sigmoid(cv)
    o_ref[0] = jnp.dot(s, w_ref[0], preferred_element_type=F32, precision=HIGHEST) + b_ref[0]


def _modvec(cvecs, w_mod, b_mod):
    depth = w_mod.shape[0]
    tn = 1536
    return pl.pallas_call(
        _modvec_kernel,
        out_shape=jax.ShapeDtypeStruct((depth, 8, N_MOD * D_MODEL), F32),
        grid=(depth, N_MOD * D_MODEL // tn),
        in_specs=[pl.BlockSpec((8, D_MODEL), lambda l, j: (0, 0)),
                  pl.BlockSpec((1, D_MODEL, tn), lambda l, j: (l, 0, j)),
                  pl.BlockSpec((1, 1, tn), lambda l, j: (l, 0, j))],
        out_specs=pl.BlockSpec((1, 8, tn), lambda l, j: (l, 0, j)),
        compiler_params=_cparams(("arbitrary", "arbitrary")),
        name="modvec",
    )(cvecs, w_mod, b_mod.reshape(depth, 1, N_MOD * D_MODEL))


def _head_norm_rope(xh, gain, rope):
    ms = jnp.sum(xh * xh, axis=-1, keepdims=True) * (1.0 / QK)
    xh = xh * lax.rsqrt(ms + EPS) * gain
    if rope is not None:
        rc, rs1, rs2 = rope
        xh = xh * rc + pltpu.roll(xh, ROPE // 2, 1) * rs1 + pltpu.roll(xh, HEAD_PAD - ROPE // 2, 1) * rs2
    return xh


def _premix_kernel(*refs, use_rope, tm):
    if use_rope:
        (x_ref, shift_ref, scale_ref, g1_ref, win_ref, gqa_ref, wuq_ref, gqn_ref, gkva_ref, wk_ref, wvt_ref,
         gkn_ref, gmg_ref, gmw_ref, gmb_ref, qpad_ref, kpad_ref, rc_ref, rs1_ref, rs2_ref,
         q_ref, k_ref, vt_ref, m_ref, hy_ref, gt_ref) = refs
        rope = (rc_ref[...], rs1_ref[...], rs2_ref[...])
    else:
        (x_ref, shift_ref, scale_ref, g1_ref, win_ref, gqa_ref, wuq_ref, gqn_ref, gkva_ref, wk_ref, wvt_ref,
         gkn_ref, gmg_ref, gmw_ref, gmb_ref, qpad_ref, kpad_ref,
         q_ref, k_ref, vt_ref, m_ref, hy_ref, gt_ref) = refs
        rope = None

    x = x_ref[...]
    h = _rms(x) * g1_ref[...]
    h = h * (1.0 + scale_ref[0]) + shift_ref[0]
    hb = h.astype(BF16)

    def proj(lo, width):
        return jnp.dot(hb, win_ref[:, lo:lo + width], preferred_element_type=F32)

    qa = (_rms(proj(P_Q, Q_RANK)) * gqa_ref[...]).astype(BF16)
    q = jnp.dot(qa, wuq_ref[...], preferred_element_type=F32)
    for hh in range(HEADS):
        qh = _head_norm_rope(q[:, hh * HEAD_PAD:(hh + 1) * HEAD_PAD], gqn_ref[...], rope)
        q_ref[:, hh * HEAD_PAD:(hh + 1) * HEAD_PAD] = (qh * QSCALE + qpad_ref[...]).astype(BF16)

    kva = (_rms(proj(P_KV, KV_RANK)) * gkva_ref[...]).astype(BF16)
    kr = pltpu.roll(proj(P_KR, HEAD_PAD), NOPE, 1)
    kn = jnp.dot(kva, wk_ref[...], preferred_element_type=F32)
    for hh in range(HEADS):
        kh = _head_norm_rope(kn[:, hh * HEAD_PAD:(hh + 1) * HEAD_PAD] + kr, gkn_ref[...], rope)
        k_ref[:, hh * HEAD_PAD:(hh + 1) * HEAD_PAD] = (kh + kpad_ref[...]).astype(BF16)
    vt = _nt_dot(wvt_ref[...], kva)
    row = lax.broadcasted_iota(jnp.int32, (V_ROWS - VDIM, tm), 0)
    ones_rows = jnp.where(row == 0, 1.0, 0.0).astype(BF16)
    for hh in range(HEADS):
        vt_ref[0, 0, hh * V_ROWS:hh * V_ROWS + VDIM, :] = vt[hh * VDIM:(hh + 1) * VDIM].astype(BF16)
        vt_ref[0, 0, hh * V_ROWS + VDIM:(hh + 1) * V_ROWS, :] = ones_rows

    gg = jax.nn.gelu(proj(P_GM, 2 * GM_W), approximate=True)
    gu = gg[:, :GM_W]
    gv = (_rms(gg[:, GM_W:]) * gmg_ref[...]).astype(BF16)
    grp = lax.broadcasted_iota(jnp.int32, (GM_CHUNK, GM_W), 1) // (GM_W // GM_GROUPS)
    for ci in range(tm // GM_CHUNK):
        vc = gv[ci * GM_CHUNK:(ci + 1) * GM_CHUNK]
        s = jnp.zeros((GM_CHUNK, GM_W), F32)
        for g in range(GM_GROUPS):
            sg = jnp.dot(gmw_ref[g], vc, preferred_element_type=F32)
            s = jnp.where(grp == g, sg, s)
        m_ref[ci * GM_CHUNK:(ci + 1) * GM_CHUNK, :] = (
            gu[ci * GM_CHUNK:(ci + 1) * GM_CHUNK] * (s + gmb_ref[...])).astype(BF16)

    hy_ref[...] = proj(P_HY, 3 * HY_W).astype(BF16)
    for j in range(3):
        gt_ref[:, j * D_MODEL:(j + 1) * D_MODEL] = _sigmoid(proj(P_GT + j * D_MODEL, D_MODEL)).astype(BF16)


def _premix(x, shift, scale, lw, rope_tabs, seq, tm):
    n = x.shape[0]
    tiles_per_seq = seq // tm
    batch = n // seq
    use_rope = rope_tabs is not None
    const = lambda i: (0, 0)
    in_specs = [
        pl.BlockSpec((tm, D_MODEL), lambda i: (i, 0)),
        pl.BlockSpec((1, 1, D_MODEL), lambda i: (i // tiles_per_seq, 0, 0)),
        pl.BlockSpec((1, 1, D_MODEL), lambda i: (i // tiles_per_seq, 0, 0)),
        pl.BlockSpec((1, D_MODEL), const),
        pl.BlockSpec((D_MODEL, P_W), const, pipeline_mode=pl.Buffered(1)),
        pl.BlockSpec((1, Q_RANK), const),
        pl.BlockSpec((Q_RANK, HEADS * HEAD_PAD), const),
        pl.BlockSpec((1, HEAD_PAD), const),
        pl.BlockSpec((1, KV_RANK), const),
        pl.BlockSpec((KV_RANK, HEADS * HEAD_PAD), const),
        pl.BlockSpec((HEADS * VDIM, KV_RANK), const),
        pl.BlockSpec((1, HEAD_PAD), const),
        pl.BlockSpec((1, GM_W), const),
        pl.BlockSpec((GM_GROUPS, GM_CHUNK, GM_CHUNK), lambda i: (0, 0, 0)),
        pl.BlockSpec((GM_CHUNK, GM_W), const),
        pl.BlockSpec((1, HEAD_PAD), const),
        pl.BlockSpec((1, HEAD_PAD), const),
    ]
    args = [x, shift, scale, lw['g1'], lw['w_in'], lw['g_qa'], lw['w_uq'], lw['g_qn'], lw['g_kva'], lw['w_k'],
            lw['w_vt'], lw['g_kn'], lw['gm_g'], lw['gm_w'], lw['gm_b'], lw['q_pad'], lw['k_pad']]
    tkv = min(seq, ATTN_TK)
    sub = tkv // tm
    if use_rope:
        in_specs += [pl.BlockSpec((tm, HEAD_PAD), lambda i: (i % tiles_per_seq, 0))] * 3
        args += list(rope_tabs)
    out_shape = (
        jax.ShapeDtypeStruct((n, HEADS * HEAD_PAD), BF16),
        jax.ShapeDtypeStruct((n, HEADS * HEAD_PAD), BF16),
        jax.ShapeDtypeStruct((batch, seq // tkv, HEADS * V_ROWS, tkv), BF16),
        jax.ShapeDtypeStruct((n, GM_W), BF16),
        jax.ShapeDtypeStruct((n, 3 * HY_W), BF16),
        jax.ShapeDtypeStruct((n, 3 * D_MODEL), BF16),
    )
    out_specs = (
        pl.BlockSpec((tm, HEADS * HEAD_PAD), lambda i: (i, 0)),
        pl.BlockSpec((tm, HEADS * HEAD_PAD), lambda i: (i, 0)),
        pl.BlockSpec((1, 1, HEADS * V_ROWS, tm),
                     lambda i: (i // tiles_per_seq, (i % tiles_per_seq) // sub, 0, (i % tiles_per_seq) % sub)),
        pl.BlockSpec((tm, GM_W), lambda i: (i, 0)),
        pl.BlockSpec((tm, 3 * HY_W), lambda i: (i, 0)),
        pl.BlockSpec((tm, 3 * D_MODEL), lambda i: (i, 0)),
    )
    return pl.pallas_call(
        functools.partial(_premix_kernel, use_rope=use_rope, tm=tm),
        out_shape=out_shape, grid=(n // tm,), in_specs=in_specs, out_specs=out_specs,
        compiler_params=_cparams(("arbitrary",)), name="premix",
    )(*args)


def _attn_kernel(*refs, n_chunks, has_ctx, tq):
    if has_ctx:
        q_ref, k_ref, vt_ref, kc_ref, vtc_ref, o_ref = refs
    else:
        q_ref, k_ref, vt_ref, o_ref = refs
    tk = vt_ref.shape[-1]

    def step(hh, kc, vtc, m, acc):
        s = _nt_dot(kc, q_ref[:, hh * HEAD_PAD:(hh + 1) * HEAD_PAD])
        m_new = jnp.maximum(m, jnp.max(s, axis=0, keepdims=True))
        p = jnp.exp2(s - m_new).astype(BF16)
        alpha = jnp.exp2(m - m_new)
        return m_new, acc * alpha + jnp.dot(vtc, p, preferred_element_type=F32)

    def body(i, carry):
        out = []
        for hh in range(2):
            m, acc = carry[hh]
            start = pl.multiple_of(i * tk, tk)
            kc = k_ref[pl.ds(start, tk), hh * HEAD_PAD:(hh + 1) * HEAD_PAD]
            vtc = vt_ref[0, i, hh * V_ROWS:(hh + 1) * V_ROWS, :]
            out.append(step(hh, kc, vtc, m, acc))
        return tuple(out)

    init = tuple((jnp.full((1, tq), NEG_BIG, F32), jnp.zeros((V_ROWS, tq), F32)) for _ in range(2))
    carry = lax.fori_loop(0, n_chunks, body, init)
    outs = []
    for hh in range(2):
        m, acc = carry[hh]
        if has_ctx:
            m, acc = step(hh, kc_ref[:, hh * HEAD_PAD:(hh + 1) * HEAD_PAD],
                          vtc_ref[0, 0, hh * V_ROWS:(hh + 1) * V_ROWS, :], m, acc)
        outs.append(acc[:VDIM] / acc[VDIM:VDIM + 1])
    o_ref[...] = jnp.concatenate(outs, axis=0).T.astype(BF16)


def _attn_bounded_kernel(*refs, n_chunks, has_ctx, tq):
    if has_ctx:
        q_ref, k_ref, vt_ref, kc_ref, vtc_ref, o_ref = refs
    else:
        q_ref, k_ref, vt_ref, o_ref = refs
    tk = vt_ref.shape[-1]

    def step(hh, kc, vtc, acc):
        s = _nt_dot(kc, q_ref[:, hh * HEAD_PAD:(hh + 1) * HEAD_PAD])
        return acc + jnp.dot(vtc, jnp.exp2(s).astype(BF16), preferred_element_type=F32)

    def body(i, carry):
        start = pl.multiple_of(i * tk, tk)
        return tuple(step(hh, k_ref[pl.ds(start, tk), hh * HEAD_PAD:(hh + 1) * HEAD_PAD],
                          vt_ref[0, i, hh * V_ROWS:(hh + 1) * V_ROWS, :], carry[hh]) for hh in range(2))

    carry = lax.fori_loop(0, n_chunks, body, tuple(jnp.zeros((V_ROWS, tq), F32) for _ in range(2)), unroll=True)
    outs = []
    for hh in range(2):
        acc = carry[hh]
        if has_ctx:
            acc = step(hh, kc_ref[:, hh * HEAD_PAD:(hh + 1) * HEAD_PAD],
                       vtc_ref[0, 0, hh * V_ROWS:(hh + 1) * V_ROWS, :], acc)
        outs.append(acc[:VDIM] / acc[VDIM:VDIM + 1])
    o_ref[...] = jnp.concatenate(outs, axis=0).T.astype(BF16)


def _attention(q, k, vt, ctx_kv, seq, tq, bound):
    n = q.shape[0]
    batch = n // seq
    n_chunks, tk = vt.shape[1], vt.shape[3]
    q_tiles = seq // tq
    has_ctx = ctx_kv is not None
    in_specs = [
        pl.BlockSpec((tq, 2 * HEAD_PAD), lambda b, j, i: (b * q_tiles + i, j)),
        pl.BlockSpec((seq, 2 * HEAD_PAD), lambda b, j, i: (b, j)),
        pl.BlockSpec((1, n_chunks, 2 * V_ROWS, tk), lambda b, j, i: (b, 0, j, 0)),
    ]
    args = [q, k, vt]
    if has_ctx:
        kc, vtc = ctx_kv
        lc = vtc.shape[3]
        in_specs += [pl.BlockSpec((lc, 2 * HEAD_PAD), lambda b, j, i: (b, j)),
                     pl.BlockSpec((1, 1, 2 * V_ROWS, lc), lambda b, j, i: (b, 0, j, 0))]
        args += [kc, vtc]
    def call(body, name):
        return pl.pallas_call(
            functools.partial(body, n_chunks=n_chunks, has_ctx=has_ctx, tq=tq),
            out_shape=jax.ShapeDtypeStruct((n, HEADS * VDIM), BF16),
            grid=(batch, HEADS // 2, q_tiles),
            in_specs=in_specs,
            out_specs=pl.BlockSpec((tq, 2 * VDIM), lambda b, j, i: (b * q_tiles + i, j)),
            compiler_params=_cparams(("arbitrary", "arbitrary", "arbitrary")),
            name=name,
        )(*args)

    return lax.cond(bound < ATTN_BOUND_MAX,
                    lambda: call(_attn_bounded_kernel, "attn_bounded"),
                    lambda: call(_attn_kernel, "attn_online"))


def _hy_filter_kernel(zz_ref, w1_ref, b1_ref, w2_ref, b2_ref, w3_ref, b3_ref, fr_ref, dec_ref, f_ref, ss_ref, *, emb):
    fr = fr_ref[...]
    dec = jnp.abs(dec_ref[...])
    zz = zz_ref[...]
    hdn = jnp.sin(fr * (jnp.dot(zz, w1_ref[...], preferred_element_type=F32, precision=HIGHEST) + b1_ref[...]))
    hdn = jnp.sin(fr * (jnp.dot(hdn, w2_ref[...], preferred_element_type=F32, precision=HIGHEST) + b2_ref[...]))
    k = jnp.dot(hdn, w3_ref[...], preferred_element_type=F32, precision=HIGHEST) + b3_ref[...]
    col = lax.broadcasted_iota(jnp.int32, k.shape, 1)
    k = k * jnp.exp(-jnp.where(col < HY_W, zz[:, 0:1], zz[:, emb:emb + 1]) * dec)
    first_block = pl.program_id(0) == 0
    row = lax.broadcasted_iota(jnp.int32, k.shape, 0)
    k = jnp.where(first_block & (row == 0) & (col >= HY_W), 0.0, k)
    f_ref[0] = k[:, :HY_W].astype(BF16)
    f_ref[1] = k[:, HY_W:].astype(BF16)
    hid = hdn.shape[1] // 2
    k_0 = jnp.dot(pltpu.roll(hdn[0:8], hid, 1), w3_ref[...], preferred_element_type=F32,
                  precision=HIGHEST) + b3_ref[...]
    k_0 = k_0[0:1] * jnp.exp(-zz[0:1, 0:1] * dec)
    extra = jnp.where(first_block & (col[0:1] >= HY_W), k_0 * k_0, 0.0)
    ss_ref[0] = jnp.sum(k * k, axis=0, keepdims=True) + extra


def _hy_filters(z, lw, tr):
    seq, emb = z.shape
    nblk = seq // tr
    hid2 = lw['hy_w2p'].shape[0]
    const = lambda i: (0, 0)
    zz = jnp.concatenate([z, jnp.concatenate([z[0:1], z[:0:-1]], axis=0)], axis=1)
    return pl.pallas_call(
        functools.partial(_hy_filter_kernel, emb=emb),
        out_shape=(jax.ShapeDtypeStruct((2, seq, HY_W), BF16), jax.ShapeDtypeStruct((nblk, 1, 2 * HY_W), F32)),
        grid=(nblk,),
        in_specs=[pl.BlockSpec((tr, 2 * emb), lambda i: (i, 0)),
                  pl.BlockSpec((2 * emb, hid2), const), pl.BlockSpec((1, hid2), const),
                  pl.BlockSpec((hid2, hid2), const), pl.BlockSpec((1, hid2), const),
                  pl.BlockSpec((hid2, 2 * HY_W), const), pl.BlockSpec((1, 2 * HY_W), const),
                  pl.BlockSpec((1, hid2), const), pl.BlockSpec((1, 2 * HY_W), const)],
        out_specs=(pl.BlockSpec((2, tr, HY_W), lambda i: (0, i, 0)),
                   pl.BlockSpec((1, 1, 2 * HY_W), lambda i: (i, 0, 0))),
        compiler_params=_cparams(("arbitrary",)), name="hy_filter",
    )(zz, lw['hy_w1p'], lw['hy_b1p'], lw['hy_w2p'], lw['hy_b2p'], lw['hy_w3p'], lw['hy_b3'], lw['hy_freqp'],
      lw['hy_decay'])


def _hy_conv3_kernel(p_ref, prev_ref, next_ref, w_ref, b_ref, x0_ref, u_ref, *, tiles_per_seq, tr):
    i = pl.program_id(0)
    p = p_ref[...].astype(F32)
    first = (i % tiles_per_seq) == 0
    last = (i % tiles_per_seq) == tiles_per_seq - 1
    prev_row = jnp.where(first, 0.0, prev_ref[...].astype(F32)[15:16, :])
    next_row = jnp.where(last, 0.0, next_ref[...].astype(F32)[0:1, :])
    row = lax.broadcasted_iota(jnp.int32, p.shape, 0)
    p_prev = jnp.where(row == 0, prev_row, pltpu.roll(p, 1, 0))
    p_next = jnp.where(row == tr - 1, next_row, pltpu.roll(p, tr - 1, 0))
    z = b_ref[...] + p_prev * w_ref[0:1, :] + p * w_ref[1:2, :] + p_next * w_ref[2:3, :]
    x0_ref[...] = z[:, :HY_W].astype(BF16)
    u_ref[...] = (z[:, 2 * HY_W:] * z[:, HY_W:2 * HY_W]).astype(BF16)


def _hy_conv3(phy, conv_w, conv_b, seq, tr):
    n = phy.shape[0]
    tiles_per_seq = seq // tr
    hb = tr // 16
    nhb = n // 16
    return pl.pallas_call(
        functools.partial(_hy_conv3_kernel, tiles_per_seq=tiles_per_seq, tr=tr),
        out_shape=(jax.ShapeDtypeStruct((n, HY_W), BF16), jax.ShapeDtypeStruct((n, HY_W), BF16)),
        grid=(n // tr,),
        in_specs=[pl.BlockSpec((tr, 3 * HY_W), lambda i: (i, 0)),
                  pl.BlockSpec((16, 3 * HY_W), lambda i: (jnp.maximum(i * hb - 1, 0), 0)),
                  pl.BlockSpec((16, 3 * HY_W), lambda i: (jnp.minimum((i + 1) * hb, nhb - 1), 0)),
                  pl.BlockSpec((3, 3 * HY_W), lambda i: (0, 0)),
                  pl.BlockSpec((1, 3 * HY_W), lambda i: (0, 0))],
        out_specs=(pl.BlockSpec((tr, HY_W), lambda i: (i, 0)), pl.BlockSpec((tr, HY_W), lambda i: (i, 0))),
        compiler_params=_cparams(("arbitrary",)), name="hy_conv3",
    )(phy, phy, phy, conv_w, conv_b)


def _dft_outer_kernel(wr_ref, wi_ref, u_ref, ar_ref, ai_ref):
    u = u_ref[0]
    ar_ref[0] = jnp.dot(wr_ref[...], u, preferred_element_type=F32).astype(BF16)
    ai_ref[0] = jnp.dot(wi_ref[...], u, preferred_element_type=F32).astype(BF16)


def _dft_outer(wr, wi, u, tc):
    nb, kk, cols = u.shape
    n1 = wr.shape[0]
    return pl.pallas_call(
        _dft_outer_kernel,
        out_shape=(jax.ShapeDtypeStruct((nb, n1, cols), BF16),) * 2,
        grid=(nb, cols // tc),
        in_specs=[pl.BlockSpec((n1, kk), lambda b, j: (0, 0)), pl.BlockSpec((n1, kk), lambda b, j: (0, 0)),
                  pl.BlockSpec((1, kk, tc), lambda b, j: (b, 0, j))],
        out_specs=(pl.BlockSpec((1, n1, tc), lambda b, j: (b, 0, j)),) * 2,
        compiler_params=_cparams(("arbitrary", "arbitrary")), name="dft_outer",
    )(wr, wi, u)


def _cdot(mr, mi, xr, xi):
    rr = jnp.dot(mr, xr, preferred_element_type=F32) - jnp.dot(mi, xi, preferred_element_type=F32)
    ri = jnp.dot(mr, xi, preferred_element_type=F32) + jnp.dot(mi, xr, preferred_element_type=F32)
    return rr, ri


def _dft_inner_fwd_kernel(mr_ref, mi_ref, ar_ref, ai_ref, fr_ref, fi_ref):
    for j in range(DFT_K1_STEP):
        fr, fi = _cdot(mr_ref[j], mi_ref[j], ar_ref[0, j], ai_ref[0, j])
        fr_ref[j] = fr.astype(BF16)
        fi_ref[j] = fi.astype(BF16)


def _dft_inner_fwd(mr, mi, ar, ai):
    n1, n2, _ = mr.shape
    ch = ar.shape[-1]
    mspec = pl.BlockSpec((DFT_K1_STEP, n2, n2), lambda k: (k, 0, 0))
    aspec = pl.BlockSpec((1, DFT_K1_STEP, n2, ch), lambda k: (0, k, 0, 0))
    ospec = pl.BlockSpec((DFT_K1_STEP, n2, ch), lambda k: (k, 0, 0))
    return pl.pallas_call(
        _dft_inner_fwd_kernel,
        out_shape=(jax.ShapeDtypeStruct((n1, n2, ch), BF16),) * 2,
        grid=(n1 // DFT_K1_STEP,), in_specs=[mspec, mspec, aspec, aspec], out_specs=(ospec, ospec),
        compiler_params=_cparams(("arbitrary",)), name="dft_inner_fwd",
    )(mr, mi, ar, ai)


def _dft_inner_conv_kernel(mr_ref, mi_ref, tr_ref, ti_ref, ar_ref, ai_ref, fr_ref, fi_ref, br_ref, bi_ref, *, nb):
    for j in range(DFT_K1_STEP):
        fr = fr_ref[j].astype(F32)
        fi = fi_ref[j].astype(F32)
        for b in range(nb):
            ur, ui = _cdot(mr_ref[j], mi_ref[j], ar_ref[b, j], ai_ref[b, j])
            yr = (ur * fr - ui * fi).astype(BF16)
            yi = (ur * fi + ui * fr).astype(BF16)
            br, bi = _cdot(tr_ref[j], ti_ref[j], yr, yi)
            br_ref[b, j] = br.astype(BF16)
            bi_ref[b, j] = bi.astype(BF16)


def _dft_inner_conv(mr, mi, tr, ti, ar, ai, fr, fi):
    n1, n2, _ = mr.shape
    nb, _, _, ch = ar.shape
    mspec = pl.BlockSpec((DFT_K1_STEP, n2, n2), lambda k: (k, 0, 0))
    aspec = pl.BlockSpec((nb, DFT_K1_STEP, n2, ch), lambda k: (0, k, 0, 0))
    fspec = pl.BlockSpec((DFT_K1_STEP, n2, ch), lambda k: (k, 0, 0))
    return pl.pallas_call(
        functools.partial(_dft_inner_conv_kernel, nb=nb),
        out_shape=(jax.ShapeDtypeStruct((nb, n1, n2, ch), BF16),) * 2,
        grid=(n1 // DFT_K1_STEP,), in_specs=[mspec, mspec, mspec, mspec, aspec, aspec, fspec, fspec],
        out_specs=(aspec, aspec),
        compiler_params=_cparams(("arbitrary",)), name="dft_inner_conv",
    )(mr, mi, tr, ti, ar, ai, fr, fi)


def _filter_scale(ss_ref):
    ss = jnp.sum(ss_ref[...], axis=0)
    return lax.rsqrt(ss[:, :HY_W] + ss[:, HY_W:] + EPS)


def _hy_final_kernel(cr_ref, ci_ref, br_ref, bi_ref, x0_ref, u_ref, ss_ref, bias_ref, o_ref, *, reps):
    y = (jnp.dot(cr_ref[...], br_ref[0], preferred_element_type=F32)
         + jnp.dot(ci_ref[...], bi_ref[0], preferred_element_type=F32))
    scale = jnp.tile(_filter_scale(ss_ref), (1, reps))
    bias = jnp.tile(bias_ref[...], (1, reps))
    u = u_ref[0].astype(F32)
    o_ref[0] = (x0_ref[0].astype(F32) * (y * scale + u * bias)).astype(BF16)


def _hy_final(cr, ci, br, bi, x0, u, ss, bias, tc):
    nb, n1, cols = br.shape
    n1h = cr.shape[0]
    nblk = ss.shape[0]
    cspec = pl.BlockSpec((n1h, n1), lambda b, j: (0, 0))
    bspec = pl.BlockSpec((1, n1, tc), lambda b, j: (b, 0, j))
    xspec = pl.BlockSpec((1, n1h, tc), lambda b, j: (b, 0, j))
    return pl.pallas_call(
        functools.partial(_hy_final_kernel, reps=tc // HY_W),
        out_shape=jax.ShapeDtypeStruct((nb, n1h, cols), BF16),
        grid=(nb, cols // tc),
        in_specs=[cspec, cspec, bspec, bspec, xspec, xspec,
                  pl.BlockSpec((nblk, 1, 2 * HY_W), lambda b, j: (0, 0, 0)),
                  pl.BlockSpec((1, HY_W), lambda b, j: (0, 0))],
        out_specs=xspec,
        compiler_params=_cparams(("arbitrary", "arbitrary")), name="hy_final",
    )(cr, ci, br, bi, x0, u, ss, bias)


def _hy_direct_kernel(f_ref, x0_ref, u_ref, ss_ref, bias_ref, o_ref, kk_ref, *, seq):
    kk_ref[0:seq, :] = f_ref[1].astype(F32)
    kk_ref[seq:2 * seq, :] = f_ref[0].astype(F32)
    u = u_ref[0].astype(F32)
    y = jnp.zeros((seq, HY_W), F32)
    for j in range(seq):
        y = y + kk_ref[seq - j:2 * seq - j, :] * u[j:j + 1, :]
    o_ref[0] = (x0_ref[0].astype(F32) * (y * _filter_scale(ss_ref) + u * bias_ref[...])).astype(BF16)


def _hy_direct(f, x0, u, ss, bias):
    nb, seq, _ = u.shape
    nblk = ss.shape[0]
    xspec = pl.BlockSpec((1, seq, HY_W), lambda b: (b, 0, 0))
    return pl.pallas_call(
        functools.partial(_hy_direct_kernel, seq=seq),
        out_shape=jax.ShapeDtypeStruct((nb, seq, HY_W), BF16),
        grid=(nb,),
        in_specs=[pl.BlockSpec((2, seq, HY_W), lambda b: (0, 0, 0)), xspec, xspec,
                  pl.BlockSpec((nblk, 1, 2 * HY_W), lambda b: (0, 0, 0)),
                  pl.BlockSpec((1, HY_W), lambda b: (0, 0))],
        out_specs=xspec,
        scratch_shapes=[pltpu.VMEM((2 * seq, HY_W), F32)],
        compiler_params=_cparams(("arbitrary",)), name="hy_direct",
    )(f, x0, u, ss, bias)


def _hy_embedding(seq):
    t = jnp.arange(seq, dtype=F32)
    t_unit = t / max(seq - 1, 1)
    bands = jnp.linspace(1e-4, HY_BANDS - 1, HY_BANDS, dtype=F32)
    ang = (2 * jnp.pi / seq) * t[:, None] * bands[None, :]
    return jnp.concatenate([t_unit[:, None], jnp.cos(ang), -jnp.sin(ang)], axis=-1)


def _dft_tables(seq):
    n = 2 * seq
    n2 = DFT_N2
    n1 = n // n2
    a = jnp.arange(n1, dtype=jnp.int32)
    th1 = (2 * jnp.pi / n1) * ((a[:, None] * a[None, :]) % n1).astype(F32)
    w1r, w1i = jnp.cos(th1), -jnp.sin(th1)
    b = jnp.arange(n2, dtype=jnp.int32)
    tha = (2 * jnp.pi / n) * (a[:, None] * b[None, :]).astype(F32)
    thb = (2 * jnp.pi / n2) * ((b[:, None] * b[None, :]) % n2).astype(F32)
    ar, ai = jnp.cos(tha), -jnp.sin(tha)
    br, bi = jnp.cos(thb), -jnp.sin(thb)
    mr = ar[:, None, :] * br[None] - ai[:, None, :] * bi[None]
    mi = ar[:, None, :] * bi[None] + ai[:, None, :] * br[None]
    tr = ar[:, :, None] * br[None] - ai[:, :, None] * bi[None]
    ti = -(ar[:, :, None] * bi[None] + ai[:, :, None] * br[None])
    return dict(
        w1r=w1r.astype(BF16), w1i=w1i.astype(BF16),
        mr=mr.astype(BF16), mi=mi.astype(BF16), tr=tr.astype(BF16), ti=ti.astype(BF16),
        cr=(w1r[:n1 // 2] / n).astype(BF16), ci=(w1i[:n1 // 2] / n).astype(BF16))


def _hyena_long(phy, lw, filt, seq, tabs):
    n = phy.shape[0]
    nb = n // seq
    n2 = DFT_N2
    n1 = 2 * seq // n2
    f, ss = filt
    x0, u = _hy_conv3(phy, lw['hy_conv_w'], lw['hy_conv_b'], seq, min(seq, HY_CONV_TILE))
    tc = min(n2 * HY_W, DFT_COL_TILE)
    far, fai = _dft_outer(tabs['w1r'], tabs['w1i'], f.reshape(1, n1, n2 * HY_W), tc)
    fr, fi = _dft_inner_fwd(tabs['mr'], tabs['mi'], far.reshape(1, n1, n2, HY_W), fai.reshape(1, n1, n2, HY_W))
    ar, ai = _dft_outer(tabs['w1r'][:, :n1 // 2], tabs['w1i'][:, :n1 // 2], u.reshape(nb, n1 // 2, n2 * HY_W), tc)
    br, bi = _dft_inner_conv(tabs['mr'], tabs['mi'], tabs['tr'], tabs['ti'],
                             ar.reshape(nb, n1, n2, HY_W), ai.reshape(nb, n1, n2, HY_W), fr, fi)
    out = _hy_final(tabs['cr'], tabs['ci'], br.reshape(nb, n1, n2 * HY_W), bi.reshape(nb, n1, n2 * HY_W),
                    x0.reshape(nb, n1 // 2, n2 * HY_W), u.reshape(nb, n1 // 2, n2 * HY_W), ss, lw['hy_bias'], tc)
    return out.reshape(n, HY_W)


def _hyena_short(phy, lw, filt, seq):
    n = phy.shape[0]
    nb = n // seq
    f, ss = filt
    x0, u = _hy_conv3(phy, lw['hy_conv_w'], lw['hy_conv_b'], seq, seq)
    out = _hy_direct(f, x0.reshape(nb, seq, HY_W), u.reshape(nb, seq, HY_W), ss, lw['hy_bias'])
    return out.reshape(n, HY_W)


def _pair_top2_sum(a, b, c, d):
    return jnp.maximum(jnp.maximum(jnp.maximum(a + b, a + c), jnp.maximum(a + d, b + c)),
                       jnp.maximum(b + d, c + d))


def _route(logits_t, rb):
    aff = jax.nn.sigmoid(logits_t)
    sel = aff + rb
    rows = [sel[e:e + 1, :] for e in range(N_EXPERTS)]
    affr = [aff[e:e + 1, :] for e in range(N_EXPERTS)]
    best, bidx = None, None
    for g in range(N_GROUPS):
        gs = _pair_top2_sum(*rows[g * EXP_PER_GROUP:(g + 1) * EXP_PER_GROUP])
        if g == 0:
            best, bidx = gs, jnp.zeros(gs.shape, jnp.int32)
        else:
            upd = gs > best
            bidx = jnp.where(upd, g, bidx)
            best = jnp.where(upd, gs, best)
    vals = [jnp.where(bidx == e // EXP_PER_GROUP, rows[e], -jnp.inf) for e in range(N_EXPERTS)]
    m1, i1 = vals[0], jnp.zeros(best.shape, jnp.int32)
    for e in range(1, N_EXPERTS):
        upd = vals[e] > m1
        i1 = jnp.where(upd, e, i1)
        m1 = jnp.where(upd, vals[e], m1)
    m2, i2 = jnp.full(best.shape, -jnp.inf, F32), jnp.zeros(best.shape, jnp.int32)
    for e in range(N_EXPERTS):
        cand = jnp.where(i1 == e, -jnp.inf, vals[e])
        upd = cand > m2
        i2 = jnp.where(upd, e, i2)
        m2 = jnp.where(upd, cand, m2)
    a1 = sum(jnp.where(i1 == e, affr[e], 0.0) for e in range(N_EXPERTS))
    a2 = sum(jnp.where(i2 == e, affr[e], 0.0) for e in range(N_EXPERTS))
    inv = 1.0 / (a1 + a2)
    return jnp.concatenate([i1, i2], axis=0), jnp.concatenate([a1 * inv, a2 * inv], axis=0)


def _merge_kernel(a_ref, b_ref, m_ref, gt_ref, x_ref, gate_ref, shift_ref, scale_ref, g2_ref,
                  wpa_ref, wpb_ref, wpc_ref, wout_ref, rwt_ref, rb_ref, xo_ref, h2_ref, idx_ref, wts_ref):
    y = gt_ref[:, 0:D_MODEL].astype(F32) * jnp.dot(a_ref[...], wpa_ref[...], preferred_element_type=F32)
    y = y + gt_ref[:, D_MODEL:2 * D_MODEL].astype(F32) * jnp.dot(b_ref[...], wpb_ref[...],
                                                                 preferred_element_type=F32)
    y = y + gt_ref[:, 2 * D_MODEL:].astype(F32) * jnp.dot(m_ref[...], wpc_ref[...], preferred_element_type=F32)
    y2 = jnp.dot(y.astype(BF16), wout_ref[...], preferred_element_type=F32)
    xn = x_ref[...] + gate_ref[0] * y2
    xo_ref[...] = xn
    h2 = _rms(xn) * g2_ref[...]
    h2 = h2 * (1.0 + scale_ref[0]) + shift_ref[0]
    h2_ref[...] = h2
    logits_t = lax.dot_general(rwt_ref[...], h2, (((1,), (1,)), ((), ())), preferred_element_type=F32,
                               precision=HIGHEST)
    idx_ref[...], wts_ref[...] = _route(logits_t, rb_ref[...])


def _merge(a, b, m, gt, x, gate, shift, scale, lw, rw_t, rb, seq, tm):
    n = x.shape[0]
    tiles_per_seq = seq // tm
    const = lambda i: (0, 0)
    row = lambda w: pl.BlockSpec((tm, w), lambda i: (i, 0))
    vec = pl.BlockSpec((1, 1, D_MODEL), lambda i: (i // tiles_per_seq, 0, 0))
    return pl.pallas_call(
        _merge_kernel,
        out_shape=(jax.ShapeDtypeStruct((n, D_MODEL), F32), jax.ShapeDtypeStruct((n, D_MODEL), F32),
                   jax.ShapeDtypeStruct((TOP_K, n), jnp.int32), jax.ShapeDtypeStruct((TOP_K, n), F32)),
        grid=(n // tm,),
        in_specs=[row(HEADS * VDIM), row(HY_W), row(GM_W), row(3 * D_MODEL), row(D_MODEL), vec, vec, vec,
                  pl.BlockSpec((1, D_MODEL), const),
                  pl.BlockSpec((HEADS * VDIM, D_MODEL), const), pl.BlockSpec((HY_W, D_MODEL), const),
                  pl.BlockSpec((GM_W, D_MODEL), const), pl.BlockSpec((D_MODEL, D_MODEL), const),
                  pl.BlockSpec((N_EXPERTS, D_MODEL), const), pl.BlockSpec((N_EXPERTS, 1), const)],
        out_specs=(row(D_MODEL), row(D_MODEL), pl.BlockSpec((TOP_K, tm), lambda i: (0, i)),
                   pl.BlockSpec((TOP_K, tm), lambda i: (0, i))),
        compiler_params=_cparams(("arbitrary",)), name="merge",
    )(a, b, m, gt, x, gate, shift, scale, lw['g2'], lw['w_pa'], lw['w_pb'], lw['w_pc'], lw['w_out'], rw_t, rb)


def _rows_copy(src_hbm, dst, sem, idx_ref, rows):
    return [pltpu.make_async_copy(src_hbm.at[pl.ds(idx_ref[0, 0, r], 1)], dst.at[pl.ds(r, 1)], sem)
            for r in range(rows)]


def _rows_wait(src_hbm, dst, sem, rows):
    pltpu.make_async_copy(src_hbm.at[pl.ds(0, rows)], dst, sem).wait()


def _gather_pipeline(src_hbm, cur_ref, nxt_ref, buf, sem, rows):
    i = pl.program_id(0)
    slot = i % 2

    @pl.when(i == 0)
    def _():
        for cp in _rows_copy(src_hbm, buf.at[0], sem.at[0], cur_ref, rows):
            cp.start()

    _rows_wait(src_hbm, buf.at[slot], sem.at[slot], rows)
    for cp in _rows_copy(src_hbm, buf.at[1 - slot], sem.at[1 - slot], nxt_ref, rows):
        cp.start()
    return slot


def _gather_drain(src_hbm, buf, sem, rows):
    i = pl.program_id(0)

    @pl.when(i == pl.num_programs(0) - 1)
    def _():
        _rows_wait(src_hbm, buf.at[1 - i % 2], sem.at[1 - i % 2], rows)


def _moe_dispatch_kernel(pos_ref, h_ref, xs_in_hbm, xs_hbm, sem, *, tc):
    del xs_in_hbm
    for r in range(tc):
        pltpu.make_async_copy(h_ref.at[pl.ds(r, 1)], xs_hbm.at[pl.ds(pos_ref[0, 0, r], 1)], sem).start()
    pltpu.make_async_copy(h_ref, xs_hbm.at[pl.ds(0, tc)], sem).wait()


def _moe_dispatch(pos_tiles, h2, rows, tc):
    n = h2.shape[0]
    return pl.pallas_call(
        functools.partial(_moe_dispatch_kernel, tc=tc),
        out_shape=jax.ShapeDtypeStruct((rows, D_MODEL), F32),
        grid=(n // tc,),
        in_specs=[pl.BlockSpec((1, 1, tc), lambda i: (i, 0, 0), memory_space=pltpu.SMEM),
                  pl.BlockSpec((tc, D_MODEL), lambda i: (i, 0)),
                  pl.BlockSpec(memory_space=pl.ANY)],
        out_specs=pl.BlockSpec(memory_space=pl.ANY),
        scratch_shapes=[pltpu.SemaphoreType.DMA],
        input_output_aliases={2: 0},
        compiler_params=_cparams(("arbitrary",)), name="moe_dispatch",
    )(pos_tiles, h2, jnp.zeros((rows, D_MODEL), F32))


def _moe_ffn_kernel(ea_ref, eb_ref, x_ref, wgua_ref, wda_ref, wgub_ref, wdb_ref, y_ref):
    del ea_ref, eb_ref
    xb = x_ref[...].astype(BF16)
    for j, (wgu_ref, wd_ref) in enumerate(((wgua_ref, wda_ref), (wgub_ref, wdb_ref))):
        gu = jnp.dot(xb, wgu_ref[0], preferred_element_type=F32)
        g = gu[:, :D_EXPERT]
        hid = (g * _sigmoid(g) * gu[:, D_EXPERT:]).astype(BF16)
        y_ref[:, j * D_MODEL:(j + 1) * D_MODEL] = jnp.dot(hid, wd_ref[0], preferred_element_type=F32)


def _moe_ffn(blk_ea, blk_eb, x_sorted, lw):
    nblk = blk_ea.shape[0]
    wgu_spec = lambda sel: pl.BlockSpec((1, D_MODEL, 2 * D_EXPERT), lambda i, ea, eb: (sel(ea, eb)[i], 0, 0))
    wd_spec = lambda sel: pl.BlockSpec((1, D_EXPERT, D_MODEL), lambda i, ea, eb: (sel(ea, eb)[i], 0, 0))
    first, second = (lambda ea, eb: ea), (lambda ea, eb: eb)
    return pl.pallas_call(
        _moe_ffn_kernel,
        out_shape=jax.ShapeDtypeStruct((nblk * MOE_BM, TOP_K * D_MODEL), F32),
        grid_spec=pltpu.PrefetchScalarGridSpec(
            num_scalar_prefetch=2, grid=(nblk,),
            in_specs=[pl.BlockSpec((MOE_BM, D_MODEL), lambda i, ea, eb: (i, 0)),
                      wgu_spec(first), wd_spec(first), wgu_spec(second), wd_spec(second)],
            out_specs=pl.BlockSpec((MOE_BM, TOP_K * D_MODEL), lambda i, ea, eb: (i, 0))),
        compiler_params=_cparams(("arbitrary",)), name="moe_ffn",
    )(blk_ea, blk_eb, x_sorted, lw['w_gu'], lw['w_d'], lw['w_gu'], lw['w_d'])


def _moe_combine_kernel(cur_ref, nxt_ref, y_hbm, x_ref, w_ref, gate_ref, o_ref, buf, sem, *, tc):
    slot = _gather_pipeline(y_hbm, cur_ref, nxt_ref, buf, sem, tc)
    w = w_ref[...]
    y = w[:, 0:1] * buf[slot, :, 0:D_MODEL] + w[:, 1:2] * buf[slot, :, D_MODEL:]
    o_ref[...] = x_ref[...] + gate_ref[0] * y
    _gather_drain(y_hbm, buf, sem, tc)


def _moe_combine(pos_tiles, y_sorted, x, w_cols, gate, seq, tc):
    n = x.shape[0]
    tiles_per_seq = seq // tc
    idx_spec = lambda off: pl.BlockSpec((1, 1, tc), lambda i: (i + off, 0, 0), memory_space=pltpu.SMEM)
    return pl.pallas_call(
        functools.partial(_moe_combine_kernel, tc=tc),
        out_shape=jax.ShapeDtypeStruct((n, D_MODEL), F32),
        grid=(n // tc,),
        in_specs=[idx_spec(0), idx_spec(1), pl.BlockSpec(memory_space=pl.ANY),
                  pl.BlockSpec((tc, D_MODEL), lambda i: (i, 0)),
                  pl.BlockSpec((tc, TOP_K), lambda i: (i, 0)),
                  pl.BlockSpec((1, 1, D_MODEL), lambda i: (i // tiles_per_seq, 0, 0))],
        out_specs=pl.BlockSpec((tc, D_MODEL), lambda i: (i, 0)),
        scratch_shapes=[pltpu.VMEM((2, tc, TOP_K * D_MODEL), F32), pltpu.SemaphoreType.DMA((2,))],
        compiler_params=_cparams(("arbitrary",)), name="moe_combine",
    )(pos_tiles, pos_tiles, y_sorted, x, w_cols, gate)


def _cumsum_rows(onehot):
    n, c = onehot.shape
    blk = math.gcd(n, 256)
    x = onehot.astype(F32).reshape(n // blk, blk, c)
    tri = (jnp.arange(blk)[:, None] >= jnp.arange(blk)[None, :]).astype(F32)
    within = jnp.einsum('ij,bjc->bic', tri, x, precision=HIGHEST)
    nb = n // blk
    before = (jnp.arange(nb)[:, None] > jnp.arange(nb)[None, :]).astype(F32)
    offset = jnp.dot(before, within[:, -1, :], precision=HIGHEST)
    return (within + offset[:, None, :]).reshape(n, c).astype(jnp.int32)


def _pair_tables():
    pairs = [(a, b) for a in range(EXP_PER_GROUP) for b in range(a + 1, EXP_PER_GROUP)]
    ea = [g * EXP_PER_GROUP + a for g in range(N_GROUPS) for a, _ in pairs]
    eb = [g * EXP_PER_GROUP + b for g in range(N_GROUPS) for _, b in pairs]
    return jnp.array(ea, jnp.int32), jnp.array(eb, jnp.int32)


def _moe(h2, idx, wts, x, gate, lw, seq):
    n = x.shape[0]
    tab_a, tab_b = _pair_tables()
    n_cls = tab_a.shape[0]
    nblk = n // MOE_BM + n_cls
    swap = idx[0] > idx[1]
    e_lo, e_hi = jnp.minimum(idx[0], idx[1]), jnp.maximum(idx[0], idx[1])
    w_cols = jnp.stack([jnp.where(swap, wts[1], wts[0]), jnp.where(swap, wts[0], wts[1])], axis=1)
    onehot = ((e_lo[:, None] == tab_a[None, :]) & (e_hi[:, None] == tab_b[None, :])).astype(jnp.int32)
    csum = _cumsum_rows(onehot)
    counts = csum[-1]
    rank = jnp.sum(csum * onehot, axis=1) - 1
    seg_len = (counts + MOE_BM - 1) // MOE_BM * MOE_BM
    seg_end = jnp.cumsum(seg_len)
    pos = jnp.sum(onehot * (seg_end - seg_len)[None, :], axis=1) + rank
    blk_start = jnp.arange(nblk, dtype=jnp.int32) * MOE_BM
    blk_cls = jnp.minimum(jnp.sum((seg_end[None, :] <= blk_start[:, None]).astype(jnp.int32), axis=1), n_cls - 1)
    tc = min(seq, MOE_TC)
    pos_tiles = pos.reshape(n // tc, 1, tc)
    x_sorted = _moe_dispatch(pos_tiles, h2, nblk * MOE_BM, tc)
    y_sorted = _moe_ffn(tab_a[blk_cls], tab_b[blk_cls], x_sorted, lw)
    pos_next = jnp.concatenate([pos_tiles, jnp.zeros((1, 1, tc), jnp.int32)], axis=0)
    return _moe_combine(pos_next, y_sorted, x, w_cols, gate, seq, tc)


def _head_perm():
    rope_idx = list(range(NOPE, QK, 2)) + list(range(NOPE + 1, QK, 2))
    return jnp.array(list(range(NOPE)) + rope_idx, dtype=jnp.int32)


def _block_diag(a, b):
    return jnp.concatenate([jnp.pad(a, ((0, 0), (0, b.shape[1]))), jnp.pad(b, ((0, 0), (a.shape[1], 0)))], axis=0)


def _layer_weights(p, l):
    perm = _head_perm()
    row = lambda v: v.reshape(1, -1)
    w_in = p['w_in'][l]
    kr_perm = jnp.array(list(range(0, ROPE, 2)) + list(range(1, ROPE, 2)), dtype=jnp.int32)
    w_kr = jnp.pad(w_in[:, OFF_KR:OFF_HY][:, kr_perm], ((0, 0), (0, HEAD_PAD - ROPE)))
    w_in2 = jnp.concatenate([w_in[:, OFF_Q:OFF_KR], w_kr, w_in[:, OFF_HY:]], axis=1).astype(BF16)
    w_uq = p['w_uq'][l].reshape(Q_RANK, HEADS, QK)[:, :, perm]
    w_uq = jnp.pad(w_uq, ((0, 0), (0, 0), (0, HEAD_PAD - QK))).reshape(Q_RANK, HEADS * HEAD_PAD).astype(BF16)
    w_ukv = p['w_ukv'][l].reshape(KV_RANK, HEADS, NOPE + VDIM)
    w_k = jnp.pad(w_ukv[:, :, :NOPE], ((0, 0), (0, 0), (0, HEAD_PAD - NOPE)))
    w_k = w_k.reshape(KV_RANK, HEADS * HEAD_PAD).astype(BF16)
    w_vt = w_ukv[:, :, NOPE:].reshape(KV_RANK, HEADS * VDIM).T.astype(BF16)
    pad_gain = lambda g: jnp.pad(g[perm], (0, HEAD_PAD - QK)).reshape(1, HEAD_PAD)
    gm_b = jnp.repeat(p['gm_bs'][l].T, GM_W // GM_GROUPS, axis=1)
    bound = 1.02 * QSCALE * QK * jnp.max(jnp.abs(p['g_qn'][l])) * jnp.max(jnp.abs(p['g_kn'][l]))
    pad_lane = jnp.arange(HEAD_PAD) == QK
    return dict(
        attn_bound=bound,
        q_pad=jnp.where(pad_lane, -bound, 0.0).reshape(1, HEAD_PAD).astype(F32),
        k_pad=jnp.where(pad_lane, 1.0, 0.0).reshape(1, HEAD_PAD).astype(F32),
        g1=row(p['norm1_g'][l]), g2=row(p['norm2_g'][l]), w_in=w_in2,
        g_qa=row(p['g_qa'][l]), w_uq=w_uq, g_qn=pad_gain(p['g_qn'][l]),
        g_kva=row(p['g_kva'][l]), w_k=w_k, w_vt=w_vt, g_kn=pad_gain(p['g_kn'][l]),
        gm_g=row(p['gm_norm_g'][l]), gm_w=p['gm_ws'][l].astype(BF16), gm_b=gm_b,
        hy_conv_w=p['hy_conv_w'][l], hy_conv_b=row(p['hy_conv_b'][l]),
        hy_w1p=_block_diag(p['hy_w1'][l], p['hy_w1'][l]), hy_b1p=row(jnp.tile(p['hy_b1'][l], 2)),
        hy_w2p=_block_diag(p['hy_w2'][l], p['hy_w2'][l]), hy_b2p=row(jnp.tile(p['hy_b2'][l], 2)),
        hy_w3p=_block_diag(p['hy_w3'][l][:, :HY_W], p['hy_w3'][l][:, HY_W:]), hy_b3=row(p['hy_b3'][l]),
        hy_freqp=row(jnp.tile(p['hy_freq'][l], 2)),
        hy_decay=p['hy_decay'][l].reshape(1, 2 * HY_W), hy_bias=row(p['hy_bias'][l]),
        w_pa=p['w_pa'][l].astype(BF16), w_pb=p['w_pb'][l].astype(BF16), w_pc=p['w_pc'][l].astype(BF16),
        w_out=p['w_out'][l].astype(BF16),
        w_gu=jnp.concatenate([p['moe_w_gate'][l], p['moe_w_up'][l]], axis=-1).astype(BF16),
        w_d=p['moe_w_down'][l].astype(BF16))


def _rope_tables(seq):
    rows = seq // GRID_W
    row = jnp.repeat(jnp.arange(rows, dtype=F32), GRID_W)
    col = jnp.tile(jnp.arange(GRID_W, dtype=F32), rows)
    n_freq = ROPE // 4
    inv = ROPE_THETA ** (-jnp.arange(n_freq, dtype=F32) / n_freq)
    ang = jnp.concatenate([row[:, None] * inv, col[:, None] * inv], axis=-1)
    c, s = jnp.cos(ang), jnp.sin(ang)
    z = lambda w: jnp.zeros((seq, w), F32)
    rc = jnp.concatenate([jnp.ones((seq, NOPE), F32), c, c, z(HEAD_PAD - QK)], axis=1)
    rs1 = jnp.concatenate([z(NOPE), z(ROPE // 2), s, z(HEAD_PAD - QK)], axis=1)
    rs2 = jnp.concatenate([z(NOPE), -s, z(ROPE // 2), z(HEAD_PAD - QK)], axis=1)
    return rc, rs1, rs2


def _mixer_and_ffn(x, mods, lw, rw_t, rb, seq, tm, rope_tabs, ctx_kv, filt, dft_tabs):
    shift1, scale1, gate1, shift2, scale2, gate2 = mods
    q, k, vt, m, phy, gt = _premix(x, shift1, scale1, lw, rope_tabs, seq, tm)
    a = _attention(q, k, vt, ctx_kv, seq, min(seq, ATTN_TQ), lw['attn_bound'])
    if dft_tabs is not None:
        b = _hyena_long(phy, lw, filt, seq, dft_tabs)
    else:
        b = _hyena_short(phy, lw, filt, seq)
    xn, h2, idx, wts = _merge(a, b, m, gt, x, gate1, shift2, scale2, lw, rw_t, rb, seq, tm)
    return _moe(h2, idx, wts, xn, gate2, lw, seq), k, vt


def _forward(p):
    x, ctx = p['x'], p['ctx']
    batch, seq, _ = x.shape
    lc = ctx.shape[1]
    depth = p['w_mod'].shape[0]

    cvecs = jnp.concatenate([p['c'], p['c_ctx'][None], jnp.zeros((8 - batch - 1, D_MODEL), F32)], axis=0)
    mod_all = _modvec(cvecs, p['w_mod'], p['b_mod'])
    rw_t = p['router_w'].T
    rb = p['router_b'].reshape(N_EXPERTS, 1)
    rope_tabs = _rope_tables(seq)
    dft_tabs = _dft_tables(seq)
    z_lat, z_ctx = _hy_embedding(seq), _hy_embedding(lc)

    xl = x.reshape(batch * seq, D_MODEL)
    xc = ctx.reshape(batch * lc, D_MODEL)
    tm_lat = min(seq, TOKEN_TILE)
    for l in range(depth):
        lw = _layer_weights(p, l)
        mod = mod_all[l].reshape(8, N_MOD, D_MODEL)
        mods_lat = [mod[:batch, j].reshape(batch, 1, D_MODEL) for j in range(N_MOD)]
        mods_ctx = [jnp.broadcast_to(mod[batch, j].reshape(1, 1, D_MODEL), (batch, 1, D_MODEL))
                    for j in range(N_MOD)]
        if l == depth - 1:
            _, k_c, vt_c, _, _, _ = _premix(xc, mods_ctx[0], mods_ctx[1], lw, None, lc, lc)
        else:
            xc, k_c, vt_c = _mixer_and_ffn(xc, mods_ctx, lw, rw_t, rb, lc, lc, None, None,
                                           _hy_filters(z_ctx, lw, lc), None)
        filt = _hy_filters(z_lat, lw, min(seq, HY_FILTER_TILE))
        xl, _, _ = _mixer_and_ffn(xl, mods_lat, lw, rw_t, rb, seq, tm_lat, rope_tabs, (k_c, vt_c), filt, dft_tabs)
    return xl.reshape(batch, seq, D_MODEL)


def kernel(x, c, ctx, c_ctx, w_mod, b_mod, norm1_g, norm2_g, w_in, g_qa, w_uq, g_kva, w_ukv, g_qn, g_kn,
           hy_conv_w, hy_conv_b, hy_w1, hy_b1, hy_w2, hy_b2, hy_w3, hy_b3, hy_freq, hy_decay, hy_bias,
           gm_norm_g, gm_ws, gm_bs, w_pa, w_pb, w_pc, w_out, router_w, router_b,
           moe_w_gate, moe_w_up, moe_w_down):
    return _forward(dict(
        x=x, c=c, ctx=ctx, c_ctx=c_ctx, w_mod=w_mod, b_mod=b_mod, norm1_g=norm1_g, norm2_g=norm2_g, w_in=w_in,
        g_qa=g_qa, w_uq=w_uq, g_kva=g_kva, w_ukv=w_ukv, g_qn=g_qn, g_kn=g_kn, hy_conv_w=hy_conv_w,
        hy_conv_b=hy_conv_b, hy_w1=hy_w1, hy_b1=hy_b1, hy_w2=hy_w2, hy_b2=hy_b2, hy_w3=hy_w3, hy_b3=hy_b3,
        hy_freq=hy_freq, hy_decay=hy_decay, hy_bias=hy_bias, gm_norm_g=gm_norm_g, gm_ws=gm_ws, gm_bs=gm_bs,
        w_pa=w_pa, w_pb=w_pb, w_pc=w_pc, w_out=w_out, router_w=router_w, router_b=router_b,
        moe_w_gate=moe_w_gate, moe_w_up=moe_w_up, moe_w_down=moe_w_down))
```

```python
import functools
import math

import jax
import jax.numpy as jnp
from jax import lax
from jax.experimental import pallas as pl
from jax.experimental.pallas import tpu as pltpu

F32 = jnp.float32
BF16 = jnp.bfloat16
HIGHEST = lax.Precision.HIGHEST

D_MODEL = 1024
GRID_W = 64
EPS = 1e-6
N_MOD = 6

HEADS = 8
Q_RANK = 384
KV_RANK = 256
NOPE = 64
ROPE = 32
QK = NOPE + ROPE
VDIM = 64
HEAD_PAD = 128
ROPE_THETA = 10000.0
V_ROWS = 80

HY_W = 256
HY_BANDS = 16
GM_W = 256
GM_CHUNK = 128
GM_GROUPS = 4

OFF_Q = 0
OFF_KV = OFF_Q + Q_RANK
OFF_KR = OFF_KV + KV_RANK
OFF_HY = OFF_KR + ROPE
OFF_GM = OFF_HY + 3 * HY_W
OFF_GT = OFF_GM + 2 * GM_W

P_Q = 0
P_KV = P_Q + Q_RANK
P_KR = P_KV + KV_RANK
P_HY = P_KR + HEAD_PAD
P_GM = P_HY + 3 * HY_W
P_GT = P_GM + 2 * GM_W
P_W = P_GT + 3 * D_MODEL

N_EXPERTS = 16
N_GROUPS = 4
EXP_PER_GROUP = 4
TOP_K = 2
D_EXPERT = 512
MOE_BM = 256
MOE_DISPATCH_TILE = 1024
MOE_COMBINE_TILE = 512

DFT_N2 = 256
DFT_K1_STEP = 4
DFT_COL_TILE = 8192

TOKEN_TILE = 512
HY_CONV_TILE = 1024
HY_FILTER_TILE = 2048

VMEM_LIMIT = 56 * 1024 * 1024
ATTN_TQ = 512
ATTN_TK = 2048
ATTN_BOUND_MAX = 50.0
NEG_BIG = -1e30
LOG2E = 1.4426950408889634
QSCALE = QK ** -0.5 * LOG2E


def _cparams(sem):
    return pltpu.CompilerParams(dimension_semantics=sem, vmem_limit_bytes=VMEM_LIMIT)


def _rms(x):
    return x * lax.rsqrt(jnp.mean(x * x, axis=-1, keepdims=True) + EPS)


def _sigmoid(x):
    return 0.5 * jnp.tanh(0.5 * x) + 0.5


def _nt_dot(a, b):
    return lax.dot_general(a, b, (((1,), (1,)), ((), ())), preferred_element_type=F32)


def _modvec_kernel(c_ref, w_ref, b_ref, o_ref):
    cv = c_ref[...]
    s = cv * jax.nn.sigmoid(cv)
    o_ref[0] = jnp.dot(s, w_ref[0], preferred_element_type=F32, precision=HIGHEST) + b_ref[0]


def _modvec(cvecs, w_mod, b_mod):
    depth = w_mod.shape[0]
    tn = 1536
    return pl.pallas_call(
        _modvec_kernel,
        out_shape=jax.ShapeDtypeStruct((depth, 8, N_MOD * D_MODEL), F32),
        grid=(depth, N_MOD * D_MODEL // tn),
        in_specs=[pl.BlockSpec((8, D_MODEL), lambda l, j: (0, 0)),
                  pl.BlockSpec((1, D_MODEL, tn), lambda l, j: (l, 0, j)),
                  pl.BlockSpec((1, 1, tn), lambda l, j: (l, 0, j))],
        out_specs=pl.BlockSpec((1, 8, tn), lambda l, j: (l, 0, j)),
        compiler_params=_cparams(("arbitrary", "arbitrary")),
        name="modvec",
    )(cvecs, w_mod, b_mod.reshape(depth, 1, N_MOD * D_MODEL))


def _head_norm_rope(xh, gain, rope):
    ms = jnp.sum(xh * xh, axis=-1, keepdims=True) * (1.0 / QK)
    xh = xh * lax.rsqrt(ms + EPS) * gain
    if rope is not None:
        rc, rs1, rs2 = rope
        xh = xh * rc + pltpu.roll(xh, ROPE // 2, 1) * rs1 + pltpu.roll(xh, HEAD_PAD - ROPE // 2, 1) * rs2
    return xh


def _premix_kernel(*refs, use_rope, tm):
    if use_rope:
        (x_ref, shift_ref, scale_ref, g1_ref, win_ref, gqa_ref, wuq_ref, gqn_ref, gkva_ref, wk_ref, wvt_ref,
         gkn_ref, gmg_ref, gmw_ref, gmb_ref, qpad_ref, kpad_ref, rc_ref, rs1_ref, rs2_ref,
         q_ref, k_ref, vt_ref, m_ref, hy_ref, gt_ref) = refs
        rope = (rc_ref[...], rs1_ref[...], rs2_ref[...])
    else:
        (x_ref, shift_ref, scale_ref, g1_ref, win_ref, gqa_ref, wuq_ref, gqn_ref, gkva_ref, wk_ref, wvt_ref,
         gkn_ref, gmg_ref, gmw_ref, gmb_ref, qpad_ref, kpad_ref,
         q_ref, k_ref, vt_ref, m_ref, hy_ref, gt_ref) = refs
        rope = None

    x = x_ref[...]
    h = _rms(x) * g1_ref[...]
    h = h * (1.0 + scale_ref[0]) + shift_ref[0]
    hb = h.astype(BF16)

    def proj(lo, width):
        return jnp.dot(hb, win_ref[:, lo:lo + width], preferred_element_type=F32)

    qa = (_rms(proj(P_Q, Q_RANK)) * gqa_ref[...]).astype(BF16)
    q = jnp.dot(qa, wuq_ref[...], preferred_element_type=F32)
    for hh in range(HEADS):
        qh = _head_norm_rope(q[:, hh * HEAD_PAD:(hh + 1) * HEAD_PAD], gqn_ref[...], rope)
        q_ref[:, hh * HEAD_PAD:(hh + 1) * HEAD_PAD] = (qh * QSCALE + qpad_ref[...]).astype(BF16)

    kva = (_rms(proj(P_KV, KV_RANK)) * gkva_ref[...]).astype(BF16)
    kr = pltpu.roll(proj(P_KR, HEAD_PAD), NOPE, 1)
    kn = jnp.dot(kva, wk_ref[...], preferred_element_type=F32)
    for hh in range(HEADS):
        kh = _head_norm_rope(kn[:, hh * HEAD_PAD:(hh + 1) * HEAD_PAD] + kr, gkn_ref[...], rope)
        k_ref[:, hh * HEAD_PAD:(hh + 1) * HEAD_PAD] = (kh + kpad_ref[...]).astype(BF16)
    vt = _nt_dot(wvt_ref[...], kva)
    row = lax.broadcasted_iota(jnp.int32, (V_ROWS - VDIM, tm), 0)
    ones_rows = jnp.where(row == 0, 1.0, 0.0).astype(BF16)
    for hh in range(HEADS):
        vt_ref[0, 0, hh * V_ROWS:hh * V_ROWS + VDIM, :] = vt[hh * VDIM:(hh + 1) * VDIM].astype(BF16)
        vt_ref[0, 0, hh * V_ROWS + VDIM:(hh + 1) * V_ROWS, :] = ones_rows

    gg = jax.nn.gelu(proj(P_GM, 2 * GM_W), approximate=True)
    gu = gg[:, :GM_W]
    gv = (_rms(gg[:, GM_W:]) * gmg_ref[...]).astype(BF16)
    grp = lax.broadcasted_iota(jnp.int32, (GM_CHUNK, GM_W), 1) // (GM_W // GM_GROUPS)
    for ci in range(tm // GM_CHUNK):
        vc = gv[ci * GM_CHUNK:(ci + 1) * GM_CHUNK]
        s = jnp.zeros((GM_CHUNK, GM_W), F32)
        for g in range(GM_GROUPS):
            sg = jnp.dot(gmw_ref[g], vc, preferred_element_type=F32)
            s = jnp.where(grp == g, sg, s)
        m_ref[ci * GM_CHUNK:(ci + 1) * GM_CHUNK, :] = (
            gu[ci * GM_CHUNK:(ci + 1) * GM_CHUNK] * (s + gmb_ref[...])).astype(BF16)

    hy_ref[...] = proj(P_HY, 3 * HY_W).astype(BF16)
    for j in range(3):
        gt_ref[:, j * D_MODEL:(j + 1) * D_MODEL] = _sigmoid(proj(P_GT + j * D_MODEL, D_MODEL)).astype(BF16)


def _premix(x, shift, scale, lw, rope_tabs, seq, tm):
    n = x.shape[0]
    tiles_per_seq = seq // tm
    batch = n // seq
    use_rope = rope_tabs is not None
    const = lambda i: (0, 0)
    in_specs = [
        pl.BlockSpec((tm, D_MODEL), lambda i: (i, 0)),
        pl.BlockSpec((1, 1, D_MODEL), lambda i: (i // tiles_per_seq, 0, 0)),
        pl.BlockSpec((1, 1, D_MODEL), lambda i: (i // tiles_per_seq, 0, 0)),
        pl.BlockSpec((1, D_MODEL), const),
        pl.BlockSpec((D_MODEL, P_W), const, pipeline_mode=pl.Buffered(1)),
        pl.BlockSpec((1, Q_RANK), const),
        pl.BlockSpec((Q_RANK, HEADS * HEAD_PAD), const),
        pl.BlockSpec((1, HEAD_PAD), const),
        pl.BlockSpec((1, KV_RANK), const),
        pl.BlockSpec((KV_RANK, HEADS * HEAD_PAD), const),
        pl.BlockSpec((HEADS * VDIM, KV_RANK), const),
        pl.BlockSpec((1, HEAD_PAD), const),
        pl.BlockSpec((1, GM_W), const),
        pl.BlockSpec((GM_GROUPS, GM_CHUNK, GM_CHUNK), lambda i: (0, 0, 0)),
        pl.BlockSpec((GM_CHUNK, GM_W), const),
        pl.BlockSpec((1, HEAD_PAD), const),
        pl.BlockSpec((1, HEAD_PAD), const),
    ]
    args = [x, shift, scale, lw['g1'], lw['w_in'], lw['g_qa'], lw['w_uq'], lw['g_qn'], lw['g_kva'], lw['w_k'],
            lw['w_vt'], lw['g_kn'], lw['gm_g'], lw['gm_w'], lw['gm_b'], lw['q_pad'], lw['k_pad']]
    tkv = min(seq, ATTN_TK)
    sub = tkv // tm
    if use_rope:
        in_specs += [pl.BlockSpec((tm, HEAD_PAD), lambda i: (i % tiles_per_seq, 0))] * 3
        args += list(rope_tabs)
    out_shape = (
        jax.ShapeDtypeStruct((n, HEADS * HEAD_PAD), BF16),
        jax.ShapeDtypeStruct((n, HEADS * HEAD_PAD), BF16),
        jax.ShapeDtypeStruct((batch, seq // tkv, HEADS * V_ROWS, tkv), BF16),
        jax.ShapeDtypeStruct((n, GM_W), BF16),
        jax.ShapeDtypeStruct((n, 3 * HY_W), BF16),
        jax.ShapeDtypeStruct((n, 3 * D_MODEL), BF16),
    )
    out_specs = (
        pl.BlockSpec((tm, HEADS * HEAD_PAD), lambda i: (i, 0)),
        pl.BlockSpec((tm, HEADS * HEAD_PAD), lambda i: (i, 0)),
        pl.BlockSpec((1, 1, HEADS * V_ROWS, tm),
                     lambda i: (i // tiles_per_seq, (i % tiles_per_seq) // sub, 0, (i % tiles_per_seq) % sub)),
        pl.BlockSpec((tm, GM_W), lambda i: (i, 0)),
        pl.BlockSpec((tm, 3 * HY_W), lambda i: (i, 0)),
        pl.BlockSpec((tm, 3 * D_MODEL), lambda i: (i, 0)),
    )
    return pl.pallas_call(
        functools.partial(_premix_kernel, use_rope=use_rope, tm=tm),
        out_shape=out_shape, grid=(n // tm,), in_specs=in_specs, out_specs=out_specs,
        compiler_params=_cparams(("arbitrary",)), name="premix",
    )(*args)


def _attn_kernel(*refs, n_chunks, has_ctx, tq):
    if has_ctx:
        q_ref, k_ref, vt_ref, kc_ref, vtc_ref, o_ref = refs
    else:
        q_ref, k_ref, vt_ref, o_ref = refs
    tk = vt_ref.shape[-1]

    def step(hh, kc, vtc, m, acc):
        s = _nt_dot(kc, q_ref[:, hh * HEAD_PAD:(hh + 1) * HEAD_PAD])
        m_new = jnp.maximum(m, jnp.max(s, axis=0, keepdims=True))
        p = jnp.exp2(s - m_new).astype(BF16)
        alpha = jnp.exp2(m - m_new)
        return m_new, acc * alpha + jnp.dot(vtc, p, preferred_element_type=F32)

    def body(i, carry):
        out = []
        for hh in range(2):
            m, acc = carry[hh]
            start = pl.multiple_of(i * tk, tk)
            kc = k_ref[pl.ds(start, tk), hh * HEAD_PAD:(hh + 1) * HEAD_PAD]
            vtc = vt_ref[0, i, hh * V_ROWS:(hh + 1) * V_ROWS, :]
            out.append(step(hh, kc, vtc, m, acc))
        return tuple(out)

    init = tuple((jnp.full((1, tq), NEG_BIG, F32), jnp.zeros((V_ROWS, tq), F32)) for _ in range(2))
    carry = lax.fori_loop(0, n_chunks, body, init)
    outs = []
    for hh in range(2):
        m, acc = carry[hh]
        if has_ctx:
            m, acc = step(hh, kc_ref[:, hh * HEAD_PAD:(hh + 1) * HEAD_PAD],
                          vtc_ref[0, 0, hh * V_ROWS:(hh + 1) * V_ROWS, :], m, acc)
        outs.append(acc[:VDIM] / acc[VDIM:VDIM + 1])
    o_ref[...] = jnp.concatenate(outs, axis=0).T.astype(BF16)


def _attn_bounded_kernel(*refs, n_chunks, has_ctx, tq):
    if has_ctx:
        q_ref, k_ref, vt_ref, kc_ref, vtc_ref, o_ref = refs
    else:
        q_ref, k_ref, vt_ref, o_ref = refs
    tk = vt_ref.shape[-1]

    def step(hh, kc, vtc, acc):
        s = _nt_dot(kc, q_ref[:, hh * HEAD_PAD:(hh + 1) * HEAD_PAD])
        return acc + jnp.dot(vtc, jnp.exp2(s).astype(BF16), preferred_element_type=F32)

    def body(i, carry):
        start = pl.multiple_of(i * tk, tk)
        return tuple(step(hh, k_ref[pl.ds(start, tk), hh * HEAD_PAD:(hh + 1) * HEAD_PAD],
                          vt_ref[0, i, hh * V_ROWS:(hh + 1) * V_ROWS, :], carry[hh]) for hh in range(2))

    carry = lax.fori_loop(0, n_chunks, body, tuple(jnp.zeros((V_ROWS, tq), F32) for _ in range(2)), unroll=True)
    outs = []
    for hh in range(2):
        acc = carry[hh]
        if has_ctx:
            acc = step(hh, kc_ref[:, hh * HEAD_PAD:(hh + 1) * HEAD_PAD],
                       vtc_ref[0, 0, hh * V_ROWS:(hh + 1) * V_ROWS, :], acc)
        outs.append(acc[:VDIM] / acc[VDIM:VDIM + 1])
    o_ref[...] = jnp.concatenate(outs, axis=0).T.astype(BF16)


def _attention(q, k, vt, ctx_kv, seq, tq, bound):
    n = q.shape[0]
    batch = n // seq
    n_chunks, tk = vt.shape[1], vt.shape[3]
    q_tiles = seq // tq
    has_ctx = ctx_kv is not None
    in_specs = [
        pl.BlockSpec((tq, 2 * HEAD_PAD), lambda b, j, i: (b * q_tiles + i, j)),
        pl.BlockSpec((seq, 2 * HEAD_PAD), lambda b, j, i: (b, j)),
        pl.BlockSpec((1, n_chunks, 2 * V_ROWS, tk), lambda b, j, i: (b, 0, j, 0)),
    ]
    args = [q, k, vt]
    if has_ctx:
        kc, vtc = ctx_kv
        lc = vtc.shape[3]
        in_specs += [pl.BlockSpec((lc, 2 * HEAD_PAD), lambda b, j, i: (b, j)),
                     pl.BlockSpec((1, 1, 2 * V_ROWS, lc), lambda b, j, i: (b, 0, j, 0))]
        args += [kc, vtc]
    def call(body, name):
        return pl.pallas_call(
            functools.partial(body, n_chunks=n_chunks, has_ctx=has_ctx, tq=tq),
            out_shape=jax.ShapeDtypeStruct((n, HEADS * VDIM), BF16),
            grid=(batch, HEADS // 2, q_tiles),
            in_specs=in_specs,
            out_specs=pl.BlockSpec((tq, 2 * VDIM), lambda b, j, i: (b * q_tiles + i, j)),
            compiler_params=_cparams(("arbitrary", "arbitrary", "arbitrary")),
            name=name,
        )(*args)

    return lax.cond(bound < ATTN_BOUND_MAX,
                    lambda: call(_attn_bounded_kernel, "attn_bounded"),
                    lambda: call(_attn_kernel, "attn_online"))


def _hy_filter_kernel(zz_ref, w1_ref, b1_ref, w2_ref, b2_ref, w3_ref, b3_ref, fr_ref, dec_ref, f_ref, ss_ref, *, emb):
    fr = fr_ref[...]
    dec = jnp.abs(dec_ref[...])
    zz = zz_ref[...]
    hdn = jnp.sin(fr * (jnp.dot(zz, w1_ref[...], preferred_element_type=F32, precision=HIGHEST) + b1_ref[...]))
    hdn = jnp.sin(fr * (jnp.dot(hdn, w2_ref[...], preferred_element_type=F32, precision=HIGHEST) + b2_ref[...]))
    k = jnp.dot(hdn, w3_ref[...], preferred_element_type=F32, precision=HIGHEST) + b3_ref[...]
    col = lax.broadcasted_iota(jnp.int32, k.shape, 1)
    k = k * jnp.exp(-jnp.where(col < HY_W, zz[:, 0:1], zz[:, emb:emb + 1]) * dec)
    first_block = pl.program_id(0) == 0
    row = lax.broadcasted_iota(jnp.int32, k.shape, 0)
    k = jnp.where(first_block & (row == 0) & (col >= HY_W), 0.0, k)
    f_ref[0] = k[:, :HY_W].astype(BF16)
    f_ref[1] = k[:, HY_W:].astype(BF16)
    hid = hdn.shape[1] // 2
    k_0 = jnp.dot(pltpu.roll(hdn[0:8], hid, 1), w3_ref[...], preferred_element_type=F32,
                  precision=HIGHEST) + b3_ref[...]
    k_0 = k_0[0:1] * jnp.exp(-zz[0:1, 0:1] * dec)
    extra = jnp.where(first_block & (col[0:1] >= HY_W), k_0 * k_0, 0.0)
    ss_ref[0] = jnp.sum(k * k, axis=0, keepdims=True) + extra


def _hy_filters(z, lw, tr):
    seq, emb = z.shape
    nblk = seq // tr
    hid2 = lw['hy_w2p'].shape[0]
    const = lambda i: (0, 0)
    zz = jnp.concatenate([z, jnp.concatenate([z[0:1], z[:0:-1]], axis=0)], axis=1)
    return pl.pallas_call(
        functools.partial(_hy_filter_kernel, emb=emb),
        out_shape=(jax.ShapeDtypeStruct((2, seq, HY_W), BF16), jax.ShapeDtypeStruct((nblk, 1, 2 * HY_W), F32)),
        grid=(nblk,),
        in_specs=[pl.BlockSpec((tr, 2 * emb), lambda i: (i, 0)),
                  pl.BlockSpec((2 * emb, hid2), const), pl.BlockSpec((1, hid2), const),
                  pl.BlockSpec((hid2, hid2), const), pl.BlockSpec((1, hid2), const),
                  pl.BlockSpec((hid2, 2 * HY_W), const), pl.BlockSpec((1, 2 * HY_W), const),
                  pl.BlockSpec((1, hid2), const), pl.BlockSpec((1, 2 * HY_W), const)],
        out_specs=(pl.BlockSpec((2, tr, HY_W), lambda i: (0, i, 0)),
                   pl.BlockSpec((1, 1, 2 * HY_W), lambda i: (i, 0, 0))),
        compiler_params=_cparams(("arbitrary",)), name="hy_filter",
    )(zz, lw['hy_w1p'], lw['hy_b1p'], lw['hy_w2p'], lw['hy_b2p'], lw['hy_w3p'], lw['hy_b3'], lw['hy_freqp'],
      lw['hy_decay'])


def _hy_conv3_kernel(p_ref, prev_ref, next_ref, w_ref, b_ref, x0_ref, u_ref, *, tiles_per_seq, tr):
    i = pl.program_id(0)
    p = p_ref[...].astype(F32)
    first = (i % tiles_per_seq) == 0
    last = (i % tiles_per_seq) == tiles_per_seq - 1
    prev_row = jnp.where(first, 0.0, prev_ref[...].astype(F32)[15:16, :])
    next_row = jnp.where(last, 0.0, next_ref[...].astype(F32)[0:1, :])
    row = lax.broadcasted_iota(jnp.int32, p.shape, 0)
    p_prev = jnp.where(row == 0, prev_row, pltpu.roll(p, 1, 0))
    p_next = jnp.where(row == tr - 1, next_row, pltpu.roll(p, tr - 1, 0))
    z = b_ref[...] + p_prev * w_ref[0:1, :] + p * w_ref[1:2, :] + p_next * w_ref[2:3, :]
    x0_ref[...] = z[:, :HY_W].astype(BF16)
    u_ref[...] = (z[:, 2 * HY_W:] * z[:, HY_W:2 * HY_W]).astype(BF16)


def _hy_conv3(phy, conv_w, conv_b, seq, tr):
    n = phy.shape[0]
    tiles_per_seq = seq // tr
    hb = tr // 16
    nhb = n // 16
    return pl.pallas_call(
        functools.partial(_hy_conv3_kernel, tiles_per_seq=tiles_per_seq, tr=tr),
        out_shape=(jax.ShapeDtypeStruct((n, HY_W), BF16), jax.ShapeDtypeStruct((n, HY_W), BF16)),
        grid=(n // tr,),
        in_specs=[pl.BlockSpec((tr, 3 * HY_W), lambda i: (i, 0)),
                  pl.BlockSpec((16, 3 * HY_W), lambda i: (jnp.maximum(i * hb - 1, 0), 0)),
                  pl.BlockSpec((16, 3 * HY_W), lambda i: (jnp.minimum((i + 1) * hb, nhb - 1), 0)),
                  pl.BlockSpec((3, 3 * HY_W), lambda i: (0, 0)),
                  pl.BlockSpec((1, 3 * HY_W), lambda i: (0, 0))],
        out_specs=(pl.BlockSpec((tr, HY_W), lambda i: (i, 0)), pl.BlockSpec((tr, HY_W), lambda i: (i, 0))),
        compiler_params=_cparams(("arbitrary",)), name="hy_conv3",
    )(phy, phy, phy, conv_w, conv_b)


def _dft_outer_kernel(wr_ref, wi_ref, u_ref, ar_ref, ai_ref):
    u = u_ref[0]
    ar_ref[0] = jnp.dot(wr_ref[...], u, preferred_element_type=F32).astype(BF16)
    ai_ref[0] = jnp.dot(wi_ref[...], u, preferred_element_type=F32).astype(BF16)


def _dft_outer(wr, wi, u, tc):
    nb, kk, cols = u.shape
    n1 = wr.shape[0]
    return pl.pallas_call(
        _dft_outer_kernel,
        out_shape=(jax.ShapeDtypeStruct((nb, n1, cols), BF16),) * 2,
        grid=(nb, cols // tc),
        in_specs=[pl.BlockSpec((n1, kk), lambda b, j: (0, 0)), pl.BlockSpec((n1, kk), lambda b, j: (0, 0)),
                  pl.BlockSpec((1, kk, tc), lambda b, j: (b, 0, j))],
        out_specs=(pl.BlockSpec((1, n1, tc), lambda b, j: (b, 0, j)),) * 2,
        compiler_params=_cparams(("arbitrary", "arbitrary")), name="dft_outer",
    )(wr, wi, u)


def _cdot(mr, mi, xr, xi):
    rr = jnp.dot(mr, xr, preferred_element_type=F32) - jnp.dot(mi, xi, preferred_element_type=F32)
    ri = jnp.dot(mr, xi, preferred_element_type=F32) + jnp.dot(mi, xr, preferred_element_type=F32)
    return rr, ri


def _dft_inner_fwd_kernel(mr_ref, mi_ref, ar_ref, ai_ref, fr_ref, fi_ref):
    for j in range(DFT_K1_STEP):
        fr, fi = _cdot(mr_ref[j], mi_ref[j], ar_ref[0, j], ai_ref[0, j])
        fr_ref[j] = fr.astype(BF16)
        fi_ref[j] = fi.astype(BF16)


def _dft_inner_fwd(mr, mi, ar, ai):
    n1, n2, _ = mr.shape
    ch = ar.shape[-1]
    mspec = pl.BlockSpec((DFT_K1_STEP, n2, n2), lambda k: (k, 0, 0))
    aspec = pl.BlockSpec((1, DFT_K1_STEP, n2, ch), lambda k: (0, k, 0, 0))
    ospec = pl.BlockSpec((DFT_K1_STEP, n2, ch), lambda k: (k, 0, 0))
    return pl.pallas_call(
        _dft_inner_fwd_kernel,
        out_shape=(jax.ShapeDtypeStruct((n1, n2, ch), BF16),) * 2,
        grid=(n1 // DFT_K1_STEP,), in_specs=[mspec, mspec, aspec, aspec], out_specs=(ospec, ospec),
        compiler_params=_cparams(("arbitrary",)), name="dft_inner_fwd",
    )(mr, mi, ar, ai)


def _dft_inner_conv_kernel(mr_ref, mi_ref, tr_ref, ti_ref, ar_ref, ai_ref, fr_ref, fi_ref, br_ref, bi_ref, *, nb):
    for j in range(DFT_K1_STEP):
        fr = fr_ref[j].astype(F32)
        fi = fi_ref[j].astype(F32)
        for b in range(nb):
            ur, ui = _cdot(mr_ref[j], mi_ref[j], ar_ref[b, j], ai_ref[b, j])
            yr = (ur * fr - ui * fi).astype(BF16)
            yi = (ur * fi + ui * fr).astype(BF16)
            br, bi = _cdot(tr_ref[j], ti_ref[j], yr, yi)
            br_ref[b, j] = br.astype(BF16)
            bi_ref[b, j] = bi.astype(BF16)


def _dft_inner_conv(mr, mi, tr, ti, ar, ai, fr, fi):
    n1, n2, _ = mr.shape
    nb, _, _, ch = ar.shape
    mspec = pl.BlockSpec((DFT_K1_STEP, n2, n2), lambda k: (k, 0, 0))
    aspec = pl.BlockSpec((nb, DFT_K1_STEP, n2, ch), lambda k: (0, k, 0, 0))
    fspec = pl.BlockSpec((DFT_K1_STEP, n2, ch), lambda k: (k, 0, 0))
    return pl.pallas_call(
        functools.partial(_dft_inner_conv_kernel, nb=nb),
        out_shape=(jax.ShapeDtypeStruct((nb, n1, n2, ch), BF16),) * 2,
        grid=(n1 // DFT_K1_STEP,), in_specs=[mspec, mspec, mspec, mspec, aspec, aspec, fspec, fspec],
        out_specs=(aspec, aspec),
        compiler_params=_cparams(("arbitrary",)), name="dft_inner_conv",
    )(mr, mi, tr, ti, ar, ai, fr, fi)


def _filter_scale(ss_ref):
    ss = jnp.sum(ss_ref[...], axis=0)
    return lax.rsqrt(ss[:, :HY_W] + ss[:, HY_W:] + EPS)


def _hy_final_kernel(cr_ref, ci_ref, br_ref, bi_ref, x0_ref, u_ref, ss_ref, bias_ref, o_ref, *, reps):
    y = (jnp.dot(cr_ref[...], br_ref[0], preferred_element_type=F32)
         + jnp.dot(ci_ref[...], bi_ref[0], preferred_element_type=F32))
    scale = jnp.tile(_filter_scale(ss_ref), (1, reps))
    bias = jnp.tile(bias_ref[...], (1, reps))
    u = u_ref[0].astype(F32)
    o_ref[0] = (x0_ref[0].astype(F32) * (y * scale + u * bias)).astype(BF16)


def _hy_final(cr, ci, br, bi, x0, u, ss, bias, tc):
    nb, n1, cols = br.shape
    n1h = cr.shape[0]
    nblk = ss.shape[0]
    cspec = pl.BlockSpec((n1h, n1), lambda b, j: (0, 0))
    bspec = pl.BlockSpec((1, n1, tc), lambda b, j: (b, 0, j))
    xspec = pl.BlockSpec((1, n1h, tc), lambda b, j: (b, 0, j))
    return pl.pallas_call(
        functools.partial(_hy_final_kernel, reps=tc // HY_W),
        out_shape=jax.ShapeDtypeStruct((nb, n1h, cols), BF16),
        grid=(nb, cols // tc),
        in_specs=[cspec, cspec, bspec, bspec, xspec, xspec,
                  pl.BlockSpec((nblk, 1, 2 * HY_W), lambda b, j: (0, 0, 0)),
                  pl.BlockSpec((1, HY_W), lambda b, j: (0, 0))],
        out_specs=xspec,
        compiler_params=_cparams(("arbitrary", "arbitrary")), name="hy_final",
    )(cr, ci, br, bi, x0, u, ss, bias)


def _hy_direct_kernel(f_ref, x0_ref, u_ref, ss_ref, bias_ref, o_ref, kk_ref, *, seq):
    kk_ref[0:seq, :] = f_ref[1].astype(F32)
    kk_ref[seq:2 * seq, :] = f_ref[0].astype(F32)
    u = u_ref[0].astype(F32)
    y = jnp.zeros((seq, HY_W), F32)
    for j in range(seq):
        y = y + kk_ref[seq - j:2 * seq - j, :] * u[j:j + 1, :]
    o_ref[0] = (x0_ref[0].astype(F32) * (y * _filter_scale(ss_ref) + u * bias_ref[...])).astype(BF16)


def _hy_direct(f, x0, u, ss, bias):
    nb, seq, _ = u.shape
    nblk = ss.shape[0]
    xspec = pl.BlockSpec((1, seq, HY_W), lambda b: (b, 0, 0))
    return pl.pallas_call(
        functools.partial(_hy_direct_kernel, seq=seq),
        out_shape=jax.ShapeDtypeStruct((nb, seq, HY_W), BF16),
        grid=(nb,),
        in_specs=[pl.BlockSpec((2, seq, HY_W), lambda b: (0, 0, 0)), xspec, xspec,
                  pl.BlockSpec((nblk, 1, 2 * HY_W), lambda b: (0, 0, 0)),
                  pl.BlockSpec((1, HY_W), lambda b: (0, 0))],
        out_specs=xspec,
        scratch_shapes=[pltpu.VMEM((2 * seq, HY_W), F32)],
        compiler_params=_cparams(("arbitrary",)), name="hy_direct",
    )(f, x0, u, ss, bias)


def _hy_embedding(seq):
    t = jnp.arange(seq, dtype=F32)
    t_unit = t / max(seq - 1, 1)
    bands = jnp.linspace(1e-4, HY_BANDS - 1, HY_BANDS, dtype=F32)
    ang = (2 * jnp.pi / seq) * t[:, None] * bands[None, :]
    return jnp.concatenate([t_unit[:, None], jnp.cos(ang), -jnp.sin(ang)], axis=-1)


def _dft_tables(seq):
    n = 2 * seq
    n2 = DFT_N2
    n1 = n // n2
    a = jnp.arange(n1, dtype=jnp.int32)
    th1 = (2 * jnp.pi / n1) * ((a[:, None] * a[None, :]) % n1).astype(F32)
    w1r, w1i = jnp.cos(th1), -jnp.sin(th1)
    b = jnp.arange(n2, dtype=jnp.int32)
    tha = (2 * jnp.pi / n) * (a[:, None] * b[None, :]).astype(F32)
    thb = (2 * jnp.pi / n2) * ((b[:, None] * b[None, :]) % n2).astype(F32)
    ar, ai = jnp.cos(tha), -jnp.sin(tha)
    br, bi = jnp.cos(thb), -jnp.sin(thb)
    mr = ar[:, None, :] * br[None] - ai[:, None, :] * bi[None]
    mi = ar[:, None, :] * bi[None] + ai[:, None, :] * br[None]
    tr = ar[:, :, None] * br[None] - ai[:, :, None] * bi[None]
    ti = -(ar[:, :, None] * bi[None] + ai[:, :, None] * br[None])
    return dict(
        w1r=w1r.astype(BF16), w1i=w1i.astype(BF16),
        mr=mr.astype(BF16), mi=mi.astype(BF16), tr=tr.astype(BF16), ti=ti.astype(BF16),
        cr=(w1r[:n1 // 2] / n).astype(BF16), ci=(w1i[:n1 // 2] / n).astype(BF16))


def _hyena_long(phy, lw, filt, seq, tabs):
    n = phy.shape[0]
    nb = n // seq
    n2 = DFT_N2
    n1 = 2 * seq // n2
    f, ss = filt
    x0, u = _hy_conv3(phy, lw['hy_conv_w'], lw['hy_conv_b'], seq, min(seq, HY_CONV_TILE))
    tc = min(n2 * HY_W, DFT_COL_TILE)
    far, fai = _dft_outer(tabs['w1r'], tabs['w1i'], f.reshape(1, n1, n2 * HY_W), tc)
    fr, fi = _dft_inner_fwd(tabs['mr'], tabs['mi'], far.reshape(1, n1, n2, HY_W), fai.reshape(1, n1, n2, HY_W))
    ar, ai = _dft_outer(tabs['w1r'][:, :n1 // 2], tabs['w1i'][:, :n1 // 2], u.reshape(nb, n1 // 2, n2 * HY_W), tc)
    br, bi = _dft_inner_conv(tabs['mr'], tabs['mi'], tabs['tr'], tabs['ti'],
                             ar.reshape(nb, n1, n2, HY_W), ai.reshape(nb, n1, n2, HY_W), fr, fi)
    out = _hy_final(tabs['cr'], tabs['ci'], br.reshape(nb, n1, n2 * HY_W), bi.reshape(nb, n1, n2 * HY_W),
                    x0.reshape(nb, n1 // 2, n2 * HY_W), u.reshape(nb, n1 // 2, n2 * HY_W), ss, lw['hy_bias'], tc)
    return out.reshape(n, HY_W)


def _hyena_short(phy, lw, filt, seq):
    n = phy.shape[0]
    nb = n // seq
    f, ss = filt
    x0, u = _hy_conv3(phy, lw['hy_conv_w'], lw['hy_conv_b'], seq, seq)
    out = _hy_direct(f, x0.reshape(nb, seq, HY_W), u.reshape(nb, seq, HY_W), ss, lw['hy_bias'])
    return out.reshape(n, HY_W)


def _pair_top2_sum(a, b, c, d):
    return jnp.maximum(jnp.maximum(jnp.maximum(a + b, a + c), jnp.maximum(a + d, b + c)),
                       jnp.maximum(b + d, c + d))


def _route(logits_t, rb):
    aff = jax.nn.sigmoid(logits_t)
    sel = aff + rb
    rows = [sel[e:e + 1, :] for e in range(N_EXPERTS)]
    affr = [aff[e:e + 1, :] for e in range(N_EXPERTS)]
    best, bidx = None, None
    for g in range(N_GROUPS):
        gs = _pair_top2_sum(*rows[g * EXP_PER_GROUP:(g + 1) * EXP_PER_GROUP])
        if g == 0:
            best, bidx = gs, jnp.zeros(gs.shape, jnp.int32)
        else:
            upd = gs > best
            bidx = jnp.where(upd, g, bidx)
            best = jnp.where(upd, gs, best)
    vals = [jnp.where(bidx == e // EXP_PER_GROUP, rows[e], -jnp.inf) for e in range(N_EXPERTS)]
    m1, i1 = vals[0], jnp.zeros(best.shape, jnp.int32)
    for e in range(1, N_EXPERTS):
        upd = vals[e] > m1
        i1 = jnp.where(upd, e, i1)
        m1 = jnp.where(upd, vals[e], m1)
    m2, i2 = jnp.full(best.shape, -jnp.inf, F32), jnp.zeros(best.shape, jnp.int32)
    for e in range(N_EXPERTS):
        cand = jnp.where(i1 == e, -jnp.inf, vals[e])
        upd = cand > m2
        i2 = jnp.where(upd, e, i2)
        m2 = jnp.where(upd, cand, m2)
    a1 = sum(jnp.where(i1 == e, affr[e], 0.0) for e in range(N_EXPERTS))
    a2 = sum(jnp.where(i2 == e, affr[e], 0.0) for e in range(N_EXPERTS))
    inv = 1.0 / (a1 + a2)
    return jnp.concatenate([i1, i2], axis=0), jnp.concatenate([a1 * inv, a2 * inv], axis=0)


def _merge_kernel(a_ref, b_ref, m_ref, gt_ref, x_ref, gate_ref, shift_ref, scale_ref, g2_ref,
                  wpa_ref, wpb_ref, wpc_ref, wout_ref, rwt_ref, rb_ref, xo_ref, h2_ref, idx_ref, wts_ref):
    y = gt_ref[:, 0:D_MODEL].astype(F32) * jnp.dot(a_ref[...], wpa_ref[...], preferred_element_type=F32)
    y = y + gt_ref[:, D_MODEL:2 * D_MODEL].astype(F32) * jnp.dot(b_ref[...], wpb_ref[...],
                                                                 preferred_element_type=F32)
    y = y + gt_ref[:, 2 * D_MODEL:].astype(F32) * jnp.dot(m_ref[...], wpc_ref[...], preferred_element_type=F32)
    y2 = jnp.dot(y.astype(BF16), wout_ref[...], preferred_element_type=F32)
    xn = x_ref[...] + gate_ref[0] * y2
    xo_ref[...] = xn
    h2 = _rms(xn) * g2_ref[...]
    h2 = h2 * (1.0 + scale_ref[0]) + shift_ref[0]
    h2_ref[...] = h2
    logits_t = lax.dot_general(rwt_ref[...], h2, (((1,), (1,)), ((), ())), preferred_element_type=F32,
                               precision=HIGHEST)
    idx_ref[...], wts_ref[...] = _route(logits_t, rb_ref[...])


def _merge(a, b, m, gt, x, gate, shift, scale, lw, rw_t, rb, seq, tm):
    n = x.shape[0]
    tiles_per_seq = seq // tm
    const = lambda i: (0, 0)
    row = lambda w: pl.BlockSpec((tm, w), lambda i: (i, 0))
    vec = pl.BlockSpec((1, 1, D_MODEL), lambda i: (i // tiles_per_seq, 0, 0))
    return pl.pallas_call(
        _merge_kernel,
        out_shape=(jax.ShapeDtypeStruct((n, D_MODEL), F32), jax.ShapeDtypeStruct((n, D_MODEL), F32),
                   jax.ShapeDtypeStruct((TOP_K, n), jnp.int32), jax.ShapeDtypeStruct((TOP_K, n), F32)),
        grid=(n // tm,),
        in_specs=[row(HEADS * VDIM), row(HY_W), row(GM_W), row(3 * D_MODEL), row(D_MODEL), vec, vec, vec,
                  pl.BlockSpec((1, D_MODEL), const),
                  pl.BlockSpec((HEADS * VDIM, D_MODEL), const), pl.BlockSpec((HY_W, D_MODEL), const),
                  pl.BlockSpec((GM_W, D_MODEL), const), pl.BlockSpec((D_MODEL, D_MODEL), const),
                  pl.BlockSpec((N_EXPERTS, D_MODEL), const), pl.BlockSpec((N_EXPERTS, 1), const)],
        out_specs=(row(D_MODEL), row(D_MODEL), pl.BlockSpec((TOP_K, tm), lambda i: (0, i)),
                   pl.BlockSpec((TOP_K, tm), lambda i: (0, i))),
        compiler_params=_cparams(("arbitrary",)), name="merge",
    )(a, b, m, gt, x, gate, shift, scale, lw['g2'], lw['w_pa'], lw['w_pb'], lw['w_pc'], lw['w_out'], rw_t, rb)


def _rows_copy(src_hbm, dst, sem, idx_ref, rows):
    return [pltpu.make_async_copy(src_hbm.at[pl.ds(idx_ref[0, 0, r], 1)], dst.at[pl.ds(r, 1)], sem)
            for r in range(rows)]


def _rows_wait(src_hbm, dst, sem, rows):
    pltpu.make_async_copy(src_hbm.at[pl.ds(0, rows)], dst, sem).wait()


def _gather_pipeline(src_hbm, cur_ref, nxt_ref, buf, sem, rows):
    i = pl.program_id(0)
    slot = i % 2

    @pl.when(i == 0)
    def _():
        for cp in _rows_copy(src_hbm, buf.at[0], sem.at[0], cur_ref, rows):
            cp.start()

    _rows_wait(src_hbm, buf.at[slot], sem.at[slot], rows)
    for cp in _rows_copy(src_hbm, buf.at[1 - slot], sem.at[1 - slot], nxt_ref, rows):
        cp.start()
    return slot


def _gather_drain(src_hbm, buf, sem, rows):
    i = pl.program_id(0)

    @pl.when(i == pl.num_programs(0) - 1)
    def _():
        _rows_wait(src_hbm, buf.at[1 - i % 2], sem.at[1 - i % 2], rows)


def _moe_dispatch_kernel(pos_ref, h_ref, xs_in_hbm, xs_hbm, sem, *, tc):
    del xs_in_hbm
    for r in range(tc):
        pltpu.make_async_copy(h_ref.at[pl.ds(r, 1)], xs_hbm.at[pl.ds(pos_ref[0, 0, r], 1)], sem).start()
    pltpu.make_async_copy(h_ref, xs_hbm.at[pl.ds(0, tc)], sem).wait()


def _moe_dispatch(pos_tiles, h2, rows, tc):
    n = h2.shape[0]
    return pl.pallas_call(
        functools.partial(_moe_dispatch_kernel, tc=tc),
        out_shape=jax.ShapeDtypeStruct((rows, D_MODEL), F32),
        grid=(n // tc,),
        in_specs=[pl.BlockSpec((1, 1, tc), lambda i: (i, 0, 0), memory_space=pltpu.SMEM),
                  pl.BlockSpec((tc, D_MODEL), lambda i: (i, 0)),
                  pl.BlockSpec(memory_space=pl.ANY)],
        out_specs=pl.BlockSpec(memory_space=pl.ANY),
        scratch_shapes=[pltpu.SemaphoreType.DMA],
        input_output_aliases={2: 0},
        compiler_params=_cparams(("arbitrary",)), name="moe_dispatch",
    )(pos_tiles, h2, jnp.zeros((rows, D_MODEL), F32))


def _moe_ffn_kernel(ea_ref, eb_ref, x_ref, wgua_ref, wda_ref, wgub_ref, wdb_ref, y_ref):
    del ea_ref, eb_ref
    xb = x_ref[...].astype(BF16)
    for j, (wgu_ref, wd_ref) in enumerate(((wgua_ref, wda_ref), (wgub_ref, wdb_ref))):
        gu = jnp.dot(xb, wgu_ref[0], preferred_element_type=F32)
        g = gu[:, :D_EXPERT]
        hid = (g * _sigmoid(g) * gu[:, D_EXPERT:]).astype(BF16)
        y_ref[:, j * D_MODEL:(j + 1) * D_MODEL] = jnp.dot(hid, wd_ref[0], preferred_element_type=F32)


def _moe_ffn(blk_ea, blk_eb, x_sorted, lw):
    nblk = blk_ea.shape[0]
    wgu_spec = lambda sel: pl.BlockSpec((1, D_MODEL, 2 * D_EXPERT), lambda i, ea, eb: (sel(ea, eb)[i], 0, 0))
    wd_spec = lambda sel: pl.BlockSpec((1, D_EXPERT, D_MODEL), lambda i, ea, eb: (sel(ea, eb)[i], 0, 0))
    first, second = (lambda ea, eb: ea), (lambda ea, eb: eb)
    return pl.pallas_call(
        _moe_ffn_kernel,
        out_shape=jax.ShapeDtypeStruct((nblk * MOE_BM, TOP_K * D_MODEL), F32),
        grid_spec=pltpu.PrefetchScalarGridSpec(
            num_scalar_prefetch=2, grid=(nblk,),
            in_specs=[pl.BlockSpec((MOE_BM, D_MODEL), lambda i, ea, eb: (i, 0)),
                      wgu_spec(first), wd_spec(first), wgu_spec(second), wd_spec(second)],
            out_specs=pl.BlockSpec((MOE_BM, TOP_K * D_MODEL), lambda i, ea, eb: (i, 0))),
        compiler_params=_cparams(("arbitrary",)), name="moe_ffn",
    )(blk_ea, blk_eb, x_sorted, lw['w_gu'], lw['w_d'], lw['w_gu'], lw['w_d'])


def _moe_combine_kernel(cur_ref, nxt_ref, y_hbm, x_ref, w_ref, gate_ref, o_ref, buf, sem, *, tc):
    slot = _gather_pipeline(y_hbm, cur_ref, nxt_ref, buf, sem, tc)
    w = w_ref[...]
    y = w[:, 0:1] * buf[slot, :, 0:D_MODEL] + w[:, 1:2] * buf[slot, :, D_MODEL:]
    o_ref[...] = x_ref[...] + gate_ref[0] * y
    _gather_drain(y_hbm, buf, sem, tc)


def _moe_combine(pos_tiles, y_sorted, x, w_cols, gate, seq, tc):
    n = x.shape[0]
    tiles_per_seq = seq // tc
    idx_spec = lambda off: pl.BlockSpec((1, 1, tc), lambda i: (i + off, 0, 0), memory_space=pltpu.SMEM)
    return pl.pallas_call(
        functools.partial(_moe_combine_kernel, tc=tc),
        out_shape=jax.ShapeDtypeStruct((n, D_MODEL), F32),
        grid=(n // tc,),
        in_specs=[idx_spec(0), idx_spec(1), pl.BlockSpec(memory_space=pl.ANY),
                  pl.BlockSpec((tc, D_MODEL), lambda i: (i, 0)),
                  pl.BlockSpec((tc, TOP_K), lambda i: (i, 0)),
                  pl.BlockSpec((1, 1, D_MODEL), lambda i: (i // tiles_per_seq, 0, 0))],
        out_specs=pl.BlockSpec((tc, D_MODEL), lambda i: (i, 0)),
        scratch_shapes=[pltpu.VMEM((2, tc, TOP_K * D_MODEL), F32), pltpu.SemaphoreType.DMA((2,))],
        compiler_params=_cparams(("arbitrary",)), name="moe_combine",
    )(pos_tiles, pos_tiles, y_sorted, x, w_cols, gate)


def _cumsum_rows(onehot):
    n, c = onehot.shape
    blk = math.gcd(n, 256)
    x = onehot.astype(F32).reshape(n // blk, blk, c)
    tri = (jnp.arange(blk)[:, None] >= jnp.arange(blk)[None, :]).astype(F32)
    within = jnp.einsum('ij,bjc->bic', tri, x, precision=HIGHEST)
    nb = n // blk
    before = (jnp.arange(nb)[:, None] > jnp.arange(nb)[None, :]).astype(F32)
    offset = jnp.dot(before, within[:, -1, :], precision=HIGHEST)
    return (within + offset[:, None, :]).reshape(n, c).astype(jnp.int32)


def _pair_tables():
    pairs = [(a, b) for a in range(EXP_PER_GROUP) for b in range(a + 1, EXP_PER_GROUP)]
    ea = [g * EXP_PER_GROUP + a for g in range(N_GROUPS) for a, _ in pairs]
    eb = [g * EXP_PER_GROUP + b for g in range(N_GROUPS) for _, b in pairs]
    return jnp.array(ea, jnp.int32), jnp.array(eb, jnp.int32)


def _moe(h2, idx, wts, x, gate, lw, seq):
    n = x.shape[0]
    tab_a, tab_b = _pair_tables()
    n_cls = tab_a.shape[0]
    nblk = n // MOE_BM + n_cls
    swap = idx[0] > idx[1]
    e_lo, e_hi = jnp.minimum(idx[0], idx[1]), jnp.maximum(idx[0], idx[1])
    w_cols = jnp.stack([jnp.where(swap, wts[1], wts[0]), jnp.where(swap, wts[0], wts[1])], axis=1)
    onehot = ((e_lo[:, None] == tab_a[None, :]) & (e_hi[:, None] == tab_b[None, :])).astype(jnp.int32)
    csum = _cumsum_rows(onehot)
    counts = csum[-1]
    rank = jnp.sum(csum * onehot, axis=1) - 1
    seg_len = (counts + MOE_BM - 1) // MOE_BM * MOE_BM
    seg_end = jnp.cumsum(seg_len)
    pos = jnp.sum(onehot * (seg_end - seg_len)[None, :], axis=1) + rank
    blk_start = jnp.arange(nblk, dtype=jnp.int32) * MOE_BM
    blk_cls = jnp.minimum(jnp.sum((seg_end[None, :] <= blk_start[:, None]).astype(jnp.int32), axis=1), n_cls - 1)
    td = min(n, MOE_DISPATCH_TILE)
    x_sorted = _moe_dispatch(pos.reshape(n // td, 1, td), h2, nblk * MOE_BM, td)
    y_sorted = _moe_ffn(tab_a[blk_cls], tab_b[blk_cls], x_sorted, lw)
    tc = min(seq, MOE_COMBINE_TILE)
    pos_next = jnp.concatenate([pos.reshape(n // tc, 1, tc), jnp.zeros((1, 1, tc), jnp.int32)], axis=0)
    return _moe_combine(pos_next, y_sorted, x, w_cols, gate, seq, tc)


def _head_perm():
    rope_idx = list(range(NOPE, QK, 2)) + list(range(NOPE + 1, QK, 2))
    return jnp.array(list(range(NOPE)) + rope_idx, dtype=jnp.int32)


def _block_diag(a, b):
    return jnp.concatenate([jnp.pad(a, ((0, 0), (0, b.shape[1]))), jnp.pad(b, ((0, 0), (a.shape[1], 0)))], axis=0)


def _layer_weights(p, l):
    perm = _head_perm()
    row = lambda v: v.reshape(1, -1)
    w_in = p['w_in'][l]
    kr_perm = jnp.array(list(range(0, ROPE, 2)) + list(range(1, ROPE, 2)), dtype=jnp.int32)
    w_kr = jnp.pad(w_in[:, OFF_KR:OFF_HY][:, kr_perm], ((0, 0), (0, HEAD_PAD - ROPE)))
    w_in2 = jnp.concatenate([w_in[:, OFF_Q:OFF_KR], w_kr, w_in[:, OFF_HY:]], axis=1).astype(BF16)
    w_uq = p['w_uq'][l].reshape(Q_RANK, HEADS, QK)[:, :, perm]
    w_uq = jnp.pad(w_uq, ((0, 0), (0, 0), (0, HEAD_PAD - QK))).reshape(Q_RANK, HEADS * HEAD_PAD).astype(BF16)
    w_ukv = p['w_ukv'][l].reshape(KV_RANK, HEADS, NOPE + VDIM)
    w_k = jnp.pad(w_ukv[:, :, :NOPE], ((0, 0), (0, 0), (0, HEAD_PAD - NOPE)))
    w_k = w_k.reshape(KV_RANK, HEADS * HEAD_PAD).astype(BF16)
    w_vt = w_ukv[:, :, NOPE:].reshape(KV_RANK, HEADS * VDIM).T.astype(BF16)
    pad_gain = lambda g: jnp.pad(g[perm], (0, HEAD_PAD - QK)).reshape(1, HEAD_PAD)
    gm_b = jnp.repeat(p['gm_bs'][l].T, GM_W // GM_GROUPS, axis=1)
    bound = 1.02 * QSCALE * QK * jnp.max(jnp.abs(p['g_qn'][l])) * jnp.max(jnp.abs(p['g_kn'][l]))
    pad_lane = jnp.arange(HEAD_PAD) == QK
    return dict(
        attn_bound=bound,
        q_pad=jnp.where(pad_lane, -bound, 0.0).reshape(1, HEAD_PAD).astype(F32),
        k_pad=jnp.where(pad_lane, 1.0, 0.0).reshape(1, HEAD_PAD).astype(F32),
        g1=row(p['norm1_g'][l]), g2=row(p['norm2_g'][l]), w_in=w_in2,
        g_qa=row(p['g_qa'][l]), w_uq=w_uq, g_qn=pad_gain(p['g_qn'][l]),
        g_kva=row(p['g_kva'][l]), w_k=w_k, w_vt=w_vt, g_kn=pad_gain(p['g_kn'][l]),
        gm_g=row(p['gm_norm_g'][l]), gm_w=p['gm_ws'][l].astype(BF16), gm_b=gm_b,
        hy_conv_w=p['hy_conv_w'][l], hy_conv_b=row(p['hy_conv_b'][l]),
        hy_w1p=_block_diag(p['hy_w1'][l], p['hy_w1'][l]), hy_b1p=row(jnp.tile(p['hy_b1'][l], 2)),
        hy_w2p=_block_diag(p['hy_w2'][l], p['hy_w2'][l]), hy_b2p=row(jnp.tile(p['hy_b2'][l], 2)),
        hy_w3p=_block_diag(p['hy_w3'][l][:, :HY_W], p['hy_w3'][l][:, HY_W:]), hy_b3=row(p['hy_b3'][l]),
        hy_freqp=row(jnp.tile(p['hy_freq'][l], 2)),
        hy_decay=p['hy_decay'][l].reshape(1, 2 * HY_W), hy_bias=row(p['hy_bias'][l]),
        w_pa=p['w_pa'][l].astype(BF16), w_pb=p['w_pb'][l].astype(BF16), w_pc=p['w_pc'][l].astype(BF16),
        w_out=p['w_out'][l].astype(BF16),
        w_gu=jnp.concatenate([p['moe_w_gate'][l], p['moe_w_up'][l]], axis=-1).astype(BF16),
        w_d=p['moe_w_down'][l].astype(BF16))


def _rope_tables(seq):
    rows = seq // GRID_W
    row = jnp.repeat(jnp.arange(rows, dtype=F32), GRID_W)
    col = jnp.tile(jnp.arange(GRID_W, dtype=F32), rows)
    n_freq = ROPE // 4
    inv = ROPE_THETA ** (-jnp.arange(n_freq, dtype=F32) / n_freq)
    ang = jnp.concatenate([row[:, None] * inv, col[:, None] * inv], axis=-1)
    c, s = jnp.cos(ang), jnp.sin(ang)
    z = lambda w: jnp.zeros((seq, w), F32)
    rc = jnp.concatenate([jnp.ones((seq, NOPE), F32), c, c, z(HEAD_PAD - QK)], axis=1)
    rs1 = jnp.concatenate([z(NOPE), z(ROPE // 2), s, z(HEAD_PAD - QK)], axis=1)
    rs2 = jnp.concatenate([z(NOPE), -s, z(ROPE // 2), z(HEAD_PAD - QK)], axis=1)
    return rc, rs1, rs2


def _mixer_and_ffn(x, mods, lw, rw_t, rb, seq, tm, rope_tabs, ctx_kv, filt, dft_tabs):
    shift1, scale1, gate1, shift2, scale2, gate2 = mods
    q, k, vt, m, phy, gt = _premix(x, shift1, scale1, lw, rope_tabs, seq, tm)
    a = _attention(q, k, vt, ctx_kv, seq, min(seq, ATTN_TQ), lw['attn_bound'])
    if dft_tabs is not None:
        b = _hyena_long(phy, lw, filt, seq, dft_tabs)
    else:
        b = _hyena_short(phy, lw, filt, seq)
    xn, h2, idx, wts = _merge(a, b, m, gt, x, gate1, shift2, scale2, lw, rw_t, rb, seq, tm)
    return _moe(h2, idx, wts, xn, gate2, lw, seq), k, vt


def _forward(p):
    x, ctx = p['x'], p['ctx']
    batch, seq, _ = x.shape
    lc = ctx.shape[1]
    depth = p['w_mod'].shape[0]

    cvecs = jnp.concatenate([p['c'], p['c_ctx'][None], jnp.zeros((8 - batch - 1, D_MODEL), F32)], axis=0)
    mod_all = _modvec(cvecs, p['w_mod'], p['b_mod'])
    rw_t = p['router_w'].T
    rb = p['router_b'].reshape(N_EXPERTS, 1)
    rope_tabs = _rope_tables(seq)
    dft_tabs = _dft_tables(seq)
    z_lat, z_ctx = _hy_embedding(seq), _hy_embedding(lc)

    xl = x.reshape(batch * seq, D_MODEL)
    xc = ctx.reshape(batch * lc, D_MODEL)
    tm_lat = min(seq, TOKEN_TILE)
    for l in range(depth):
        lw = _layer_weights(p, l)
        mod = mod_all[l].reshape(8, N_MOD, D_MODEL)
        mods_lat = [mod[:batch, j].reshape(batch, 1, D_MODEL) for j in range(N_MOD)]
        mods_ctx = [jnp.broadcast_to(mod[batch, j].reshape(1, 1, D_MODEL), (batch, 1, D_MODEL))
                    for j in range(N_MOD)]
        if l == depth - 1:
            _, k_c, vt_c, _, _, _ = _premix(xc, mods_ctx[0], mods_ctx[1], lw, None, lc, lc)
        else:
            xc, k_c, vt_c = _mixer_and_ffn(xc, mods_ctx, lw, rw_t, rb, lc, lc, None, None,
                                           _hy_filters(z_ctx, lw, lc), None)
        filt = _hy_filters(z_lat, lw, min(seq, HY_FILTER_TILE))
        xl, _, _ = _mixer_and_ffn(xl, mods_lat, lw, rw_t, rb, seq, tm_lat, rope_tabs, (k_c, vt_c), filt, dft_tabs)
    return xl.reshape(batch, seq, D_MODEL)


def kernel(x, c, ctx, c_ctx, w_mod, b_mod, norm1_g, norm2_g, w_in, g_qa, w_uq, g_kva, w_ukv, g_qn, g_kn,
           hy_conv_w, hy_conv_b, hy_w1, hy_b1, hy_w2, hy_b2, hy_w3, hy_b3, hy_freq, hy_decay, hy_bias,
           gm_norm_g, gm_ws, gm_bs, w_pa, w_pb, w_pc, w_out, router_w, router_b,
           moe_w_gate, moe_w_up, moe_w_down):
    return _forward(dict(
        x=x, c=c, ctx=ctx, c_ctx=c_ctx, w_mod=w_mod, b_mod=b_mod, norm1_g=norm1_g, norm2_g=norm2_g, w_in=w_in,
        g_qa=g_qa, w_uq=w_uq, g_kva=g_kva, w_ukv=w_ukv, g_qn=g_qn, g_kn=g_kn, hy_conv_w=hy_conv_w,
        hy_conv_b=hy_conv_b, hy_w1=hy_w1, hy_b1=hy_b1, hy_w2=hy_w2, hy_b2=hy_b2, hy_w3=hy_w3, hy_b3=hy_b3,
        hy_freq=hy_freq, hy_decay=hy_decay, hy_bias=hy_bias, gm_norm_g=gm_norm_g, gm_ws=gm_ws, gm_bs=gm_bs,
        w_pa=w_pa, w_pb=w_pb, w_pc=w_pc, w_out=w_out, router_w=router_w, router_b=router_b,
        moe_w_gate=moe_w_gate, moe_w_up=moe_w_up, moe_w_down=moe_w_down))
```

```python
import functools
import math

import jax
import jax.numpy as jnp
from jax import lax
from jax.experimental import pallas as pl
from jax.experimental.pallas import tpu as pltpu

F32 = jnp.float32
BF16 = jnp.bfloat16
HIGHEST = lax.Precision.HIGHEST

D_MODEL = 1024
GRID_W = 64
EPS = 1e-6
N_MOD = 6

HEADS = 8
Q_RANK = 384
KV_RANK = 256
NOPE = 64
ROPE = 32
QK = NOPE + ROPE
VDIM = 64
HEAD_PAD = 128
ROPE_THETA = 10000.0
V_ROWS = 80

HY_W = 256
HY_BANDS = 16
GM_W = 256
GM_CHUNK = 128
GM_GROUPS = 4

OFF_Q = 0
OFF_KV = OFF_Q + Q_RANK
OFF_KR = OFF_KV + KV_RANK
OFF_HY = OFF_KR + ROPE
OFF_GM = OFF_HY + 3 * HY_W
OFF_GT = OFF_GM + 2 * GM_W

P_Q = 0
P_KV = P_Q + Q_RANK
P_KR = P_KV + KV_RANK
P_HY = P_KR + HEAD_PAD
P_GM = P_HY + 3 * HY_W
P_GT = P_GM + 2 * GM_W
P_W = P_GT + 3 * D_MODEL

N_EXPERTS = 16
N_GROUPS = 4
EXP_PER_GROUP = 4
TOP_K = 2
D_EXPERT = 512
MOE_BM = 256
MOE_DISPATCH_TILE = 1024
MOE_COMBINE_TILE = 512

DFT_N2 = 256
DFT_K1_STEP = 4
DFT_COL_TILE = 8192

TOKEN_TILE = 512
MERGE_TILE = 1024
HY_CONV_TILE = 1024
HY_FILTER_TILE = 2048

VMEM_LIMIT = 56 * 1024 * 1024
ATTN_TQ = 512
ATTN_TK = 2048
ATTN_BOUND_MAX = 50.0
NEG_BIG = -1e30
LOG2E = 1.4426950408889634
QSCALE = QK ** -0.5 * LOG2E


def _cparams(sem):
    return pltpu.CompilerParams(dimension_semantics=sem, vmem_limit_bytes=VMEM_LIMIT)


def _rms(x):
    return x * lax.rsqrt(jnp.mean(x * x, axis=-1, keepdims=True) + EPS)


def _sigmoid(x):
    return 0.5 * jnp.tanh(0.5 * x) + 0.5


def _nt_dot(a, b):
    return lax.dot_general(a, b, (((1,), (1,)), ((), ())), preferred_element_type=F32)


def _modvec_kernel(c_ref, w_ref, b_ref, o_ref):
    cv = c_ref[...]
    s = cv * jax.nn.sigmoid(cv)
    o_ref[0] = jnp.dot(s, w_ref[0], preferred_element_type=F32, precision=HIGHEST) + b_ref[0]


def _modvec(cvecs, w_mod, b_mod):
    depth = w_mod.shape[0]
    tn = 1536
    return pl.pallas_call(
        _modvec_kernel,
        out_shape=jax.ShapeDtypeStruct((depth, 8, N_MOD * D_MODEL), F32),
        grid=(depth, N_MOD * D_MODEL // tn),
        in_specs=[pl.BlockSpec((8, D_MODEL), lambda l, j: (0, 0)),
                  pl.BlockSpec((1, D_MODEL, tn), lambda l, j: (l, 0, j)),
                  pl.BlockSpec((1, 1, tn), lambda l, j: (l, 0, j))],
        out_specs=pl.BlockSpec((1, 8, tn), lambda l, j: (l, 0, j)),
        compiler_params=_cparams(("arbitrary", "arbitrary")),
        name="modvec",
    )(cvecs, w_mod, b_mod.reshape(depth, 1, N_MOD * D_MODEL))


def _head_norm_rope(xh, gain, rope):
    ms = jnp.sum(xh * xh, axis=-1, keepdims=True) * (1.0 / QK)
    xh = xh * lax.rsqrt(ms + EPS) * gain
    if rope is not None:
        rc, rs1, rs2 = rope
        xh = xh * rc + pltpu.roll(xh, ROPE // 2, 1) * rs1 + pltpu.roll(xh, HEAD_PAD - ROPE // 2, 1) * rs2
    return xh


def _premix_kernel(*refs, use_rope, tm):
    if use_rope:
        (x_ref, shift_ref, scale_ref, g1_ref, win_ref, gqa_ref, wuq_ref, gqn_ref, gkva_ref, wk_ref, wvt_ref,
         gkn_ref, gmg_ref, gmw_ref, gmb_ref, qpad_ref, kpad_ref, rc_ref, rs1_ref, rs2_ref,
         q_ref, k_ref, vt_ref, m_ref, hy_ref, gt_ref) = refs
        rope = (rc_ref[...], rs1_ref[...], rs2_ref[...])
    else:
        (x_ref, shift_ref, scale_ref, g1_ref, win_ref, gqa_ref, wuq_ref, gqn_ref, gkva_ref, wk_ref, wvt_ref,
         gkn_ref, gmg_ref, gmw_ref, gmb_ref, qpad_ref, kpad_ref,
         q_ref, k_ref, vt_ref, m_ref, hy_ref, gt_ref) = refs
        rope = None

    x = x_ref[...]
    h = _rms(x) * g1_ref[...]
    h = h * (1.0 + scale_ref[0]) + shift_ref[0]
    hb = h.astype(BF16)

    def proj(lo, width):
        return jnp.dot(hb, win_ref[:, lo:lo + width], preferred_element_type=F32)

    qa = (_rms(proj(P_Q, Q_RANK)) * gqa_ref[...]).astype(BF16)
    q = jnp.dot(qa, wuq_ref[...], preferred_element_type=F32)
    for hh in range(HEADS):
        qh = _head_norm_rope(q[:, hh * HEAD_PAD:(hh + 1) * HEAD_PAD], gqn_ref[...], rope)
        q_ref[:, hh * HEAD_PAD:(hh + 1) * HEAD_PAD] = (qh * QSCALE + qpad_ref[...]).astype(BF16)

    kva = (_rms(proj(P_KV, KV_RANK)) * gkva_ref[...]).astype(BF16)
    kr = pltpu.roll(proj(P_KR, HEAD_PAD), NOPE, 1)
    kn = jnp.dot(kva, wk_ref[...], preferred_element_type=F32)
    for hh in range(HEADS):
        kh = _head_norm_rope(kn[:, hh * HEAD_PAD:(hh + 1) * HEAD_PAD] + kr, gkn_ref[...], rope)
        k_ref[:, hh * HEAD_PAD:(hh + 1) * HEAD_PAD] = (kh + kpad_ref[...]).astype(BF16)
    vt = _nt_dot(wvt_ref[...], kva)
    row = lax.broadcasted_iota(jnp.int32, (V_ROWS - VDIM, tm), 0)
    ones_rows = jnp.where(row == 0, 1.0, 0.0).astype(BF16)
    for hh in range(HEADS):
        vt_ref[0, 0, hh * V_ROWS:hh * V_ROWS + VDIM, :] = vt[hh * VDIM:(hh + 1) * VDIM].astype(BF16)
        vt_ref[0, 0, hh * V_ROWS + VDIM:(hh + 1) * V_ROWS, :] = ones_rows

    gg = jax.nn.gelu(proj(P_GM, 2 * GM_W), approximate=True)
    gu = gg[:, :GM_W]
    gv = (_rms(gg[:, GM_W:]) * gmg_ref[...]).astype(BF16)
    grp = lax.broadcasted_iota(jnp.int32, (GM_CHUNK, GM_W), 1) // (GM_W // GM_GROUPS)
    for ci in range(tm // GM_CHUNK):
        vc = gv[ci * GM_CHUNK:(ci + 1) * GM_CHUNK]
        s = jnp.zeros((GM_CHUNK, GM_W), F32)
        for g in range(GM_GROUPS):
            sg = jnp.dot(gmw_ref[g], vc, preferred_element_type=F32)
            s = jnp.where(grp == g, sg, s)
        m_ref[ci * GM_CHUNK:(ci + 1) * GM_CHUNK, :] = (
            gu[ci * GM_CHUNK:(ci + 1) * GM_CHUNK] * (s + gmb_ref[...])).astype(BF16)

    hy_ref[...] = proj(P_HY, 3 * HY_W).astype(BF16)
    for j in range(3):
        gt_ref[:, j * D_MODEL:(j + 1) * D_MODEL] = _sigmoid(proj(P_GT + j * D_MODEL, D_MODEL)).astype(BF16)


def _premix(x, shift, scale, lw, rope_tabs, seq, tm):
    n = x.shape[0]
    tiles_per_seq = seq // tm
    batch = n // seq
    use_rope = rope_tabs is not None
    const = lambda i: (0, 0)
    in_specs = [
        pl.BlockSpec((tm, D_MODEL), lambda i: (i, 0)),
        pl.BlockSpec((1, 1, D_MODEL), lambda i: (i // tiles_per_seq, 0, 0)),
        pl.BlockSpec((1, 1, D_MODEL), lambda i: (i // tiles_per_seq, 0, 0)),
        pl.BlockSpec((1, D_MODEL), const),
        pl.BlockSpec((D_MODEL, P_W), const, pipeline_mode=pl.Buffered(1)),
        pl.BlockSpec((1, Q_RANK), const),
        pl.BlockSpec((Q_RANK, HEADS * HEAD_PAD), const),
        pl.BlockSpec((1, HEAD_PAD), const),
        pl.BlockSpec((1, KV_RANK), const),
        pl.BlockSpec((KV_RANK, HEADS * HEAD_PAD), const),
        pl.BlockSpec((HEADS * VDIM, KV_RANK), const),
        pl.BlockSpec((1, HEAD_PAD), const),
        pl.BlockSpec((1, GM_W), const),
        pl.BlockSpec((GM_GROUPS, GM_CHUNK, GM_CHUNK), lambda i: (0, 0, 0)),
        pl.BlockSpec((GM_CHUNK, GM_W), const),
        pl.BlockSpec((1, HEAD_PAD), const),
        pl.BlockSpec((1, HEAD_PAD), const),
    ]
    args = [x, shift, scale, lw['g1'], lw['w_in'], lw['g_qa'], lw['w_uq'], lw['g_qn'], lw['g_kva'], lw['w_k'],
            lw['w_vt'], lw['g_kn'], lw['gm_g'], lw['gm_w'], lw['gm_b'], lw['q_pad'], lw['k_pad']]
    tkv = min(seq, ATTN_TK)
    sub = tkv // tm
    if use_rope:
        in_specs += [pl.BlockSpec((tm, HEAD_PAD), lambda i: (i % tiles_per_seq, 0))] * 3
        args += list(rope_tabs)
    out_shape = (
        jax.ShapeDtypeStruct((n, HEADS * HEAD_PAD), BF16),
        jax.ShapeDtypeStruct((n, HEADS * HEAD_PAD), BF16),
        jax.ShapeDtypeStruct((batch, seq // tkv, HEADS * V_ROWS, tkv), BF16),
        jax.ShapeDtypeStruct((n, GM_W), BF16),
        jax.ShapeDtypeStruct((n, 3 * HY_W), BF16),
        jax.ShapeDtypeStruct((n, 3 * D_MODEL), BF16),
    )
    out_specs = (
        pl.BlockSpec((tm, HEADS * HEAD_PAD), lambda i: (i, 0)),
        pl.BlockSpec((tm, HEADS * HEAD_PAD), lambda i: (i, 0)),
        pl.BlockSpec((1, 1, HEADS * V_ROWS, tm),
                     lambda i: (i // tiles_per_seq, (i % tiles_per_seq) // sub, 0, (i % tiles_per_seq) % sub)),
        pl.BlockSpec((tm, GM_W), lambda i: (i, 0)),
        pl.BlockSpec((tm, 3 * HY_W), lambda i: (i, 0)),
        pl.BlockSpec((tm, 3 * D_MODEL), lambda i: (i, 0)),
    )
    return pl.pallas_call(
        functools.partial(_premix_kernel, use_rope=use_rope, tm=tm),
        out_shape=out_shape, grid=(n // tm,), in_specs=in_specs, out_specs=out_specs,
        compiler_params=_cparams(("arbitrary",)), name="premix",
    )(*args)


def _attn_kernel(*refs, n_chunks, has_ctx, tq):
    if has_ctx:
        q_ref, k_ref, vt_ref, kc_ref, vtc_ref, o_ref = refs
    else:
        q_ref, k_ref, vt_ref, o_ref = refs
    tk = vt_ref.shape[-1]

    def step(hh, kc, vtc, m, acc):
        s = _nt_dot(kc, q_ref[:, hh * HEAD_PAD:(hh + 1) * HEAD_PAD])
        m_new = jnp.maximum(m, jnp.max(s, axis=0, keepdims=True))
        p = jnp.exp2(s - m_new).astype(BF16)
        alpha = jnp.exp2(m - m_new)
        return m_new, acc * alpha + jnp.dot(vtc, p, preferred_element_type=F32)

    def body(i, carry):
        out = []
        for hh in range(2):
            m, acc = carry[hh]
            start = pl.multiple_of(i * tk, tk)
            kc = k_ref[pl.ds(start, tk), hh * HEAD_PAD:(hh + 1) * HEAD_PAD]
            vtc = vt_ref[0, i, hh * V_ROWS:(hh + 1) * V_ROWS, :]
            out.append(step(hh, kc, vtc, m, acc))
        return tuple(out)

    init = tuple((jnp.full((1, tq), NEG_BIG, F32), jnp.zeros((V_ROWS, tq), F32)) for _ in range(2))
    carry = lax.fori_loop(0, n_chunks, body, init)
    outs = []
    for hh in range(2):
        m, acc = carry[hh]
        if has_ctx:
            m, acc = step(hh, kc_ref[:, hh * HEAD_PAD:(hh + 1) * HEAD_PAD],
                          vtc_ref[0, 0, hh * V_ROWS:(hh + 1) * V_ROWS, :], m, acc)
        outs.append(acc[:VDIM] / acc[VDIM:VDIM + 1])
    o_ref[...] = jnp.concatenate(outs, axis=0).T.astype(BF16)


def _attn_bounded_kernel(*refs, n_chunks, has_ctx, tq):
    if has_ctx:
        q_ref, k_ref, vt_ref, kc_ref, vtc_ref, o_ref = refs
    else:
        q_ref, k_ref, vt_ref, o_ref = refs
    tk = vt_ref.shape[-1]

    def step(hh, kc, vtc, acc):
        s = _nt_dot(kc, q_ref[:, hh * HEAD_PAD:(hh + 1) * HEAD_PAD])
        return acc + jnp.dot(vtc, jnp.exp2(s).astype(BF16), preferred_element_type=F32)

    def body(i, carry):
        start = pl.multiple_of(i * tk, tk)
        return tuple(step(hh, k_ref[pl.ds(start, tk), hh * HEAD_PAD:(hh + 1) * HEAD_PAD],
                          vt_ref[0, i, hh * V_ROWS:(hh + 1) * V_ROWS, :], carry[hh]) for hh in range(2))

    carry = lax.fori_loop(0, n_chunks, body, tuple(jnp.zeros((V_ROWS, tq), F32) for _ in range(2)), unroll=True)
    outs = []
    for hh in range(2):
        acc = carry[hh]
        if has_ctx:
            acc = step(hh, kc_ref[:, hh * HEAD_PAD:(hh + 1) * HEAD_PAD],
                       vtc_ref[0, 0, hh * V_ROWS:(hh + 1) * V_ROWS, :], acc)
        outs.append(acc[:VDIM] / acc[VDIM:VDIM + 1])
    o_ref[...] = jnp.concatenate(outs, axis=0).T.astype(BF16)


def _attention(q, k, vt, ctx_kv, seq, tq, bound):
    n = q.shape[0]
    batch = n // seq
    n_chunks, tk = vt.shape[1], vt.shape[3]
    q_tiles = seq // tq
    has_ctx = ctx_kv is not None
    in_specs = [
        pl.BlockSpec((tq, 2 * HEAD_PAD), lambda b, j, i: (b * q_tiles + i, j)),
        pl.BlockSpec((seq, 2 * HEAD_PAD), lambda b, j, i: (b, j)),
        pl.BlockSpec((1, n_chunks, 2 * V_ROWS, tk), lambda b, j, i: (b, 0, j, 0)),
    ]
    args = [q, k, vt]
    if has_ctx:
        kc, vtc = ctx_kv
        lc = vtc.shape[3]
        in_specs += [pl.BlockSpec((lc, 2 * HEAD_PAD), lambda b, j, i: (b, j)),
                     pl.BlockSpec((1, 1, 2 * V_ROWS, lc), lambda b, j, i: (b, 0, j, 0))]
        args += [kc, vtc]
    def call(body, name):
        return pl.pallas_call(
            functools.partial(body, n_chunks=n_chunks, has_ctx=has_ctx, tq=tq),
            out_shape=jax.ShapeDtypeStruct((n, HEADS * VDIM), BF16),
            grid=(batch, HEADS // 2, q_tiles),
            in_specs=in_specs,
            out_specs=pl.BlockSpec((tq, 2 * VDIM), lambda b, j, i: (b * q_tiles + i, j)),
            compiler_params=_cparams(("arbitrary", "arbitrary", "arbitrary")),
            name=name,
        )(*args)

    return lax.cond(bound < ATTN_BOUND_MAX,
                    lambda: call(_attn_bounded_kernel, "attn_bounded"),
                    lambda: call(_attn_kernel, "attn_online"))


def _hy_filter_kernel(zz_ref, w1_ref, b1_ref, w2_ref, b2_ref, w3_ref, b3_ref, fr_ref, dec_ref, f_ref, ss_ref, *, emb):
    fr = fr_ref[...]
    dec = jnp.abs(dec_ref[...])
    zz = zz_ref[...]
    hdn = jnp.sin(fr * (jnp.dot(zz, w1_ref[...], preferred_element_type=F32, precision=HIGHEST) + b1_ref[...]))
    hdn = jnp.sin(fr * (jnp.dot(hdn, w2_ref[...], preferred_element_type=F32, precision=HIGHEST) + b2_ref[...]))
    k = jnp.dot(hdn, w3_ref[...], preferred_element_type=F32, precision=HIGHEST) + b3_ref[...]
    col = lax.broadcasted_iota(jnp.int32, k.shape, 1)
    k = k * jnp.exp(-jnp.where(col < HY_W, zz[:, 0:1], zz[:, emb:emb + 1]) * dec)
    first_block = pl.program_id(0) == 0
    row = lax.broadcasted_iota(jnp.int32, k.shape, 0)
    k = jnp.where(first_block & (row == 0) & (col >= HY_W), 0.0, k)
    f_ref[0] = k[:, :HY_W].astype(BF16)
    f_ref[1] = k[:, HY_W:].astype(BF16)
    hid = hdn.shape[1] // 2
    k_0 = jnp.dot(pltpu.roll(hdn[0:8], hid, 1), w3_ref[...], preferred_element_type=F32,
                  precision=HIGHEST) + b3_ref[...]
    k_0 = k_0[0:1] * jnp.exp(-zz[0:1, 0:1] * dec)
    extra = jnp.where(first_block & (col[0:1] >= HY_W), k_0 * k_0, 0.0)
    ss_ref[0] = jnp.sum(k * k, axis=0, keepdims=True) + extra


def _hy_filters(z, lw, tr):
    seq, emb = z.shape
    nblk = seq // tr
    hid2 = lw['hy_w2p'].shape[0]
    const = lambda i: (0, 0)
    zz = jnp.concatenate([z, jnp.concatenate([z[0:1], z[:0:-1]], axis=0)], axis=1)
    return pl.pallas_call(
        functools.partial(_hy_filter_kernel, emb=emb),
        out_shape=(jax.ShapeDtypeStruct((2, seq, HY_W), BF16), jax.ShapeDtypeStruct((nblk, 1, 2 * HY_W), F32)),
        grid=(nblk,),
        in_specs=[pl.BlockSpec((tr, 2 * emb), lambda i: (i, 0)),
                  pl.BlockSpec((2 * emb, hid2), const), pl.BlockSpec((1, hid2), const),
                  pl.BlockSpec((hid2, hid2), const), pl.BlockSpec((1, hid2), const),
                  pl.BlockSpec((hid2, 2 * HY_W), const), pl.BlockSpec((1, 2 * HY_W), const),
                  pl.BlockSpec((1, hid2), const), pl.BlockSpec((1, 2 * HY_W), const)],
        out_specs=(pl.BlockSpec((2, tr, HY_W), lambda i: (0, i, 0)),
                   pl.BlockSpec((1, 1, 2 * HY_W), lambda i: (i, 0, 0))),
        compiler_params=_cparams(("arbitrary",)), name="hy_filter",
    )(zz, lw['hy_w1p'], lw['hy_b1p'], lw['hy_w2p'], lw['hy_b2p'], lw['hy_w3p'], lw['hy_b3'], lw['hy_freqp'],
      lw['hy_decay'])


def _hy_conv3_kernel(p_ref, prev_ref, next_ref, w_ref, b_ref, x0_ref, u_ref, *, tiles_per_seq, tr):
    i = pl.program_id(0)
    p = p_ref[...].astype(F32)
    first = (i % tiles_per_seq) == 0
    last = (i % tiles_per_seq) == tiles_per_seq - 1
    prev_row = jnp.where(first, 0.0, prev_ref[...].astype(F32)[15:16, :])
    next_row = jnp.where(last, 0.0, next_ref[...].astype(F32)[0:1, :])
    row = lax.broadcasted_iota(jnp.int32, p.shape, 0)
    p_prev = jnp.where(row == 0, prev_row, pltpu.roll(p, 1, 0))
    p_next = jnp.where(row == tr - 1, next_row, pltpu.roll(p, tr - 1, 0))
    z = b_ref[...] + p_prev * w_ref[0:1, :] + p * w_ref[1:2, :] + p_next * w_ref[2:3, :]
    x0_ref[...] = z[:, :HY_W].astype(BF16)
    u_ref[...] = (z[:, 2 * HY_W:] * z[:, HY_W:2 * HY_W]).astype(BF16)


def _hy_conv3(phy, conv_w, conv_b, seq, tr):
    n = phy.shape[0]
    tiles_per_seq = seq // tr
    hb = tr // 16
    nhb = n // 16
    return pl.pallas_call(
        functools.partial(_hy_conv3_kernel, tiles_per_seq=tiles_per_seq, tr=tr),
        out_shape=(jax.ShapeDtypeStruct((n, HY_W), BF16), jax.ShapeDtypeStruct((n, HY_W), BF16)),
        grid=(n // tr,),
        in_specs=[pl.BlockSpec((tr, 3 * HY_W), lambda i: (i, 0)),
                  pl.BlockSpec((16, 3 * HY_W), lambda i: (jnp.maximum(i * hb - 1, 0), 0)),
                  pl.BlockSpec((16, 3 * HY_W), lambda i: (jnp.minimum((i + 1) * hb, nhb - 1), 0)),
                  pl.BlockSpec((3, 3 * HY_W), lambda i: (0, 0)),
                  pl.BlockSpec((1, 3 * HY_W), lambda i: (0, 0))],
        out_specs=(pl.BlockSpec((tr, HY_W), lambda i: (i, 0)), pl.BlockSpec((tr, HY_W), lambda i: (i, 0))),
        compiler_params=_cparams(("arbitrary",)), name="hy_conv3",
    )(phy, phy, phy, conv_w, conv_b)


def _dft_outer_kernel(wr_ref, wi_ref, u_ref, ar_ref, ai_ref):
    u = u_ref[0]
    ar_ref[0] = jnp.dot(wr_ref[...], u, preferred_element_type=F32).astype(BF16)
    ai_ref[0] = jnp.dot(wi_ref[...], u, preferred_element_type=F32).astype(BF16)


def _dft_outer(wr, wi, u, tc):
    nb, kk, cols = u.shape
    n1 = wr.shape[0]
    return pl.pallas_call(
        _dft_outer_kernel,
        out_shape=(jax.ShapeDtypeStruct((nb, n1, cols), BF16),) * 2,
        grid=(nb, cols // tc),
        in_specs=[pl.BlockSpec((n1, kk), lambda b, j: (0, 0)), pl.BlockSpec((n1, kk), lambda b, j: (0, 0)),
                  pl.BlockSpec((1, kk, tc), lambda b, j: (b, 0, j))],
        out_specs=(pl.BlockSpec((1, n1, tc), lambda b, j: (b, 0, j)),) * 2,
        compiler_params=_cparams(("arbitrary", "arbitrary")), name="dft_outer",
    )(wr, wi, u)


def _cdot(mr, mi, xr, xi):
    rr = jnp.dot(mr, xr, preferred_element_type=F32) - jnp.dot(mi, xi, preferred_element_type=F32)
    ri = jnp.dot(mr, xi, preferred_element_type=F32) + jnp.dot(mi, xr, preferred_element_type=F32)
    return rr, ri


def _dft_inner_fwd_kernel(mr_ref, mi_ref, ar_ref, ai_ref, fr_ref, fi_ref):
    for j in range(DFT_K1_STEP):
        fr, fi = _cdot(mr_ref[j], mi_ref[j], ar_ref[0, j], ai_ref[0, j])
        fr_ref[j] = fr.astype(BF16)
        fi_ref[j] = fi.astype(BF16)


def _dft_inner_fwd(mr, mi, ar, ai):
    n1, n2, _ = mr.shape
    ch = ar.shape[-1]
    mspec = pl.BlockSpec((DFT_K1_STEP, n2, n2), lambda k: (k, 0, 0))
    aspec = pl.BlockSpec((1, DFT_K1_STEP, n2, ch), lambda k: (0, k, 0, 0))
    ospec = pl.BlockSpec((DFT_K1_STEP, n2, ch), lambda k: (k, 0, 0))
    return pl.pallas_call(
        _dft_inner_fwd_kernel,
        out_shape=(jax.ShapeDtypeStruct((n1, n2, ch), BF16),) * 2,
        grid=(n1 // DFT_K1_STEP,), in_specs=[mspec, mspec, aspec, aspec], out_specs=(ospec, ospec),
        compiler_params=_cparams(("arbitrary",)), name="dft_inner_fwd",
    )(mr, mi, ar, ai)


def _dft_inner_conv_kernel(mr_ref, mi_ref, tr_ref, ti_ref, ar_ref, ai_ref, fr_ref, fi_ref, br_ref, bi_ref, *, nb):
    for j in range(DFT_K1_STEP):
        fr = fr_ref[j].astype(F32)
        fi = fi_ref[j].astype(F32)
        for b in range(nb):
            ur, ui = _cdot(mr_ref[j], mi_ref[j], ar_ref[b, j], ai_ref[b, j])
            yr = (ur * fr - ui * fi).astype(BF16)
            yi = (ur * fi + ui * fr).astype(BF16)
            br, bi = _cdot(tr_ref[j], ti_ref[j], yr, yi)
            br_ref[b, j] = br.astype(BF16)
            bi_ref[b, j] = bi.astype(BF16)


def _dft_inner_conv(mr, mi, tr, ti, ar, ai, fr, fi):
    n1, n2, _ = mr.shape
    nb, _, _, ch = ar.shape
    mspec = pl.BlockSpec((DFT_K1_STEP, n2, n2), lambda k: (k, 0, 0))
    aspec = pl.BlockSpec((nb, DFT_K1_STEP, n2, ch), lambda k: (0, k, 0, 0))
    fspec = pl.BlockSpec((DFT_K1_STEP, n2, ch), lambda k: (k, 0, 0))
    return pl.pallas_call(
        functools.partial(_dft_inner_conv_kernel, nb=nb),
        out_shape=(jax.ShapeDtypeStruct((nb, n1, n2, ch), BF16),) * 2,
        grid=(n1 // DFT_K1_STEP,), in_specs=[mspec, mspec, mspec, mspec, aspec, aspec, fspec, fspec],
        out_specs=(aspec, aspec),
        compiler_params=_cparams(("arbitrary",)), name="dft_inner_conv",
    )(mr, mi, tr, ti, ar, ai, fr, fi)


def _filter_scale(ss_ref):
    ss = jnp.sum(ss_ref[...], axis=0)
    return lax.rsqrt(ss[:, :HY_W] + ss[:, HY_W:] + EPS)


def _hy_final_kernel(cr_ref, ci_ref, br_ref, bi_ref, x0_ref, u_ref, ss_ref, bias_ref, o_ref, *, reps):
    y = (jnp.dot(cr_ref[...], br_ref[0], preferred_element_type=F32)
         + jnp.dot(ci_ref[...], bi_ref[0], preferred_element_type=F32))
    scale = jnp.tile(_filter_scale(ss_ref), (1, reps))
    bias = jnp.tile(bias_ref[...], (1, reps))
    u = u_ref[0].astype(F32)
    o_ref[0] = (x0_ref[0].astype(F32) * (y * scale + u * bias)).astype(BF16)


def _hy_final(cr, ci, br, bi, x0, u, ss, bias, tc):
    nb, n1, cols = br.shape
    n1h = cr.shape[0]
    nblk = ss.shape[0]
    cspec = pl.BlockSpec((n1h, n1), lambda b, j: (0, 0))
    bspec = pl.BlockSpec((1, n1, tc), lambda b, j: (b, 0, j))
    xspec = pl.BlockSpec((1, n1h, tc), lambda b, j: (b, 0, j))
    return pl.pallas_call(
        functools.partial(_hy_final_kernel, reps=tc // HY_W),
        out_shape=jax.ShapeDtypeStruct((nb, n1h, cols), BF16),
        grid=(nb, cols // tc),
        in_specs=[cspec, cspec, bspec, bspec, xspec, xspec,
                  pl.BlockSpec((nblk, 1, 2 * HY_W), lambda b, j: (0, 0, 0)),
                  pl.BlockSpec((1, HY_W), lambda b, j: (0, 0))],
        out_specs=xspec,
        compiler_params=_cparams(("arbitrary", "arbitrary")), name="hy_final",
    )(cr, ci, br, bi, x0, u, ss, bias)


def _hy_direct_kernel(f_ref, x0_ref, u_ref, ss_ref, bias_ref, o_ref, kk_ref, *, seq):
    kk_ref[0:seq, :] = f_ref[1].astype(F32)
    kk_ref[seq:2 * seq, :] = f_ref[0].astype(F32)
    u = u_ref[0].astype(F32)
    y = jnp.zeros((seq, HY_W), F32)
    for j in range(seq):
        y = y + kk_ref[seq - j:2 * seq - j, :] * u[j:j + 1, :]
    o_ref[0] = (x0_ref[0].astype(F32) * (y * _filter_scale(ss_ref) + u * bias_ref[...])).astype(BF16)


def _hy_direct(f, x0, u, ss, bias):
    nb, seq, _ = u.shape
    nblk = ss.shape[0]
    xspec = pl.BlockSpec((1, seq, HY_W), lambda b: (b, 0, 0))
    return pl.pallas_call(
        functools.partial(_hy_direct_kernel, seq=seq),
        out_shape=jax.ShapeDtypeStruct((nb, seq, HY_W), BF16),
        grid=(nb,),
        in_specs=[pl.BlockSpec((2, seq, HY_W), lambda b: (0, 0, 0)), xspec, xspec,
                  pl.BlockSpec((nblk, 1, 2 * HY_W), lambda b: (0, 0, 0)),
                  pl.BlockSpec((1, HY_W), lambda b: (0, 0))],
        out_specs=xspec,
        scratch_shapes=[pltpu.VMEM((2 * seq, HY_W), F32)],
        compiler_params=_cparams(("arbitrary",)), name="hy_direct",
    )(f, x0, u, ss, bias)


def _hy_embedding(seq):
    t = jnp.arange(seq, dtype=F32)
    t_unit = t / max(seq - 1, 1)
    bands = jnp.linspace(1e-4, HY_BANDS - 1, HY_BANDS, dtype=F32)
    ang = (2 * jnp.pi / seq) * t[:, None] * bands[None, :]
    return jnp.concatenate([t_unit[:, None], jnp.cos(ang), -jnp.sin(ang)], axis=-1)


def _dft_tables(seq):
    n = 2 * seq
    n2 = DFT_N2
    n1 = n // n2
    a = jnp.arange(n1, dtype=jnp.int32)
    th1 = (2 * jnp.pi / n1) * ((a[:, None] * a[None, :]) % n1).astype(F32)
    w1r, w1i = jnp.cos(th1), -jnp.sin(th1)
    b = jnp.arange(n2, dtype=jnp.int32)
    tha = (2 * jnp.pi / n) * (a[:, None] * b[None, :]).astype(F32)
    thb = (2 * jnp.pi / n2) * ((b[:, None] * b[None, :]) % n2).astype(F32)
    ar, ai = jnp.cos(tha), -jnp.sin(tha)
    br, bi = jnp.cos(thb), -jnp.sin(thb)
    mr = ar[:, None, :] * br[None] - ai[:, None, :] * bi[None]
    mi = ar[:, None, :] * bi[None] + ai[:, None, :] * br[None]
    tr = ar[:, :, None] * br[None] - ai[:, :, None] * bi[None]
    ti = -(ar[:, :, None] * bi[None] + ai[:, :, None] * br[None])
    return dict(
        w1r=w1r.astype(BF16), w1i=w1i.astype(BF16),
        mr=mr.astype(BF16), mi=mi.astype(BF16), tr=tr.astype(BF16), ti=ti.astype(BF16),
        cr=(w1r[:n1 // 2] / n).astype(BF16), ci=(w1i[:n1 // 2] / n).astype(BF16))


def _hyena_long(phy, lw, filt, seq, tabs):
    n = phy.shape[0]
    nb = n // seq
    n2 = DFT_N2
    n1 = 2 * seq // n2
    f, ss = filt
    x0, u = _hy_conv3(phy, lw['hy_conv_w'], lw['hy_conv_b'], seq, min(seq, HY_CONV_TILE))
    tc = min(n2 * HY_W, DFT_COL_TILE)
    far, fai = _dft_outer(tabs['w1r'], tabs['w1i'], f.reshape(1, n1, n2 * HY_W), tc)
    fr, fi = _dft_inner_fwd(tabs['mr'], tabs['mi'], far.reshape(1, n1, n2, HY_W), fai.reshape(1, n1, n2, HY_W))
    ar, ai = _dft_outer(tabs['w1r'][:, :n1 // 2], tabs['w1i'][:, :n1 // 2], u.reshape(nb, n1 // 2, n2 * HY_W), tc)
    br, bi = _dft_inner_conv(tabs['mr'], tabs['mi'], tabs['tr'], tabs['ti'],
                             ar.reshape(nb, n1, n2, HY_W), ai.reshape(nb, n1, n2, HY_W), fr, fi)
    out = _hy_final(tabs['cr'], tabs['ci'], br.reshape(nb, n1, n2 * HY_W), bi.reshape(nb, n1, n2 * HY_W),
                    x0.reshape(nb, n1 // 2, n2 * HY_W), u.reshape(nb, n1 // 2, n2 * HY_W), ss, lw['hy_bias'], tc)
    return out.reshape(n, HY_W)


def _hyena_short(phy, lw, filt, seq):
    n = phy.shape[0]
    nb = n // seq
    f, ss = filt
    x0, u = _hy_conv3(phy, lw['hy_conv_w'], lw['hy_conv_b'], seq, seq)
    out = _hy_direct(f, x0.reshape(nb, seq, HY_W), u.reshape(nb, seq, HY_W), ss, lw['hy_bias'])
    return out.reshape(n, HY_W)


def _pair_top2_sum(a, b, c, d):
    return jnp.maximum(jnp.maximum(jnp.maximum(a + b, a + c), jnp.maximum(a + d, b + c)),
                       jnp.maximum(b + d, c + d))


def _route(logits_t, rb):
    aff = jax.nn.sigmoid(logits_t)
    sel = aff + rb
    rows = [sel[e:e + 1, :] for e in range(N_EXPERTS)]
    affr = [aff[e:e + 1, :] for e in range(N_EXPERTS)]
    best, bidx = None, None
    for g in range(N_GROUPS):
        gs = _pair_top2_sum(*rows[g * EXP_PER_GROUP:(g + 1) * EXP_PER_GROUP])
        if g == 0:
            best, bidx = gs, jnp.zeros(gs.shape, jnp.int32)
        else:
            upd = gs > best
            bidx = jnp.where(upd, g, bidx)
            best = jnp.where(upd, gs, best)
    vals = [jnp.where(bidx == e // EXP_PER_GROUP, rows[e], -jnp.inf) for e in range(N_EXPERTS)]
    m1, i1 = vals[0], jnp.zeros(best.shape, jnp.int32)
    for e in range(1, N_EXPERTS):
        upd = vals[e] > m1
        i1 = jnp.where(upd, e, i1)
        m1 = jnp.where(upd, vals[e], m1)
    m2, i2 = jnp.full(best.shape, -jnp.inf, F32), jnp.zeros(best.shape, jnp.int32)
    for e in range(N_EXPERTS):
        cand = jnp.where(i1 == e, -jnp.inf, vals[e])
        upd = cand > m2
        i2 = jnp.where(upd, e, i2)
        m2 = jnp.where(upd, cand, m2)
    a1 = sum(jnp.where(i1 == e, affr[e], 0.0) for e in range(N_EXPERTS))
    a2 = sum(jnp.where(i2 == e, affr[e], 0.0) for e in range(N_EXPERTS))
    inv = 1.0 / (a1 + a2)
    return jnp.concatenate([i1, i2], axis=0), jnp.concatenate([a1 * inv, a2 * inv], axis=0)


def _merge_kernel(a_ref, b_ref, m_ref, gt_ref, x_ref, gate_ref, shift_ref, scale_ref, g2_ref,
                  wpa_ref, wpb_ref, wpc_ref, wout_ref, rwt_ref, rb_ref, xo_ref, h2_ref, idx_ref, wts_ref):
    y = gt_ref[:, 0:D_MODEL].astype(F32) * jnp.dot(a_ref[...], wpa_ref[...], preferred_element_type=F32)
    y = y + gt_ref[:, D_MODEL:2 * D_MODEL].astype(F32) * jnp.dot(b_ref[...], wpb_ref[...],
                                                                 preferred_element_type=F32)
    y = y + gt_ref[:, 2 * D_MODEL:].astype(F32) * jnp.dot(m_ref[...], wpc_ref[...], preferred_element_type=F32)
    y2 = jnp.dot(y.astype(BF16), wout_ref[...], preferred_element_type=F32)
    xn = x_ref[...] + gate_ref[0] * y2
    xo_ref[...] = xn
    h2 = _rms(xn) * g2_ref[...]
    h2 = h2 * (1.0 + scale_ref[0]) + shift_ref[0]
    h2_ref[...] = h2
    logits_t = lax.dot_general(rwt_ref[...], h2, (((1,), (1,)), ((), ())), preferred_element_type=F32,
                               precision=HIGHEST)
    idx_ref[...], wts_ref[...] = _route(logits_t, rb_ref[...])


def _merge(a, b, m, gt, x, gate, shift, scale, lw, rw_t, rb, seq, tm):
    n = x.shape[0]
    tiles_per_seq = seq // tm
    const = lambda i: (0, 0)
    row = lambda w: pl.BlockSpec((tm, w), lambda i: (i, 0))
    vec = pl.BlockSpec((1, 1, D_MODEL), lambda i: (i // tiles_per_seq, 0, 0))
    return pl.pallas_call(
        _merge_kernel,
        out_shape=(jax.ShapeDtypeStruct((n, D_MODEL), F32), jax.ShapeDtypeStruct((n, D_MODEL), F32),
                   jax.ShapeDtypeStruct((TOP_K, n), jnp.int32), jax.ShapeDtypeStruct((TOP_K, n), F32)),
        grid=(n // tm,),
        in_specs=[row(HEADS * VDIM), row(HY_W), row(GM_W), row(3 * D_MODEL), row(D_MODEL), vec, vec, vec,
                  pl.BlockSpec((1, D_MODEL), const),
                  pl.BlockSpec((HEADS * VDIM, D_MODEL), const), pl.BlockSpec((HY_W, D_MODEL), const),
                  pl.BlockSpec((GM_W, D_MODEL), const), pl.BlockSpec((D_MODEL, D_MODEL), const),
                  pl.BlockSpec((N_EXPERTS, D_MODEL), const), pl.BlockSpec((N_EXPERTS, 1), const)],
        out_specs=(row(D_MODEL), row(D_MODEL), pl.BlockSpec((TOP_K, tm), lambda i: (0, i)),
                   pl.BlockSpec((TOP_K, tm), lambda i: (0, i))),
        compiler_params=_cparams(("arbitrary",)), name="merge",
    )(a, b, m, gt, x, gate, shift, scale, lw['g2'], lw['w_pa'], lw['w_pb'], lw['w_pc'], lw['w_out'], rw_t, rb)


def _rows_copy(src_hbm, dst, sem, idx_ref, rows):
    return [pltpu.make_async_copy(src_hbm.at[pl.ds(idx_ref[0, 0, r], 1)], dst.at[pl.ds(r, 1)], sem)
            for r in range(rows)]


def _rows_wait(src_hbm, dst, sem, rows):
    pltpu.make_async_copy(src_hbm.at[pl.ds(0, rows)], dst, sem).wait()


def _gather_pipeline(src_hbm, cur_ref, nxt_ref, buf, sem, rows):
    i = pl.program_id(0)
    slot = i % 2

    @pl.when(i == 0)
    def _():
        for cp in _rows_copy(src_hbm, buf.at[0], sem.at[0], cur_ref, rows):
            cp.start()

    _rows_wait(src_hbm, buf.at[slot], sem.at[slot], rows)
    for cp in _rows_copy(src_hbm, buf.at[1 - slot], sem.at[1 - slot], nxt_ref, rows):
        cp.start()
    return slot


def _gather_drain(src_hbm, buf, sem, rows):
    i = pl.program_id(0)

    @pl.when(i == pl.num_programs(0) - 1)
    def _():
        _rows_wait(src_hbm, buf.at[1 - i % 2], sem.at[1 - i % 2], rows)


def _moe_dispatch_kernel(pos_ref, h_ref, xs_in_hbm, xs_hbm, sem, *, tc):
    del xs_in_hbm
    for r in range(tc):
        pltpu.make_async_copy(h_ref.at[pl.ds(r, 1)], xs_hbm.at[pl.ds(pos_ref[0, 0, r], 1)], sem).start()
    pltpu.make_async_copy(h_ref, xs_hbm.at[pl.ds(0, tc)], sem).wait()


def _moe_dispatch(pos_tiles, h2, rows, tc):
    n = h2.shape[0]
    return pl.pallas_call(
        functools.partial(_moe_dispatch_kernel, tc=tc),
        out_shape=jax.ShapeDtypeStruct((rows, D_MODEL), F32),
        grid=(n // tc,),
        in_specs=[pl.BlockSpec((1, 1, tc), lambda i: (i, 0, 0), memory_space=pltpu.SMEM),
                  pl.BlockSpec((tc, D_MODEL), lambda i: (i, 0)),
                  pl.BlockSpec(memory_space=pl.ANY)],
        out_specs=pl.BlockSpec(memory_space=pl.ANY),
        scratch_shapes=[pltpu.SemaphoreType.DMA],
        input_output_aliases={2: 0},
        compiler_params=_cparams(("arbitrary",)), name="moe_dispatch",
    )(pos_tiles, h2, jnp.zeros((rows, D_MODEL), F32))


def _moe_ffn_kernel(ea_ref, eb_ref, x_ref, wgua_ref, wda_ref, wgub_ref, wdb_ref, y_ref):
    del ea_ref, eb_ref
    xb = x_ref[...].astype(BF16)
    for j, (wgu_ref, wd_ref) in enumerate(((wgua_ref, wda_ref), (wgub_ref, wdb_ref))):
        gu = jnp.dot(xb, wgu_ref[0], preferred_element_type=F32)
        g = gu[:, :D_EXPERT]
        hid = (g * _sigmoid(g) * gu[:, D_EXPERT:]).astype(BF16)
        y_ref[:, j * D_MODEL:(j + 1) * D_MODEL] = jnp.dot(hid, wd_ref[0], preferred_element_type=F32)


def _moe_ffn(blk_ea, blk_eb, x_sorted, lw):
    nblk = blk_ea.shape[0]
    wgu_spec = lambda sel: pl.BlockSpec((1, D_MODEL, 2 * D_EXPERT), lambda i, ea, eb: (sel(ea, eb)[i], 0, 0))
    wd_spec = lambda sel: pl.BlockSpec((1, D_EXPERT, D_MODEL), lambda i, ea, eb: (sel(ea, eb)[i], 0, 0))
    first, second = (lambda ea, eb: ea), (lambda ea, eb: eb)
    return pl.pallas_call(
        _moe_ffn_kernel,
        out_shape=jax.ShapeDtypeStruct((nblk * MOE_BM, TOP_K * D_MODEL), F32),
        grid_spec=pltpu.PrefetchScalarGridSpec(
            num_scalar_prefetch=2, grid=(nblk,),
            in_specs=[pl.BlockSpec((MOE_BM, D_MODEL), lambda i, ea, eb: (i, 0)),
                      wgu_spec(first), wd_spec(first), wgu_spec(second), wd_spec(second)],
            out_specs=pl.BlockSpec((MOE_BM, TOP_K * D_MODEL), lambda i, ea, eb: (i, 0))),
        compiler_params=_cparams(("arbitrary",)), name="moe_ffn",
    )(blk_ea, blk_eb, x_sorted, lw['w_gu'], lw['w_d'], lw['w_gu'], lw['w_d'])


def _moe_combine_kernel(cur_ref, nxt_ref, y_hbm, x_ref, w_ref, gate_ref, o_ref, buf, sem, *, tc):
    slot = _gather_pipeline(y_hbm, cur_ref, nxt_ref, buf, sem, tc)
    w = w_ref[...]
    y = w[:, 0:1] * buf[slot, :, 0:D_MODEL] + w[:, 1:2] * buf[slot, :, D_MODEL:]
    o_ref[...] = x_ref[...] + gate_ref[0] * y
    _gather_drain(y_hbm, buf, sem, tc)


def _moe_combine(pos_tiles, y_sorted, x, w_cols, gate, seq, tc):
    n = x.shape[0]
    tiles_per_seq = seq // tc
    idx_spec = lambda off: pl.BlockSpec((1, 1, tc), lambda i: (i + off, 0, 0), memory_space=pltpu.SMEM)
    return pl.pallas_call(
        functools.partial(_moe_combine_kernel, tc=tc),
        out_shape=jax.ShapeDtypeStruct((n, D_MODEL), F32),
        grid=(n // tc,),
        in_specs=[idx_spec(0), idx_spec(1), pl.BlockSpec(memory_space=pl.ANY),
                  pl.BlockSpec((tc, D_MODEL), lambda i: (i, 0)),
                  pl.BlockSpec((tc, TOP_K), lambda i: (i, 0)),
                  pl.BlockSpec((1, 1, D_MODEL), lambda i: (i // tiles_per_seq, 0, 0))],
        out_specs=pl.BlockSpec((tc, D_MODEL), lambda i: (i, 0)),
        scratch_shapes=[pltpu.VMEM((2, tc, TOP_K * D_MODEL), F32), pltpu.SemaphoreType.DMA((2,))],
        compiler_params=_cparams(("arbitrary",)), name="moe_combine",
    )(pos_tiles, pos_tiles, y_sorted, x, w_cols, gate)


def _cumsum_rows(onehot):
    n, c = onehot.shape
    blk = math.gcd(n, 256)
    x = onehot.astype(F32).reshape(n // blk, blk, c)
    tri = (jnp.arange(blk)[:, None] >= jnp.arange(blk)[None, :]).astype(F32)
    within = jnp.einsum('ij,bjc->bic', tri, x, precision=HIGHEST)
    nb = n // blk
    before = (jnp.arange(nb)[:, None] > jnp.arange(nb)[None, :]).astype(F32)
    offset = jnp.dot(before, within[:, -1, :], precision=HIGHEST)
    return (within + offset[:, None, :]).reshape(n, c).astype(jnp.int32)


def _pair_tables():
    pairs = [(a, b) for a in range(EXP_PER_GROUP) for b in range(a + 1, EXP_PER_GROUP)]
    ea = [g * EXP_PER_GROUP + a for g in range(N_GROUPS) for a, _ in pairs]
    eb = [g * EXP_PER_GROUP + b for g in range(N_GROUPS) for _, b in pairs]
    return jnp.array(ea, jnp.int32), jnp.array(eb, jnp.int32)


def _moe(h2, idx, wts, x, gate, lw, seq):
    n = x.shape[0]
    tab_a, tab_b = _pair_tables()
    n_cls = tab_a.shape[0]
    nblk = n // MOE_BM + n_cls
    swap = idx[0] > idx[1]
    e_lo, e_hi = jnp.minimum(idx[0], idx[1]), jnp.maximum(idx[0], idx[1])
    w_cols = jnp.stack([jnp.where(swap, wts[1], wts[0]), jnp.where(swap, wts[0], wts[1])], axis=1)
    onehot = ((e_lo[:, None] == tab_a[None, :]) & (e_hi[:, None] == tab_b[None, :])).astype(jnp.int32)
    csum = _cumsum_rows(onehot)
    counts = csum[-1]
    rank = jnp.sum(csum * onehot, axis=1) - 1
    seg_len = (counts + MOE_BM - 1) // MOE_BM * MOE_BM
    seg_end = jnp.cumsum(seg_len)
    pos = jnp.sum(onehot * (seg_end - seg_len)[None, :], axis=1) + rank
    blk_start = jnp.arange(nblk, dtype=jnp.int32) * MOE_BM
    blk_cls = jnp.minimum(jnp.sum((seg_end[None, :] <= blk_start[:, None]).astype(jnp.int32), axis=1), n_cls - 1)
    td = min(n, MOE_DISPATCH_TILE)
    x_sorted = _moe_dispatch(pos.reshape(n // td, 1, td), h2, nblk * MOE_BM, td)
    y_sorted = _moe_ffn(tab_a[blk_cls], tab_b[blk_cls], x_sorted, lw)
    tc = min(seq, MOE_COMBINE_TILE)
    pos_next = jnp.concatenate([pos.reshape(n // tc, 1, tc), jnp.zeros((1, 1, tc), jnp.int32)], axis=0)
    return _moe_combine(pos_next, y_sorted, x, w_cols, gate, seq, tc)


def _head_perm():
    rope_idx = list(range(NOPE, QK, 2)) + list(range(NOPE + 1, QK, 2))
    return jnp.array(list(range(NOPE)) + rope_idx, dtype=jnp.int32)


def _block_diag(a, b):
    return jnp.concatenate([jnp.pad(a, ((0, 0), (0, b.shape[1]))), jnp.pad(b, ((0, 0), (a.shape[1], 0)))], axis=0)


def _layer_weights(p, l):
    perm = _head_perm()
    row = lambda v: v.reshape(1, -1)
    w_in = p['w_in'][l].astype(BF16)
    kr_perm = jnp.array(list(range(0, ROPE, 2)) + list(range(1, ROPE, 2)), dtype=jnp.int32)
    w_kr = jnp.pad(w_in[:, OFF_KR:OFF_HY][:, kr_perm], ((0, 0), (0, HEAD_PAD - ROPE)))
    w_in2 = jnp.concatenate([w_in[:, OFF_Q:OFF_KR], w_kr, w_in[:, OFF_HY:]], axis=1)
    w_uq = p['w_uq'][l].reshape(Q_RANK, HEADS, QK)[:, :, perm]
    w_uq = jnp.pad(w_uq, ((0, 0), (0, 0), (0, HEAD_PAD - QK))).reshape(Q_RANK, HEADS * HEAD_PAD).astype(BF16)
    w_ukv = p['w_ukv'][l].reshape(KV_RANK, HEADS, NOPE + VDIM)
    w_k = jnp.pad(w_ukv[:, :, :NOPE], ((0, 0), (0, 0), (0, HEAD_PAD - NOPE)))
    w_k = w_k.reshape(KV_RANK, HEADS * HEAD_PAD).astype(BF16)
    w_vt = w_ukv[:, :, NOPE:].reshape(KV_RANK, HEADS * VDIM).T.astype(BF16)
    pad_gain = lambda g: jnp.pad(g[perm], (0, HEAD_PAD - QK)).reshape(1, HEAD_PAD)
    gm_b = jnp.repeat(p['gm_bs'][l].T, GM_W // GM_GROUPS, axis=1)
    bound = 1.02 * QSCALE * QK * jnp.max(jnp.abs(p['g_qn'][l])) * jnp.max(jnp.abs(p['g_kn'][l]))
    pad_lane = jnp.arange(HEAD_PAD) == QK
    return dict(
        attn_bound=bound,
        q_pad=jnp.where(pad_lane, -bound, 0.0).reshape(1, HEAD_PAD).astype(F32),
        k_pad=jnp.where(pad_lane, 1.0, 0.0).reshape(1, HEAD_PAD).astype(F32),
        g1=row(p['norm1_g'][l]), g2=row(p['norm2_g'][l]), w_in=w_in2,
        g_qa=row(p['g_qa'][l]), w_uq=w_uq, g_qn=pad_gain(p['g_qn'][l]),
        g_kva=row(p['g_kva'][l]), w_k=w_k, w_vt=w_vt, g_kn=pad_gain(p['g_kn'][l]),
        gm_g=row(p['gm_norm_g'][l]), gm_w=p['gm_ws'][l].astype(BF16), gm_b=gm_b,
        hy_conv_w=p['hy_conv_w'][l], hy_conv_b=row(p['hy_conv_b'][l]),
        hy_w1p=_block_diag(p['hy_w1'][l], p['hy_w1'][l]), hy_b1p=row(jnp.tile(p['hy_b1'][l], 2)),
        hy_w2p=_block_diag(p['hy_w2'][l], p['hy_w2'][l]), hy_b2p=row(jnp.tile(p['hy_b2'][l], 2)),
        hy_w3p=_block_diag(p['hy_w3'][l][:, :HY_W], p['hy_w3'][l][:, HY_W:]), hy_b3=row(p['hy_b3'][l]),
        hy_freqp=row(jnp.tile(p['hy_freq'][l], 2)),
        hy_decay=p['hy_decay'][l].reshape(1, 2 * HY_W), hy_bias=row(p['hy_bias'][l]),
        w_pa=p['w_pa'][l].astype(BF16), w_pb=p['w_pb'][l].astype(BF16), w_pc=p['w_pc'][l].astype(BF16),
        w_out=p['w_out'][l].astype(BF16),
        w_gu=jnp.concatenate([p['moe_w_gate'][l], p['moe_w_up'][l]], axis=-1).astype(BF16),
        w_d=p['moe_w_down'][l].astype(BF16))


def _rope_tables(seq):
    rows = seq // GRID_W
    row = jnp.repeat(jnp.arange(rows, dtype=F32), GRID_W)
    col = jnp.tile(jnp.arange(GRID_W, dtype=F32), rows)
    n_freq = ROPE // 4
    inv = ROPE_THETA ** (-jnp.arange(n_freq, dtype=F32) / n_freq)
    ang = jnp.concatenate([row[:, None] * inv, col[:, None] * inv], axis=-1)
    c, s = jnp.cos(ang), jnp.sin(ang)
    z = lambda w: jnp.zeros((seq, w), F32)
    rc = jnp.concatenate([jnp.ones((seq, NOPE), F32), c, c, z(HEAD_PAD - QK)], axis=1)
    rs1 = jnp.concatenate([z(NOPE), z(ROPE // 2), s, z(HEAD_PAD - QK)], axis=1)
    rs2 = jnp.concatenate([z(NOPE), -s, z(ROPE // 2), z(HEAD_PAD - QK)], axis=1)
    return rc, rs1, rs2


def _mixer_and_ffn(x, mods, lw, rw_t, rb, seq, tm, rope_tabs, ctx_kv, filt, dft_tabs):
    shift1, scale1, gate1, shift2, scale2, gate2 = mods
    q, k, vt, m, phy, gt = _premix(x, shift1, scale1, lw, rope_tabs, seq, tm)
    a = _attention(q, k, vt, ctx_kv, seq, min(seq, ATTN_TQ), lw['attn_bound'])
    if dft_tabs is not None:
        b = _hyena_long(phy, lw, filt, seq, dft_tabs)
    else:
        b = _hyena_short(phy, lw, filt, seq)
    xn, h2, idx, wts = _merge(a, b, m, gt, x, gate1, shift2, scale2, lw, rw_t, rb, seq, min(seq, MERGE_TILE))
    return _moe(h2, idx, wts, xn, gate2, lw, seq), k, vt


def _forward(p):
    x, ctx = p['x'], p['ctx']
    batch, seq, _ = x.shape
    lc = ctx.shape[1]
    depth = p['w_mod'].shape[0]

    cvecs = jnp.concatenate([p['c'], p['c_ctx'][None], jnp.zeros((8 - batch - 1, D_MODEL), F32)], axis=0)
    mod_all = _modvec(cvecs, p['w_mod'], p['b_mod'])
    rw_t = p['router_w'].T
    rb = p['router_b'].reshape(N_EXPERTS, 1)
    rope_tabs = _rope_tables(seq)
    dft_tabs = _dft_tables(seq)
    z_lat, z_ctx = _hy_embedding(seq), _hy_embedding(lc)

    xl = x.reshape(batch * seq, D_MODEL)
    xc = ctx.reshape(batch * lc, D_MODEL)
    tm_lat = min(seq, TOKEN_TILE)
    for l in range(depth):
        lw = _layer_weights(p, l)
        mod = mod_all[l].reshape(8, N_MOD, D_MODEL)
        mods_lat = [mod[:batch, j].reshape(batch, 1, D_MODEL) for j in range(N_MOD)]
        mods_ctx = [jnp.broadcast_to(mod[batch, j].reshape(1, 1, D_MODEL), (batch, 1, D_MODEL))
                    for j in range(N_MOD)]
        if l == depth - 1:
            _, k_c, vt_c, _, _, _ = _premix(xc, mods_ctx[0], mods_ctx[1], lw, None, lc, lc)
        else:
            xc, k_c, vt_c = _mixer_and_ffn(xc, mods_ctx, lw, rw_t, rb, lc, lc, None, None,
                                           _hy_filters(z_ctx, lw, lc), None)
        filt = _hy_filters(z_lat, lw, min(seq, HY_FILTER_TILE))
        xl, _, _ = _mixer_and_ffn(xl, mods_lat, lw, rw_t, rb, seq, tm_lat, rope_tabs, (k_c, vt_c), filt, dft_tabs)
    return xl.reshape(batch, seq, D_MODEL)


def kernel(x, c, ctx, c_ctx, w_mod, b_mod, norm1_g, norm2_g, w_in, g_qa, w_uq, g_kva, w_ukv, g_qn, g_kn,
           hy_conv_w, hy_conv_b, hy_w1, hy_b1, hy_w2, hy_b2, hy_w3, hy_b3, hy_freq, hy_decay, hy_bias,
           gm_norm_g, gm_ws, gm_bs, w_pa, w_pb, w_pc, w_out, router_w, router_b,
           moe_w_gate, moe_w_up, moe_w_down):
    return _forward(dict(
        x=x, c=c, ctx=ctx, c_ctx=c_ctx, w_mod=w_mod, b_mod=b_mod, norm1_g=norm1_g, norm2_g=norm2_g, w_in=w_in,
        g_qa=g_qa, w_uq=w_uq, g_kva=g_kva, w_ukv=w_ukv, g_qn=g_qn, g_kn=g_kn, hy_conv_w=hy_conv_w,
        hy_conv_b=hy_conv_b, hy_w1=hy_w1, hy_b1=hy_b1, hy_w2=hy_w2, hy_b2=hy_b2, hy_w3=hy_w3, hy_b3=hy_b3,
        hy_freq=hy_freq, hy_decay=hy_decay, hy_bias=hy_bias, gm_norm_g=gm_norm_g, gm_ws=gm_ws, gm_bs=gm_bs,
        w_pa=w_pa, w_pb=w_pb, w_pc=w_pc, w_out=w_out, router_w=router_w, router_b=router_b,
        moe_w_gate=moe_w_gate, moe_w_up=moe_w_up, moe_w_down=moe_w_down))
```

```python
import functools
import math

import jax
import jax.numpy as jnp
from jax import lax
from jax.experimental import pallas as pl
from jax.experimental.pallas import tpu as pltpu

F32 = jnp.float32
BF16 = jnp.bfloat16
HIGHEST = lax.Precision.HIGHEST

D_MODEL = 1024
GRID_W = 64
EPS = 1e-6
N_MOD = 6

HEADS = 8
Q_RANK = 384
KV_RANK = 256
NOPE = 64
ROPE = 32
QK = NOPE + ROPE
VDIM = 64
HEAD_PAD = 128
ROPE_THETA = 10000.0
V_ROWS = 80

HY_W = 256
HY_BANDS = 16
GM_W = 256
GM_CHUNK = 128
GM_GROUPS = 4

OFF_Q = 0
OFF_KV = OFF_Q + Q_RANK
OFF_KR = OFF_KV + KV_RANK
OFF_HY = OFF_KR + ROPE
OFF_GM = OFF_HY + 3 * HY_W
OFF_GT = OFF_GM + 2 * GM_W

P_Q = 0
P_KV = P_Q + Q_RANK
P_KR = P_KV + KV_RANK
P_HY = P_KR + HEAD_PAD
P_GM = P_HY + 3 * HY_W
P_GT = P_GM + 2 * GM_W
P_W = P_GT + 3 * D_MODEL

N_EXPERTS = 16
N_GROUPS = 4
EXP_PER_GROUP = 4
TOP_K = 2
D_EXPERT = 512
MOE_BM = 256
MOE_DISPATCH_TILE = 1024
MOE_COMBINE_TILE = 512

DFT_N2 = 256
DFT_K1_STEP = 4
DFT_COL_TILE = 8192

TOKEN_TILE = 512
MERGE_TILE = 1024
HY_CONV_TILE = 1024
HY_FILTER_TILE = 2048

VMEM_LIMIT = 56 * 1024 * 1024
ATTN_TQ = 512
ATTN_TK = 2048
ATTN_BOUND_MAX = 50.0
NEG_BIG = -1e30
LOG2E = 1.4426950408889634
QSCALE = QK ** -0.5 * LOG2E


def _cparams(sem):
    return pltpu.CompilerParams(dimension_semantics=sem, vmem_limit_bytes=VMEM_LIMIT)


def _rms(x):
    return x * lax.rsqrt(jnp.mean(x * x, axis=-1, keepdims=True) + EPS)


def _sigmoid(x):
    return 0.5 * jnp.tanh(0.5 * x) + 0.5


def _nt_dot(a, b):
    return lax.dot_general(a, b, (((1,), (1,)), ((), ())), preferred_element_type=F32)


def _modvec_kernel(c_ref, w_ref, b_ref, o_ref):
    cv = c_ref[...]
    s = cv * jax.nn.sigmoid(cv)
    o_ref[0] = jnp.dot(s, w_ref[0], preferred_element_type=F32, precision=HIGHEST) + b_ref[0]


def _modvec(cvecs, w_mod, b_mod):
    depth = w_mod.shape[0]
    tn = 1536
    return pl.pallas_call(
        _modvec_kernel,
        out_shape=jax.ShapeDtypeStruct((depth, 8, N_MOD * D_MODEL), F32),
        grid=(depth, N_MOD * D_MODEL // tn),
        in_specs=[pl.BlockSpec((8, D_MODEL), lambda l, j: (0, 0)),
                  pl.BlockSpec((1, D_MODEL, tn), lambda l, j: (l, 0, j)),
                  pl.BlockSpec((1, 1, tn), lambda l, j: (l, 0, j))],
        out_specs=pl.BlockSpec((1, 8, tn), lambda l, j: (l, 0, j)),
        compiler_params=_cparams(("arbitrary", "arbitrary")),
        name="modvec",
    )(cvecs, w_mod, b_mod.reshape(depth, 1, N_MOD * D_MODEL))


def _head_norm_rope(xh, gain, rope):
    ms = jnp.sum(xh * xh, axis=-1, keepdims=True) * (1.0 / QK)
    xh = xh * lax.rsqrt(ms + EPS) * gain
    if rope is not None:
        rc, rs1, rs2 = rope
        xh = xh * rc + pltpu.roll(xh, ROPE // 2, 1) * rs1 + pltpu.roll(xh, HEAD_PAD - ROPE // 2, 1) * rs2
    return xh


def _premix_kernel(*refs, use_rope, tm):
    if use_rope:
        (x_ref, shift_ref, scale_ref, g1_ref, win_ref, gqa_ref, wuq_ref, gqn_ref, gkva_ref, wk_ref, wvt_ref,
         gkn_ref, gmg_ref, gmw_ref, gmb_ref, qpad_ref, kpad_ref, rc_ref, rs1_ref, rs2_ref,
         q_ref, k_ref, vt_ref, m_ref, hy_ref, gt_ref) = refs
        rope = (rc_ref[...], rs1_ref[...], rs2_ref[...])
    else:
        (x_ref, shift_ref, scale_ref, g1_ref, win_ref, gqa_ref, wuq_ref, gqn_ref, gkva_ref, wk_ref, wvt_ref,
         gkn_ref, gmg_ref, gmw_ref, gmb_ref, qpad_ref, kpad_ref,
         q_ref, k_ref, vt_ref, m_ref, hy_ref, gt_ref) = refs
        rope = None

    x = x_ref[...]
    h = _rms(x) * g1_ref[...]
    h = h * (1.0 + scale_ref[0]) + shift_ref[0]
    hb = h.astype(BF16)

    def proj(lo, width):
        return jnp.dot(hb, win_ref[:, lo:lo + width], preferred_element_type=F32)

    qa = (_rms(proj(P_Q, Q_RANK)) * gqa_ref[...]).astype(BF16)
    q = jnp.dot(qa, wuq_ref[...], preferred_element_type=F32)
    for hh in range(HEADS):
        qh = _head_norm_rope(q[:, hh * HEAD_PAD:(hh + 1) * HEAD_PAD], gqn_ref[...], rope)
        q_ref[:, hh * HEAD_PAD:(hh + 1) * HEAD_PAD] = (qh * QSCALE + qpad_ref[...]).astype(BF16)

    kva = (_rms(proj(P_KV, KV_RANK)) * gkva_ref[...]).astype(BF16)
    kr = pltpu.roll(proj(P_KR, HEAD_PAD), NOPE, 1)
    kn = jnp.dot(kva, wk_ref[...], preferred_element_type=F32)
    for hh in range(HEADS):
        kh = _head_norm_rope(kn[:, hh * HEAD_PAD:(hh + 1) * HEAD_PAD] + kr, gkn_ref[...], rope)
        k_ref[:, hh * HEAD_PAD:(hh + 1) * HEAD_PAD] = (kh + kpad_ref[...]).astype(BF16)
    vt = _nt_dot(wvt_ref[...], kva)
    row = lax.broadcasted_iota(jnp.int32, (V_ROWS - VDIM, tm), 0)
    ones_rows = jnp.where(row == 0, 1.0, 0.0).astype(BF16)
    for hh in range(HEADS):
        vt_ref[0, 0, hh * V_ROWS:hh * V_ROWS + VDIM, :] = vt[hh * VDIM:(hh + 1) * VDIM].astype(BF16)
        vt_ref[0, 0, hh * V_ROWS + VDIM:(hh + 1) * V_ROWS, :] = ones_rows

    gg = jax.nn.gelu(proj(P_GM, 2 * GM_W), approximate=True)
    gu = gg[:, :GM_W]
    gv = (_rms(gg[:, GM_W:]) * gmg_ref[...]).astype(BF16)
    grp = lax.broadcasted_iota(jnp.int32, (GM_CHUNK, GM_W), 1) // (GM_W // GM_GROUPS)
    for ci in range(tm // GM_CHUNK):
        vc = gv[ci * GM_CHUNK:(ci + 1) * GM_CHUNK]
        s = jnp.zeros((GM_CHUNK, GM_W), F32)
        for g in range(GM_GROUPS):
            sg = jnp.dot(gmw_ref[g], vc, preferred_element_type=F32)
            s = jnp.where(grp == g, sg, s)
        m_ref[ci * GM_CHUNK:(ci + 1) * GM_CHUNK, :] = (
            gu[ci * GM_CHUNK:(ci + 1) * GM_CHUNK] * (s + gmb_ref[...])).astype(BF16)

    hy_ref[...] = proj(P_HY, 3 * HY_W).astype(BF16)
    for j in range(3):
        gt_ref[:, j * D_MODEL:(j + 1) * D_MODEL] = _sigmoid(proj(P_GT + j * D_MODEL, D_MODEL)).astype(BF16)


def _premix(x, shift, scale, lw, rope_tabs, seq, tm):
    n = x.shape[0]
    tiles_per_seq = seq // tm
    batch = n // seq
    use_rope = rope_tabs is not None
    const = lambda i: (0, 0)
    in_specs = [
        pl.BlockSpec((tm, D_MODEL), lambda i: (i, 0)),
        pl.BlockSpec((1, 1, D_MODEL), lambda i: (i // tiles_per_seq, 0, 0)),
        pl.BlockSpec((1, 1, D_MODEL), lambda i: (i // tiles_per_seq, 0, 0)),
        pl.BlockSpec((1, D_MODEL), const),
        pl.BlockSpec((D_MODEL, P_W), const, pipeline_mode=pl.Buffered(1)),
        pl.BlockSpec((1, Q_RANK), const),
        pl.BlockSpec((Q_RANK, HEADS * HEAD_PAD), const),
        pl.BlockSpec((1, HEAD_PAD), const),
        pl.BlockSpec((1, KV_RANK), const),
        pl.BlockSpec((KV_RANK, HEADS * HEAD_PAD), const),
        pl.BlockSpec((HEADS * VDIM, KV_RANK), const),
        pl.BlockSpec((1, HEAD_PAD), const),
        pl.BlockSpec((1, GM_W), const),
        pl.BlockSpec((GM_GROUPS, GM_CHUNK, GM_CHUNK), lambda i: (0, 0, 0)),
        pl.BlockSpec((GM_CHUNK, GM_W), const),
        pl.BlockSpec((1, HEAD_PAD), const),
        pl.BlockSpec((1, HEAD_PAD), const),
    ]
    args = [x, shift, scale, lw['g1'], lw['w_in'], lw['g_qa'], lw['w_uq'], lw['g_qn'], lw['g_kva'], lw['w_k'],
            lw['w_vt'], lw['g_kn'], lw['gm_g'], lw['gm_w'], lw['gm_b'], lw['q_pad'], lw['k_pad']]
    tkv = min(seq, ATTN_TK)
    sub = tkv // tm
    if use_rope:
        in_specs += [pl.BlockSpec((tm, HEAD_PAD), lambda i: (i % tiles_per_seq, 0))] * 3
        args += list(rope_tabs)
    out_shape = (
        jax.ShapeDtypeStruct((n, HEADS * HEAD_PAD), BF16),
        jax.ShapeDtypeStruct((n, HEADS * HEAD_PAD), BF16),
        jax.ShapeDtypeStruct((batch, seq // tkv, HEADS * V_ROWS, tkv), BF16),
        jax.ShapeDtypeStruct((n, GM_W), BF16),
        jax.ShapeDtypeStruct((n, 3 * HY_W), BF16),
        jax.ShapeDtypeStruct((n, 3 * D_MODEL), BF16),
    )
    out_specs = (
        pl.BlockSpec((tm, HEADS * HEAD_PAD), lambda i: (i, 0)),
        pl.BlockSpec((tm, HEADS * HEAD_PAD), lambda i: (i, 0)),
        pl.BlockSpec((1, 1, HEADS * V_ROWS, tm),
                     lambda i: (i // tiles_per_seq, (i % tiles_per_seq) // sub, 0, (i % tiles_per_seq) % sub)),
        pl.BlockSpec((tm, GM_W), lambda i: (i, 0)),
        pl.BlockSpec((tm, 3 * HY_W), lambda i: (i, 0)),
        pl.BlockSpec((tm, 3 * D_MODEL), lambda i: (i, 0)),
    )
    return pl.pallas_call(
        functools.partial(_premix_kernel, use_rope=use_rope, tm=tm),
        out_shape=out_shape, grid=(n // tm,), in_specs=in_specs, out_specs=out_specs,
        compiler_params=_cparams(("arbitrary",)), name="premix",
    )(*args)


def _attn_kernel(*refs, n_chunks, has_ctx, tq):
    if has_ctx:
        q_ref, k_ref, vt_ref, kc_ref, vtc_ref, o_ref = refs
    else:
        q_ref, k_ref, vt_ref, o_ref = refs
    tk = vt_ref.shape[-1]

    def step(hh, kc, vtc, m, acc):
        s = _nt_dot(kc, q_ref[:, hh * HEAD_PAD:(hh + 1) * HEAD_PAD])
        m_new = jnp.maximum(m, jnp.max(s, axis=0, keepdims=True))
        p = jnp.exp2(s - m_new).astype(BF16)
        alpha = jnp.exp2(m - m_new)
        return m_new, acc * alpha + jnp.dot(vtc, p, preferred_element_type=F32)

    def body(i, carry):
        out = []
        for hh in range(2):
            m, acc = carry[hh]
            start = pl.multiple_of(i * tk, tk)
            kc = k_ref[pl.ds(start, tk), hh * HEAD_PAD:(hh + 1) * HEAD_PAD]
            vtc = vt_ref[0, i, hh * V_ROWS:(hh + 1) * V_ROWS, :]
            out.append(step(hh, kc, vtc, m, acc))
        return tuple(out)

    init = tuple((jnp.full((1, tq), NEG_BIG, F32), jnp.zeros((V_ROWS, tq), F32)) for _ in range(2))
    carry = lax.fori_loop(0, n_chunks, body, init)
    outs = []
    for hh in range(2):
        m, acc = carry[hh]
        if has_ctx:
            m, acc = step(hh, kc_ref[:, hh * HEAD_PAD:(hh + 1) * HEAD_PAD],
                          vtc_ref[0, 0, hh * V_ROWS:(hh + 1) * V_ROWS, :], m, acc)
        outs.append(acc[:VDIM] / acc[VDIM:VDIM + 1])
    o_ref[...] = jnp.concatenate(outs, axis=0).T.astype(BF16)


def _attn_bounded_kernel(*refs, n_chunks, has_ctx, tq):
    if has_ctx:
        q_ref, k_ref, vt_ref, kc_ref, vtc_ref, o_ref = refs
    else:
        q_ref, k_ref, vt_ref, o_ref = refs
    tk = vt_ref.shape[-1]

    def step(hh, kc, vtc, acc):
        s = _nt_dot(kc, q_ref[:, hh * HEAD_PAD:(hh + 1) * HEAD_PAD])
        return acc + jnp.dot(vtc, jnp.exp2(s).astype(BF16), preferred_element_type=F32)

    def body(i, carry):
        start = pl.multiple_of(i * tk, tk)
        return tuple(step(hh, k_ref[pl.ds(start, tk), hh * HEAD_PAD:(hh + 1) * HEAD_PAD],
                          vt_ref[0, i, hh * V_ROWS:(hh + 1) * V_ROWS, :], carry[hh]) for hh in range(2))

    carry = lax.fori_loop(0, n_chunks, body, tuple(jnp.zeros((V_ROWS, tq), F32) for _ in range(2)), unroll=True)
    outs = []
    for hh in range(2):
        acc = carry[hh]
        if has_ctx:
            acc = step(hh, kc_ref[:, hh * HEAD_PAD:(hh + 1) * HEAD_PAD],
                       vtc_ref[0, 0, hh * V_ROWS:(hh + 1) * V_ROWS, :], acc)
        outs.append(acc[:VDIM] / acc[VDIM:VDIM + 1])
    o_ref[...] = jnp.concatenate(outs, axis=0).T.astype(BF16)


def _attention(q, k, vt, ctx_kv, seq, tq, bound):
    n = q.shape[0]
    batch = n // seq
    n_chunks, tk = vt.shape[1], vt.shape[3]
    q_tiles = seq // tq
    has_ctx = ctx_kv is not None
    in_specs = [
        pl.BlockSpec((tq, 2 * HEAD_PAD), lambda b, j, i: (b * q_tiles + i, j)),
        pl.BlockSpec((seq, 2 * HEAD_PAD), lambda b, j, i: (b, j)),
        pl.BlockSpec((1, n_chunks, 2 * V_ROWS, tk), lambda b, j, i: (b, 0, j, 0)),
    ]
    args = [q, k, vt]
    if has_ctx:
        kc, vtc = ctx_kv
        lc = vtc.shape[3]
        in_specs += [pl.BlockSpec((lc, 2 * HEAD_PAD), lambda b, j, i: (b, j)),
                     pl.BlockSpec((1, 1, 2 * V_ROWS, lc), lambda b, j, i: (b, 0, j, 0))]
        args += [kc, vtc]
    def call(body, name):
        return pl.pallas_call(
            functools.partial(body, n_chunks=n_chunks, has_ctx=has_ctx, tq=tq),
            out_shape=jax.ShapeDtypeStruct((n, HEADS * VDIM), BF16),
            grid=(batch, HEADS // 2, q_tiles),
            in_specs=in_specs,
            out_specs=pl.BlockSpec((tq, 2 * VDIM), lambda b, j, i: (b * q_tiles + i, j)),
            compiler_params=_cparams(("arbitrary", "arbitrary", "arbitrary")),
            name=name,
        )(*args)

    return lax.cond(bound < ATTN_BOUND_MAX,
                    lambda: call(_attn_bounded_kernel, "attn_bounded"),
                    lambda: call(_attn_kernel, "attn_online"))


def _hy_filter_kernel(zz_ref, w1_ref, b1_ref, w2_ref, b2_ref, w3_ref, b3_ref, fr_ref, dec_ref, f_ref, ss_ref, *, emb):
    fr = fr_ref[...]
    dec = jnp.abs(dec_ref[...])
    zz = zz_ref[...]
    hdn = jnp.sin(fr * (jnp.dot(zz, w1_ref[...], preferred_element_type=F32, precision=HIGHEST) + b1_ref[...]))
    hdn = jnp.sin(fr * (jnp.dot(hdn, w2_ref[...], preferred_element_type=F32, precision=HIGHEST) + b2_ref[...]))
    k = jnp.dot(hdn, w3_ref[...], preferred_element_type=F32, precision=HIGHEST) + b3_ref[...]
    col = lax.broadcasted_iota(jnp.int32, k.shape, 1)
    k = k * jnp.exp(-jnp.where(col < HY_W, zz[:, 0:1], zz[:, emb:emb + 1]) * dec)
    first_block = pl.program_id(0) == 0
    row = lax.broadcasted_iota(jnp.int32, k.shape, 0)
    k = jnp.where(first_block & (row == 0) & (col >= HY_W), 0.0, k)
    f_ref[0] = k[:, :HY_W].astype(BF16)
    f_ref[1] = k[:, HY_W:].astype(BF16)
    hid = hdn.shape[1] // 2
    k_0 = jnp.dot(pltpu.roll(hdn[0:8], hid, 1), w3_ref[...], preferred_element_type=F32,
                  precision=HIGHEST) + b3_ref[...]
    k_0 = k_0[0:1] * jnp.exp(-zz[0:1, 0:1] * dec)
    extra = jnp.where(first_block & (col[0:1] >= HY_W), k_0 * k_0, 0.0)
    ss_ref[0] = jnp.sum(k * k, axis=0, keepdims=True) + extra


def _hy_filters(z, lw, tr):
    seq, emb = z.shape
    nblk = seq // tr
    hid2 = lw['hy_w2p'].shape[0]
    const = lambda i: (0, 0)
    zz = jnp.concatenate([z, jnp.concatenate([z[0:1], z[:0:-1]], axis=0)], axis=1)
    return pl.pallas_call(
        functools.partial(_hy_filter_kernel, emb=emb),
        out_shape=(jax.ShapeDtypeStruct((2, seq, HY_W), BF16), jax.ShapeDtypeStruct((nblk, 1, 2 * HY_W), F32)),
        grid=(nblk,),
        in_specs=[pl.BlockSpec((tr, 2 * emb), lambda i: (i, 0)),
                  pl.BlockSpec((2 * emb, hid2), const), pl.BlockSpec((1, hid2), const),
                  pl.BlockSpec((hid2, hid2), const), pl.BlockSpec((1, hid2), const),
                  pl.BlockSpec((hid2, 2 * HY_W), const), pl.BlockSpec((1, 2 * HY_W), const),
                  pl.BlockSpec((1, hid2), const), pl.BlockSpec((1, 2 * HY_W), const)],
        out_specs=(pl.BlockSpec((2, tr, HY_W), lambda i: (0, i, 0)),
                   pl.BlockSpec((1, 1, 2 * HY_W), lambda i: (i, 0, 0))),
        compiler_params=_cparams(("arbitrary",)), name="hy_filter",
    )(zz, lw['hy_w1p'], lw['hy_b1p'], lw['hy_w2p'], lw['hy_b2p'], lw['hy_w3p'], lw['hy_b3'], lw['hy_freqp'],
      lw['hy_decay'])


def _hy_conv3_kernel(p_ref, prev_ref, next_ref, w_ref, b_ref, x0_ref, u_ref, *, tiles_per_seq, tr):
    i = pl.program_id(0)
    p = p_ref[...].astype(F32)
    first = (i % tiles_per_seq) == 0
    last = (i % tiles_per_seq) == tiles_per_seq - 1
    prev_row = jnp.where(first, 0.0, prev_ref[...].astype(F32)[15:16, :])
    next_row = jnp.where(last, 0.0, next_ref[...].astype(F32)[0:1, :])
    row = lax.broadcasted_iota(jnp.int32, p.shape, 0)
    p_prev = jnp.where(row == 0, prev_row, pltpu.roll(p, 1, 0))
    p_next = jnp.where(row == tr - 1, next_row, pltpu.roll(p, tr - 1, 0))
    z = b_ref[...] + p_prev * w_ref[0:1, :] + p * w_ref[1:2, :] + p_next * w_ref[2:3, :]
    x0_ref[...] = z[:, :HY_W].astype(BF16)
    u_ref[...] = (z[:, 2 * HY_W:] * z[:, HY_W:2 * HY_W]).astype(BF16)


def _hy_conv3(phy, conv_w, conv_b, seq, tr):
    n = phy.shape[0]
    tiles_per_seq = seq // tr
    hb = tr // 16
    nhb = n // 16
    return pl.pallas_call(
        functools.partial(_hy_conv3_kernel, tiles_per_seq=tiles_per_seq, tr=tr),
        out_shape=(jax.ShapeDtypeStruct((n, HY_W), BF16), jax.ShapeDtypeStruct((n, HY_W), BF16)),
        grid=(n // tr,),
        in_specs=[pl.BlockSpec((tr, 3 * HY_W), lambda i: (i, 0)),
                  pl.BlockSpec((16, 3 * HY_W), lambda i: (jnp.maximum(i * hb - 1, 0), 0)),
                  pl.BlockSpec((16, 3 * HY_W), lambda i: (jnp.minimum((i + 1) * hb, nhb - 1), 0)),
                  pl.BlockSpec((3, 3 * HY_W), lambda i: (0, 0)),
                  pl.BlockSpec((1, 3 * HY_W), lambda i: (0, 0))],
        out_specs=(pl.BlockSpec((tr, HY_W), lambda i: (i, 0)), pl.BlockSpec((tr, HY_W), lambda i: (i, 0))),
        compiler_params=_cparams(("arbitrary",)), name="hy_conv3",
    )(phy, phy, phy, conv_w, conv_b)


def _dft_outer_kernel(wr_ref, wi_ref, u_ref, ar_ref, ai_ref):
    u = u_ref[0]
    ar_ref[0] = jnp.dot(wr_ref[...], u, preferred_element_type=F32).astype(BF16)
    ai_ref[0] = jnp.dot(wi_ref[...], u, preferred_element_type=F32).astype(BF16)


def _dft_outer(wr, wi, u, tc):
    nb, kk, cols = u.shape
    n1 = wr.shape[0]
    return pl.pallas_call(
        _dft_outer_kernel,
        out_shape=(jax.ShapeDtypeStruct((nb, n1, cols), BF16),) * 2,
        grid=(nb, cols // tc),
        in_specs=[pl.BlockSpec((n1, kk), lambda b, j: (0, 0)), pl.BlockSpec((n1, kk), lambda b, j: (0, 0)),
                  pl.BlockSpec((1, kk, tc), lambda b, j: (b, 0, j))],
        out_specs=(pl.BlockSpec((1, n1, tc), lambda b, j: (b, 0, j)),) * 2,
        compiler_params=_cparams(("arbitrary", "arbitrary")), name="dft_outer",
    )(wr, wi, u)


def _cdot(mr, mi, xr, xi):
    rr = jnp.dot(mr, xr, preferred_element_type=F32) - jnp.dot(mi, xi, preferred_element_type=F32)
    ri = jnp.dot(mr, xi, preferred_element_type=F32) + jnp.dot(mi, xr, preferred_element_type=F32)
    return rr, ri


def _dft_inner_fwd_kernel(mr_ref, mi_ref, ar_ref, ai_ref, fr_ref, fi_ref):
    for j in range(DFT_K1_STEP):
        fr, fi = _cdot(mr_ref[j], mi_ref[j], ar_ref[0, j], ai_ref[0, j])
        fr_ref[j] = fr.astype(BF16)
        fi_ref[j] = fi.astype(BF16)


def _dft_inner_fwd(mr, mi, ar, ai):
    n1, n2, _ = mr.shape
    ch = ar.shape[-1]
    mspec = pl.BlockSpec((DFT_K1_STEP, n2, n2), lambda k: (k, 0, 0))
    aspec = pl.BlockSpec((1, DFT_K1_STEP, n2, ch), lambda k: (0, k, 0, 0))
    ospec = pl.BlockSpec((DFT_K1_STEP, n2, ch), lambda k: (k, 0, 0))
    return pl.pallas_call(
        _dft_inner_fwd_kernel,
        out_shape=(jax.ShapeDtypeStruct((n1, n2, ch), BF16),) * 2,
        grid=(n1 // DFT_K1_STEP,), in_specs=[mspec, mspec, aspec, aspec], out_specs=(ospec, ospec),
        compiler_params=_cparams(("arbitrary",)), name="dft_inner_fwd",
    )(mr, mi, ar, ai)


def _dft_inner_conv_kernel(mr_ref, mi_ref, tr_ref, ti_ref, ar_ref, ai_ref, fr_ref, fi_ref, br_ref, bi_ref, *, nb):
    for j in range(DFT_K1_STEP):
        fr = fr_ref[j].astype(F32)
        fi = fi_ref[j].astype(F32)
        for b in range(nb):
            ur, ui = _cdot(mr_ref[j], mi_ref[j], ar_ref[b, j], ai_ref[b, j])
            yr = (ur * fr - ui * fi).astype(BF16)
            yi = (ur * fi + ui * fr).astype(BF16)
            br, bi = _cdot(tr_ref[j], ti_ref[j], yr, yi)
            br_ref[b, j] = br.astype(BF16)
            bi_ref[b, j] = bi.astype(BF16)


def _dft_inner_conv(mr, mi, tr, ti, ar, ai, fr, fi):
    n1, n2, _ = mr.shape
    nb, _, _, ch = ar.shape
    mspec = pl.BlockSpec((DFT_K1_STEP, n2, n2), lambda k: (k, 0, 0))
    aspec = pl.BlockSpec((nb, DFT_K1_STEP, n2, ch), lambda k: (0, k, 0, 0))
    fspec = pl.BlockSpec((DFT_K1_STEP, n2, ch), lambda k: (k, 0, 0))
    return pl.pallas_call(
        functools.partial(_dft_inner_conv_kernel, nb=nb),
        out_shape=(jax.ShapeDtypeStruct((nb, n1, n2, ch), BF16),) * 2,
        grid=(n1 // DFT_K1_STEP,), in_specs=[mspec, mspec, mspec, mspec, aspec, aspec, fspec, fspec],
        out_specs=(aspec, aspec),
        compiler_params=_cparams(("arbitrary",)), name="dft_inner_conv",
    )(mr, mi, tr, ti, ar, ai, fr, fi)


def _filter_scale(ss_ref):
    ss = jnp.sum(ss_ref[...], axis=0)
    return lax.rsqrt(ss[:, :HY_W] + ss[:, HY_W:] + EPS)


def _hy_final_kernel(cr_ref, ci_ref, br_ref, bi_ref, x0_ref, u_ref, ss_ref, bias_ref, o_ref, *, reps):
    y = (jnp.dot(cr_ref[...], br_ref[0], preferred_element_type=F32)
         + jnp.dot(ci_ref[...], bi_ref[0], preferred_element_type=F32))
    scale = jnp.tile(_filter_scale(ss_ref), (1, reps))
    bias = jnp.tile(bias_ref[...], (1, reps))
    u = u_ref[0].astype(F32)
    o_ref[0] = (x0_ref[0].astype(F32) * (y * scale + u * bias)).astype(BF16)


def _hy_final(cr, ci, br, bi, x0, u, ss, bias, tc):
    nb, n1, cols = br.shape
    n1h = cr.shape[0]
    nblk = ss.shape[0]
    cspec = pl.BlockSpec((n1h, n1), lambda b, j: (0, 0))
    bspec = pl.BlockSpec((1, n1, tc), lambda b, j: (b, 0, j))
    xspec = pl.BlockSpec((1, n1h, tc), lambda b, j: (b, 0, j))
    return pl.pallas_call(
        functools.partial(_hy_final_kernel, reps=tc // HY_W),
        out_shape=jax.ShapeDtypeStruct((nb, n1h, cols), BF16),
        grid=(nb, cols // tc),
        in_specs=[cspec, cspec, bspec, bspec, xspec, xspec,
                  pl.BlockSpec((nblk, 1, 2 * HY_W), lambda b, j: (0, 0, 0)),
                  pl.BlockSpec((1, HY_W), lambda b, j: (0, 0))],
        out_specs=xspec,
        compiler_params=_cparams(("arbitrary", "arbitrary")), name="hy_final",
    )(cr, ci, br, bi, x0, u, ss, bias)


def _hy_direct_kernel(f_ref, x0_ref, u_ref, ss_ref, bias_ref, o_ref, kk_ref, *, seq):
    kk_ref[0:seq, :] = f_ref[1].astype(F32)
    kk_ref[seq:2 * seq, :] = f_ref[0].astype(F32)
    u = u_ref[0].astype(F32)
    y = jnp.zeros((seq, HY_W), F32)
    for j in range(seq):
        y = y + kk_ref[seq - j:2 * seq - j, :] * u[j:j + 1, :]
    o_ref[0] = (x0_ref[0].astype(F32) * (y * _filter_scale(ss_ref) + u * bias_ref[...])).astype(BF16)


def _hy_direct(f, x0, u, ss, bias):
    nb, seq, _ = u.shape
    nblk = ss.shape[0]
    xspec = pl.BlockSpec((1, seq, HY_W), lambda b: (b, 0, 0))
    return pl.pallas_call(
        functools.partial(_hy_direct_kernel, seq=seq),
        out_shape=jax.ShapeDtypeStruct((nb, seq, HY_W), BF16),
        grid=(nb,),
        in_specs=[pl.BlockSpec((2, seq, HY_W), lambda b: (0, 0, 0)), xspec, xspec,
                  pl.BlockSpec((nblk, 1, 2 * HY_W), lambda b: (0, 0, 0)),
                  pl.BlockSpec((1, HY_W), lambda b: (0, 0))],
        out_specs=xspec,
        scratch_shapes=[pltpu.VMEM((2 * seq, HY_W), F32)],
        compiler_params=_cparams(("arbitrary",)), name="hy_direct",
    )(f, x0, u, ss, bias)


def _hy_embedding(seq):
    t = jnp.arange(seq, dtype=F32)
    t_unit = t / max(seq - 1, 1)
    bands = jnp.linspace(1e-4, HY_BANDS - 1, HY_BANDS, dtype=F32)
    ang = (2 * jnp.pi / seq) * t[:, None] * bands[None, :]
    return jnp.concatenate([t_unit[:, None], jnp.cos(ang), -jnp.sin(ang)], axis=-1)


def _dft_tables(seq):
    n = 2 * seq
    n2 = DFT_N2
    n1 = n // n2
    a = jnp.arange(n1, dtype=jnp.int32)
    th1 = (2 * jnp.pi / n1) * ((a[:, None] * a[None, :]) % n1).astype(F32)
    w1r, w1i = jnp.cos(th1), -jnp.sin(th1)
    b = jnp.arange(n2, dtype=jnp.int32)
    tha = (2 * jnp.pi / n) * (a[:, None] * b[None, :]).astype(F32)
    thb = (2 * jnp.pi / n2) * ((b[:, None] * b[None, :]) % n2).astype(F32)
    ar, ai = jnp.cos(tha), -jnp.sin(tha)
    br, bi = jnp.cos(thb), -jnp.sin(thb)
    mr = ar[:, None, :] * br[None] - ai[:, None, :] * bi[None]
    mi = ar[:, None, :] * bi[None] + ai[:, None, :] * br[None]
    tr = ar[:, :, None] * br[None] - ai[:, :, None] * bi[None]
    ti = -(ar[:, :, None] * bi[None] + ai[:, :, None] * br[None])
    return dict(
        w1r=w1r.astype(BF16), w1i=w1i.astype(BF16),
        mr=mr.astype(BF16), mi=mi.astype(BF16), tr=tr.astype(BF16), ti=ti.astype(BF16),
        cr=(w1r[:n1 // 2] / n).astype(BF16), ci=(w1i[:n1 // 2] / n).astype(BF16))


def _hyena_long(phy, lw, filt, seq, tabs):
    n = phy.shape[0]
    nb = n // seq
    n2 = DFT_N2
    n1 = 2 * seq // n2
    f, ss = filt
    x0, u = _hy_conv3(phy, lw['hy_conv_w'], lw['hy_conv_b'], seq, min(seq, HY_CONV_TILE))
    tc = min(n2 * HY_W, DFT_COL_TILE)
    far, fai = _dft_outer(tabs['w1r'], tabs['w1i'], f.reshape(1, n1, n2 * HY_W), tc)
    fr, fi = _dft_inner_fwd(tabs['mr'], tabs['mi'], far.reshape(1, n1, n2, HY_W), fai.reshape(1, n1, n2, HY_W))
    ar, ai = _dft_outer(tabs['w1r'][:, :n1 // 2], tabs['w1i'][:, :n1 // 2], u.reshape(nb, n1 // 2, n2 * HY_W), tc)
    br, bi = _dft_inner_conv(tabs['mr'], tabs['mi'], tabs['tr'], tabs['ti'],
                             ar.reshape(nb, n1, n2, HY_W), ai.reshape(nb, n1, n2, HY_W), fr, fi)
    out = _hy_final(tabs['cr'], tabs['ci'], br.reshape(nb, n1, n2 * HY_W), bi.reshape(nb, n1, n2 * HY_W),
                    x0.reshape(nb, n1 // 2, n2 * HY_W), u.reshape(nb, n1 // 2, n2 * HY_W), ss, lw['hy_bias'], tc)
    return out.reshape(n, HY_W)


def _hyena_short(phy, lw, filt, seq):
    n = phy.shape[0]
    nb = n // seq
    f, ss = filt
    x0, u = _hy_conv3(phy, lw['hy_conv_w'], lw['hy_conv_b'], seq, seq)
    out = _hy_direct(f, x0.reshape(nb, seq, HY_W), u.reshape(nb, seq, HY_W), ss, lw['hy_bias'])
    return out.reshape(n, HY_W)


def _pair_top2_sum(a, b, c, d):
    return jnp.maximum(jnp.maximum(jnp.maximum(a + b, a + c), jnp.maximum(a + d, b + c)),
                       jnp.maximum(b + d, c + d))


def _route(logits_t, rb):
    aff = jax.nn.sigmoid(logits_t)
    sel = aff + rb
    rows = [sel[e:e + 1, :] for e in range(N_EXPERTS)]
    affr = [aff[e:e + 1, :] for e in range(N_EXPERTS)]
    best, bidx = None, None
    for g in range(N_GROUPS):
        gs = _pair_top2_sum(*rows[g * EXP_PER_GROUP:(g + 1) * EXP_PER_GROUP])
        if g == 0:
            best, bidx = gs, jnp.zeros(gs.shape, jnp.int32)
        else:
            upd = gs > best
            bidx = jnp.where(upd, g, bidx)
            best = jnp.where(upd, gs, best)
    vals = [jnp.where(bidx == e // EXP_PER_GROUP, rows[e], -jnp.inf) for e in range(N_EXPERTS)]
    m1, i1 = vals[0], jnp.zeros(best.shape, jnp.int32)
    for e in range(1, N_EXPERTS):
        upd = vals[e] > m1
        i1 = jnp.where(upd, e, i1)
        m1 = jnp.where(upd, vals[e], m1)
    m2, i2 = jnp.full(best.shape, -jnp.inf, F32), jnp.zeros(best.shape, jnp.int32)
    for e in range(N_EXPERTS):
        cand = jnp.where(i1 == e, -jnp.inf, vals[e])
        upd = cand > m2
        i2 = jnp.where(upd, e, i2)
        m2 = jnp.where(upd, cand, m2)
    a1 = sum(jnp.where(i1 == e, affr[e], 0.0) for e in range(N_EXPERTS))
    a2 = sum(jnp.where(i2 == e, affr[e], 0.0) for e in range(N_EXPERTS))
    inv = 1.0 / (a1 + a2)
    return jnp.concatenate([i1, i2], axis=0), jnp.concatenate([a1 * inv, a2 * inv], axis=0)


def _merge_kernel(a_ref, b_ref, m_ref, gt_ref, x_ref, gate_ref, shift_ref, scale_ref, g2_ref,
                  wpa_ref, wpb_ref, wpc_ref, wout_ref, rwt_ref, rb_ref, xo_ref, h2_ref, idx_ref, wts_ref):
    y = gt_ref[:, 0:D_MODEL].astype(F32) * jnp.dot(a_ref[...], wpa_ref[...], preferred_element_type=F32)
    y = y + gt_ref[:, D_MODEL:2 * D_MODEL].astype(F32) * jnp.dot(b_ref[...], wpb_ref[...],
                                                                 preferred_element_type=F32)
    y = y + gt_ref[:, 2 * D_MODEL:].astype(F32) * jnp.dot(m_ref[...], wpc_ref[...], preferred_element_type=F32)
    y2 = jnp.dot(y.astype(BF16), wout_ref[...], preferred_element_type=F32)
    xn = x_ref[...] + gate_ref[0] * y2
    xo_ref[...] = xn
    h2 = _rms(xn) * g2_ref[...]
    h2 = h2 * (1.0 + scale_ref[0]) + shift_ref[0]
    h2_ref[...] = h2
    logits_t = lax.dot_general(rwt_ref[...], h2, (((1,), (1,)), ((), ())), preferred_element_type=F32,
                               precision=HIGHEST)
    idx_ref[...], wts_ref[...] = _route(logits_t, rb_ref[...])


def _merge(a, b, m, gt, x, gate, shift, scale, lw, rw_t, rb, seq, tm):
    n = x.shape[0]
    tiles_per_seq = seq // tm
    const = lambda i: (0, 0)
    row = lambda w: pl.BlockSpec((tm, w), lambda i: (i, 0))
    vec = pl.BlockSpec((1, 1, D_MODEL), lambda i: (i // tiles_per_seq, 0, 0))
    return pl.pallas_call(
        _merge_kernel,
        out_shape=(jax.ShapeDtypeStruct((n, D_MODEL), F32), jax.ShapeDtypeStruct((n, D_MODEL), F32),
                   jax.ShapeDtypeStruct((TOP_K, n), jnp.int32), jax.ShapeDtypeStruct((TOP_K, n), F32)),
        grid=(n // tm,),
        in_specs=[row(HEADS * VDIM), row(HY_W), row(GM_W), row(3 * D_MODEL), row(D_MODEL), vec, vec, vec,
                  pl.BlockSpec((1, D_MODEL), const),
                  pl.BlockSpec((HEADS * VDIM, D_MODEL), const), pl.BlockSpec((HY_W, D_MODEL), const),
                  pl.BlockSpec((GM_W, D_MODEL), const), pl.BlockSpec((D_MODEL, D_MODEL), const),
                  pl.BlockSpec((N_EXPERTS, D_MODEL), const), pl.BlockSpec((N_EXPERTS, 1), const)],
        out_specs=(row(D_MODEL), row(D_MODEL), pl.BlockSpec((TOP_K, tm), lambda i: (0, i)),
                   pl.BlockSpec((TOP_K, tm), lambda i: (0, i))),
        compiler_params=_cparams(("arbitrary",)), name="merge",
    )(a, b, m, gt, x, gate, shift, scale, lw['g2'], lw['w_pa'], lw['w_pb'], lw['w_pc'], lw['w_out'], rw_t, rb)


def _rows_copy(src_hbm, dst, sem, idx_ref, rows):
    return [pltpu.make_async_copy(src_hbm.at[pl.ds(idx_ref[0, 0, r], 1)], dst.at[pl.ds(r, 1)], sem)
            for r in range(rows)]


def _rows_wait(src_hbm, dst, sem, rows):
    pltpu.make_async_copy(src_hbm.at[pl.ds(0, rows)], dst, sem).wait()


def _gather_pipeline(src_hbm, cur_ref, nxt_ref, buf, sem, rows):
    i = pl.program_id(0)
    slot = i % 2

    @pl.when(i == 0)
    def _():
        for cp in _rows_copy(src_hbm, buf.at[0], sem.at[0], cur_ref, rows):
            cp.start()

    _rows_wait(src_hbm, buf.at[slot], sem.at[slot], rows)
    for cp in _rows_copy(src_hbm, buf.at[1 - slot], sem.at[1 - slot], nxt_ref, rows):
        cp.start()
    return slot


def _gather_drain(src_hbm, buf, sem, rows):
    i = pl.program_id(0)

    @pl.when(i == pl.num_programs(0) - 1)
    def _():
        _rows_wait(src_hbm, buf.at[1 - i % 2], sem.at[1 - i % 2], rows)


def _moe_dispatch_kernel(pos_ref, pad_ref, h_ref, xs_hbm, zero_ref, sem, *, tc, n_pad):
    for r in range(tc):
        pltpu.make_async_copy(h_ref.at[pl.ds(r, 1)], xs_hbm.at[pl.ds(pos_ref[0, 0, r], 1)], sem.at[0]).start()
    zero_ref[...] = jnp.zeros_like(zero_ref)

    def zero_row(r):
        return pltpu.make_async_copy(zero_ref.at[pl.ds(0, 1)], xs_hbm.at[pl.ds(pad_ref[0, 0, r], 1)], sem.at[1])

    def start(r, carry):
        zero_row(r).start()
        return carry

    def wait(r, carry):
        zero_row(r).wait()
        return carry

    lax.fori_loop(0, n_pad, start, 0)
    pltpu.make_async_copy(h_ref, xs_hbm.at[pl.ds(0, tc)], sem.at[0]).wait()
    lax.fori_loop(0, n_pad, wait, 0)


def _moe_dispatch(pos_tiles, pad_tiles, h2, rows, tc):
    n = h2.shape[0]
    n_pad = pad_tiles.shape[-1]
    return pl.pallas_call(
        functools.partial(_moe_dispatch_kernel, tc=tc, n_pad=n_pad),
        out_shape=jax.ShapeDtypeStruct((rows, D_MODEL), F32),
        grid=(n // tc,),
        in_specs=[pl.BlockSpec((1, 1, tc), lambda i: (i, 0, 0), memory_space=pltpu.SMEM),
                  pl.BlockSpec((1, 1, n_pad), lambda i: (i, 0, 0), memory_space=pltpu.SMEM),
                  pl.BlockSpec((tc, D_MODEL), lambda i: (i, 0))],
        out_specs=pl.BlockSpec(memory_space=pl.ANY),
        scratch_shapes=[pltpu.VMEM((8, D_MODEL), F32), pltpu.SemaphoreType.DMA((2,))],
        compiler_params=_cparams(("arbitrary",)), name="moe_dispatch",
    )(pos_tiles, pad_tiles, h2)


def _moe_ffn_kernel(ea_ref, eb_ref, x_ref, wgua_ref, wda_ref, wgub_ref, wdb_ref, y_ref):
    del ea_ref, eb_ref
    xb = x_ref[...].astype(BF16)
    for j, (wgu_ref, wd_ref) in enumerate(((wgua_ref, wda_ref), (wgub_ref, wdb_ref))):
        gu = jnp.dot(xb, wgu_ref[0], preferred_element_type=F32)
        g = gu[:, :D_EXPERT]
        hid = (g * _sigmoid(g) * gu[:, D_EXPERT:]).astype(BF16)
        y_ref[:, j * D_MODEL:(j + 1) * D_MODEL] = jnp.dot(hid, wd_ref[0], preferred_element_type=F32)


def _moe_ffn(blk_ea, blk_eb, x_sorted, lw):
    nblk = blk_ea.shape[0]
    wgu_spec = lambda sel: pl.BlockSpec((1, D_MODEL, 2 * D_EXPERT), lambda i, ea, eb: (sel(ea, eb)[i], 0, 0))
    wd_spec = lambda sel: pl.BlockSpec((1, D_EXPERT, D_MODEL), lambda i, ea, eb: (sel(ea, eb)[i], 0, 0))
    first, second = (lambda ea, eb: ea), (lambda ea, eb: eb)
    return pl.pallas_call(
        _moe_ffn_kernel,
        out_shape=jax.ShapeDtypeStruct((nblk * MOE_BM, TOP_K * D_MODEL), F32),
        grid_spec=pltpu.PrefetchScalarGridSpec(
            num_scalar_prefetch=2, grid=(nblk,),
            in_specs=[pl.BlockSpec((MOE_BM, D_MODEL), lambda i, ea, eb: (i, 0)),
                      wgu_spec(first), wd_spec(first), wgu_spec(second), wd_spec(second)],
            out_specs=pl.BlockSpec((MOE_BM, TOP_K * D_MODEL), lambda i, ea, eb: (i, 0))),
        compiler_params=_cparams(("arbitrary",)), name="moe_ffn",
    )(blk_ea, blk_eb, x_sorted, lw['w_gu'], lw['w_d'], lw['w_gu'], lw['w_d'])


def _moe_combine_kernel(cur_ref, nxt_ref, y_hbm, x_ref, w_ref, gate_ref, o_ref, buf, sem, *, tc):
    slot = _gather_pipeline(y_hbm, cur_ref, nxt_ref, buf, sem, tc)
    w = w_ref[...]
    y = w[:, 0:1] * buf[slot, :, 0:D_MODEL] + w[:, 1:2] * buf[slot, :, D_MODEL:]
    o_ref[...] = x_ref[...] + gate_ref[0] * y
    _gather_drain(y_hbm, buf, sem, tc)


def _moe_combine(pos_tiles, y_sorted, x, w_cols, gate, seq, tc):
    n = x.shape[0]
    tiles_per_seq = seq // tc
    idx_spec = lambda off: pl.BlockSpec((1, 1, tc), lambda i: (i + off, 0, 0), memory_space=pltpu.SMEM)
    return pl.pallas_call(
        functools.partial(_moe_combine_kernel, tc=tc),
        out_shape=jax.ShapeDtypeStruct((n, D_MODEL), F32),
        grid=(n // tc,),
        in_specs=[idx_spec(0), idx_spec(1), pl.BlockSpec(memory_space=pl.ANY),
                  pl.BlockSpec((tc, D_MODEL), lambda i: (i, 0)),
                  pl.BlockSpec((tc, TOP_K), lambda i: (i, 0)),
                  pl.BlockSpec((1, 1, D_MODEL), lambda i: (i // tiles_per_seq, 0, 0))],
        out_specs=pl.BlockSpec((tc, D_MODEL), lambda i: (i, 0)),
        scratch_shapes=[pltpu.VMEM((2, tc, TOP_K * D_MODEL), F32), pltpu.SemaphoreType.DMA((2,))],
        compiler_params=_cparams(("arbitrary",)), name="moe_combine",
    )(pos_tiles, pos_tiles, y_sorted, x, w_cols, gate)


def _cumsum_rows(onehot):
    n, c = onehot.shape
    blk = math.gcd(n, 256)
    x = onehot.astype(F32).reshape(n // blk, blk, c)
    tri = (jnp.arange(blk)[:, None] >= jnp.arange(blk)[None, :]).astype(F32)
    within = jnp.einsum('ij,bjc->bic', tri, x, precision=HIGHEST)
    nb = n // blk
    before = (jnp.arange(nb)[:, None] > jnp.arange(nb)[None, :]).astype(F32)
    offset = jnp.dot(before, within[:, -1, :], precision=HIGHEST)
    return (within + offset[:, None, :]).reshape(n, c).astype(jnp.int32)


def _pair_tables():
    pairs = [(a, b) for a in range(EXP_PER_GROUP) for b in range(a + 1, EXP_PER_GROUP)]
    ea = [g * EXP_PER_GROUP + a for g in range(N_GROUPS) for a, _ in pairs]
    eb = [g * EXP_PER_GROUP + b for g in range(N_GROUPS) for _, b in pairs]
    return jnp.array(ea, jnp.int32), jnp.array(eb, jnp.int32)


def _moe(h2, idx, wts, x, gate, lw, seq):
    n = x.shape[0]
    tab_a, tab_b = _pair_tables()
    n_cls = tab_a.shape[0]
    nblk = n // MOE_BM + n_cls
    swap = idx[0] > idx[1]
    e_lo, e_hi = jnp.minimum(idx[0], idx[1]), jnp.maximum(idx[0], idx[1])
    w_cols = jnp.stack([jnp.where(swap, wts[1], wts[0]), jnp.where(swap, wts[0], wts[1])], axis=1)
    onehot = ((e_lo[:, None] == tab_a[None, :]) & (e_hi[:, None] == tab_b[None, :])).astype(jnp.int32)
    csum = _cumsum_rows(onehot)
    counts = csum[-1]
    rank = jnp.sum(csum * onehot, axis=1) - 1
    seg_len = (counts + MOE_BM - 1) // MOE_BM * MOE_BM
    seg_end = jnp.cumsum(seg_len)
    pos = jnp.sum(onehot * (seg_end - seg_len)[None, :], axis=1) + rank
    blk_start = jnp.arange(nblk, dtype=jnp.int32) * MOE_BM
    blk_cls = jnp.minimum(jnp.sum((seg_end[None, :] <= blk_start[:, None]).astype(jnp.int32), axis=1), n_cls - 1)
    rows = nblk * MOE_BM
    pad_len = seg_len - counts
    pad_end = jnp.cumsum(pad_len)
    slot = jnp.arange(rows - n, dtype=jnp.int32)
    slot_cls = jnp.sum((pad_end[None, :] <= slot[:, None]).astype(jnp.int32), axis=1)
    first_pad = jnp.concatenate([seg_end - pad_len, seg_end[-1:]])[slot_cls]
    pad_before = jnp.concatenate([jnp.zeros((1,), jnp.int32), pad_end])[slot_cls]
    pad_pos = first_pad + slot - pad_before
    td = min(n, MOE_DISPATCH_TILE)
    x_sorted = _moe_dispatch(pos.reshape(n // td, 1, td), pad_pos.reshape(n // td, 1, (rows - n) // (n // td)),
                             h2, rows, td)
    y_sorted = _moe_ffn(tab_a[blk_cls], tab_b[blk_cls], x_sorted, lw)
    tc = min(seq, MOE_COMBINE_TILE)
    pos_next = jnp.concatenate([pos.reshape(n // tc, 1, tc), jnp.zeros((1, 1, tc), jnp.int32)], axis=0)
    return _moe_combine(pos_next, y_sorted, x, w_cols, gate, seq, tc)


def _head_perm():
    rope_idx = list(range(NOPE, QK, 2)) + list(range(NOPE + 1, QK, 2))
    return jnp.array(list(range(NOPE)) + rope_idx, dtype=jnp.int32)


def _block_diag(a, b):
    return jnp.concatenate([jnp.pad(a, ((0, 0), (0, b.shape[1]))), jnp.pad(b, ((0, 0), (a.shape[1], 0)))], axis=0)


def _layer_weights(p, l):
    perm = _head_perm()
    row = lambda v: v.reshape(1, -1)
    w_in = p['w_in'][l].astype(BF16)
    kr_perm = jnp.array(list(range(0, ROPE, 2)) + list(range(1, ROPE, 2)), dtype=jnp.int32)
    w_kr = jnp.pad(w_in[:, OFF_KR:OFF_HY][:, kr_perm], ((0, 0), (0, HEAD_PAD - ROPE)))
    w_in2 = jnp.concatenate([w_in[:, OFF_Q:OFF_KR], w_kr, w_in[:, OFF_HY:]], axis=1)
    w_uq = p['w_uq'][l].reshape(Q_RANK, HEADS, QK)[:, :, perm]
    w_uq = jnp.pad(w_uq, ((0, 0), (0, 0), (0, HEAD_PAD - QK))).reshape(Q_RANK, HEADS * HEAD_PAD).astype(BF16)
    w_ukv = p['w_ukv'][l].reshape(KV_RANK, HEADS, NOPE + VDIM)
    w_k = jnp.pad(w_ukv[:, :, :NOPE], ((0, 0), (0, 0), (0, HEAD_PAD - NOPE)))
    w_k = w_k.reshape(KV_RANK, HEADS * HEAD_PAD).astype(BF16)
    w_vt = w_ukv[:, :, NOPE:].reshape(KV_RANK, HEADS * VDIM).T.astype(BF16)
    pad_gain = lambda g: jnp.pad(g[perm], (0, HEAD_PAD - QK)).reshape(1, HEAD_PAD)
    gm_b = jnp.repeat(p['gm_bs'][l].T, GM_W // GM_GROUPS, axis=1)
    bound = 1.02 * QSCALE * QK * jnp.max(jnp.abs(p['g_qn'][l])) * jnp.max(jnp.abs(p['g_kn'][l]))
    pad_lane = jnp.arange(HEAD_PAD) == QK
    return dict(
        attn_bound=bound,
        q_pad=jnp.where(pad_lane, -bound, 0.0).reshape(1, HEAD_PAD).astype(F32),
        k_pad=jnp.where(pad_lane, 1.0, 0.0).reshape(1, HEAD_PAD).astype(F32),
        g1=row(p['norm1_g'][l]), g2=row(p['norm2_g'][l]), w_in=w_in2,
        g_qa=row(p['g_qa'][l]), w_uq=w_uq, g_qn=pad_gain(p['g_qn'][l]),
        g_kva=row(p['g_kva'][l]), w_k=w_k, w_vt=w_vt, g_kn=pad_gain(p['g_kn'][l]),
        gm_g=row(p['gm_norm_g'][l]), gm_w=p['gm_ws'][l].astype(BF16), gm_b=gm_b,
        hy_conv_w=p['hy_conv_w'][l], hy_conv_b=row(p['hy_conv_b'][l]),
        hy_w1p=_block_diag(p['hy_w1'][l], p['hy_w1'][l]), hy_b1p=row(jnp.tile(p['hy_b1'][l], 2)),
        hy_w2p=_block_diag(p['hy_w2'][l], p['hy_w2'][l]), hy_b2p=row(jnp.tile(p['hy_b2'][l], 2)),
        hy_w3p=_block_diag(p['hy_w3'][l][:, :HY_W], p['hy_w3'][l][:, HY_W:]), hy_b3=row(p['hy_b3'][l]),
        hy_freqp=row(jnp.tile(p['hy_freq'][l], 2)),
        hy_decay=p['hy_decay'][l].reshape(1, 2 * HY_W), hy_bias=row(p['hy_bias'][l]),
        w_pa=p['w_pa'][l].astype(BF16), w_pb=p['w_pb'][l].astype(BF16), w_pc=p['w_pc'][l].astype(BF16),
        w_out=p['w_out'][l].astype(BF16),
        w_gu=jnp.concatenate([p['moe_w_gate'][l].astype(BF16), p['moe_w_up'][l].astype(BF16)], axis=-1),
        w_d=p['moe_w_down'][l].astype(BF16))


def _rope_tables(seq):
    rows = seq // GRID_W
    row = jnp.repeat(jnp.arange(rows, dtype=F32), GRID_W)
    col = jnp.tile(jnp.arange(GRID_W, dtype=F32), rows)
    n_freq = ROPE // 4
    inv = ROPE_THETA ** (-jnp.arange(n_freq, dtype=F32) / n_freq)
    ang = jnp.concatenate([row[:, None] * inv, col[:, None] * inv], axis=-1)
    c, s = jnp.cos(ang), jnp.sin(ang)
    z = lambda w: jnp.zeros((seq, w), F32)
    rc = jnp.concatenate([jnp.ones((seq, NOPE), F32), c, c, z(HEAD_PAD - QK)], axis=1)
    rs1 = jnp.concatenate([z(NOPE), z(ROPE // 2), s, z(HEAD_PAD - QK)], axis=1)
    rs2 = jnp.concatenate([z(NOPE), -s, z(ROPE // 2), z(HEAD_PAD - QK)], axis=1)
    return rc, rs1, rs2


def _mixer_and_ffn(x, mods, lw, rw_t, rb, seq, tm, rope_tabs, ctx_kv, filt, dft_tabs):
    shift1, scale1, gate1, shift2, scale2, gate2 = mods
    q, k, vt, m, phy, gt = _premix(x, shift1, scale1, lw, rope_tabs, seq, tm)
    a = _attention(q, k, vt, ctx_kv, seq, min(seq, ATTN_TQ), lw['attn_bound'])
    if dft_tabs is not None:
        b = _hyena_long(phy, lw, filt, seq, dft_tabs)
    else:
        b = _hyena_short(phy, lw, filt, seq)
    xn, h2, idx, wts = _merge(a, b, m, gt, x, gate1, shift2, scale2, lw, rw_t, rb, seq, min(seq, MERGE_TILE))
    return _moe(h2, idx, wts, xn, gate2, lw, seq), k, vt


def _forward(p):
    x, ctx = p['x'], p['ctx']
    batch, seq, _ = x.shape
    lc = ctx.shape[1]
    depth = p['w_mod'].shape[0]

    cvecs = jnp.concatenate([p['c'], p['c_ctx'][None], jnp.zeros((8 - batch - 1, D_MODEL), F32)], axis=0)
    mod_all = _modvec(cvecs, p['w_mod'], p['b_mod'])
    rw_t = p['router_w'].T
    rb = p['router_b'].reshape(N_EXPERTS, 1)
    rope_tabs = _rope_tables(seq)
    dft_tabs = _dft_tables(seq)
    z_lat, z_ctx = _hy_embedding(seq), _hy_embedding(lc)

    xl = x.reshape(batch * seq, D_MODEL)
    xc = ctx.reshape(batch * lc, D_MODEL)
    tm_lat = min(seq, TOKEN_TILE)
    for l in range(depth):
        lw = _layer_weights(p, l)
        mod = mod_all[l].reshape(8, N_MOD, D_MODEL)
        mods_lat = [mod[:batch, j].reshape(batch, 1, D_MODEL) for j in range(N_MOD)]
        mods_ctx = [jnp.broadcast_to(mod[batch, j].reshape(1, 1, D_MODEL), (batch, 1, D_MODEL))
                    for j in range(N_MOD)]
        if l == depth - 1:
            _, k_c, vt_c, _, _, _ = _premix(xc, mods_ctx[0], mods_ctx[1], lw, None, lc, lc)
        else:
            xc, k_c, vt_c = _mixer_and_ffn(xc, mods_ctx, lw, rw_t, rb, lc, lc, None, None,
                                           _hy_filters(z_ctx, lw, lc), None)
        filt = _hy_filters(z_lat, lw, min(seq, HY_FILTER_TILE))
        xl, _, _ = _mixer_and_ffn(xl, mods_lat, lw, rw_t, rb, seq, tm_lat, rope_tabs, (k_c, vt_c), filt, dft_tabs)
    return xl.reshape(batch, seq, D_MODEL)


def kernel(x, c, ctx, c_ctx, w_mod, b_mod, norm1_g, norm2_g, w_in, g_qa, w_uq, g_kva, w_ukv, g_qn, g_kn,
           hy_conv_w, hy_conv_b, hy_w1, hy_b1, hy_w2, hy_b2, hy_w3, hy_b3, hy_freq, hy_decay, hy_bias,
           gm_norm_g, gm_ws, gm_bs, w_pa, w_pb, w_pc, w_out, router_w, router_b,
           moe_w_gate, moe_w_up, moe_w_down):
    return _forward(dict(
        x=x, c=c, ctx=ctx, c_ctx=c_ctx, w_mod=w_mod, b_mod=b_mod, norm1_g=norm1_g, norm2_g=norm2_g, w_in=w_in,
        g_qa=g_qa, w_uq=w_uq, g_kva=g_kva, w_ukv=w_ukv, g_qn=g_qn, g_kn=g_kn, hy_conv_w=hy_conv_w,
        hy_conv_b=hy_conv_b, hy_w1=hy_w1, hy_b1=hy_b1, hy_w2=hy_w2, hy_b2=hy_b2, hy_w3=hy_w3, hy_b3=hy_b3,
        hy_freq=hy_freq, hy_decay=hy_decay, hy_bias=hy_bias, gm_norm_g=gm_norm_g, gm_ws=gm_ws, gm_bs=gm_bs,
        w_pa=w_pa, w_pb=w_pb, w_pc=w_pc, w_out=w_out, router_w=router_w, router_b=router_b,
        moe_w_gate=moe_w_gate, moe_w_up=moe_w_up, moe_w_down=moe_w_down))
```

```python
import functools
import math

import jax
import jax.numpy as jnp
from jax import lax
from jax.experimental import pallas as pl
from jax.experimental.pallas import tpu as pltpu

F32 = jnp.float32
BF16 = jnp.bfloat16
HIGHEST = lax.Precision.HIGHEST

D_MODEL = 1024
GRID_W = 64
EPS = 1e-6
N_MOD = 6

HEADS = 8
Q_RANK = 384
KV_RANK = 256
NOPE = 64
ROPE = 32
QK = NOPE + ROPE
VDIM = 64
HEAD_PAD = 128
ROPE_THETA = 10000.0
V_ROWS = 80

HY_W = 256
HY_BANDS = 16
GM_W = 256
GM_CHUNK = 128
GM_GROUPS = 4

OFF_Q = 0
OFF_KV = OFF_Q + Q_RANK
OFF_KR = OFF_KV + KV_RANK
OFF_HY = OFF_KR + ROPE
OFF_GM = OFF_HY + 3 * HY_W
OFF_GT = OFF_GM + 2 * GM_W

P_Q = 0
P_KV = P_Q + Q_RANK
P_KR = P_KV + KV_RANK
P_HY = P_KR + HEAD_PAD
P_GM = P_HY + 3 * HY_W
P_GT = P_GM + 2 * GM_W
P_W = P_GT + 3 * D_MODEL

N_EXPERTS = 16
N_GROUPS = 4
EXP_PER_GROUP = 4
TOP_K = 2
D_EXPERT = 512
MOE_BM = 256
MOE_DISPATCH_TILE = 1024
MOE_COMBINE_TILE = 512

DFT_N2 = 256
DFT_K1_STEP = 4
DFT_COL_TILE = 8192

TOKEN_TILE = 512
MERGE_TILE = 1024
HY_CONV_TILE = 1024
HY_FILTER_TILE = 2048

VMEM_LIMIT = 56 * 1024 * 1024
ATTN_TQ = 512
ATTN_TK = 2048
ATTN_BOUND_MAX = 50.0
NEG_BIG = -1e30
LOG2E = 1.4426950408889634
QSCALE = QK ** -0.5 * LOG2E


def _cparams(sem):
    return pltpu.CompilerParams(dimension_semantics=sem, vmem_limit_bytes=VMEM_LIMIT)


def _rms(x):
    return x * lax.rsqrt(jnp.mean(x * x, axis=-1, keepdims=True) + EPS)


def _sigmoid(x):
    return 0.5 * jnp.tanh(0.5 * x) + 0.5


def _nt_dot(a, b):
    return lax.dot_general(a, b, (((1,), (1,)), ((), ())), preferred_element_type=F32)


def _modvec_kernel(c_ref, w_ref, b_ref, o_ref):
    cv = c_ref[...]
    s = cv * jax.nn.sigmoid(cv)
    o_ref[0] = jnp.dot(s, w_ref[0], preferred_element_type=F32, precision=HIGHEST) + b_ref[0]


def _modvec(cvecs, w_mod, b_mod):
    depth = w_mod.shape[0]
    tn = 1536
    return pl.pallas_call(
        _modvec_kernel,
        out_shape=jax.ShapeDtypeStruct((depth, 8, N_MOD * D_MODEL), F32),
        grid=(depth, N_MOD * D_MODEL // tn),
        in_specs=[pl.BlockSpec((8, D_MODEL), lambda l, j: (0, 0)),
                  pl.BlockSpec((1, D_MODEL, tn), lambda l, j: (l, 0, j)),
                  pl.BlockSpec((1, 1, tn), lambda l, j: (l, 0, j))],
        out_specs=pl.BlockSpec((1, 8, tn), lambda l, j: (l, 0, j)),
        compiler_params=_cparams(("arbitrary", "arbitrary")),
        name="modvec",
    )(cvecs, w_mod, b_mod.reshape(depth, 1, N_MOD * D_MODEL))


def _head_norm_rope(xh, gain, rope):
    ms = jnp.sum(xh * xh, axis=-1, keepdims=True) * (1.0 / QK)
    xh = xh * lax.rsqrt(ms + EPS) * gain
    if rope is not None:
        rc, rs1, rs2 = rope
        xh = xh * rc + pltpu.roll(xh, ROPE // 2, 1) * rs1 + pltpu.roll(xh, HEAD_PAD - ROPE // 2, 1) * rs2
    return xh


def _premix_kernel(*refs, use_rope, tm):
    if use_rope:
        (x_ref, shift_ref, scale_ref, g1_ref, win_ref, gqa_ref, wuq_ref, gqn_ref, gkva_ref, wk_ref, wvt_ref,
         gkn_ref, gmg_ref, gmw_ref, gmb_ref, qpad_ref, kpad_ref, rc_ref, rs1_ref, rs2_ref,
         q_ref, k_ref, vt_ref, m_ref, hy_ref, gt_ref) = refs
        rope = (rc_ref[...], rs1_ref[...], rs2_ref[...])
    else:
        (x_ref, shift_ref, scale_ref, g1_ref, win_ref, gqa_ref, wuq_ref, gqn_ref, gkva_ref, wk_ref, wvt_ref,
         gkn_ref, gmg_ref, gmw_ref, gmb_ref, qpad_ref, kpad_ref,
         q_ref, k_ref, vt_ref, m_ref, hy_ref, gt_ref) = refs
        rope = None

    x = x_ref[...]
    h = _rms(x) * g1_ref[...]
    h = h * (1.0 + scale_ref[0]) + shift_ref[0]
    hb = h.astype(BF16)

    def proj(lo, width):
        return jnp.dot(hb, win_ref[:, lo:lo + width], preferred_element_type=F32)

    qa = (_rms(proj(P_Q, Q_RANK)) * gqa_ref[...]).astype(BF16)
    q = jnp.dot(qa, wuq_ref[...], preferred_element_type=F32)
    for hh in range(HEADS):
        qh = _head_norm_rope(q[:, hh * HEAD_PAD:(hh + 1) * HEAD_PAD], gqn_ref[...], rope)
        q_ref[0, hh * HEAD_PAD:(hh + 1) * HEAD_PAD, :] = (qh * QSCALE + qpad_ref[...]).T.astype(BF16)

    kva = (_rms(proj(P_KV, KV_RANK)) * gkva_ref[...]).astype(BF16)
    kr = pltpu.roll(proj(P_KR, HEAD_PAD), NOPE, 1)
    kn = jnp.dot(kva, wk_ref[...], preferred_element_type=F32)
    for hh in range(HEADS):
        kh = _head_norm_rope(kn[:, hh * HEAD_PAD:(hh + 1) * HEAD_PAD] + kr, gkn_ref[...], rope)
        k_ref[:, hh * HEAD_PAD:(hh + 1) * HEAD_PAD] = (kh + kpad_ref[...]).astype(BF16)
    vt = _nt_dot(wvt_ref[...], kva)
    row = lax.broadcasted_iota(jnp.int32, (V_ROWS - VDIM, tm), 0)
    ones_rows = jnp.where(row == 0, 1.0, 0.0).astype(BF16)
    for hh in range(HEADS):
        vt_ref[0, 0, hh * V_ROWS:hh * V_ROWS + VDIM, :] = vt[hh * VDIM:(hh + 1) * VDIM].astype(BF16)
        vt_ref[0, 0, hh * V_ROWS + VDIM:(hh + 1) * V_ROWS, :] = ones_rows

    gg = jax.nn.gelu(proj(P_GM, 2 * GM_W), approximate=True)
    gu = gg[:, :GM_W]
    gv = (_rms(gg[:, GM_W:]) * gmg_ref[...]).astype(BF16)
    grp = lax.broadcasted_iota(jnp.int32, (GM_CHUNK, GM_W), 1) // (GM_W // GM_GROUPS)
    for ci in range(tm // GM_CHUNK):
        vc = gv[ci * GM_CHUNK:(ci + 1) * GM_CHUNK]
        s = jnp.zeros((GM_CHUNK, GM_W), F32)
        for g in range(GM_GROUPS):
            sg = jnp.dot(gmw_ref[g], vc, preferred_element_type=F32)
            s = jnp.where(grp == g, sg, s)
        m_ref[ci * GM_CHUNK:(ci + 1) * GM_CHUNK, :] = (
            gu[ci * GM_CHUNK:(ci + 1) * GM_CHUNK] * (s + gmb_ref[...])).astype(BF16)

    hy_ref[...] = proj(P_HY, 3 * HY_W).astype(BF16)
    for j in range(3):
        gt_ref[:, j * D_MODEL:(j + 1) * D_MODEL] = _sigmoid(proj(P_GT + j * D_MODEL, D_MODEL)).astype(BF16)


def _premix(x, shift, scale, lw, rope_tabs, seq, tm):
    n = x.shape[0]
    tiles_per_seq = seq // tm
    batch = n // seq
    use_rope = rope_tabs is not None
    const = lambda i: (0, 0)
    in_specs = [
        pl.BlockSpec((tm, D_MODEL), lambda i: (i, 0)),
        pl.BlockSpec((1, 1, D_MODEL), lambda i: (i // tiles_per_seq, 0, 0)),
        pl.BlockSpec((1, 1, D_MODEL), lambda i: (i // tiles_per_seq, 0, 0)),
        pl.BlockSpec((1, D_MODEL), const),
        pl.BlockSpec((D_MODEL, P_W), const, pipeline_mode=pl.Buffered(1)),
        pl.BlockSpec((1, Q_RANK), const),
        pl.BlockSpec((Q_RANK, HEADS * HEAD_PAD), const),
        pl.BlockSpec((1, HEAD_PAD), const),
        pl.BlockSpec((1, KV_RANK), const),
        pl.BlockSpec((KV_RANK, HEADS * HEAD_PAD), const),
        pl.BlockSpec((HEADS * VDIM, KV_RANK), const),
        pl.BlockSpec((1, HEAD_PAD), const),
        pl.BlockSpec((1, GM_W), const),
        pl.BlockSpec((GM_GROUPS, GM_CHUNK, GM_CHUNK), lambda i: (0, 0, 0)),
        pl.BlockSpec((GM_CHUNK, GM_W), const),
        pl.BlockSpec((1, HEAD_PAD), const),
        pl.BlockSpec((1, HEAD_PAD), const),
    ]
    args = [x, shift, scale, lw['g1'], lw['w_in'], lw['g_qa'], lw['w_uq'], lw['g_qn'], lw['g_kva'], lw['w_k'],
            lw['w_vt'], lw['g_kn'], lw['gm_g'], lw['gm_w'], lw['gm_b'], lw['q_pad'], lw['k_pad']]
    tkv = min(seq, ATTN_TK)
    sub = tkv // tm
    if use_rope:
        in_specs += [pl.BlockSpec((tm, HEAD_PAD), lambda i: (i % tiles_per_seq, 0))] * 3
        args += list(rope_tabs)
    out_shape = (
        jax.ShapeDtypeStruct((n // tm, HEADS * HEAD_PAD, tm), BF16),
        jax.ShapeDtypeStruct((n, HEADS * HEAD_PAD), BF16),
        jax.ShapeDtypeStruct((batch, seq // tkv, HEADS * V_ROWS, tkv), BF16),
        jax.ShapeDtypeStruct((n, GM_W), BF16),
        jax.ShapeDtypeStruct((n, 3 * HY_W), BF16),
        jax.ShapeDtypeStruct((n, 3 * D_MODEL), BF16),
    )
    out_specs = (
        pl.BlockSpec((1, HEADS * HEAD_PAD, tm), lambda i: (i, 0, 0)),
        pl.BlockSpec((tm, HEADS * HEAD_PAD), lambda i: (i, 0)),
        pl.BlockSpec((1, 1, HEADS * V_ROWS, tm),
                     lambda i: (i // tiles_per_seq, (i % tiles_per_seq) // sub, 0, (i % tiles_per_seq) % sub)),
        pl.BlockSpec((tm, GM_W), lambda i: (i, 0)),
        pl.BlockSpec((tm, 3 * HY_W), lambda i: (i, 0)),
        pl.BlockSpec((tm, 3 * D_MODEL), lambda i: (i, 0)),
    )
    return pl.pallas_call(
        functools.partial(_premix_kernel, use_rope=use_rope, tm=tm),
        out_shape=out_shape, grid=(n // tm,), in_specs=in_specs, out_specs=out_specs,
        compiler_params=_cparams(("arbitrary",)), name="premix",
    )(*args)


def _attn_kernel(*refs, n_chunks, has_ctx, tq):
    if has_ctx:
        q_ref, k_ref, vt_ref, kc_ref, vtc_ref, o_ref = refs
    else:
        q_ref, k_ref, vt_ref, o_ref = refs
    tk = vt_ref.shape[-1]

    def step(hh, kc, vtc, m, acc):
        s = jnp.dot(kc, q_ref[0, hh * HEAD_PAD:(hh + 1) * HEAD_PAD, :], preferred_element_type=F32)
        m_new = jnp.maximum(m, jnp.max(s, axis=0, keepdims=True))
        p = jnp.exp2(s - m_new).astype(BF16)
        alpha = jnp.exp2(m - m_new)
        return m_new, acc * alpha + jnp.dot(vtc, p, preferred_element_type=F32)

    def body(i, carry):
        out = []
        for hh in range(2):
            m, acc = carry[hh]
            start = pl.multiple_of(i * tk, tk)
            kc = k_ref[pl.ds(start, tk), hh * HEAD_PAD:(hh + 1) * HEAD_PAD]
            vtc = vt_ref[0, i, hh * V_ROWS:(hh + 1) * V_ROWS, :]
            out.append(step(hh, kc, vtc, m, acc))
        return tuple(out)

    init = tuple((jnp.full((1, tq), NEG_BIG, F32), jnp.zeros((V_ROWS, tq), F32)) for _ in range(2))
    carry = lax.fori_loop(0, n_chunks, body, init)
    outs = []
    for hh in range(2):
        m, acc = carry[hh]
        if has_ctx:
            m, acc = step(hh, kc_ref[:, hh * HEAD_PAD:(hh + 1) * HEAD_PAD],
                          vtc_ref[0, 0, hh * V_ROWS:(hh + 1) * V_ROWS, :], m, acc)
        outs.append(acc[:VDIM] / acc[VDIM:VDIM + 1])
    o_ref[...] = jnp.concatenate(outs, axis=0).T.astype(BF16)


def _attn_bounded_kernel(*refs, n_chunks, has_ctx, tq):
    if has_ctx:
        q_ref, k_ref, vt_ref, kc_ref, vtc_ref, o_ref = refs
    else:
        q_ref, k_ref, vt_ref, o_ref = refs
    tk = vt_ref.shape[-1]

    def step(hh, kc, vtc, acc):
        s = jnp.dot(kc, q_ref[0, hh * HEAD_PAD:(hh + 1) * HEAD_PAD, :], preferred_element_type=F32)
        return acc + jnp.dot(vtc, jnp.exp2(s).astype(BF16), preferred_element_type=F32)

    def body(i, carry):
        start = pl.multiple_of(i * tk, tk)
        return tuple(step(hh, k_ref[pl.ds(start, tk), hh * HEAD_PAD:(hh + 1) * HEAD_PAD],
                          vt_ref[0, i, hh * V_ROWS:(hh + 1) * V_ROWS, :], carry[hh]) for hh in range(2))

    carry = lax.fori_loop(0, n_chunks, body, tuple(jnp.zeros((V_ROWS, tq), F32) for _ in range(2)), unroll=True)
    outs = []
    for hh in range(2):
        acc = carry[hh]
        if has_ctx:
            acc = step(hh, kc_ref[:, hh * HEAD_PAD:(hh + 1) * HEAD_PAD],
                       vtc_ref[0, 0, hh * V_ROWS:(hh + 1) * V_ROWS, :], acc)
        outs.append(acc[:VDIM] / acc[VDIM:VDIM + 1])
    o_ref[...] = jnp.concatenate(outs, axis=0).T.astype(BF16)


def _attention(q, k, vt, ctx_kv, seq, tq, bound):
    n = k.shape[0]
    batch = n // seq
    n_chunks, tk = vt.shape[1], vt.shape[3]
    q_tiles = seq // tq
    assert q.shape == (n // tq, HEADS * HEAD_PAD, tq)
    has_ctx = ctx_kv is not None
    in_specs = [
        pl.BlockSpec((1, 2 * HEAD_PAD, tq), lambda b, j, i: (b * q_tiles + i, j, 0)),
        pl.BlockSpec((seq, 2 * HEAD_PAD), lambda b, j, i: (b, j)),
        pl.BlockSpec((1, n_chunks, 2 * V_ROWS, tk), lambda b, j, i: (b, 0, j, 0)),
    ]
    args = [q, k, vt]
    if has_ctx:
        kc, vtc = ctx_kv
        lc = vtc.shape[3]
        in_specs += [pl.BlockSpec((lc, 2 * HEAD_PAD), lambda b, j, i: (b, j)),
                     pl.BlockSpec((1, 1, 2 * V_ROWS, lc), lambda b, j, i: (b, 0, j, 0))]
        args += [kc, vtc]
    def call(body, name):
        return pl.pallas_call(
            functools.partial(body, n_chunks=n_chunks, has_ctx=has_ctx, tq=tq),
            out_shape=jax.ShapeDtypeStruct((n, HEADS * VDIM), BF16),
            grid=(batch, HEADS // 2, q_tiles),
            in_specs=in_specs,
            out_specs=pl.BlockSpec((tq, 2 * VDIM), lambda b, j, i: (b * q_tiles + i, j)),
            compiler_params=_cparams(("arbitrary", "arbitrary", "arbitrary")),
            name=name,
        )(*args)

    return lax.cond(bound < ATTN_BOUND_MAX,
                    lambda: call(_attn_bounded_kernel, "attn_bounded"),
                    lambda: call(_attn_kernel, "attn_online"))


def _hy_filter_kernel(zz_ref, w1_ref, b1_ref, w2_ref, b2_ref, w3_ref, b3_ref, fr_ref, dec_ref, f_ref, ss_ref, *, emb):
    fr = fr_ref[...]
    dec = jnp.abs(dec_ref[...])
    zz = zz_ref[...]
    hdn = jnp.sin(fr * (jnp.dot(zz, w1_ref[...], preferred_element_type=F32, precision=HIGHEST) + b1_ref[...]))
    hdn = jnp.sin(fr * (jnp.dot(hdn, w2_ref[...], preferred_element_type=F32, precision=HIGHEST) + b2_ref[...]))
    k = jnp.dot(hdn, w3_ref[...], preferred_element_type=F32, precision=HIGHEST) + b3_ref[...]
    col = lax.broadcasted_iota(jnp.int32, k.shape, 1)
    k = k * jnp.exp(-jnp.where(col < HY_W, zz[:, 0:1], zz[:, emb:emb + 1]) * dec)
    first_block = pl.program_id(0) == 0
    row = lax.broadcasted_iota(jnp.int32, k.shape, 0)
    k = jnp.where(first_block & (row == 0) & (col >= HY_W), 0.0, k)
    f_ref[0] = k[:, :HY_W].astype(BF16)
    f_ref[1] = k[:, HY_W:].astype(BF16)
    hid = hdn.shape[1] // 2
    k_0 = jnp.dot(pltpu.roll(hdn[0:8], hid, 1), w3_ref[...], preferred_element_type=F32,
                  precision=HIGHEST) + b3_ref[...]
    k_0 = k_0[0:1] * jnp.exp(-zz[0:1, 0:1] * dec)
    extra = jnp.where(first_block & (col[0:1] >= HY_W), k_0 * k_0, 0.0)
    ss_ref[0] = jnp.sum(k * k, axis=0, keepdims=True) + extra


def _hy_filters(z, lw, tr):
    seq, emb = z.shape
    nblk = seq // tr
    hid2 = lw['hy_w2p'].shape[0]
    const = lambda i: (0, 0)
    zz = jnp.concatenate([z, jnp.concatenate([z[0:1], z[:0:-1]], axis=0)], axis=1)
    return pl.pallas_call(
        functools.partial(_hy_filter_kernel, emb=emb),
        out_shape=(jax.ShapeDtypeStruct((2, seq, HY_W), BF16), jax.ShapeDtypeStruct((nblk, 1, 2 * HY_W), F32)),
        grid=(nblk,),
        in_specs=[pl.BlockSpec((tr, 2 * emb), lambda i: (i, 0)),
                  pl.BlockSpec((2 * emb, hid2), const), pl.BlockSpec((1, hid2), const),
                  pl.BlockSpec((hid2, hid2), const), pl.BlockSpec((1, hid2), const),
                  pl.BlockSpec((hid2, 2 * HY_W), const), pl.BlockSpec((1, 2 * HY_W), const),
                  pl.BlockSpec((1, hid2), const), pl.BlockSpec((1, 2 * HY_W), const)],
        out_specs=(pl.BlockSpec((2, tr, HY_W), lambda i: (0, i, 0)),
                   pl.BlockSpec((1, 1, 2 * HY_W), lambda i: (i, 0, 0))),
        compiler_params=_cparams(("arbitrary",)), name="hy_filter",
    )(zz, lw['hy_w1p'], lw['hy_b1p'], lw['hy_w2p'], lw['hy_b2p'], lw['hy_w3p'], lw['hy_b3'], lw['hy_freqp'],
      lw['hy_decay'])


def _hy_conv3_kernel(p_ref, prev_ref, next_ref, w_ref, b_ref, x0_ref, u_ref, *, tiles_per_seq, tr):
    i = pl.program_id(0)
    p = p_ref[...].astype(F32)
    first = (i % tiles_per_seq) == 0
    last = (i % tiles_per_seq) == tiles_per_seq - 1
    prev_row = jnp.where(first, 0.0, prev_ref[...].astype(F32)[15:16, :])
    next_row = jnp.where(last, 0.0, next_ref[...].astype(F32)[0:1, :])
    row = lax.broadcasted_iota(jnp.int32, p.shape, 0)
    p_prev = jnp.where(row == 0, prev_row, pltpu.roll(p, 1, 0))
    p_next = jnp.where(row == tr - 1, next_row, pltpu.roll(p, tr - 1, 0))
    z = b_ref[...] + p_prev * w_ref[0:1, :] + p * w_ref[1:2, :] + p_next * w_ref[2:3, :]
    x0_ref[...] = z[:, :HY_W].astype(BF16)
    u_ref[...] = (z[:, 2 * HY_W:] * z[:, HY_W:2 * HY_W]).astype(BF16)


def _hy_conv3(phy, conv_w, conv_b, seq, tr):
    n = phy.shape[0]
    tiles_per_seq = seq // tr
    hb = tr // 16
    nhb = n // 16
    return pl.pallas_call(
        functools.partial(_hy_conv3_kernel, tiles_per_seq=tiles_per_seq, tr=tr),
        out_shape=(jax.ShapeDtypeStruct((n, HY_W), BF16), jax.ShapeDtypeStruct((n, HY_W), BF16)),
        grid=(n // tr,),
        in_specs=[pl.BlockSpec((tr, 3 * HY_W), lambda i: (i, 0)),
                  pl.BlockSpec((16, 3 * HY_W), lambda i: (jnp.maximum(i * hb - 1, 0), 0)),
                  pl.BlockSpec((16, 3 * HY_W), lambda i: (jnp.minimum((i + 1) * hb, nhb - 1), 0)),
                  pl.BlockSpec((3, 3 * HY_W), lambda i: (0, 0)),
                  pl.BlockSpec((1, 3 * HY_W), lambda i: (0, 0))],
        out_specs=(pl.BlockSpec((tr, HY_W), lambda i: (i, 0)), pl.BlockSpec((tr, HY_W), lambda i: (i, 0))),
        compiler_params=_cparams(("arbitrary",)), name="hy_conv3",
    )(phy, phy, phy, conv_w, conv_b)


def _dft_outer_kernel(wr_ref, wi_ref, u_ref, ar_ref, ai_ref):
    u = u_ref[0]
    ar_ref[0] = jnp.dot(wr_ref[...], u, preferred_element_type=F32).astype(BF16)
    ai_ref[0] = jnp.dot(wi_ref[...], u, preferred_element_type=F32).astype(BF16)


def _dft_outer(wr, wi, u, tc):
    nb, kk, cols = u.shape
    n1 = wr.shape[0]
    return pl.pallas_call(
        _dft_outer_kernel,
        out_shape=(jax.ShapeDtypeStruct((nb, n1, cols), BF16),) * 2,
        grid=(nb, cols // tc),
        in_specs=[pl.BlockSpec((n1, kk), lambda b, j: (0, 0)), pl.BlockSpec((n1, kk), lambda b, j: (0, 0)),
                  pl.BlockSpec((1, kk, tc), lambda b, j: (b, 0, j))],
        out_specs=(pl.BlockSpec((1, n1, tc), lambda b, j: (b, 0, j)),) * 2,
        compiler_params=_cparams(("arbitrary", "arbitrary")), name="dft_outer",
    )(wr, wi, u)


def _cdot(mr, mi, xr, xi):
    rr = jnp.dot(mr, xr, preferred_element_type=F32) - jnp.dot(mi, xi, preferred_element_type=F32)
    ri = jnp.dot(mr, xi, preferred_element_type=F32) + jnp.dot(mi, xr, preferred_element_type=F32)
    return rr, ri


def _dft_inner_fwd_kernel(mr_ref, mi_ref, ar_ref, ai_ref, fr_ref, fi_ref):
    for j in range(DFT_K1_STEP):
        fr, fi = _cdot(mr_ref[j], mi_ref[j], ar_ref[0, j], ai_ref[0, j])
        fr_ref[j] = fr.astype(BF16)
        fi_ref[j] = fi.astype(BF16)


def _dft_inner_fwd(mr, mi, ar, ai):
    n1, n2, _ = mr.shape
    ch = ar.shape[-1]
    mspec = pl.BlockSpec((DFT_K1_STEP, n2, n2), lambda k: (k, 0, 0))
    aspec = pl.BlockSpec((1, DFT_K1_STEP, n2, ch), lambda k: (0, k, 0, 0))
    ospec = pl.BlockSpec((DFT_K1_STEP, n2, ch), lambda k: (k, 0, 0))
    return pl.pallas_call(
        _dft_inner_fwd_kernel,
        out_shape=(jax.ShapeDtypeStruct((n1, n2, ch), BF16),) * 2,
        grid=(n1 // DFT_K1_STEP,), in_specs=[mspec, mspec, aspec, aspec], out_specs=(ospec, ospec),
        compiler_params=_cparams(("arbitrary",)), name="dft_inner_fwd",
    )(mr, mi, ar, ai)


def _dft_inner_conv_kernel(mr_ref, mi_ref, tr_ref, ti_ref, ar_ref, ai_ref, fr_ref, fi_ref, br_ref, bi_ref, *, nb):
    for j in range(DFT_K1_STEP):
        fr = fr_ref[j].astype(F32)
        fi = fi_ref[j].astype(F32)
        for b in range(nb):
            ur, ui = _cdot(mr_ref[j], mi_ref[j], ar_ref[b, j], ai_ref[b, j])
            yr = (ur * fr - ui * fi).astype(BF16)
            yi = (ur * fi + ui * fr).astype(BF16)
            br, bi = _cdot(tr_ref[j], ti_ref[j], yr, yi)
            br_ref[b, j] = br.astype(BF16)
            bi_ref[b, j] = bi.astype(BF16)


def _dft_inner_conv(mr, mi, tr, ti, ar, ai, fr, fi):
    n1, n2, _ = mr.shape
    nb, _, _, ch = ar.shape
    mspec = pl.BlockSpec((DFT_K1_STEP, n2, n2), lambda k: (k, 0, 0))
    aspec = pl.BlockSpec((nb, DFT_K1_STEP, n2, ch), lambda k: (0, k, 0, 0))
    fspec = pl.BlockSpec((DFT_K1_STEP, n2, ch), lambda k: (k, 0, 0))
    return pl.pallas_call(
        functools.partial(_dft_inner_conv_kernel, nb=nb),
        out_shape=(jax.ShapeDtypeStruct((nb, n1, n2, ch), BF16),) * 2,
        grid=(n1 // DFT_K1_STEP,), in_specs=[mspec, mspec, mspec, mspec, aspec, aspec, fspec, fspec],
        out_specs=(aspec, aspec),
        compiler_params=_cparams(("arbitrary",)), name="dft_inner_conv",
    )(mr, mi, tr, ti, ar, ai, fr, fi)


def _filter_scale(ss_ref):
    ss = jnp.sum(ss_ref[...], axis=0)
    return lax.rsqrt(ss[:, :HY_W] + ss[:, HY_W:] + EPS)


def _hy_final_kernel(cr_ref, ci_ref, br_ref, bi_ref, x0_ref, u_ref, ss_ref, bias_ref, o_ref, *, reps):
    y = (jnp.dot(cr_ref[...], br_ref[0], preferred_element_type=F32)
         + jnp.dot(ci_ref[...], bi_ref[0], preferred_element_type=F32))
    scale = jnp.tile(_filter_scale(ss_ref), (1, reps))
    bias = jnp.tile(bias_ref[...], (1, reps))
    u = u_ref[0].astype(F32)
    o_ref[0] = (x0_ref[0].astype(F32) * (y * scale + u * bias)).astype(BF16)


def _hy_final(cr, ci, br, bi, x0, u, ss, bias, tc):
    nb, n1, cols = br.shape
    n1h = cr.shape[0]
    nblk = ss.shape[0]
    cspec = pl.BlockSpec((n1h, n1), lambda b, j: (0, 0))
    bspec = pl.BlockSpec((1, n1, tc), lambda b, j: (b, 0, j))
    xspec = pl.BlockSpec((1, n1h, tc), lambda b, j: (b, 0, j))
    return pl.pallas_call(
        functools.partial(_hy_final_kernel, reps=tc // HY_W),
        out_shape=jax.ShapeDtypeStruct((nb, n1h, cols), BF16),
        grid=(nb, cols // tc),
        in_specs=[cspec, cspec, bspec, bspec, xspec, xspec,
                  pl.BlockSpec((nblk, 1, 2 * HY_W), lambda b, j: (0, 0, 0)),
                  pl.BlockSpec((1, HY_W), lambda b, j: (0, 0))],
        out_specs=xspec,
        compiler_params=_cparams(("arbitrary", "arbitrary")), name="hy_final",
    )(cr, ci, br, bi, x0, u, ss, bias)


def _hy_direct_kernel(f_ref, x0_ref, u_ref, ss_ref, bias_ref, o_ref, kk_ref, *, seq):
    kk_ref[0:seq, :] = f_ref[1].astype(F32)
    kk_ref[seq:2 * seq, :] = f_ref[0].astype(F32)
    u = u_ref[0].astype(F32)
    y = jnp.zeros((seq, HY_W), F32)
    for j in range(seq):
        y = y + kk_ref[seq - j:2 * seq - j, :] * u[j:j + 1, :]
    o_ref[0] = (x0_ref[0].astype(F32) * (y * _filter_scale(ss_ref) + u * bias_ref[...])).astype(BF16)


def _hy_direct(f, x0, u, ss, bias):
    nb, seq, _ = u.shape
    nblk = ss.shape[0]
    xspec = pl.BlockSpec((1, seq, HY_W), lambda b: (b, 0, 0))
    return pl.pallas_call(
        functools.partial(_hy_direct_kernel, seq=seq),
        out_shape=jax.ShapeDtypeStruct((nb, seq, HY_W), BF16),
        grid=(nb,),
        in_specs=[pl.BlockSpec((2, seq, HY_W), lambda b: (0, 0, 0)), xspec, xspec,
                  pl.BlockSpec((nblk, 1, 2 * HY_W), lambda b: (0, 0, 0)),
                  pl.BlockSpec((1, HY_W), lambda b: (0, 0))],
        out_specs=xspec,
        scratch_shapes=[pltpu.VMEM((2 * seq, HY_W), F32)],
        compiler_params=_cparams(("arbitrary",)), name="hy_direct",
    )(f, x0, u, ss, bias)


def _hy_embedding(seq):
    t = jnp.arange(seq, dtype=F32)
    t_unit = t / max(seq - 1, 1)
    bands = jnp.linspace(1e-4, HY_BANDS - 1, HY_BANDS, dtype=F32)
    ang = (2 * jnp.pi / seq) * t[:, None] * bands[None, :]
    return jnp.concatenate([t_unit[:, None], jnp.cos(ang), -jnp.sin(ang)], axis=-1)


def _dft_tables(seq):
    n = 2 * seq
    n2 = DFT_N2
    n1 = n // n2
    a = jnp.arange(n1, dtype=jnp.int32)
    th1 = (2 * jnp.pi / n1) * ((a[:, None] * a[None, :]) % n1).astype(F32)
    w1r, w1i = jnp.cos(th1), -jnp.sin(th1)
    b = jnp.arange(n2, dtype=jnp.int32)
    tha = (2 * jnp.pi / n) * (a[:, None] * b[None, :]).astype(F32)
    thb = (2 * jnp.pi / n2) * ((b[:, None] * b[None, :]) % n2).astype(F32)
    ar, ai = jnp.cos(tha), -jnp.sin(tha)
    br, bi = jnp.cos(thb), -jnp.sin(thb)
    mr = ar[:, None, :] * br[None] - ai[:, None, :] * bi[None]
    mi = ar[:, None, :] * bi[None] + ai[:, None, :] * br[None]
    tr = ar[:, :, None] * br[None] - ai[:, :, None] * bi[None]
    ti = -(ar[:, :, None] * bi[None] + ai[:, :, None] * br[None])
    return dict(
        w1r=w1r.astype(BF16), w1i=w1i.astype(BF16),
        mr=mr.astype(BF16), mi=mi.astype(BF16), tr=tr.astype(BF16), ti=ti.astype(BF16),
        cr=(w1r[:n1 // 2] / n).astype(BF16), ci=(w1i[:n1 // 2] / n).astype(BF16))


def _hyena_long(phy, lw, filt, seq, tabs):
    n = phy.shape[0]
    nb = n // seq
    n2 = DFT_N2
    n1 = 2 * seq // n2
    f, ss = filt
    x0, u = _hy_conv3(phy, lw['hy_conv_w'], lw['hy_conv_b'], seq, min(seq, HY_CONV_TILE))
    tc = min(n2 * HY_W, DFT_COL_TILE)
    far, fai = _dft_outer(tabs['w1r'], tabs['w1i'], f.reshape(1, n1, n2 * HY_W), tc)
    fr, fi = _dft_inner_fwd(tabs['mr'], tabs['mi'], far.reshape(1, n1, n2, HY_W), fai.reshape(1, n1, n2, HY_W))
    ar, ai = _dft_outer(tabs['w1r'][:, :n1 // 2], tabs['w1i'][:, :n1 // 2], u.reshape(nb, n1 // 2, n2 * HY_W), tc)
    br, bi = _dft_inner_conv(tabs['mr'], tabs['mi'], tabs['tr'], tabs['ti'],
                             ar.reshape(nb, n1, n2, HY_W), ai.reshape(nb, n1, n2, HY_W), fr, fi)
    out = _hy_final(tabs['cr'], tabs['ci'], br.reshape(nb, n1, n2 * HY_W), bi.reshape(nb, n1, n2 * HY_W),
                    x0.reshape(nb, n1 // 2, n2 * HY_W), u.reshape(nb, n1 // 2, n2 * HY_W), ss, lw['hy_bias'], tc)
    return out.reshape(n, HY_W)


def _hyena_short(phy, lw, filt, seq):
    n = phy.shape[0]
    nb = n // seq
    f, ss = filt
    x0, u = _hy_conv3(phy, lw['hy_conv_w'], lw['hy_conv_b'], seq, seq)
    out = _hy_direct(f, x0.reshape(nb, seq, HY_W), u.reshape(nb, seq, HY_W), ss, lw['hy_bias'])
    return out.reshape(n, HY_W)


def _pair_top2_sum(a, b, c, d):
    return jnp.maximum(jnp.maximum(jnp.maximum(a + b, a + c), jnp.maximum(a + d, b + c)),
                       jnp.maximum(b + d, c + d))


def _route(logits_t, rb):
    aff = jax.nn.sigmoid(logits_t)
    sel = aff + rb
    rows = [sel[e:e + 1, :] for e in range(N_EXPERTS)]
    affr = [aff[e:e + 1, :] for e in range(N_EXPERTS)]
    best, bidx = None, None
    for g in range(N_GROUPS):
        gs = _pair_top2_sum(*rows[g * EXP_PER_GROUP:(g + 1) * EXP_PER_GROUP])
        if g == 0:
            best, bidx = gs, jnp.zeros(gs.shape, jnp.int32)
        else:
            upd = gs > best
            bidx = jnp.where(upd, g, bidx)
            best = jnp.where(upd, gs, best)
    vals = [jnp.where(bidx == e // EXP_PER_GROUP, rows[e], -jnp.inf) for e in range(N_EXPERTS)]
    m1, i1 = vals[0], jnp.zeros(best.shape, jnp.int32)
    for e in range(1, N_EXPERTS):
        upd = vals[e] > m1
        i1 = jnp.where(upd, e, i1)
        m1 = jnp.where(upd, vals[e], m1)
    m2, i2 = jnp.full(best.shape, -jnp.inf, F32), jnp.zeros(best.shape, jnp.int32)
    for e in range(N_EXPERTS):
        cand = jnp.where(i1 == e, -jnp.inf, vals[e])
        upd = cand > m2
        i2 = jnp.where(upd, e, i2)
        m2 = jnp.where(upd, cand, m2)
    a1 = sum(jnp.where(i1 == e, affr[e], 0.0) for e in range(N_EXPERTS))
    a2 = sum(jnp.where(i2 == e, affr[e], 0.0) for e in range(N_EXPERTS))
    inv = 1.0 / (a1 + a2)
    return jnp.concatenate([i1, i2], axis=0), jnp.concatenate([a1 * inv, a2 * inv], axis=0)


def _merge_kernel(a_ref, b_ref, m_ref, gt_ref, x_ref, gate_ref, shift_ref, scale_ref, g2_ref,
                  wpa_ref, wpb_ref, wpc_ref, wout_ref, rwt_ref, rb_ref, xo_ref, h2_ref, idx_ref, wts_ref):
    y = gt_ref[:, 0:D_MODEL].astype(F32) * jnp.dot(a_ref[...], wpa_ref[...], preferred_element_type=F32)
    y = y + gt_ref[:, D_MODEL:2 * D_MODEL].astype(F32) * jnp.dot(b_ref[...], wpb_ref[...],
                                                                 preferred_element_type=F32)
    y = y + gt_ref[:, 2 * D_MODEL:].astype(F32) * jnp.dot(m_ref[...], wpc_ref[...], preferred_element_type=F32)
    y2 = jnp.dot(y.astype(BF16), wout_ref[...], preferred_element_type=F32)
    xn = x_ref[...] + gate_ref[0] * y2
    xo_ref[...] = xn
    h2 = _rms(xn) * g2_ref[...]
    h2 = h2 * (1.0 + scale_ref[0]) + shift_ref[0]
    h2_ref[...] = h2
    logits_t = lax.dot_general(rwt_ref[...], h2, (((1,), (1,)), ((), ())), preferred_element_type=F32,
                               precision=HIGHEST)
    idx_ref[...], wts_ref[...] = _route(logits_t, rb_ref[...])


def _merge(a, b, m, gt, x, gate, shift, scale, lw, rw_t, rb, seq, tm):
    n = x.shape[0]
    tiles_per_seq = seq // tm
    const = lambda i: (0, 0)
    row = lambda w: pl.BlockSpec((tm, w), lambda i: (i, 0))
    vec = pl.BlockSpec((1, 1, D_MODEL), lambda i: (i // tiles_per_seq, 0, 0))
    return pl.pallas_call(
        _merge_kernel,
        out_shape=(jax.ShapeDtypeStruct((n, D_MODEL), F32), jax.ShapeDtypeStruct((n, D_MODEL), F32),
                   jax.ShapeDtypeStruct((TOP_K, n), jnp.int32), jax.ShapeDtypeStruct((TOP_K, n), F32)),
        grid=(n // tm,),
        in_specs=[row(HEADS * VDIM), row(HY_W), row(GM_W), row(3 * D_MODEL), row(D_MODEL), vec, vec, vec,
                  pl.BlockSpec((1, D_MODEL), const),
                  pl.BlockSpec((HEADS * VDIM, D_MODEL), const), pl.BlockSpec((HY_W, D_MODEL), const),
                  pl.BlockSpec((GM_W, D_MODEL), const), pl.BlockSpec((D_MODEL, D_MODEL), const),
                  pl.BlockSpec((N_EXPERTS, D_MODEL), const), pl.BlockSpec((N_EXPERTS, 1), const)],
        out_specs=(row(D_MODEL), row(D_MODEL), pl.BlockSpec((TOP_K, tm), lambda i: (0, i)),
                   pl.BlockSpec((TOP_K, tm), lambda i: (0, i))),
        compiler_params=_cparams(("arbitrary",)), name="merge",
    )(a, b, m, gt, x, gate, shift, scale, lw['g2'], lw['w_pa'], lw['w_pb'], lw['w_pc'], lw['w_out'], rw_t, rb)


def _rows_copy(src_hbm, dst, sem, idx_ref, rows):
    return [pltpu.make_async_copy(src_hbm.at[pl.ds(idx_ref[0, 0, r], 1)], dst.at[pl.ds(r, 1)], sem)
            for r in range(rows)]


def _rows_wait(src_hbm, dst, sem, rows):
    pltpu.make_async_copy(src_hbm.at[pl.ds(0, rows)], dst, sem).wait()


def _gather_pipeline(src_hbm, cur_ref, nxt_ref, buf, sem, rows):
    i = pl.program_id(0)
    slot = i % 2

    @pl.when(i == 0)
    def _():
        for cp in _rows_copy(src_hbm, buf.at[0], sem.at[0], cur_ref, rows):
            cp.start()

    _rows_wait(src_hbm, buf.at[slot], sem.at[slot], rows)
    for cp in _rows_copy(src_hbm, buf.at[1 - slot], sem.at[1 - slot], nxt_ref, rows):
        cp.start()
    return slot


def _gather_drain(src_hbm, buf, sem, rows):
    i = pl.program_id(0)

    @pl.when(i == pl.num_programs(0) - 1)
    def _():
        _rows_wait(src_hbm, buf.at[1 - i % 2], sem.at[1 - i % 2], rows)


def _moe_dispatch_kernel(pos_ref, h_ref, xs_in_hbm, xs_hbm, sem, *, tc):
    del xs_in_hbm
    for r in range(tc):
        pltpu.make_async_copy(h_ref.at[pl.ds(r, 1)], xs_hbm.at[pl.ds(pos_ref[0, 0, r], 1)], sem).start()
    pltpu.make_async_copy(h_ref, xs_hbm.at[pl.ds(0, tc)], sem).wait()


def _moe_dispatch(pos_tiles, h2, rows, tc):
    n = h2.shape[0]
    return pl.pallas_call(
        functools.partial(_moe_dispatch_kernel, tc=tc),
        out_shape=jax.ShapeDtypeStruct((rows, D_MODEL), F32),
        grid=(n // tc,),
        in_specs=[pl.BlockSpec((1, 1, tc), lambda i: (i, 0, 0), memory_space=pltpu.SMEM),
                  pl.BlockSpec((tc, D_MODEL), lambda i: (i, 0)),
                  pl.BlockSpec(memory_space=pl.ANY)],
        out_specs=pl.BlockSpec(memory_space=pl.ANY),
        scratch_shapes=[pltpu.SemaphoreType.DMA],
        input_output_aliases={2: 0},
        compiler_params=_cparams(("arbitrary",)), name="moe_dispatch",
    )(pos_tiles, h2, jnp.zeros((rows, D_MODEL), F32))


def _moe_ffn_kernel(ea_ref, eb_ref, x_ref, wgua_ref, wda_ref, wgub_ref, wdb_ref, y_ref):
    del ea_ref, eb_ref
    xb = x_ref[...].astype(BF16)
    for j, (wgu_ref, wd_ref) in enumerate(((wgua_ref, wda_ref), (wgub_ref, wdb_ref))):
        gu = jnp.dot(xb, wgu_ref[0], preferred_element_type=F32)
        g = gu[:, :D_EXPERT]
        hid = (g * _sigmoid(g) * gu[:, D_EXPERT:]).astype(BF16)
        y_ref[:, j * D_MODEL:(j + 1) * D_MODEL] = jnp.dot(hid, wd_ref[0], preferred_element_type=F32)


def _moe_ffn(blk_ea, blk_eb, x_sorted, lw):
    nblk = blk_ea.shape[0]
    wgu_spec = lambda sel: pl.BlockSpec((1, D_MODEL, 2 * D_EXPERT), lambda i, ea, eb: (sel(ea, eb)[i], 0, 0))
    wd_spec = lambda sel: pl.BlockSpec((1, D_EXPERT, D_MODEL), lambda i, ea, eb: (sel(ea, eb)[i], 0, 0))
    first, second = (lambda ea, eb: ea), (lambda ea, eb: eb)
    return pl.pallas_call(
        _moe_ffn_kernel,
        out_shape=jax.ShapeDtypeStruct((nblk * MOE_BM, TOP_K * D_MODEL), F32),
        grid_spec=pltpu.PrefetchScalarGridSpec(
            num_scalar_prefetch=2, grid=(nblk,),
            in_specs=[pl.BlockSpec((MOE_BM, D_MODEL), lambda i, ea, eb: (i, 0)),
                      wgu_spec(first), wd_spec(first), wgu_spec(second), wd_spec(second)],
            out_specs=pl.BlockSpec((MOE_BM, TOP_K * D_MODEL), lambda i, ea, eb: (i, 0))),
        compiler_params=_cparams(("arbitrary",)), name="moe_ffn",
    )(blk_ea, blk_eb, x_sorted, lw['w_gu'], lw['w_d'], lw['w_gu'], lw['w_d'])


def _moe_combine_kernel(cur_ref, nxt_ref, y_hbm, x_ref, w_ref, gate_ref, o_ref, buf, sem, *, tc):
    slot = _gather_pipeline(y_hbm, cur_ref, nxt_ref, buf, sem, tc)
    w = w_ref[...]
    y = w[:, 0:1] * buf[slot, :, 0:D_MODEL] + w[:, 1:2] * buf[slot, :, D_MODEL:]
    o_ref[...] = x_ref[...] + gate_ref[0] * y
    _gather_drain(y_hbm, buf, sem, tc)


def _moe_combine(pos_tiles, y_sorted, x, w_cols, gate, seq, tc):
    n = x.shape[0]
    tiles_per_seq = seq // tc
    idx_spec = lambda off: pl.BlockSpec((1, 1, tc), lambda i: (i + off, 0, 0), memory_space=pltpu.SMEM)
    return pl.pallas_call(
        functools.partial(_moe_combine_kernel, tc=tc),
        out_shape=jax.ShapeDtypeStruct((n, D_MODEL), F32),
        grid=(n // tc,),
        in_specs=[idx_spec(0), idx_spec(1), pl.BlockSpec(memory_space=pl.ANY),
                  pl.BlockSpec((tc, D_MODEL), lambda i: (i, 0)),
                  pl.BlockSpec((tc, TOP_K), lambda i: (i, 0)),
                  pl.BlockSpec((1, 1, D_MODEL), lambda i: (i // tiles_per_seq, 0, 0))],
        out_specs=pl.BlockSpec((tc, D_MODEL), lambda i: (i, 0)),
        scratch_shapes=[pltpu.VMEM((2, tc, TOP_K * D_MODEL), F32), pltpu.SemaphoreType.DMA((2,))],
        compiler_params=_cparams(("arbitrary",)), name="moe_combine",
    )(pos_tiles, pos_tiles, y_sorted, x, w_cols, gate)


def _cumsum_rows(onehot):
    n, c = onehot.shape
    blk = math.gcd(n, 256)
    x = onehot.astype(F32).reshape(n // blk, blk, c)
    tri = (jnp.arange(blk)[:, None] >= jnp.arange(blk)[None, :]).astype(F32)
    within = jnp.einsum('ij,bjc->bic', tri, x, precision=HIGHEST)
    nb = n // blk
    before = (jnp.arange(nb)[:, None] > jnp.arange(nb)[None, :]).astype(F32)
    offset = jnp.dot(before, within[:, -1, :], precision=HIGHEST)
    return (within + offset[:, None, :]).reshape(n, c).astype(jnp.int32)


def _pair_tables():
    pairs = [(a, b) for a in range(EXP_PER_GROUP) for b in range(a + 1, EXP_PER_GROUP)]
    ea = [g * EXP_PER_GROUP + a for g in range(N_GROUPS) for a, _ in pairs]
    eb = [g * EXP_PER_GROUP + b for g in range(N_GROUPS) for _, b in pairs]
    return jnp.array(ea, jnp.int32), jnp.array(eb, jnp.int32)


def _moe(h2, idx, wts, x, gate, lw, seq):
    n = x.shape[0]
    tab_a, tab_b = _pair_tables()
    n_cls = tab_a.shape[0]
    nblk = n // MOE_BM + n_cls
    swap = idx[0] > idx[1]
    e_lo, e_hi = jnp.minimum(idx[0], idx[1]), jnp.maximum(idx[0], idx[1])
    w_cols = jnp.stack([jnp.where(swap, wts[1], wts[0]), jnp.where(swap, wts[0], wts[1])], axis=1)
    onehot = ((e_lo[:, None] == tab_a[None, :]) & (e_hi[:, None] == tab_b[None, :])).astype(jnp.int32)
    csum = _cumsum_rows(onehot)
    counts = csum[-1]
    rank = jnp.sum(csum * onehot, axis=1) - 1
    seg_len = (counts + MOE_BM - 1) // MOE_BM * MOE_BM
    seg_end = jnp.cumsum(seg_len)
    pos = jnp.sum(onehot * (seg_end - seg_len)[None, :], axis=1) + rank
    blk_start = jnp.arange(nblk, dtype=jnp.int32) * MOE_BM
    blk_cls = jnp.minimum(jnp.sum((seg_end[None, :] <= blk_start[:, None]).astype(jnp.int32), axis=1), n_cls - 1)
    td = min(n, MOE_DISPATCH_TILE)
    x_sorted = _moe_dispatch(pos.reshape(n // td, 1, td), h2, nblk * MOE_BM, td)
    y_sorted = _moe_ffn(tab_a[blk_cls], tab_b[blk_cls], x_sorted, lw)
    tc = min(seq, MOE_COMBINE_TILE)
    pos_next = jnp.concatenate([pos.reshape(n // tc, 1, tc), jnp.zeros((1, 1, tc), jnp.int32)], axis=0)
    return _moe_combine(pos_next, y_sorted, x, w_cols, gate, seq, tc)


def _head_perm():
    rope_idx = list(range(NOPE, QK, 2)) + list(range(NOPE + 1, QK, 2))
    return jnp.array(list(range(NOPE)) + rope_idx, dtype=jnp.int32)


def _block_diag(a, b):
    return jnp.concatenate([jnp.pad(a, ((0, 0), (0, b.shape[1]))), jnp.pad(b, ((0, 0), (a.shape[1], 0)))], axis=0)


def _layer_weights(p, l):
    perm = _head_perm()
    row = lambda v: v.reshape(1, -1)
    w_in = p['w_in'][l].astype(BF16)
    kr_perm = jnp.array(list(range(0, ROPE, 2)) + list(range(1, ROPE, 2)), dtype=jnp.int32)
    w_kr = jnp.pad(w_in[:, OFF_KR:OFF_HY][:, kr_perm], ((0, 0), (0, HEAD_PAD - ROPE)))
    w_in2 = jnp.concatenate([w_in[:, OFF_Q:OFF_KR], w_kr, w_in[:, OFF_HY:]], axis=1)
    w_uq = p['w_uq'][l].reshape(Q_RANK, HEADS, QK)[:, :, perm]
    w_uq = jnp.pad(w_uq, ((0, 0), (0, 0), (0, HEAD_PAD - QK))).reshape(Q_RANK, HEADS * HEAD_PAD).astype(BF16)
    w_ukv = p['w_ukv'][l].reshape(KV_RANK, HEADS, NOPE + VDIM)
    w_k = jnp.pad(w_ukv[:, :, :NOPE], ((0, 0), (0, 0), (0, HEAD_PAD - NOPE)))
    w_k = w_k.reshape(KV_RANK, HEADS * HEAD_PAD).astype(BF16)
    w_vt = w_ukv[:, :, NOPE:].reshape(KV_RANK, HEADS * VDIM).T.astype(BF16)
    pad_gain = lambda g: jnp.pad(g[perm], (0, HEAD_PAD - QK)).reshape(1, HEAD_PAD)
    gm_b = jnp.repeat(p['gm_bs'][l].T, GM_W // GM_GROUPS, axis=1)
    bound = 1.02 * QSCALE * QK * jnp.max(jnp.abs(p['g_qn'][l])) * jnp.max(jnp.abs(p['g_kn'][l]))
    pad_lane = jnp.arange(HEAD_PAD) == QK
    return dict(
        attn_bound=bound,
        q_pad=jnp.where(pad_lane, -bound, 0.0).reshape(1, HEAD_PAD).astype(F32),
        k_pad=jnp.where(pad_lane, 1.0, 0.0).reshape(1, HEAD_PAD).astype(F32),
        g1=row(p['norm1_g'][l]), g2=row(p['norm2_g'][l]), w_in=w_in2,
        g_qa=row(p['g_qa'][l]), w_uq=w_uq, g_qn=pad_gain(p['g_qn'][l]),
        g_kva=row(p['g_kva'][l]), w_k=w_k, w_vt=w_vt, g_kn=pad_gain(p['g_kn'][l]),
        gm_g=row(p['gm_norm_g'][l]), gm_w=p['gm_ws'][l].astype(BF16), gm_b=gm_b,
        hy_conv_w=p['hy_conv_w'][l], hy_conv_b=row(p['hy_conv_b'][l]),
        hy_w1p=_block_diag(p['hy_w1'][l], p['hy_w1'][l]), hy_b1p=row(jnp.tile(p['hy_b1'][l], 2)),
        hy_w2p=_block_diag(p['hy_w2'][l], p['hy_w2'][l]), hy_b2p=row(jnp.tile(p['hy_b2'][l], 2)),
        hy_w3p=_block_diag(p['hy_w3'][l][:, :HY_W], p['hy_w3'][l][:, HY_W:]), hy_b3=row(p['hy_b3'][l]),
        hy_freqp=row(jnp.tile(p['hy_freq'][l], 2)),
        hy_decay=p['hy_decay'][l].reshape(1, 2 * HY_W), hy_bias=row(p['hy_bias'][l]),
        w_pa=p['w_pa'][l].astype(BF16), w_pb=p['w_pb'][l].astype(BF16), w_pc=p['w_pc'][l].astype(BF16),
        w_out=p['w_out'][l].astype(BF16),
        w_gu=jnp.concatenate([p['moe_w_gate'][l], p['moe_w_up'][l]], axis=-1).astype(BF16),
        w_d=p['moe_w_down'][l].astype(BF16))


def _rope_tables(seq):
    rows = seq // GRID_W
    row = jnp.repeat(jnp.arange(rows, dtype=F32), GRID_W)
    col = jnp.tile(jnp.arange(GRID_W, dtype=F32), rows)
    n_freq = ROPE // 4
    inv = ROPE_THETA ** (-jnp.arange(n_freq, dtype=F32) / n_freq)
    ang = jnp.concatenate([row[:, None] * inv, col[:, None] * inv], axis=-1)
    c, s = jnp.cos(ang), jnp.sin(ang)
    z = lambda w: jnp.zeros((seq, w), F32)
    rc = jnp.concatenate([jnp.ones((seq, NOPE), F32), c, c, z(HEAD_PAD - QK)], axis=1)
    rs1 = jnp.concatenate([z(NOPE), z(ROPE // 2), s, z(HEAD_PAD - QK)], axis=1)
    rs2 = jnp.concatenate([z(NOPE), -s, z(ROPE // 2), z(HEAD_PAD - QK)], axis=1)
    return rc, rs1, rs2


def _mixer_and_ffn(x, mods, lw, rw_t, rb, seq, tm, rope_tabs, ctx_kv, filt, dft_tabs):
    shift1, scale1, gate1, shift2, scale2, gate2 = mods
    q, k, vt, m, phy, gt = _premix(x, shift1, scale1, lw, rope_tabs, seq, tm)
    a = _attention(q, k, vt, ctx_kv, seq, min(seq, ATTN_TQ), lw['attn_bound'])
    if dft_tabs is not None:
        b = _hyena_long(phy, lw, filt, seq, dft_tabs)
    else:
        b = _hyena_short(phy, lw, filt, seq)
    xn, h2, idx, wts = _merge(a, b, m, gt, x, gate1, shift2, scale2, lw, rw_t, rb, seq, min(seq, MERGE_TILE))
    return _moe(h2, idx, wts, xn, gate2, lw, seq), k, vt


def _forward(p):
    x, ctx = p['x'], p['ctx']
    batch, seq, _ = x.shape
    lc = ctx.shape[1]
    depth = p['w_mod'].shape[0]

    cvecs = jnp.concatenate([p['c'], p['c_ctx'][None], jnp.zeros((8 - batch - 1, D_MODEL), F32)], axis=0)
    mod_all = _modvec(cvecs, p['w_mod'], p['b_mod'])
    rw_t = p['router_w'].T
    rb = p['router_b'].reshape(N_EXPERTS, 1)
    rope_tabs = _rope_tables(seq)
    dft_tabs = _dft_tables(seq)
    z_lat, z_ctx = _hy_embedding(seq), _hy_embedding(lc)

    xl = x.reshape(batch * seq, D_MODEL)
    xc = ctx.reshape(batch * lc, D_MODEL)
    tm_lat = min(seq, TOKEN_TILE)
    for l in range(depth):
        lw = _layer_weights(p, l)
        mod = mod_all[l].reshape(8, N_MOD, D_MODEL)
        mods_lat = [mod[:batch, j].reshape(batch, 1, D_MODEL) for j in range(N_MOD)]
        mods_ctx = [jnp.broadcast_to(mod[batch, j].reshape(1, 1, D_MODEL), (batch, 1, D_MODEL))
                    for j in range(N_MOD)]
        if l == depth - 1:
            _, k_c, vt_c, _, _, _ = _premix(xc, mods_ctx[0], mods_ctx[1], lw, None, lc, lc)
        else:
            xc, k_c, vt_c = _mixer_and_ffn(xc, mods_ctx, lw, rw_t, rb, lc, lc, None, None,
                                           _hy_filters(z_ctx, lw, lc), None)
        filt = _hy_filters(z_lat, lw, min(seq, HY_FILTER_TILE))
        xl, _, _ = _mixer_and_ffn(xl, mods_lat, lw, rw_t, rb, seq, tm_lat, rope_tabs, (k_c, vt_c), filt, dft_tabs)
    return xl.reshape(batch, seq, D_MODEL)


def kernel(x, c, ctx, c_ctx, w_mod, b_mod, norm1_g, norm2_g, w_in, g_qa, w_uq, g_kva, w_ukv, g_qn, g_kn,
           hy_conv_w, hy_conv_b, hy_w1, hy_b1, hy_w2, hy_b2, hy_w3, hy_b3, hy_freq, hy_decay, hy_bias,
           gm_norm_g, gm_ws, gm_bs, w_pa, w_pb, w_pc, w_out, router_w, router_b,
           moe_w_gate, moe_w_up, moe_w_down):
    return _forward(dict(
        x=x, c=c, ctx=ctx, c_ctx=c_ctx, w_mod=w_mod, b_mod=b_mod, norm1_g=norm1_g, norm2_g=norm2_g, w_in=w_in,
        g_qa=g_qa, w_uq=w_uq, g_kva=g_kva, w_ukv=w_ukv, g_qn=g_qn, g_kn=g_kn, hy_conv_w=hy_conv_w,
        hy_conv_b=hy_conv_b, hy_w1=hy_w1, hy_b1=hy_b1, hy_w2=hy_w2, hy_b2=hy_b2, hy_w3=hy_w3, hy_b3=hy_b3,
        hy_freq=hy_freq, hy_decay=hy_decay, hy_bias=hy_bias, gm_norm_g=gm_norm_g, gm_ws=gm_ws, gm_bs=gm_bs,
        w_pa=w_pa, w_pb=w_pb, w_pc=w_pc, w_out=w_out, router_w=router_w, router_b=router_b,
        moe_w_gate=moe_w_gate, moe_w_up=moe_w_up, moe_w_down=moe_w_down))
```

```python
import functools
import math

import jax
import jax.numpy as jnp
from jax import lax
from jax.experimental import pallas as pl
from jax.experimental.pallas import tpu as pltpu

F32 = jnp.float32
BF16 = jnp.bfloat16
HIGHEST = lax.Precision.HIGHEST

D_MODEL = 1024
GRID_W = 64
EPS = 1e-6
N_MOD = 6

HEADS = 8
Q_RANK = 384
KV_RANK = 256
NOPE = 64
ROPE = 32
QK = NOPE + ROPE
VDIM = 64
HEAD_PAD = 128
ROPE_THETA = 10000.0
V_ROWS = 80

HY_W = 256
HY_BANDS = 16
GM_W = 256
GM_CHUNK = 128
GM_GROUPS = 4

OFF_Q = 0
OFF_KV = OFF_Q + Q_RANK
OFF_KR = OFF_KV + KV_RANK
OFF_HY = OFF_KR + ROPE
OFF_GM = OFF_HY + 3 * HY_W
OFF_GT = OFF_GM + 2 * GM_W

P_Q = 0
P_KV = P_Q + Q_RANK
P_KR = P_KV + KV_RANK
P_HY = P_KR + HEAD_PAD
P_GM = P_HY + 3 * HY_W
P_GT = P_GM + 2 * GM_W
P_W = P_GT + 3 * D_MODEL

N_EXPERTS = 16
N_GROUPS = 4
EXP_PER_GROUP = 4
TOP_K = 2
D_EXPERT = 512
MOE_BM = 256
MOE_DISPATCH_TILE = 1024
MOE_COMBINE_TILE = 512

DFT_N2 = 256
DFT_K1_STEP = 4
DFT_COL_TILE = 8192

TOKEN_TILE = 512
MERGE_TILE = 1024
HY_CONV_TILE = 1024
HY_FILTER_TILE = 2048

VMEM_LIMIT = 56 * 1024 * 1024
ATTN_TQ = 512
ATTN_TK = 2048
ATTN_BOUND_MAX = 50.0
NEG_BIG = -1e30
LOG2E = 1.4426950408889634
QSCALE = QK ** -0.5 * LOG2E


def _cparams(sem):
    return pltpu.CompilerParams(dimension_semantics=sem, vmem_limit_bytes=VMEM_LIMIT)


def _rms(x):
    return x * lax.rsqrt(jnp.mean(x * x, axis=-1, keepdims=True) + EPS)


def _sigmoid(x):
    return 0.5 * jnp.tanh(0.5 * x) + 0.5


def _nt_dot(a, b):
    return lax.dot_general(a, b, (((1,), (1,)), ((), ())), preferred_element_type=F32)


def _modvec_kernel(c_ref, w_ref, b_ref, o_ref):
    cv = c_ref[...]
    s = cv * jax.nn.sigmoid(cv)
    o_ref[0] = jnp.dot(s, w_ref[0], preferred_element_type=F32, precision=HIGHEST) + b_ref[0]


def _modvec(cvecs, w_mod, b_mod):
    depth = w_mod.shape[0]
    tn = 1536
    return pl.pallas_call(
        _modvec_kernel,
        out_shape=jax.ShapeDtypeStruct((depth, 8, N_MOD * D_MODEL), F32),
        grid=(depth, N_MOD * D_MODEL // tn),
        in_specs=[pl.BlockSpec((8, D_MODEL), lambda l, j: (0, 0)),
                  pl.BlockSpec((1, D_MODEL, tn), lambda l, j: (l, 0, j)),
                  pl.BlockSpec((1, 1, tn), lambda l, j: (l, 0, j))],
        out_specs=pl.BlockSpec((1, 8, tn), lambda l, j: (l, 0, j)),
        compiler_params=_cparams(("arbitrary", "arbitrary")),
        name="modvec",
    )(cvecs, w_mod, b_mod.reshape(depth, 1, N_MOD * D_MODEL))


def _head_norm_rope(xh, gain, rope):
    ms = jnp.sum(xh * xh, axis=-1, keepdims=True) * (1.0 / QK)
    xh = xh * lax.rsqrt(ms + EPS) * gain
    if rope is not None:
        rc, rs1, rs2 = rope
        xh = xh * rc + pltpu.roll(xh, ROPE // 2, 1) * rs1 + pltpu.roll(xh, HEAD_PAD - ROPE // 2, 1) * rs2
    return xh


def _premix_kernel(*refs, use_rope, tm):
    if use_rope:
        (x_ref, shift_ref, scale_ref, g1_ref, win_ref, gqa_ref, wuq_ref, gqn_ref, gkva_ref, wk_ref, wvt_ref,
         gkn_ref, gmg_ref, gmw_ref, gmb_ref, qpad_ref, kpad_ref, rc_ref, rs1_ref, rs2_ref,
         q_ref, k_ref, vt_ref, m_ref, hy_ref, gt_ref) = refs
        rope = (rc_ref[...], rs1_ref[...], rs2_ref[...])
    else:
        (x_ref, shift_ref, scale_ref, g1_ref, win_ref, gqa_ref, wuq_ref, gqn_ref, gkva_ref, wk_ref, wvt_ref,
         gkn_ref, gmg_ref, gmw_ref, gmb_ref, qpad_ref, kpad_ref,
         q_ref, k_ref, vt_ref, m_ref, hy_ref, gt_ref) = refs
        rope = None

    x = x_ref[...]
    h = _rms(x) * g1_ref[...]
    h = h * (1.0 + scale_ref[0]) + shift_ref[0]
    hb = h.astype(BF16)

    def proj(lo, width):
        return jnp.dot(hb, win_ref[:, lo:lo + width], preferred_element_type=F32)

    qa = (_rms(proj(P_Q, Q_RANK)) * gqa_ref[...]).astype(BF16)
    q = jnp.dot(qa, wuq_ref[...], preferred_element_type=F32)
    for hh in range(HEADS):
        qh = _head_norm_rope(q[:, hh * HEAD_PAD:(hh + 1) * HEAD_PAD], gqn_ref[...], rope)
        q_ref[0, hh * HEAD_PAD:(hh + 1) * HEAD_PAD, :] = (qh * QSCALE + qpad_ref[...]).T.astype(BF16)

    kva = (_rms(proj(P_KV, KV_RANK)) * gkva_ref[...]).astype(BF16)
    kr = pltpu.roll(proj(P_KR, HEAD_PAD), NOPE, 1)
    kn = jnp.dot(kva, wk_ref[...], preferred_element_type=F32)
    for hh in range(HEADS):
        kh = _head_norm_rope(kn[:, hh * HEAD_PAD:(hh + 1) * HEAD_PAD] + kr, gkn_ref[...], rope)
        k_ref[:, hh * HEAD_PAD:(hh + 1) * HEAD_PAD] = (kh + kpad_ref[...]).astype(BF16)
    vt = _nt_dot(wvt_ref[...], kva)
    row = lax.broadcasted_iota(jnp.int32, (V_ROWS - VDIM, tm), 0)
    ones_rows = jnp.where(row == 0, 1.0, 0.0).astype(BF16)
    for hh in range(HEADS):
        vt_ref[0, 0, hh * V_ROWS:hh * V_ROWS + VDIM, :] = vt[hh * VDIM:(hh + 1) * VDIM].astype(BF16)
        vt_ref[0, 0, hh * V_ROWS + VDIM:(hh + 1) * V_ROWS, :] = ones_rows

    gg = jax.nn.gelu(proj(P_GM, 2 * GM_W), approximate=True)
    gu = gg[:, :GM_W]
    gv = (_rms(gg[:, GM_W:]) * gmg_ref[...]).astype(BF16)
    grp = lax.broadcasted_iota(jnp.int32, (GM_CHUNK, GM_W), 1) // (GM_W // GM_GROUPS)
    for ci in range(tm // GM_CHUNK):
        vc = gv[ci * GM_CHUNK:(ci + 1) * GM_CHUNK]
        s = jnp.zeros((GM_CHUNK, GM_W), F32)
        for g in range(GM_GROUPS):
            sg = jnp.dot(gmw_ref[g], vc, preferred_element_type=F32)
            s = jnp.where(grp == g, sg, s)
        m_ref[ci * GM_CHUNK:(ci + 1) * GM_CHUNK, :] = (
            gu[ci * GM_CHUNK:(ci + 1) * GM_CHUNK] * (s + gmb_ref[...])).astype(BF16)

    hy_ref[...] = proj(P_HY, 3 * HY_W).astype(BF16)
    for j in range(3):
        gt_ref[:, j * D_MODEL:(j + 1) * D_MODEL] = _sigmoid(proj(P_GT + j * D_MODEL, D_MODEL)).astype(BF16)


def _premix(x, shift, scale, lw, rope_tabs, seq, tm):
    n = x.shape[0]
    tiles_per_seq = seq // tm
    batch = n // seq
    use_rope = rope_tabs is not None
    const = lambda i: (0, 0)
    in_specs = [
        pl.BlockSpec((tm, D_MODEL), lambda i: (i, 0)),
        pl.BlockSpec((1, 1, D_MODEL), lambda i: (i // tiles_per_seq, 0, 0)),
        pl.BlockSpec((1, 1, D_MODEL), lambda i: (i // tiles_per_seq, 0, 0)),
        pl.BlockSpec((1, D_MODEL), const),
        pl.BlockSpec((D_MODEL, P_W), const, pipeline_mode=pl.Buffered(1)),
        pl.BlockSpec((1, Q_RANK), const),
        pl.BlockSpec((Q_RANK, HEADS * HEAD_PAD), const),
        pl.BlockSpec((1, HEAD_PAD), const),
        pl.BlockSpec((1, KV_RANK), const),
        pl.BlockSpec((KV_RANK, HEADS * HEAD_PAD), const),
        pl.BlockSpec((HEADS * VDIM, KV_RANK), const),
        pl.BlockSpec((1, HEAD_PAD), const),
        pl.BlockSpec((1, GM_W), const),
        pl.BlockSpec((GM_GROUPS, GM_CHUNK, GM_CHUNK), lambda i: (0, 0, 0)),
        pl.BlockSpec((GM_CHUNK, GM_W), const),
        pl.BlockSpec((1, HEAD_PAD), const),
        pl.BlockSpec((1, HEAD_PAD), const),
    ]
    args = [x, shift, scale, lw['g1'], lw['w_in'], lw['g_qa'], lw['w_uq'], lw['g_qn'], lw['g_kva'], lw['w_k'],
            lw['w_vt'], lw['g_kn'], lw['gm_g'], lw['gm_w'], lw['gm_b'], lw['q_pad'], lw['k_pad']]
    tkv = min(seq, ATTN_TK)
    sub = tkv // tm
    if use_rope:
        in_specs += [pl.BlockSpec((tm, HEAD_PAD), lambda i: (i % tiles_per_seq, 0))] * 3
        args += list(rope_tabs)
    out_shape = (
        jax.ShapeDtypeStruct((n // tm, HEADS * HEAD_PAD, tm), BF16),
        jax.ShapeDtypeStruct((n, HEADS * HEAD_PAD), BF16),
        jax.ShapeDtypeStruct((batch, seq // tkv, HEADS * V_ROWS, tkv), BF16),
        jax.ShapeDtypeStruct((n, GM_W), BF16),
        jax.ShapeDtypeStruct((n, 3 * HY_W), BF16),
        jax.ShapeDtypeStruct((n, 3 * D_MODEL), BF16),
    )
    out_specs = (
        pl.BlockSpec((1, HEADS * HEAD_PAD, tm), lambda i: (i, 0, 0)),
        pl.BlockSpec((tm, HEADS * HEAD_PAD), lambda i: (i, 0)),
        pl.BlockSpec((1, 1, HEADS * V_ROWS, tm),
                     lambda i: (i // tiles_per_seq, (i % tiles_per_seq) // sub, 0, (i % tiles_per_seq) % sub)),
        pl.BlockSpec((tm, GM_W), lambda i: (i, 0)),
        pl.BlockSpec((tm, 3 * HY_W), lambda i: (i, 0)),
        pl.BlockSpec((tm, 3 * D_MODEL), lambda i: (i, 0)),
    )
    return pl.pallas_call(
        functools.partial(_premix_kernel, use_rope=use_rope, tm=tm),
        out_shape=out_shape, grid=(n // tm,), in_specs=in_specs, out_specs=out_specs,
        compiler_params=_cparams(("arbitrary",)), name="premix",
    )(*args)


def _attn_kernel(*refs, n_chunks, has_ctx, tq):
    if has_ctx:
        q_ref, k_ref, vt_ref, kc_ref, vtc_ref, o_ref = refs
    else:
        q_ref, k_ref, vt_ref, o_ref = refs
    tk = vt_ref.shape[-1]

    def step(hh, kc, vtc, m, acc):
        s = jnp.dot(kc, q_ref[0, hh * HEAD_PAD:(hh + 1) * HEAD_PAD, :], preferred_element_type=F32)
        m_new = jnp.maximum(m, jnp.max(s, axis=0, keepdims=True))
        p = jnp.exp2(s - m_new).astype(BF16)
        alpha = jnp.exp2(m - m_new)
        return m_new, acc * alpha + jnp.dot(vtc, p, preferred_element_type=F32)

    def body(i, carry):
        out = []
        for hh in range(2):
            m, acc = carry[hh]
            start = pl.multiple_of(i * tk, tk)
            kc = k_ref[pl.ds(start, tk), hh * HEAD_PAD:(hh + 1) * HEAD_PAD]
            vtc = vt_ref[0, i, hh * V_ROWS:(hh + 1) * V_ROWS, :]
            out.append(step(hh, kc, vtc, m, acc))
        return tuple(out)

    init = tuple((jnp.full((1, tq), NEG_BIG, F32), jnp.zeros((V_ROWS, tq), F32)) for _ in range(2))
    carry = lax.fori_loop(0, n_chunks, body, init)
    outs = []
    for hh in range(2):
        m, acc = carry[hh]
        if has_ctx:
            m, acc = step(hh, kc_ref[:, hh * HEAD_PAD:(hh + 1) * HEAD_PAD],
                          vtc_ref[0, 0, hh * V_ROWS:(hh + 1) * V_ROWS, :], m, acc)
        outs.append(acc[:VDIM] / acc[VDIM:VDIM + 1])
    o_ref[...] = jnp.concatenate(outs, axis=0).T.astype(BF16)


def _attn_bounded_kernel(*refs, n_chunks, has_ctx, tq):
    if has_ctx:
        q_ref, k_ref, vt_ref, kc_ref, vtc_ref, o_ref = refs
    else:
        q_ref, k_ref, vt_ref, o_ref = refs
    tk = vt_ref.shape[-1]

    def step(hh, kc, vtc, acc):
        s = jnp.dot(kc, q_ref[0, hh * HEAD_PAD:(hh + 1) * HEAD_PAD, :], preferred_element_type=F32)
        return acc + jnp.dot(vtc, jnp.exp2(s).astype(BF16), preferred_element_type=F32)

    def body(i, carry):
        start = pl.multiple_of(i * tk, tk)
        return tuple(step(hh, k_ref[pl.ds(start, tk), hh * HEAD_PAD:(hh + 1) * HEAD_PAD],
                          vt_ref[0, i, hh * V_ROWS:(hh + 1) * V_ROWS, :], carry[hh]) for hh in range(2))

    carry = lax.fori_loop(0, n_chunks, body, tuple(jnp.zeros((V_ROWS, tq), F32) for _ in range(2)), unroll=True)
    outs = []
    for hh in range(2):
        acc = carry[hh]
        if has_ctx:
            acc = step(hh, kc_ref[:, hh * HEAD_PAD:(hh + 1) * HEAD_PAD],
                       vtc_ref[0, 0, hh * V_ROWS:(hh + 1) * V_ROWS, :], acc)
        outs.append(acc[:VDIM] / acc[VDIM:VDIM + 1])
    o_ref[...] = jnp.concatenate(outs, axis=0).T.astype(BF16)


def _attention(q, k, vt, ctx_kv, seq, tq, bound):
    n = k.shape[0]
    batch = n // seq
    n_chunks, tk = vt.shape[1], vt.shape[3]
    q_tiles = seq // tq
    assert q.shape == (n // tq, HEADS * HEAD_PAD, tq)
    has_ctx = ctx_kv is not None
    in_specs = [
        pl.BlockSpec((1, 2 * HEAD_PAD, tq), lambda b, j, i: (b * q_tiles + i, j, 0)),
        pl.BlockSpec((seq, 2 * HEAD_PAD), lambda b, j, i: (b, j)),
        pl.BlockSpec((1, n_chunks, 2 * V_ROWS, tk), lambda b, j, i: (b, 0, j, 0)),
    ]
    args = [q, k, vt]
    if has_ctx:
        kc, vtc = ctx_kv
        lc = vtc.shape[3]
        in_specs += [pl.BlockSpec((lc, 2 * HEAD_PAD), lambda b, j, i: (b, j)),
                     pl.BlockSpec((1, 1, 2 * V_ROWS, lc), lambda b, j, i: (b, 0, j, 0))]
        args += [kc, vtc]
    def call(body, name):
        return pl.pallas_call(
            functools.partial(body, n_chunks=n_chunks, has_ctx=has_ctx, tq=tq),
            out_shape=jax.ShapeDtypeStruct((n, HEADS * VDIM), BF16),
            grid=(batch, HEADS // 2, q_tiles),
            in_specs=in_specs,
            out_specs=pl.BlockSpec((tq, 2 * VDIM), lambda b, j, i: (b * q_tiles + i, j)),
            compiler_params=_cparams(("arbitrary", "arbitrary", "arbitrary")),
            name=name,
        )(*args)

    return lax.cond(bound < ATTN_BOUND_MAX,
                    lambda: call(_attn_bounded_kernel, "attn_bounded"),
                    lambda: call(_attn_kernel, "attn_online"))


def _hy_filter_kernel(zz_ref, w1_ref, b1_ref, w2_ref, b2_ref, w3_ref, b3_ref, fr_ref, dec_ref, f_ref, ss_ref, *, emb):
    fr = fr_ref[...]
    dec = jnp.abs(dec_ref[...])
    zz = zz_ref[...]
    hdn = jnp.sin(fr * (jnp.dot(zz, w1_ref[...], preferred_element_type=F32, precision=HIGHEST) + b1_ref[...]))
    hdn = jnp.sin(fr * (jnp.dot(hdn, w2_ref[...], preferred_element_type=F32, precision=HIGHEST) + b2_ref[...]))
    k = jnp.dot(hdn, w3_ref[...], preferred_element_type=F32, precision=HIGHEST) + b3_ref[...]
    col = lax.broadcasted_iota(jnp.int32, k.shape, 1)
    k = k * jnp.exp(-jnp.where(col < HY_W, zz[:, 0:1], zz[:, emb:emb + 1]) * dec)
    first_block = pl.program_id(0) == 0
    row = lax.broadcasted_iota(jnp.int32, k.shape, 0)
    k = jnp.where(first_block & (row == 0) & (col >= HY_W), 0.0, k)
    f_ref[0] = k[:, :HY_W].astype(BF16)
    f_ref[1] = k[:, HY_W:].astype(BF16)
    hid = hdn.shape[1] // 2
    k_0 = jnp.dot(pltpu.roll(hdn[0:8], hid, 1), w3_ref[...], preferred_element_type=F32,
                  precision=HIGHEST) + b3_ref[...]
    k_0 = k_0[0:1] * jnp.exp(-zz[0:1, 0:1] * dec)
    extra = jnp.where(first_block & (col[0:1] >= HY_W), k_0 * k_0, 0.0)
    ss_ref[0] = jnp.sum(k * k, axis=0, keepdims=True) + extra


def _hy_filters(z, lw, tr):
    seq, emb = z.shape
    nblk = seq // tr
    hid2 = lw['hy_w2p'].shape[0]
    const = lambda i: (0, 0)
    zz = jnp.concatenate([z, jnp.concatenate([z[0:1], z[:0:-1]], axis=0)], axis=1)
    return pl.pallas_call(
        functools.partial(_hy_filter_kernel, emb=emb),
        out_shape=(jax.ShapeDtypeStruct((2, seq, HY_W), BF16), jax.ShapeDtypeStruct((nblk, 1, 2 * HY_W), F32)),
        grid=(nblk,),
        in_specs=[pl.BlockSpec((tr, 2 * emb), lambda i: (i, 0)),
                  pl.BlockSpec((2 * emb, hid2), const), pl.BlockSpec((1, hid2), const),
                  pl.BlockSpec((hid2, hid2), const), pl.BlockSpec((1, hid2), const),
                  pl.BlockSpec((hid2, 2 * HY_W), const), pl.BlockSpec((1, 2 * HY_W), const),
                  pl.BlockSpec((1, hid2), const), pl.BlockSpec((1, 2 * HY_W), const)],
        out_specs=(pl.BlockSpec((2, tr, HY_W), lambda i: (0, i, 0)),
                   pl.BlockSpec((1, 1, 2 * HY_W), lambda i: (i, 0, 0))),
        compiler_params=_cparams(("arbitrary",)), name="hy_filter",
    )(zz, lw['hy_w1p'], lw['hy_b1p'], lw['hy_w2p'], lw['hy_b2p'], lw['hy_w3p'], lw['hy_b3'], lw['hy_freqp'],
      lw['hy_decay'])


def _hy_conv3_kernel(p_ref, prev_ref, next_ref, w_ref, b_ref, x0_ref, u_ref, *, tiles_per_seq, tr):
    i = pl.program_id(0)
    p = p_ref[...].astype(F32)
    first = (i % tiles_per_seq) == 0
    last = (i % tiles_per_seq) == tiles_per_seq - 1
    prev_row = jnp.where(first, 0.0, prev_ref[...].astype(F32)[15:16, :])
    next_row = jnp.where(last, 0.0, next_ref[...].astype(F32)[0:1, :])
    row = lax.broadcasted_iota(jnp.int32, p.shape, 0)
    p_prev = jnp.where(row == 0, prev_row, pltpu.roll(p, 1, 0))
    p_next = jnp.where(row == tr - 1, next_row, pltpu.roll(p, tr - 1, 0))
    z = b_ref[...] + p_prev * w_ref[0:1, :] + p * w_ref[1:2, :] + p_next * w_ref[2:3, :]
    x0_ref[...] = z[:, :HY_W].astype(BF16)
    u_ref[...] = (z[:, 2 * HY_W:] * z[:, HY_W:2 * HY_W]).astype(BF16)


def _hy_conv3(phy, conv_w, conv_b, seq, tr):
    n = phy.shape[0]
    tiles_per_seq = seq // tr
    hb = tr // 16
    nhb = n // 16
    return pl.pallas_call(
        functools.partial(_hy_conv3_kernel, tiles_per_seq=tiles_per_seq, tr=tr),
        out_shape=(jax.ShapeDtypeStruct((n, HY_W), BF16), jax.ShapeDtypeStruct((n, HY_W), BF16)),
        grid=(n // tr,),
        in_specs=[pl.BlockSpec((tr, 3 * HY_W), lambda i: (i, 0)),
                  pl.BlockSpec((16, 3 * HY_W), lambda i: (jnp.maximum(i * hb - 1, 0), 0)),
                  pl.BlockSpec((16, 3 * HY_W), lambda i: (jnp.minimum((i + 1) * hb, nhb - 1), 0)),
                  pl.BlockSpec((3, 3 * HY_W), lambda i: (0, 0)),
                  pl.BlockSpec((1, 3 * HY_W), lambda i: (0, 0))],
        out_specs=(pl.BlockSpec((tr, HY_W), lambda i: (i, 0)), pl.BlockSpec((tr, HY_W), lambda i: (i, 0))),
        compiler_params=_cparams(("arbitrary",)), name="hy_conv3",
    )(phy, phy, phy, conv_w, conv_b)


def _dft_outer_kernel(wr_ref, wi_ref, u_ref, ar_ref, ai_ref):
    u = u_ref[0]
    ar_ref[0] = jnp.dot(wr_ref[...], u, preferred_element_type=F32).astype(BF16)
    ai_ref[0] = jnp.dot(wi_ref[...], u, preferred_element_type=F32).astype(BF16)


def _dft_outer(wr, wi, u, tc):
    nb, kk, cols = u.shape
    n1 = wr.shape[0]
    return pl.pallas_call(
        _dft_outer_kernel,
        out_shape=(jax.ShapeDtypeStruct((nb, n1, cols), BF16),) * 2,
        grid=(nb, cols // tc),
        in_specs=[pl.BlockSpec((n1, kk), lambda b, j: (0, 0)), pl.BlockSpec((n1, kk), lambda b, j: (0, 0)),
                  pl.BlockSpec((1, kk, tc), lambda b, j: (b, 0, j))],
        out_specs=(pl.BlockSpec((1, n1, tc), lambda b, j: (b, 0, j)),) * 2,
        compiler_params=_cparams(("arbitrary", "arbitrary")), name="dft_outer",
    )(wr, wi, u)


def _cdot(mr, mi, xr, xi):
    rr = jnp.dot(mr, xr, preferred_element_type=F32) - jnp.dot(mi, xi, preferred_element_type=F32)
    ri = jnp.dot(mr, xi, preferred_element_type=F32) + jnp.dot(mi, xr, preferred_element_type=F32)
    return rr, ri


def _dft_inner_fwd_kernel(mr_ref, mi_ref, ar_ref, ai_ref, fr_ref, fi_ref):
    for j in range(DFT_K1_STEP):
        fr, fi = _cdot(mr_ref[j], mi_ref[j], ar_ref[0, j], ai_ref[0, j])
        fr_ref[j] = fr.astype(BF16)
        fi_ref[j] = fi.astype(BF16)


def _dft_inner_fwd(mr, mi, ar, ai):
    n1, n2, _ = mr.shape
    ch = ar.shape[-1]
    mspec = pl.BlockSpec((DFT_K1_STEP, n2, n2), lambda k: (k, 0, 0))
    aspec = pl.BlockSpec((1, DFT_K1_STEP, n2, ch), lambda k: (0, k, 0, 0))
    ospec = pl.BlockSpec((DFT_K1_STEP, n2, ch), lambda k: (k, 0, 0))
    return pl.pallas_call(
        _dft_inner_fwd_kernel,
        out_shape=(jax.ShapeDtypeStruct((n1, n2, ch), BF16),) * 2,
        grid=(n1 // DFT_K1_STEP,), in_specs=[mspec, mspec, aspec, aspec], out_specs=(ospec, ospec),
        compiler_params=_cparams(("arbitrary",)), name="dft_inner_fwd",
    )(mr, mi, ar, ai)


def _dft_inner_conv_kernel(mr_ref, mi_ref, tr_ref, ti_ref, ar_ref, ai_ref, fr_ref, fi_ref, br_ref, bi_ref, *, nb):
    for j in range(DFT_K1_STEP):
        fr = fr_ref[j].astype(F32)
        fi = fi_ref[j].astype(F32)
        for b in range(nb):
            ur, ui = _cdot(mr_ref[j], mi_ref[j], ar_ref[b, j], ai_ref[b, j])
            yr = (ur * fr - ui * fi).astype(BF16)
            yi = (ur * fi + ui * fr).astype(BF16)
            br, bi = _cdot(tr_ref[j], ti_ref[j], yr, yi)
            br_ref[b, j] = br.astype(BF16)
            bi_ref[b, j] = bi.astype(BF16)


def _dft_inner_conv(mr, mi, tr, ti, ar, ai, fr, fi):
    n1, n2, _ = mr.shape
    nb, _, _, ch = ar.shape
    mspec = pl.BlockSpec((DFT_K1_STEP, n2, n2), lambda k: (k, 0, 0))
    aspec = pl.BlockSpec((nb, DFT_K1_STEP, n2, ch), lambda k: (0, k, 0, 0))
    fspec = pl.BlockSpec((DFT_K1_STEP, n2, ch), lambda k: (k, 0, 0))
    return pl.pallas_call(
        functools.partial(_dft_inner_conv_kernel, nb=nb),
        out_shape=(jax.ShapeDtypeStruct((nb, n1, n2, ch), BF16),) * 2,
        grid=(n1 // DFT_K1_STEP,), in_specs=[mspec, mspec, mspec, mspec, aspec, aspec, fspec, fspec],
        out_specs=(aspec, aspec),
        compiler_params=_cparams(("arbitrary",)), name="dft_inner_conv",
    )(mr, mi, tr, ti, ar, ai, fr, fi)


def _filter_scale(ss_ref):
    ss = jnp.sum(ss_ref[...], axis=0)
    return lax.rsqrt(ss[:, :HY_W] + ss[:, HY_W:] + EPS)


def _hy_final_kernel(cr_ref, ci_ref, br_ref, bi_ref, x0_ref, u_ref, ss_ref, bias_ref, o_ref, *, reps):
    y = (jnp.dot(cr_ref[...], br_ref[0], preferred_element_type=F32)
         + jnp.dot(ci_ref[...], bi_ref[0], preferred_element_type=F32))
    scale = jnp.tile(_filter_scale(ss_ref), (1, reps))
    bias = jnp.tile(bias_ref[...], (1, reps))
    u = u_ref[0].astype(F32)
    o_ref[0] = (x0_ref[0].astype(F32) * (y * scale + u * bias)).astype(BF16)


def _hy_final(cr, ci, br, bi, x0, u, ss, bias, tc):
    nb, n1, cols = br.shape
    n1h = cr.shape[0]
    nblk = ss.shape[0]
    cspec = pl.BlockSpec((n1h, n1), lambda b, j: (0, 0))
    bspec = pl.BlockSpec((1, n1, tc), lambda b, j: (b, 0, j))
    xspec = pl.BlockSpec((1, n1h, tc), lambda b, j: (b, 0, j))
    return pl.pallas_call(
        functools.partial(_hy_final_kernel, reps=tc // HY_W),
        out_shape=jax.ShapeDtypeStruct((nb, n1h, cols), BF16),
        grid=(nb, cols // tc),
        in_specs=[cspec, cspec, bspec, bspec, xspec, xspec,
                  pl.BlockSpec((nblk, 1, 2 * HY_W), lambda b, j: (0, 0, 0)),
                  pl.BlockSpec((1, HY_W), lambda b, j: (0, 0))],
        out_specs=xspec,
        compiler_params=_cparams(("arbitrary", "arbitrary")), name="hy_final",
    )(cr, ci, br, bi, x0, u, ss, bias)


def _hy_direct_kernel(f_ref, x0_ref, u_ref, ss_ref, bias_ref, o_ref, kk_ref, *, seq):
    kk_ref[0:seq, :] = f_ref[1].astype(F32)
    kk_ref[seq:2 * seq, :] = f_ref[0].astype(F32)
    u = u_ref[0].astype(F32)
    y = jnp.zeros((seq, HY_W), F32)
    for j in range(seq):
        y = y + kk_ref[seq - j:2 * seq - j, :] * u[j:j + 1, :]
    o_ref[0] = (x0_ref[0].astype(F32) * (y * _filter_scale(ss_ref) + u * bias_ref[...])).astype(BF16)


def _hy_direct(f, x0, u, ss, bias):
    nb, seq, _ = u.shape
    nblk = ss.shape[0]
    xspec = pl.BlockSpec((1, seq, HY_W), lambda b: (b, 0, 0))
    return pl.pallas_call(
        functools.partial(_hy_direct_kernel, seq=seq),
        out_shape=jax.ShapeDtypeStruct((nb, seq, HY_W), BF16),
        grid=(nb,),
        in_specs=[pl.BlockSpec((2, seq, HY_W), lambda b: (0, 0, 0)), xspec, xspec,
                  pl.BlockSpec((nblk, 1, 2 * HY_W), lambda b: (0, 0, 0)),
                  pl.BlockSpec((1, HY_W), lambda b: (0, 0))],
        out_specs=xspec,
        scratch_shapes=[pltpu.VMEM((2 * seq, HY_W), F32)],
        compiler_params=_cparams(("arbitrary",)), name="hy_direct",
    )(f, x0, u, ss, bias)


def _hy_embedding(seq):
    t = jnp.arange(seq, dtype=F32)
    t_unit = t / max(seq - 1, 1)
    bands = jnp.linspace(1e-4, HY_BANDS - 1, HY_BANDS, dtype=F32)
    ang = (2 * jnp.pi / seq) * t[:, None] * bands[None, :]
    return jnp.concatenate([t_unit[:, None], jnp.cos(ang), -jnp.sin(ang)], axis=-1)


def _dft_tables(seq):
    n = 2 * seq
    n2 = DFT_N2
    n1 = n // n2
    a = jnp.arange(n1, dtype=jnp.int32)
    th1 = (2 * jnp.pi / n1) * ((a[:, None] * a[None, :]) % n1).astype(F32)
    w1r, w1i = jnp.cos(th1), -jnp.sin(th1)
    b = jnp.arange(n2, dtype=jnp.int32)
    tha = (2 * jnp.pi / n) * (a[:, None] * b[None, :]).astype(F32)
    thb = (2 * jnp.pi / n2) * ((b[:, None] * b[None, :]) % n2).astype(F32)
    ar, ai = jnp.cos(tha), -jnp.sin(tha)
    br, bi = jnp.cos(thb), -jnp.sin(thb)
    mr = ar[:, None, :] * br[None] - ai[:, None, :] * bi[None]
    mi = ar[:, None, :] * bi[None] + ai[:, None, :] * br[None]
    tr = ar[:, :, None] * br[None] - ai[:, :, None] * bi[None]
    ti = -(ar[:, :, None] * bi[None] + ai[:, :, None] * br[None])
    return dict(
        w1r=w1r.astype(BF16), w1i=w1i.astype(BF16),
        mr=mr.astype(BF16), mi=mi.astype(BF16), tr=tr.astype(BF16), ti=ti.astype(BF16),
        cr=(w1r[:n1 // 2] / n).astype(BF16), ci=(w1i[:n1 // 2] / n).astype(BF16))


def _hyena_long(phy, lw, filt, seq, tabs):
    n = phy.shape[0]
    nb = n // seq
    n2 = DFT_N2
    n1 = 2 * seq // n2
    f, ss = filt
    x0, u = _hy_conv3(phy, lw['hy_conv_w'], lw['hy_conv_b'], seq, min(seq, HY_CONV_TILE))
    tc = min(n2 * HY_W, DFT_COL_TILE)
    far, fai = _dft_outer(tabs['w1r'], tabs['w1i'], f.reshape(1, n1, n2 * HY_W), tc)
    fr, fi = _dft_inner_fwd(tabs['mr'], tabs['mi'], far.reshape(1, n1, n2, HY_W), fai.reshape(1, n1, n2, HY_W))
    ar, ai = _dft_outer(tabs['w1r'][:, :n1 // 2], tabs['w1i'][:, :n1 // 2], u.reshape(nb, n1 // 2, n2 * HY_W), tc)
    br, bi = _dft_inner_conv(tabs['mr'], tabs['mi'], tabs['tr'], tabs['ti'],
                             ar.reshape(nb, n1, n2, HY_W), ai.reshape(nb, n1, n2, HY_W), fr, fi)
    out = _hy_final(tabs['cr'], tabs['ci'], br.reshape(nb, n1, n2 * HY_W), bi.reshape(nb, n1, n2 * HY_W),
                    x0.reshape(nb, n1 // 2, n2 * HY_W), u.reshape(nb, n1 // 2, n2 * HY_W), ss, lw['hy_bias'], tc)
    return out.reshape(n, HY_W)


def _hyena_short(phy, lw, filt, seq):
    n = phy.shape[0]
    nb = n // seq
    f, ss = filt
    x0, u = _hy_conv3(phy, lw['hy_conv_w'], lw['hy_conv_b'], seq, seq)
    out = _hy_direct(f, x0.reshape(nb, seq, HY_W), u.reshape(nb, seq, HY_W), ss, lw['hy_bias'])
    return out.reshape(n, HY_W)


def _pair_top2_sum(a, b, c, d):
    return jnp.maximum(jnp.maximum(jnp.maximum(a + b, a + c), jnp.maximum(a + d, b + c)),
                       jnp.maximum(b + d, c + d))


def _route(logits_t, rb):
    aff = jax.nn.sigmoid(logits_t)
    sel = aff + rb
    rows = [sel[e:e + 1, :] for e in range(N_EXPERTS)]
    affr = [aff[e:e + 1, :] for e in range(N_EXPERTS)]
    best, bidx = None, None
    for g in range(N_GROUPS):
        gs = _pair_top2_sum(*rows[g * EXP_PER_GROUP:(g + 1) * EXP_PER_GROUP])
        if g == 0:
            best, bidx = gs, jnp.zeros(gs.shape, jnp.int32)
        else:
            upd = gs > best
            bidx = jnp.where(upd, g, bidx)
            best = jnp.where(upd, gs, best)
    vals = [jnp.where(bidx == e // EXP_PER_GROUP, rows[e], -jnp.inf) for e in range(N_EXPERTS)]
    m1, i1 = vals[0], jnp.zeros(best.shape, jnp.int32)
    for e in range(1, N_EXPERTS):
        upd = vals[e] > m1
        i1 = jnp.where(upd, e, i1)
        m1 = jnp.where(upd, vals[e], m1)
    m2, i2 = jnp.full(best.shape, -jnp.inf, F32), jnp.zeros(best.shape, jnp.int32)
    for e in range(N_EXPERTS):
        cand = jnp.where(i1 == e, -jnp.inf, vals[e])
        upd = cand > m2
        i2 = jnp.where(upd, e, i2)
        m2 = jnp.where(upd, cand, m2)
    a1 = sum(jnp.where(i1 == e, affr[e], 0.0) for e in range(N_EXPERTS))
    a2 = sum(jnp.where(i2 == e, affr[e], 0.0) for e in range(N_EXPERTS))
    inv = 1.0 / (a1 + a2)
    return jnp.concatenate([i1, i2], axis=0), jnp.concatenate([a1 * inv, a2 * inv], axis=0)


def _merge_kernel(a_ref, b_ref, m_ref, gt_ref, x_ref, gate_ref, shift_ref, scale_ref, g2_ref,
                  wpa_ref, wpb_ref, wpc_ref, wout_ref, rwt_ref, rb_ref, xo_ref, h2_ref, idx_ref, wts_ref):
    y = gt_ref[:, 0:D_MODEL].astype(F32) * jnp.dot(a_ref[...], wpa_ref[...], preferred_element_type=F32)
    y = y + gt_ref[:, D_MODEL:2 * D_MODEL].astype(F32) * jnp.dot(b_ref[...], wpb_ref[...],
                                                                 preferred_element_type=F32)
    y = y + gt_ref[:, 2 * D_MODEL:].astype(F32) * jnp.dot(m_ref[...], wpc_ref[...], preferred_element_type=F32)
    y2 = jnp.dot(y.astype(BF16), wout_ref[...], preferred_element_type=F32)
    xn = x_ref[...] + gate_ref[0] * y2
    xo_ref[...] = xn
    h2 = _rms(xn) * g2_ref[...]
    h2 = h2 * (1.0 + scale_ref[0]) + shift_ref[0]
    h2_ref[...] = h2
    logits_t = lax.dot_general(rwt_ref[...], h2, (((1,), (1,)), ((), ())), preferred_element_type=F32,
                               precision=HIGHEST)
    idx_ref[...], wts_ref[...] = _route(logits_t, rb_ref[...])


def _merge(a, b, m, gt, x, gate, shift, scale, lw, rw_t, rb, seq, tm):
    n = x.shape[0]
    tiles_per_seq = seq // tm
    const = lambda i: (0, 0)
    row = lambda w: pl.BlockSpec((tm, w), lambda i: (i, 0))
    vec = pl.BlockSpec((1, 1, D_MODEL), lambda i: (i // tiles_per_seq, 0, 0))
    return pl.pallas_call(
        _merge_kernel,
        out_shape=(jax.ShapeDtypeStruct((n, D_MODEL), F32), jax.ShapeDtypeStruct((n, D_MODEL), F32),
                   jax.ShapeDtypeStruct((TOP_K, n), jnp.int32), jax.ShapeDtypeStruct((TOP_K, n), F32)),
        grid=(n // tm,),
        in_specs=[row(HEADS * VDIM), row(HY_W), row(GM_W), row(3 * D_MODEL), row(D_MODEL), vec, vec, vec,
                  pl.BlockSpec((1, D_MODEL), const),
                  pl.BlockSpec((HEADS * VDIM, D_MODEL), const), pl.BlockSpec((HY_W, D_MODEL), const),
                  pl.BlockSpec((GM_W, D_MODEL), const), pl.BlockSpec((D_MODEL, D_MODEL), const),
                  pl.BlockSpec((N_EXPERTS, D_MODEL), const), pl.BlockSpec((N_EXPERTS, 1), const)],
        out_specs=(row(D_MODEL), row(D_MODEL), pl.BlockSpec((TOP_K, tm), lambda i: (0, i)),
                   pl.BlockSpec((TOP_K, tm), lambda i: (0, i))),
        compiler_params=_cparams(("arbitrary",)), name="merge",
    )(a, b, m, gt, x, gate, shift, scale, lw['g2'], lw['w_pa'], lw['w_pb'], lw['w_pc'], lw['w_out'], rw_t, rb)


def _rows_copy(src_hbm, dst, sem, idx_ref, rows):
    return [pltpu.make_async_copy(src_hbm.at[pl.ds(idx_ref[0, 0, r], 1)], dst.at[pl.ds(r, 1)], sem)
            for r in range(rows)]


def _rows_wait(src_hbm, dst, sem, rows):
    pltpu.make_async_copy(src_hbm.at[pl.ds(0, rows)], dst, sem).wait()


def _gather_pipeline(src_hbm, cur_ref, nxt_ref, buf, sem, rows):
    i = pl.program_id(0)
    slot = i % 2

    @pl.when(i == 0)
    def _():
        for r, cp in enumerate(_rows_copy(src_hbm, buf.at[0], sem.at[0], cur_ref, rows)):
            cp.start(priority=r % 2)

    _rows_wait(src_hbm, buf.at[slot], sem.at[slot], rows)
    for r, cp in enumerate(_rows_copy(src_hbm, buf.at[1 - slot], sem.at[1 - slot], nxt_ref, rows)):
        cp.start(priority=r % 2)
    return slot


def _gather_drain(src_hbm, buf, sem, rows):
    i = pl.program_id(0)

    @pl.when(i == pl.num_programs(0) - 1)
    def _():
        _rows_wait(src_hbm, buf.at[1 - i % 2], sem.at[1 - i % 2], rows)


def _moe_dispatch_kernel(pos_ref, h_ref, xs_in_hbm, xs_hbm, sem, *, tc):
    del xs_in_hbm
    for r in range(tc):
        pltpu.make_async_copy(h_ref.at[pl.ds(r, 1)], xs_hbm.at[pl.ds(pos_ref[0, 0, r], 1)], sem).start(
            priority=r % 2)
    pltpu.make_async_copy(h_ref, xs_hbm.at[pl.ds(0, tc)], sem).wait()


def _moe_dispatch(pos_tiles, h2, rows, tc):
    n = h2.shape[0]
    return pl.pallas_call(
        functools.partial(_moe_dispatch_kernel, tc=tc),
        out_shape=jax.ShapeDtypeStruct((rows, D_MODEL), F32),
        grid=(n // tc,),
        in_specs=[pl.BlockSpec((1, 1, tc), lambda i: (i, 0, 0), memory_space=pltpu.SMEM),
                  pl.BlockSpec((tc, D_MODEL), lambda i: (i, 0)),
                  pl.BlockSpec(memory_space=pl.ANY)],
        out_specs=pl.BlockSpec(memory_space=pl.ANY),
        scratch_shapes=[pltpu.SemaphoreType.DMA],
        input_output_aliases={2: 0},
        compiler_params=_cparams(("arbitrary",)), name="moe_dispatch",
    )(pos_tiles, h2, jnp.zeros((rows, D_MODEL), F32))


def _moe_ffn_kernel(ea_ref, eb_ref, x_ref, wgua_ref, wda_ref, wgub_ref, wdb_ref, y_ref):
    del ea_ref, eb_ref
    xb = x_ref[...].astype(BF16)
    for j, (wgu_ref, wd_ref) in enumerate(((wgua_ref, wda_ref), (wgub_ref, wdb_ref))):
        gu = jnp.dot(xb, wgu_ref[0], preferred_element_type=F32)
        g = gu[:, :D_EXPERT]
        hid = (g * _sigmoid(g) * gu[:, D_EXPERT:]).astype(BF16)
        y_ref[:, j * D_MODEL:(j + 1) * D_MODEL] = jnp.dot(hid, wd_ref[0], preferred_element_type=F32)


def _moe_ffn(blk_ea, blk_eb, x_sorted, lw):
    nblk = blk_ea.shape[0]
    wgu_spec = lambda sel: pl.BlockSpec((1, D_MODEL, 2 * D_EXPERT), lambda i, ea, eb: (sel(ea, eb)[i], 0, 0))
    wd_spec = lambda sel: pl.BlockSpec((1, D_EXPERT, D_MODEL), lambda i, ea, eb: (sel(ea, eb)[i], 0, 0))
    first, second = (lambda ea, eb: ea), (lambda ea, eb: eb)
    return pl.pallas_call(
        _moe_ffn_kernel,
        out_shape=jax.ShapeDtypeStruct((nblk * MOE_BM, TOP_K * D_MODEL), F32),
        grid_spec=pltpu.PrefetchScalarGridSpec(
            num_scalar_prefetch=2, grid=(nblk,),
            in_specs=[pl.BlockSpec((MOE_BM, D_MODEL), lambda i, ea, eb: (i, 0)),
                      wgu_spec(first), wd_spec(first), wgu_spec(second), wd_spec(second)],
            out_specs=pl.BlockSpec((MOE_BM, TOP_K * D_MODEL), lambda i, ea, eb: (i, 0))),
        compiler_params=_cparams(("arbitrary",)), name="moe_ffn",
    )(blk_ea, blk_eb, x_sorted, lw['w_gu'], lw['w_d'], lw['w_gu'], lw['w_d'])


def _moe_combine_kernel(cur_ref, nxt_ref, y_hbm, x_ref, w_ref, gate_ref, o_ref, buf, sem, *, tc):
    slot = _gather_pipeline(y_hbm, cur_ref, nxt_ref, buf, sem, tc)
    w = w_ref[...]
    y = w[:, 0:1] * buf[slot, :, 0:D_MODEL] + w[:, 1:2] * buf[slot, :, D_MODEL:]
    o_ref[...] = x_ref[...] + gate_ref[0] * y
    _gather_drain(y_hbm, buf, sem, tc)


def _moe_combine(pos_tiles, y_sorted, x, w_cols, gate, seq, tc):
    n = x.shape[0]
    tiles_per_seq = seq // tc
    idx_spec = lambda off: pl.BlockSpec((1, 1, tc), lambda i: (i + off, 0, 0), memory_space=pltpu.SMEM)
    return pl.pallas_call(
        functools.partial(_moe_combine_kernel, tc=tc),
        out_shape=jax.ShapeDtypeStruct((n, D_MODEL), F32),
        grid=(n // tc,),
        in_specs=[idx_spec(0), idx_spec(1), pl.BlockSpec(memory_space=pl.ANY),
                  pl.BlockSpec((tc, D_MODEL), lambda i: (i, 0)),
                  pl.BlockSpec((tc, TOP_K), lambda i: (i, 0)),
                  pl.BlockSpec((1, 1, D_MODEL), lambda i: (i // tiles_per_seq, 0, 0))],
        out_specs=pl.BlockSpec((tc, D_MODEL), lambda i: (i, 0)),
        scratch_shapes=[pltpu.VMEM((2, tc, TOP_K * D_MODEL), F32), pltpu.SemaphoreType.DMA((2,))],
        compiler_params=_cparams(("arbitrary",)), name="moe_combine",
    )(pos_tiles, pos_tiles, y_sorted, x, w_cols, gate)


def _cumsum_rows(onehot):
    n, c = onehot.shape
    blk = math.gcd(n, 256)
    x = onehot.astype(F32).reshape(n // blk, blk, c)
    tri = (jnp.arange(blk)[:, None] >= jnp.arange(blk)[None, :]).astype(F32)
    within = jnp.einsum('ij,bjc->bic', tri, x, precision=HIGHEST)
    nb = n // blk
    before = (jnp.arange(nb)[:, None] > jnp.arange(nb)[None, :]).astype(F32)
    offset = jnp.dot(before, within[:, -1, :], precision=HIGHEST)
    return (within + offset[:, None, :]).reshape(n, c).astype(jnp.int32)


def _pair_tables():
    pairs = [(a, b) for a in range(EXP_PER_GROUP) for b in range(a + 1, EXP_PER_GROUP)]
    ea = [g * EXP_PER_GROUP + a for g in range(N_GROUPS) for a, _ in pairs]
    eb = [g * EXP_PER_GROUP + b for g in range(N_GROUPS) for _, b in pairs]
    return jnp.array(ea, jnp.int32), jnp.array(eb, jnp.int32)


def _moe(h2, idx, wts, x, gate, lw, seq):
    n = x.shape[0]
    tab_a, tab_b = _pair_tables()
    n_cls = tab_a.shape[0]
    nblk = n // MOE_BM + n_cls
    swap = idx[0] > idx[1]
    e_lo, e_hi = jnp.minimum(idx[0], idx[1]), jnp.maximum(idx[0], idx[1])
    w_cols = jnp.stack([jnp.where(swap, wts[1], wts[0]), jnp.where(swap, wts[0], wts[1])], axis=1)
    onehot = ((e_lo[:, None] == tab_a[None, :]) & (e_hi[:, None] == tab_b[None, :])).astype(jnp.int32)
    csum = _cumsum_rows(onehot)
    counts = csum[-1]
    rank = jnp.sum(csum * onehot, axis=1) - 1
    seg_len = (counts + MOE_BM - 1) // MOE_BM * MOE_BM
    seg_end = jnp.cumsum(seg_len)
    pos = jnp.sum(onehot * (seg_end - seg_len)[None, :], axis=1) + rank
    blk_start = jnp.arange(nblk, dtype=jnp.int32) * MOE_BM
    blk_cls = jnp.minimum(jnp.sum((seg_end[None, :] <= blk_start[:, None]).astype(jnp.int32), axis=1), n_cls - 1)
    td = min(n, MOE_DISPATCH_TILE)
    x_sorted = _moe_dispatch(pos.reshape(n // td, 1, td), h2, nblk * MOE_BM, td)
    y_sorted = _moe_ffn(tab_a[blk_cls], tab_b[blk_cls], x_sorted, lw)
    tc = min(seq, MOE_COMBINE_TILE)
    pos_next = jnp.concatenate([pos.reshape(n // tc, 1, tc), jnp.zeros((1, 1, tc), jnp.int32)], axis=0)
    return _moe_combine(pos_next, y_sorted, x, w_cols, gate, seq, tc)


def _head_perm():
    rope_idx = list(range(NOPE, QK, 2)) + list(range(NOPE + 1, QK, 2))
    return jnp.array(list(range(NOPE)) + rope_idx, dtype=jnp.int32)


def _block_diag(a, b):
    return jnp.concatenate([jnp.pad(a, ((0, 0), (0, b.shape[1]))), jnp.pad(b, ((0, 0), (a.shape[1], 0)))], axis=0)


def _layer_weights(p, l):
    perm = _head_perm()
    row = lambda v: v.reshape(1, -1)
    w_in = p['w_in'][l].astype(BF16)
    kr_perm = jnp.array(list(range(0, ROPE, 2)) + list(range(1, ROPE, 2)), dtype=jnp.int32)
    w_kr = jnp.pad(w_in[:, OFF_KR:OFF_HY][:, kr_perm], ((0, 0), (0, HEAD_PAD - ROPE)))
    w_in2 = jnp.concatenate([w_in[:, OFF_Q:OFF_KR], w_kr, w_in[:, OFF_HY:]], axis=1)
    w_uq = p['w_uq'][l].reshape(Q_RANK, HEADS, QK)[:, :, perm]
    w_uq = jnp.pad(w_uq, ((0, 0), (0, 0), (0, HEAD_PAD - QK))).reshape(Q_RANK, HEADS * HEAD_PAD).astype(BF16)
    w_ukv = p['w_ukv'][l].reshape(KV_RANK, HEADS, NOPE + VDIM)
    w_k = jnp.pad(w_ukv[:, :, :NOPE], ((0, 0), (0, 0), (0, HEAD_PAD - NOPE)))
    w_k = w_k.reshape(KV_RANK, HEADS * HEAD_PAD).astype(BF16)
    w_vt = w_ukv[:, :, NOPE:].reshape(KV_RANK, HEADS * VDIM).T.astype(BF16)
    pad_gain = lambda g: jnp.pad(g[perm], (0, HEAD_PAD - QK)).reshape(1, HEAD_PAD)
    gm_b = jnp.repeat(p['gm_bs'][l].T, GM_W // GM_GROUPS, axis=1)
    bound = 1.02 * QSCALE * QK * jnp.max(jnp.abs(p['g_qn'][l])) * jnp.max(jnp.abs(p['g_kn'][l]))
    pad_lane = jnp.arange(HEAD_PAD) == QK
    return dict(
        attn_bound=bound,
        q_pad=jnp.where(pad_lane, -bound, 0.0).reshape(1, HEAD_PAD).astype(F32),
        k_pad=jnp.where(pad_lane, 1.0, 0.0).reshape(1, HEAD_PAD).astype(F32),
        g1=row(p['norm1_g'][l]), g2=row(p['norm2_g'][l]), w_in=w_in2,
        g_qa=row(p['g_qa'][l]), w_uq=w_uq, g_qn=pad_gain(p['g_qn'][l]),
        g_kva=row(p['g_kva'][l]), w_k=w_k, w_vt=w_vt, g_kn=pad_gain(p['g_kn'][l]),
        gm_g=row(p['gm_norm_g'][l]), gm_w=p['gm_ws'][l].astype(BF16), gm_b=gm_b,
        hy_conv_w=p['hy_conv_w'][l], hy_conv_b=row(p['hy_conv_b'][l]),
        hy_w1p=_block_diag(p['hy_w1'][l], p['hy_w1'][l]), hy_b1p=row(jnp.tile(p['hy_b1'][l], 2)),
        hy_w2p=_block_diag(p['hy_w2'][l], p['hy_w2'][l]), hy_b2p=row(jnp.tile(p['hy_b2'][l], 2)),
        hy_w3p=_block_diag(p['hy_w3'][l][:, :HY_W], p['hy_w3'][l][:, HY_W:]), hy_b3=row(p['hy_b3'][l]),
        hy_freqp=row(jnp.tile(p['hy_freq'][l], 2)),
        hy_decay=p['hy_decay'][l].reshape(1, 2 * HY_W), hy_bias=row(p['hy_bias'][l]),
        w_pa=p['w_pa'][l].astype(BF16), w_pb=p['w_pb'][l].astype(BF16), w_pc=p['w_pc'][l].astype(BF16),
        w_out=p['w_out'][l].astype(BF16),
        w_gu=jnp.concatenate([p['moe_w_gate'][l], p['moe_w_up'][l]], axis=-1).astype(BF16),
        w_d=p['moe_w_down'][l].astype(BF16))


def _rope_tables(seq):
    rows = seq // GRID_W
    row = jnp.repeat(jnp.arange(rows, dtype=F32), GRID_W)
    col = jnp.tile(jnp.arange(GRID_W, dtype=F32), rows)
    n_freq = ROPE // 4
    inv = ROPE_THETA ** (-jnp.arange(n_freq, dtype=F32) / n_freq)
    ang = jnp.concatenate([row[:, None] * inv, col[:, None] * inv], axis=-1)
    c, s = jnp.cos(ang), jnp.sin(ang)
    z = lambda w: jnp.zeros((seq, w), F32)
    rc = jnp.concatenate([jnp.ones((seq, NOPE), F32), c, c, z(HEAD_PAD - QK)], axis=1)
    rs1 = jnp.concatenate([z(NOPE), z(ROPE // 2), s, z(HEAD_PAD - QK)], axis=1)
    rs2 = jnp.concatenate([z(NOPE), -s, z(ROPE // 2), z(HEAD_PAD - QK)], axis=1)
    return rc, rs1, rs2


def _mixer_and_ffn(x, mods, lw, rw_t, rb, seq, tm, rope_tabs, ctx_kv, filt, dft_tabs):
    shift1, scale1, gate1, shift2, scale2, gate2 = mods
    q, k, vt, m, phy, gt = _premix(x, shift1, scale1, lw, rope_tabs, seq, tm)
    a = _attention(q, k, vt, ctx_kv, seq, min(seq, ATTN_TQ), lw['attn_bound'])
    if dft_tabs is not None:
        b = _hyena_long(phy, lw, filt, seq, dft_tabs)
    else:
        b = _hyena_short(phy, lw, filt, seq)
    xn, h2, idx, wts = _merge(a, b, m, gt, x, gate1, shift2, scale2, lw, rw_t, rb, seq, min(seq, MERGE_TILE))
    return _moe(h2, idx, wts, xn, gate2, lw, seq), k, vt


def _forward(p):
    x, ctx = p['x'], p['ctx']
    batch, seq, _ = x.shape
    lc = ctx.shape[1]
    depth = p['w_mod'].shape[0]

    cvecs = jnp.concatenate([p['c'], p['c_ctx'][None], jnp.zeros((8 - batch - 1, D_MODEL), F32)], axis=0)
    mod_all = _modvec(cvecs, p['w_mod'], p['b_mod'])
    rw_t = p['router_w'].T
    rb = p['router_b'].reshape(N_EXPERTS, 1)
    rope_tabs = _rope_tables(seq)
    dft_tabs = _dft_tables(seq)
    z_lat, z_ctx = _hy_embedding(seq), _hy_embedding(lc)

    xl = x.reshape(batch * seq, D_MODEL)
    xc = ctx.reshape(batch * lc, D_MODEL)
    tm_lat = min(seq, TOKEN_TILE)
    for l in range(depth):
        lw = _layer_weights(p, l)
        mod = mod_all[l].reshape(8, N_MOD, D_MODEL)
        mods_lat = [mod[:batch, j].reshape(batch, 1, D_MODEL) for j in range(N_MOD)]
        mods_ctx = [jnp.broadcast_to(mod[batch, j].reshape(1, 1, D_MODEL), (batch, 1, D_MODEL))
                    for j in range(N_MOD)]
        if l == depth - 1:
            _, k_c, vt_c, _, _, _ = _premix(xc, mods_ctx[0], mods_ctx[1], lw, None, lc, lc)
        else:
            xc, k_c, vt_c = _mixer_and_ffn(xc, mods_ctx, lw, rw_t, rb, lc, lc, None, None,
                                           _hy_filters(z_ctx, lw, lc), None)
        filt = _hy_filters(z_lat, lw, min(seq, HY_FILTER_TILE))
        xl, _, _ = _mixer_and_ffn(xl, mods_lat, lw, rw_t, rb, seq, tm_lat, rope_tabs, (k_c, vt_c), filt, dft_tabs)
    return xl.reshape(batch, seq, D_MODEL)


def kernel(x, c, ctx, c_ctx, w_mod, b_mod, norm1_g, norm2_g, w_in, g_qa, w_uq, g_kva, w_ukv, g_qn, g_kn,
           hy_conv_w, hy_conv_b, hy_w1, hy_b1, hy_w2, hy_b2, hy_w3, hy_b3, hy_freq, hy_decay, hy_bias,
           gm_norm_g, gm_ws, gm_bs, w_pa, w_pb, w_pc, w_out, router_w, router_b,
           moe_w_gate, moe_w_up, moe_w_down):
    return _forward(dict(
        x=x, c=c, ctx=ctx, c_ctx=c_ctx, w_mod=w_mod, b_mod=b_mod, norm1_g=norm1_g, norm2_g=norm2_g, w_in=w_in,
        g_qa=g_qa, w_uq=w_uq, g_kva=g_kva, w_ukv=w_ukv, g_qn=g_qn, g_kn=g_kn, hy_conv_w=hy_conv_w,
        hy_conv_b=hy_conv_b, hy_w1=hy_w1, hy_b1=hy_b1, hy_w2=hy_w2, hy_b2=hy_b2, hy_w3=hy_w3, hy_b3=hy_b3,
        hy_freq=hy_freq, hy_decay=hy_decay, hy_bias=hy_bias, gm_norm_g=gm_norm_g, gm_ws=gm_ws, gm_bs=gm_bs,
        w_pa=w_pa, w_pb=w_pb, w_pc=w_pc, w_out=w_out, router_w=router_w, router_b=router_b,
        moe_w_gate=moe_w_gate, moe_w_up=moe_w_up, moe_w_down=moe_w_down))
```

```python
import functools
import math

import jax
import jax.numpy as jnp
from jax import lax
from jax.experimental import pallas as pl
from jax.experimental.pallas import tpu as pltpu

F32 = jnp.float32
BF16 = jnp.bfloat16
HIGHEST = lax.Precision.HIGHEST

D_MODEL = 1024
GRID_W = 64
EPS = 1e-6
N_MOD = 6

HEADS = 8
Q_RANK = 384
KV_RANK = 256
NOPE = 64
ROPE = 32
QK = NOPE + ROPE
VDIM = 64
HEAD_PAD = 128
ROPE_THETA = 10000.0
V_ROWS = 80

HY_W = 256
HY_BANDS = 16
GM_W = 256
GM_CHUNK = 128
GM_GROUPS = 4

OFF_Q = 0
OFF_KV = OFF_Q + Q_RANK
OFF_KR = OFF_KV + KV_RANK
OFF_HY = OFF_KR + ROPE
OFF_GM = OFF_HY + 3 * HY_W
OFF_GT = OFF_GM + 2 * GM_W

P_Q = 0
P_KV = P_Q + Q_RANK
P_KR = P_KV + KV_RANK
P_HY = P_KR + HEAD_PAD
P_GM = P_HY + 3 * HY_W
P_GT = P_GM + 2 * GM_W
P_W = P_GT + 3 * D_MODEL

N_EXPERTS = 16
N_GROUPS = 4
EXP_PER_GROUP = 4
TOP_K = 2
D_EXPERT = 512
MOE_BM = 256
MOE_DISPATCH_TILE = 1024
MOE_COMBINE_TILE = 512

DFT_N2 = 256
DFT_K1_STEP = 4
DFT_COL_TILE = 8192

TOKEN_TILE = 512
MERGE_TILE = 1024
HY_CONV_TILE = 1024
HY_FILTER_TILE = 2048

VMEM_LIMIT = 56 * 1024 * 1024
ATTN_TQ = 512
ATTN_TK = 2048
ATTN_BOUND_MAX = 50.0
NEG_BIG = -1e30
LOG2E = 1.4426950408889634
QSCALE = QK ** -0.5 * LOG2E


def _cparams(sem):
    return pltpu.CompilerParams(dimension_semantics=sem, vmem_limit_bytes=VMEM_LIMIT)


def _rms(x):
    return x * lax.rsqrt(jnp.mean(x * x, axis=-1, keepdims=True) + EPS)


def _sigmoid(x):
    return 0.5 * jnp.tanh(0.5 * x) + 0.5


def _nt_dot(a, b):
    return lax.dot_general(a, b, (((1,), (1,)), ((), ())), preferred_element_type=F32)


def _modvec_kernel(c_ref, w_ref, b_ref, o_ref):
    cv = c_ref[...]
    s = cv * jax.nn.sigmoid(cv)
    o_ref[0] = jnp.dot(s, w_ref[0], preferred_element_type=F32, precision=HIGHEST) + b_ref[0]


def _modvec(cvecs, w_mod, b_mod):
    depth = w_mod.shape[0]
    tn = 1536
    return pl.pallas_call(
        _modvec_kernel,
        out_shape=jax.ShapeDtypeStruct((depth, 8, N_MOD * D_MODEL), F32),
        grid=(depth, N_MOD * D_MODEL // tn),
        in_specs=[pl.BlockSpec((8, D_MODEL), lambda l, j: (0, 0)),
                  pl.BlockSpec((1, D_MODEL, tn), lambda l, j: (l, 0, j)),
                  pl.BlockSpec((1, 1, tn), lambda l, j: (l, 0, j))],
        out_specs=pl.BlockSpec((1, 8, tn), lambda l, j: (l, 0, j)),
        compiler_params=_cparams(("arbitrary", "arbitrary")),
        name="modvec",
    )(cvecs, w_mod, b_mod.reshape(depth, 1, N_MOD * D_MODEL))


def _head_norm_rope(xh, gain, rope):
    ms = jnp.sum(xh * xh, axis=-1, keepdims=True) * (1.0 / QK)
    xh = xh * lax.rsqrt(ms + EPS) * gain
    if rope is not None:
        rc, rs1, rs2 = rope
        xh = xh * rc + pltpu.roll(xh, ROPE // 2, 1) * rs1 + pltpu.roll(xh, HEAD_PAD - ROPE // 2, 1) * rs2
    return xh


def _premix_kernel(*refs, use_rope, tm):
    if use_rope:
        (x_ref, shift_ref, scale_ref, g1_ref, win_ref, gqa_ref, wuq_ref, gqn_ref, gkva_ref, wk_ref, wvt_ref,
         gkn_ref, gmg_ref, gmw_ref, gmb_ref, qpad_ref, kpad_ref, rc_ref, rs1_ref, rs2_ref,
         q_ref, k_ref, vt_ref, m_ref, hy_ref, gt_ref) = refs
        rope = (rc_ref[...], rs1_ref[...], rs2_ref[...])
    else:
        (x_ref, shift_ref, scale_ref, g1_ref, win_ref, gqa_ref, wuq_ref, gqn_ref, gkva_ref, wk_ref, wvt_ref,
         gkn_ref, gmg_ref, gmw_ref, gmb_ref, qpad_ref, kpad_ref,
         q_ref, k_ref, vt_ref, m_ref, hy_ref, gt_ref) = refs
        rope = None

    x = x_ref[...]
    h = _rms(x) * g1_ref[...]
    h = h * (1.0 + scale_ref[0]) + shift_ref[0]
    hb = h.astype(BF16)

    def proj(lo, width):
        return jnp.dot(hb, win_ref[:, lo:lo + width], preferred_element_type=F32)

    qa = (_rms(proj(P_Q, Q_RANK)) * gqa_ref[...]).astype(BF16)
    q = jnp.dot(qa, wuq_ref[...], preferred_element_type=F32)
    for hh in range(HEADS):
        qh = _head_norm_rope(q[:, hh * HEAD_PAD:(hh + 1) * HEAD_PAD], gqn_ref[...], rope)
        q_ref[0, hh * HEAD_PAD:(hh + 1) * HEAD_PAD, :] = (qh * QSCALE + qpad_ref[...]).T.astype(BF16)

    kva = (_rms(proj(P_KV, KV_RANK)) * gkva_ref[...]).astype(BF16)
    kr = pltpu.roll(proj(P_KR, HEAD_PAD), NOPE, 1)
    kn = jnp.dot(kva, wk_ref[...], preferred_element_type=F32)
    for hh in range(HEADS):
        kh = _head_norm_rope(kn[:, hh * HEAD_PAD:(hh + 1) * HEAD_PAD] + kr, gkn_ref[...], rope)
        k_ref[:, hh * HEAD_PAD:(hh + 1) * HEAD_PAD] = (kh + kpad_ref[...]).astype(BF16)
    vt = _nt_dot(wvt_ref[...], kva)
    row = lax.broadcasted_iota(jnp.int32, (V_ROWS - VDIM, tm), 0)
    ones_rows = jnp.where(row == 0, 1.0, 0.0).astype(BF16)
    for hh in range(HEADS):
        vt_ref[0, 0, hh * V_ROWS:hh * V_ROWS + VDIM, :] = vt[hh * VDIM:(hh + 1) * VDIM].astype(BF16)
        vt_ref[0, 0, hh * V_ROWS + VDIM:(hh + 1) * V_ROWS, :] = ones_rows

    gg = jax.nn.gelu(proj(P_GM, 2 * GM_W), approximate=True)
    gu = gg[:, :GM_W]
    gv = (_rms(gg[:, GM_W:]) * gmg_ref[...]).astype(BF16)
    grp = lax.broadcasted_iota(jnp.int32, (GM_CHUNK, GM_W), 1) // (GM_W // GM_GROUPS)
    for ci in range(tm // GM_CHUNK):
        vc = gv[ci * GM_CHUNK:(ci + 1) * GM_CHUNK]
        s = jnp.zeros((GM_CHUNK, GM_W), F32)
        for g in range(GM_GROUPS):
            sg = jnp.dot(gmw_ref[g], vc, preferred_element_type=F32)
            s = jnp.where(grp == g, sg, s)
        m_ref[ci * GM_CHUNK:(ci + 1) * GM_CHUNK, :] = (
            gu[ci * GM_CHUNK:(ci + 1) * GM_CHUNK] * (s + gmb_ref[...])).astype(BF16)

    hy_ref[...] = proj(P_HY, 3 * HY_W).astype(BF16)
    for j in range(3):
        gt_ref[:, j * D_MODEL:(j + 1) * D_MODEL] = _sigmoid(proj(P_GT + j * D_MODEL, D_MODEL)).astype(BF16)


def _premix(x, shift, scale, lw, rope_tabs, seq, tm):
    n = x.shape[0]
    tiles_per_seq = seq // tm
    batch = n // seq
    use_rope = rope_tabs is not None
    const = lambda i: (0, 0)
    in_specs = [
        pl.BlockSpec((tm, D_MODEL), lambda i: (i, 0)),
        pl.BlockSpec((1, 1, D_MODEL), lambda i: (i // tiles_per_seq, 0, 0)),
        pl.BlockSpec((1, 1, D_MODEL), lambda i: (i // tiles_per_seq, 0, 0)),
        pl.BlockSpec((1, D_MODEL), const),
        pl.BlockSpec((D_MODEL, P_W), const, pipeline_mode=pl.Buffered(1)),
        pl.BlockSpec((1, Q_RANK), const),
        pl.BlockSpec((Q_RANK, HEADS * HEAD_PAD), const),
        pl.BlockSpec((1, HEAD_PAD), const),
        pl.BlockSpec((1, KV_RANK), const),
        pl.BlockSpec((KV_RANK, HEADS * HEAD_PAD), const),
        pl.BlockSpec((HEADS * VDIM, KV_RANK), const),
        pl.BlockSpec((1, HEAD_PAD), const),
        pl.BlockSpec((1, GM_W), const),
        pl.BlockSpec((GM_GROUPS, GM_CHUNK, GM_CHUNK), lambda i: (0, 0, 0)),
        pl.BlockSpec((GM_CHUNK, GM_W), const),
        pl.BlockSpec((1, HEAD_PAD), const),
        pl.BlockSpec((1, HEAD_PAD), const),
    ]
    args = [x, shift, scale, lw['g1'], lw['w_in'], lw['g_qa'], lw['w_uq'], lw['g_qn'], lw['g_kva'], lw['w_k'],
            lw['w_vt'], lw['g_kn'], lw['gm_g'], lw['gm_w'], lw['gm_b'], lw['q_pad'], lw['k_pad']]
    tkv = min(seq, ATTN_TK)
    sub = tkv // tm
    if use_rope:
        in_specs += [pl.BlockSpec((tm, HEAD_PAD), lambda i: (i % tiles_per_seq, 0))] * 3
        args += list(rope_tabs)
    out_shape = (
        jax.ShapeDtypeStruct((n // tm, HEADS * HEAD_PAD, tm), BF16),
        jax.ShapeDtypeStruct((n, HEADS * HEAD_PAD), BF16),
        jax.ShapeDtypeStruct((batch, seq // tkv, HEADS * V_ROWS, tkv), BF16),
        jax.ShapeDtypeStruct((n, GM_W), BF16),
        jax.ShapeDtypeStruct((n, 3 * HY_W), BF16),
        jax.ShapeDtypeStruct((n, 3 * D_MODEL), BF16),
    )
    out_specs = (
        pl.BlockSpec((1, HEADS * HEAD_PAD, tm), lambda i: (i, 0, 0)),
        pl.BlockSpec((tm, HEADS * HEAD_PAD), lambda i: (i, 0)),
        pl.BlockSpec((1, 1, HEADS * V_ROWS, tm),
                     lambda i: (i // tiles_per_seq, (i % tiles_per_seq) // sub, 0, (i % tiles_per_seq) % sub)),
        pl.BlockSpec((tm, GM_W), lambda i: (i, 0)),
        pl.BlockSpec((tm, 3 * HY_W), lambda i: (i, 0)),
        pl.BlockSpec((tm, 3 * D_MODEL), lambda i: (i, 0)),
    )
    return pl.pallas_call(
        functools.partial(_premix_kernel, use_rope=use_rope, tm=tm),
        out_shape=out_shape, grid=(n // tm,), in_specs=in_specs, out_specs=out_specs,
        compiler_params=_cparams(("arbitrary",)), name="premix",
    )(*args)


def _attn_kernel(*refs, n_chunks, has_ctx, tq):
    if has_ctx:
        q_ref, k_ref, vt_ref, kc_ref, vtc_ref, o_ref = refs
    else:
        q_ref, k_ref, vt_ref, o_ref = refs
    tk = vt_ref.shape[-1]

    def step(hh, kc, vtc, m, acc):
        s = jnp.dot(kc, q_ref[0, hh * HEAD_PAD:(hh + 1) * HEAD_PAD, :], preferred_element_type=F32)
        m_new = jnp.maximum(m, jnp.max(s, axis=0, keepdims=True))
        p = jnp.exp2(s - m_new).astype(BF16)
        alpha = jnp.exp2(m - m_new)
        return m_new, acc * alpha + jnp.dot(vtc, p, preferred_element_type=F32)

    def body(i, carry):
        out = []
        for hh in range(2):
            m, acc = carry[hh]
            start = pl.multiple_of(i * tk, tk)
            kc = k_ref[pl.ds(start, tk), hh * HEAD_PAD:(hh + 1) * HEAD_PAD]
            vtc = vt_ref[0, i, hh * V_ROWS:(hh + 1) * V_ROWS, :]
            out.append(step(hh, kc, vtc, m, acc))
        return tuple(out)

    init = tuple((jnp.full((1, tq), NEG_BIG, F32), jnp.zeros((V_ROWS, tq), F32)) for _ in range(2))
    carry = lax.fori_loop(0, n_chunks, body, init)
    outs = []
    for hh in range(2):
        m, acc = carry[hh]
        if has_ctx:
            m, acc = step(hh, kc_ref[:, hh * HEAD_PAD:(hh + 1) * HEAD_PAD],
                          vtc_ref[0, 0, hh * V_ROWS:(hh + 1) * V_ROWS, :], m, acc)
        outs.append(acc[:VDIM] / acc[VDIM:VDIM + 1])
    o_ref[...] = jnp.concatenate(outs, axis=0).T.astype(BF16)


def _attn_bounded_kernel(*refs, n_chunks, has_ctx, tq):
    if has_ctx:
        q_ref, k_ref, vt_ref, kc_ref, vtc_ref, o_ref = refs
    else:
        q_ref, k_ref, vt_ref, o_ref = refs
    tk = vt_ref.shape[-1]

    def step(hh, kc, vtc, acc):
        s = jnp.dot(kc, q_ref[0, hh * HEAD_PAD:(hh + 1) * HEAD_PAD, :], preferred_element_type=F32)
        return acc + jnp.dot(vtc, jnp.exp2(s).astype(BF16), preferred_element_type=F32)

    def body(i, carry):
        start = pl.multiple_of(i * tk, tk)
        return tuple(step(hh, k_ref[pl.ds(start, tk), hh * HEAD_PAD:(hh + 1) * HEAD_PAD],
                          vt_ref[0, i, hh * V_ROWS:(hh + 1) * V_ROWS, :], carry[hh]) for hh in range(2))

    carry = lax.fori_loop(0, n_chunks, body, tuple(jnp.zeros((V_ROWS, tq), F32) for _ in range(2)), unroll=True)
    outs = []
    for hh in range(2):
        acc = carry[hh]
        if has_ctx:
            acc = step(hh, kc_ref[:, hh * HEAD_PAD:(hh + 1) * HEAD_PAD],
                       vtc_ref[0, 0, hh * V_ROWS:(hh + 1) * V_ROWS, :], acc)
        outs.append(acc[:VDIM] / acc[VDIM:VDIM + 1])
    o_ref[...] = jnp.concatenate(outs, axis=0).T.astype(BF16)


def _attention(q, k, vt, ctx_kv, seq, tq, bound):
    n = k.shape[0]
    batch = n // seq
    n_chunks, tk = vt.shape[1], vt.shape[3]
    q_tiles = seq // tq
    assert q.shape == (n // tq, HEADS * HEAD_PAD, tq)
    has_ctx = ctx_kv is not None
    in_specs = [
        pl.BlockSpec((1, 2 * HEAD_PAD, tq), lambda b, j, i: (b * q_tiles + i, j, 0)),
        pl.BlockSpec((seq, 2 * HEAD_PAD), lambda b, j, i: (b, j)),
        pl.BlockSpec((1, n_chunks, 2 * V_ROWS, tk), lambda b, j, i: (b, 0, j, 0)),
    ]
    args = [q, k, vt]
    if has_ctx:
        kc, vtc = ctx_kv
        lc = vtc.shape[3]
        in_specs += [pl.BlockSpec((lc, 2 * HEAD_PAD), lambda b, j, i: (b, j)),
                     pl.BlockSpec((1, 1, 2 * V_ROWS, lc), lambda b, j, i: (b, 0, j, 0))]
        args += [kc, vtc]
    def call(body, name):
        return pl.pallas_call(
            functools.partial(body, n_chunks=n_chunks, has_ctx=has_ctx, tq=tq),
            out_shape=jax.ShapeDtypeStruct((n, HEADS * VDIM), BF16),
            grid=(batch, HEADS // 2, q_tiles),
            in_specs=in_specs,
            out_specs=pl.BlockSpec((tq, 2 * VDIM), lambda b, j, i: (b * q_tiles + i, j)),
            compiler_params=_cparams(("arbitrary", "arbitrary", "arbitrary")),
            name=name,
        )(*args)

    return lax.cond(bound < ATTN_BOUND_MAX,
                    lambda: call(_attn_bounded_kernel, "attn_bounded"),
                    lambda: call(_attn_kernel, "attn_online"))


def _hy_filter_kernel(zz_ref, w1_ref, b1_ref, w2_ref, b2_ref, w3_ref, b3_ref, fr_ref, dec_ref, f_ref, ss_ref, *, emb):
    fr = fr_ref[...]
    dec = jnp.abs(dec_ref[...])
    zz = zz_ref[...]
    hdn = jnp.sin(fr * (jnp.dot(zz, w1_ref[...], preferred_element_type=F32, precision=HIGHEST) + b1_ref[...]))
    hdn = jnp.sin(fr * (jnp.dot(hdn, w2_ref[...], preferred_element_type=F32, precision=HIGHEST) + b2_ref[...]))
    k = jnp.dot(hdn, w3_ref[...], preferred_element_type=F32, precision=HIGHEST) + b3_ref[...]
    col = lax.broadcasted_iota(jnp.int32, k.shape, 1)
    k = k * jnp.exp(-jnp.where(col < HY_W, zz[:, 0:1], zz[:, emb:emb + 1]) * dec)
    first_block = pl.program_id(0) == 0
    row = lax.broadcasted_iota(jnp.int32, k.shape, 0)
    k = jnp.where(first_block & (row == 0) & (col >= HY_W), 0.0, k)
    f_ref[0] = k[:, :HY_W].astype(BF16)
    f_ref[1] = k[:, HY_W:].astype(BF16)
    hid = hdn.shape[1] // 2
    k_0 = jnp.dot(pltpu.roll(hdn[0:8], hid, 1), w3_ref[...], preferred_element_type=F32,
                  precision=HIGHEST) + b3_ref[...]
    k_0 = k_0[0:1] * jnp.exp(-zz[0:1, 0:1] * dec)
    extra = jnp.where(first_block & (col[0:1] >= HY_W), k_0 * k_0, 0.0)
    ss_ref[0] = jnp.sum(k * k, axis=0, keepdims=True) + extra


def _hy_filters(z, lw, tr):
    seq, emb = z.shape
    nblk = seq // tr
    hid2 = lw['hy_w2p'].shape[0]
    const = lambda i: (0, 0)
    zz = jnp.concatenate([z, jnp.concatenate([z[0:1], z[:0:-1]], axis=0)], axis=1)
    return pl.pallas_call(
        functools.partial(_hy_filter_kernel, emb=emb),
        out_shape=(jax.ShapeDtypeStruct((2, seq, HY_W), BF16), jax.ShapeDtypeStruct((nblk, 1, 2 * HY_W), F32)),
        grid=(nblk,),
        in_specs=[pl.BlockSpec((tr, 2 * emb), lambda i: (i, 0)),
                  pl.BlockSpec((2 * emb, hid2), const), pl.BlockSpec((1, hid2), const),
                  pl.BlockSpec((hid2, hid2), const), pl.BlockSpec((1, hid2), const),
                  pl.BlockSpec((hid2, 2 * HY_W), const), pl.BlockSpec((1, 2 * HY_W), const),
                  pl.BlockSpec((1, hid2), const), pl.BlockSpec((1, 2 * HY_W), const)],
        out_specs=(pl.BlockSpec((2, tr, HY_W), lambda i: (0, i, 0)),
                   pl.BlockSpec((1, 1, 2 * HY_W), lambda i: (i, 0, 0))),
        compiler_params=_cparams(("arbitrary",)), name="hy_filter",
    )(zz, lw['hy_w1p'], lw['hy_b1p'], lw['hy_w2p'], lw['hy_b2p'], lw['hy_w3p'], lw['hy_b3'], lw['hy_freqp'],
      lw['hy_decay'])


def _hy_conv3_kernel(p_ref, prev_ref, next_ref, w_ref, b_ref, x0_ref, u_ref, *, tiles_per_seq, tr):
    i = pl.program_id(0)
    p = p_ref[...].astype(F32)
    first = (i % tiles_per_seq) == 0
    last = (i % tiles_per_seq) == tiles_per_seq - 1
    prev_row = jnp.where(first, 0.0, prev_ref[...].astype(F32)[15:16, :])
    next_row = jnp.where(last, 0.0, next_ref[...].astype(F32)[0:1, :])
    row = lax.broadcasted_iota(jnp.int32, p.shape, 0)
    p_prev = jnp.where(row == 0, prev_row, pltpu.roll(p, 1, 0))
    p_next = jnp.where(row == tr - 1, next_row, pltpu.roll(p, tr - 1, 0))
    z = b_ref[...] + p_prev * w_ref[0:1, :] + p * w_ref[1:2, :] + p_next * w_ref[2:3, :]
    x0_ref[...] = z[:, :HY_W].astype(BF16)
    u_ref[...] = (z[:, 2 * HY_W:] * z[:, HY_W:2 * HY_W]).astype(BF16)


def _hy_conv3(phy, conv_w, conv_b, seq, tr):
    n = phy.shape[0]
    tiles_per_seq = seq // tr
    hb = tr // 16
    nhb = n // 16
    return pl.pallas_call(
        functools.partial(_hy_conv3_kernel, tiles_per_seq=tiles_per_seq, tr=tr),
        out_shape=(jax.ShapeDtypeStruct((n, HY_W), BF16), jax.ShapeDtypeStruct((n, HY_W), BF16)),
        grid=(n // tr,),
        in_specs=[pl.BlockSpec((tr, 3 * HY_W), lambda i: (i, 0)),
                  pl.BlockSpec((16, 3 * HY_W), lambda i: (jnp.maximum(i * hb - 1, 0), 0)),
                  pl.BlockSpec((16, 3 * HY_W), lambda i: (jnp.minimum((i + 1) * hb, nhb - 1), 0)),
                  pl.BlockSpec((3, 3 * HY_W), lambda i: (0, 0)),
                  pl.BlockSpec((1, 3 * HY_W), lambda i: (0, 0))],
        out_specs=(pl.BlockSpec((tr, HY_W), lambda i: (i, 0)), pl.BlockSpec((tr, HY_W), lambda i: (i, 0))),
        compiler_params=_cparams(("arbitrary",)), name="hy_conv3",
    )(phy, phy, phy, conv_w, conv_b)


def _dft_outer_kernel(wr_ref, wi_ref, u_ref, ar_ref, ai_ref):
    u = u_ref[0]
    ar_ref[0] = jnp.dot(wr_ref[...], u, preferred_element_type=F32).astype(BF16)
    ai_ref[0] = jnp.dot(wi_ref[...], u, preferred_element_type=F32).astype(BF16)


def _dft_outer(wr, wi, u, tc):
    nb, kk, cols = u.shape
    n1 = wr.shape[0]
    return pl.pallas_call(
        _dft_outer_kernel,
        out_shape=(jax.ShapeDtypeStruct((nb, n1, cols), BF16),) * 2,
        grid=(nb, cols // tc),
        in_specs=[pl.BlockSpec((n1, kk), lambda b, j: (0, 0)), pl.BlockSpec((n1, kk), lambda b, j: (0, 0)),
                  pl.BlockSpec((1, kk, tc), lambda b, j: (b, 0, j))],
        out_specs=(pl.BlockSpec((1, n1, tc), lambda b, j: (b, 0, j)),) * 2,
        compiler_params=_cparams(("arbitrary", "arbitrary")), name="dft_outer",
    )(wr, wi, u)


def _cdot(mr, mi, xr, xi):
    rr = jnp.dot(mr, xr, preferred_element_type=F32) - jnp.dot(mi, xi, preferred_element_type=F32)
    ri = jnp.dot(mr, xi, preferred_element_type=F32) + jnp.dot(mi, xr, preferred_element_type=F32)
    return rr, ri


def _dft_inner_fwd_kernel(mr_ref, mi_ref, ar_ref, ai_ref, fr_ref, fi_ref):
    for j in range(DFT_K1_STEP):
        fr, fi = _cdot(mr_ref[j], mi_ref[j], ar_ref[0, j], ai_ref[0, j])
        fr_ref[j] = fr.astype(BF16)
        fi_ref[j] = fi.astype(BF16)


def _dft_inner_fwd(mr, mi, ar, ai):
    n1, n2, _ = mr.shape
    ch = ar.shape[-1]
    mspec = pl.BlockSpec((DFT_K1_STEP, n2, n2), lambda k: (k, 0, 0))
    aspec = pl.BlockSpec((1, DFT_K1_STEP, n2, ch), lambda k: (0, k, 0, 0))
    ospec = pl.BlockSpec((DFT_K1_STEP, n2, ch), lambda k: (k, 0, 0))
    return pl.pallas_call(
        _dft_inner_fwd_kernel,
        out_shape=(jax.ShapeDtypeStruct((n1, n2, ch), BF16),) * 2,
        grid=(n1 // DFT_K1_STEP,), in_specs=[mspec, mspec, aspec, aspec], out_specs=(ospec, ospec),
        compiler_params=_cparams(("arbitrary",)), name="dft_inner_fwd",
    )(mr, mi, ar, ai)


def _dft_inner_conv_kernel(mr_ref, mi_ref, tr_ref, ti_ref, ar_ref, ai_ref, fr_ref, fi_ref, br_ref, bi_ref, *, nb):
    for j in range(DFT_K1_STEP):
        fr = fr_ref[j].astype(F32)
        fi = fi_ref[j].astype(F32)
        for b in range(nb):
            ur, ui = _cdot(mr_ref[j], mi_ref[j], ar_ref[b, j], ai_ref[b, j])
            yr = (ur * fr - ui * fi).astype(BF16)
            yi = (ur * fi + ui * fr).astype(BF16)
            br, bi = _cdot(tr_ref[j], ti_ref[j], yr, yi)
            br_ref[b, j] = br.astype(BF16)
            bi_ref[b, j] = bi.astype(BF16)


def _dft_inner_conv(mr, mi, tr, ti, ar, ai, fr, fi):
    n1, n2, _ = mr.shape
    nb, _, _, ch = ar.shape
    mspec = pl.BlockSpec((DFT_K1_STEP, n2, n2), lambda k: (k, 0, 0))
    aspec = pl.BlockSpec((nb, DFT_K1_STEP, n2, ch), lambda k: (0, k, 0, 0))
    fspec = pl.BlockSpec((DFT_K1_STEP, n2, ch), lambda k: (k, 0, 0))
    return pl.pallas_call(
        functools.partial(_dft_inner_conv_kernel, nb=nb),
        out_shape=(jax.ShapeDtypeStruct((nb, n1, n2, ch), BF16),) * 2,
        grid=(n1 // DFT_K1_STEP,), in_specs=[mspec, mspec, mspec, mspec, aspec, aspec, fspec, fspec],
        out_specs=(aspec, aspec),
        compiler_params=_cparams(("arbitrary",)), name="dft_inner_conv",
    )(mr, mi, tr, ti, ar, ai, fr, fi)


def _filter_scale(ss_ref):
    ss = jnp.sum(ss_ref[...], axis=0)
    return lax.rsqrt(ss[:, :HY_W] + ss[:, HY_W:] + EPS)


def _hy_final_kernel(cr_ref, ci_ref, br_ref, bi_ref, x0_ref, u_ref, ss_ref, bias_ref, o_ref, *, reps):
    y = (jnp.dot(cr_ref[...], br_ref[0], preferred_element_type=F32)
         + jnp.dot(ci_ref[...], bi_ref[0], preferred_element_type=F32))
    scale = jnp.tile(_filter_scale(ss_ref), (1, reps))
    bias = jnp.tile(bias_ref[...], (1, reps))
    u = u_ref[0].astype(F32)
    o_ref[0] = (x0_ref[0].astype(F32) * (y * scale + u * bias)).astype(BF16)


def _hy_final(cr, ci, br, bi, x0, u, ss, bias, tc):
    nb, n1, cols = br.shape
    n1h = cr.shape[0]
    nblk = ss.shape[0]
    cspec = pl.BlockSpec((n1h, n1), lambda b, j: (0, 0))
    bspec = pl.BlockSpec((1, n1, tc), lambda b, j: (b, 0, j))
    xspec = pl.BlockSpec((1, n1h, tc), lambda b, j: (b, 0, j))
    return pl.pallas_call(
        functools.partial(_hy_final_kernel, reps=tc // HY_W),
        out_shape=jax.ShapeDtypeStruct((nb, n1h, cols), BF16),
        grid=(nb, cols // tc),
        in_specs=[cspec, cspec, bspec, bspec, xspec, xspec,
                  pl.BlockSpec((nblk, 1, 2 * HY_W), lambda b, j: (0, 0, 0)),
                  pl.BlockSpec((1, HY_W), lambda b, j: (0, 0))],
        out_specs=xspec,
        compiler_params=_cparams(("arbitrary", "arbitrary")), name="hy_final",
    )(cr, ci, br, bi, x0, u, ss, bias)


def _hy_direct_kernel(f_ref, x0_ref, u_ref, ss_ref, bias_ref, o_ref, kk_ref, *, seq):
    kk_ref[0:seq, :] = f_ref[1].astype(F32)
    kk_ref[seq:2 * seq, :] = f_ref[0].astype(F32)
    u = u_ref[0].astype(F32)
    y = jnp.zeros((seq, HY_W), F32)
    for j in range(seq):
        y = y + kk_ref[seq - j:2 * seq - j, :] * u[j:j + 1, :]
    o_ref[0] = (x0_ref[0].astype(F32) * (y * _filter_scale(ss_ref) + u * bias_ref[...])).astype(BF16)


def _hy_direct(f, x0, u, ss, bias):
    nb, seq, _ = u.shape
    nblk = ss.shape[0]
    xspec = pl.BlockSpec((1, seq, HY_W), lambda b: (b, 0, 0))
    return pl.pallas_call(
        functools.partial(_hy_direct_kernel, seq=seq),
        out_shape=jax.ShapeDtypeStruct((nb, seq, HY_W), BF16),
        grid=(nb,),
        in_specs=[pl.BlockSpec((2, seq, HY_W), lambda b: (0, 0, 0)), xspec, xspec,
                  pl.BlockSpec((nblk, 1, 2 * HY_W), lambda b: (0, 0, 0)),
                  pl.BlockSpec((1, HY_W), lambda b: (0, 0))],
        out_specs=xspec,
        scratch_shapes=[pltpu.VMEM((2 * seq, HY_W), F32)],
        compiler_params=_cparams(("arbitrary",)), name="hy_direct",
    )(f, x0, u, ss, bias)


def _hy_embedding(seq):
    t = jnp.arange(seq, dtype=F32)
    t_unit = t / max(seq - 1, 1)
    bands = jnp.linspace(1e-4, HY_BANDS - 1, HY_BANDS, dtype=F32)
    ang = (2 * jnp.pi / seq) * t[:, None] * bands[None, :]
    return jnp.concatenate([t_unit[:, None], jnp.cos(ang), -jnp.sin(ang)], axis=-1)


def _dft_tables(seq):
    n = 2 * seq
    n2 = DFT_N2
    n1 = n // n2
    a = jnp.arange(n1, dtype=jnp.int32)
    th1 = (2 * jnp.pi / n1) * ((a[:, None] * a[None, :]) % n1).astype(F32)
    w1r, w1i = jnp.cos(th1), -jnp.sin(th1)
    b = jnp.arange(n2, dtype=jnp.int32)
    tha = (2 * jnp.pi / n) * (a[:, None] * b[None, :]).astype(F32)
    thb = (2 * jnp.pi / n2) * ((b[:, None] * b[None, :]) % n2).astype(F32)
    ar, ai = jnp.cos(tha), -jnp.sin(tha)
    br, bi = jnp.cos(thb), -jnp.sin(thb)
    mr = ar[:, None, :] * br[None] - ai[:, None, :] * bi[None]
    mi = ar[:, None, :] * bi[None] + ai[:, None, :] * br[None]
    tr = ar[:, :, None] * br[None] - ai[:, :, None] * bi[None]
    ti = -(ar[:, :, None] * bi[None] + ai[:, :, None] * br[None])
    return dict(
        w1r=w1r.astype(BF16), w1i=w1i.astype(BF16),
        mr=mr.astype(BF16), mi=mi.astype(BF16), tr=tr.astype(BF16), ti=ti.astype(BF16),
        cr=(w1r[:n1 // 2] / n).astype(BF16), ci=(w1i[:n1 // 2] / n).astype(BF16))


def _hyena_long(phy, lw, filt, seq, tabs):
    n = phy.shape[0]
    nb = n // seq
    n2 = DFT_N2
    n1 = 2 * seq // n2
    f, ss = filt
    x0, u = _hy_conv3(phy, lw['hy_conv_w'], lw['hy_conv_b'], seq, min(seq, HY_CONV_TILE))
    tc = min(n2 * HY_W, DFT_COL_TILE)
    far, fai = _dft_outer(tabs['w1r'], tabs['w1i'], f.reshape(1, n1, n2 * HY_W), tc)
    fr, fi = _dft_inner_fwd(tabs['mr'], tabs['mi'], far.reshape(1, n1, n2, HY_W), fai.reshape(1, n1, n2, HY_W))
    ar, ai = _dft_outer(tabs['w1r'][:, :n1 // 2], tabs['w1i'][:, :n1 // 2], u.reshape(nb, n1 // 2, n2 * HY_W), tc)
    br, bi = _dft_inner_conv(tabs['mr'], tabs['mi'], tabs['tr'], tabs['ti'],
                             ar.reshape(nb, n1, n2, HY_W), ai.reshape(nb, n1, n2, HY_W), fr, fi)
    out = _hy_final(tabs['cr'], tabs['ci'], br.reshape(nb, n1, n2 * HY_W), bi.reshape(nb, n1, n2 * HY_W),
                    x0.reshape(nb, n1 // 2, n2 * HY_W), u.reshape(nb, n1 // 2, n2 * HY_W), ss, lw['hy_bias'], tc)
    return out.reshape(n, HY_W)


def _hyena_short(phy, lw, filt, seq):
    n = phy.shape[0]
    nb = n // seq
    f, ss = filt
    x0, u = _hy_conv3(phy, lw['hy_conv_w'], lw['hy_conv_b'], seq, seq)
    out = _hy_direct(f, x0.reshape(nb, seq, HY_W), u.reshape(nb, seq, HY_W), ss, lw['hy_bias'])
    return out.reshape(n, HY_W)


def _pair_top2_sum(a, b, c, d):
    return jnp.maximum(jnp.maximum(jnp.maximum(a + b, a + c), jnp.maximum(a + d, b + c)),
                       jnp.maximum(b + d, c + d))


def _route(logits_t, rb):
    aff = jax.nn.sigmoid(logits_t)
    sel = aff + rb
    rows = [sel[e:e + 1, :] for e in range(N_EXPERTS)]
    affr = [aff[e:e + 1, :] for e in range(N_EXPERTS)]
    best, bidx = None, None
    for g in range(N_GROUPS):
        gs = _pair_top2_sum(*rows[g * EXP_PER_GROUP:(g + 1) * EXP_PER_GROUP])
        if g == 0:
            best, bidx = gs, jnp.zeros(gs.shape, jnp.int32)
        else:
            upd = gs > best
            bidx = jnp.where(upd, g, bidx)
            best = jnp.where(upd, gs, best)
    vals = [jnp.where(bidx == e // EXP_PER_GROUP, rows[e], -jnp.inf) for e in range(N_EXPERTS)]
    m1, i1 = vals[0], jnp.zeros(best.shape, jnp.int32)
    for e in range(1, N_EXPERTS):
        upd = vals[e] > m1
        i1 = jnp.where(upd, e, i1)
        m1 = jnp.where(upd, vals[e], m1)
    m2, i2 = jnp.full(best.shape, -jnp.inf, F32), jnp.zeros(best.shape, jnp.int32)
    for e in range(N_EXPERTS):
        cand = jnp.where(i1 == e, -jnp.inf, vals[e])
        upd = cand > m2
        i2 = jnp.where(upd, e, i2)
        m2 = jnp.where(upd, cand, m2)
    a1 = sum(jnp.where(i1 == e, affr[e], 0.0) for e in range(N_EXPERTS))
    a2 = sum(jnp.where(i2 == e, affr[e], 0.0) for e in range(N_EXPERTS))
    inv = 1.0 / (a1 + a2)
    return jnp.concatenate([i1, i2], axis=0), jnp.concatenate([a1 * inv, a2 * inv], axis=0)


def _merge_kernel(a_ref, b_ref, m_ref, gt_ref, x_ref, gate_ref, shift_ref, scale_ref, g2_ref,
                  wpa_ref, wpb_ref, wpc_ref, wout_ref, rwt_ref, rb_ref, xo_ref, h2_ref, idx_ref, wts_ref):
    y = gt_ref[:, 0:D_MODEL].astype(F32) * jnp.dot(a_ref[...], wpa_ref[...], preferred_element_type=F32)
    y = y + gt_ref[:, D_MODEL:2 * D_MODEL].astype(F32) * jnp.dot(b_ref[...], wpb_ref[...],
                                                                 preferred_element_type=F32)
    y = y + gt_ref[:, 2 * D_MODEL:].astype(F32) * jnp.dot(m_ref[...], wpc_ref[...], preferred_element_type=F32)
    y2 = jnp.dot(y.astype(BF16), wout_ref[...], preferred_element_type=F32)
    xn = x_ref[...] + gate_ref[0] * y2
    xo_ref[...] = xn
    h2 = _rms(xn) * g2_ref[...]
    h2 = h2 * (1.0 + scale_ref[0]) + shift_ref[0]
    h2_ref[...] = h2
    logits_t = lax.dot_general(rwt_ref[...], h2, (((1,), (1,)), ((), ())), preferred_element_type=F32,
                               precision=HIGHEST)
    idx_ref[...], wts_ref[...] = _route(logits_t, rb_ref[...])


def _merge(a, b, m, gt, x, gate, shift, scale, lw, rw_t, rb, seq, tm):
    n = x.shape[0]
    tiles_per_seq = seq // tm
    const = lambda i: (0, 0)
    row = lambda w: pl.BlockSpec((tm, w), lambda i: (i, 0))
    vec = pl.BlockSpec((1, 1, D_MODEL), lambda i: (i // tiles_per_seq, 0, 0))
    return pl.pallas_call(
        _merge_kernel,
        out_shape=(jax.ShapeDtypeStruct((n, D_MODEL), F32), jax.ShapeDtypeStruct((n, D_MODEL), F32),
                   jax.ShapeDtypeStruct((TOP_K, n), jnp.int32), jax.ShapeDtypeStruct((TOP_K, n), F32)),
        grid=(n // tm,),
        in_specs=[row(HEADS * VDIM), row(HY_W), row(GM_W), row(3 * D_MODEL), row(D_MODEL), vec, vec, vec,
                  pl.BlockSpec((1, D_MODEL), const),
                  pl.BlockSpec((HEADS * VDIM, D_MODEL), const), pl.BlockSpec((HY_W, D_MODEL), const),
                  pl.BlockSpec((GM_W, D_MODEL), const), pl.BlockSpec((D_MODEL, D_MODEL), const),
                  pl.BlockSpec((N_EXPERTS, D_MODEL), const), pl.BlockSpec((N_EXPERTS, 1), const)],
        out_specs=(row(D_MODEL), row(D_MODEL), pl.BlockSpec((TOP_K, tm), lambda i: (0, i)),
                   pl.BlockSpec((TOP_K, tm), lambda i: (0, i))),
        compiler_params=_cparams(("arbitrary",)), name="merge",
    )(a, b, m, gt, x, gate, shift, scale, lw['g2'], lw['w_pa'], lw['w_pb'], lw['w_pc'], lw['w_out'], rw_t, rb)


def _rows_copy(src_hbm, dst, sem, idx_ref, rows):
    return [pltpu.make_async_copy(src_hbm.at[pl.ds(idx_ref[0, 0, r], 1)], dst.at[pl.ds(r, 1)], sem)
            for r in range(rows)]


def _rows_wait(src_hbm, dst, sem, rows):
    pltpu.make_async_copy(src_hbm.at[pl.ds(0, rows)], dst, sem).wait()


def _gather_pipeline(src_hbm, cur_ref, nxt_ref, buf, sem, rows):
    i = pl.program_id(0)
    slot = i % 2

    @pl.when(i == 0)
    def _():
        for cp in _rows_copy(src_hbm, buf.at[0], sem.at[0], cur_ref, rows):
            cp.start(priority=1)

    _rows_wait(src_hbm, buf.at[slot], sem.at[slot], rows)
    for cp in _rows_copy(src_hbm, buf.at[1 - slot], sem.at[1 - slot], nxt_ref, rows):
        cp.start(priority=1)
    return slot


def _gather_drain(src_hbm, buf, sem, rows):
    i = pl.program_id(0)

    @pl.when(i == pl.num_programs(0) - 1)
    def _():
        _rows_wait(src_hbm, buf.at[1 - i % 2], sem.at[1 - i % 2], rows)


def _moe_dispatch_kernel(pos_ref, h_ref, xs_in_hbm, xs_hbm, sem, *, tc):
    del xs_in_hbm
    for r in range(tc):
        pltpu.make_async_copy(h_ref.at[pl.ds(r, 1)], xs_hbm.at[pl.ds(pos_ref[0, 0, r], 1)], sem).start(
            priority=r % 2)
    pltpu.make_async_copy(h_ref, xs_hbm.at[pl.ds(0, tc)], sem).wait()


def _moe_dispatch(pos_tiles, h2, rows, tc):
    n = h2.shape[0]
    return pl.pallas_call(
        functools.partial(_moe_dispatch_kernel, tc=tc),
        out_shape=jax.ShapeDtypeStruct((rows, D_MODEL), F32),
        grid=(n // tc,),
        in_specs=[pl.BlockSpec((1, 1, tc), lambda i: (i, 0, 0), memory_space=pltpu.SMEM),
                  pl.BlockSpec((tc, D_MODEL), lambda i: (i, 0)),
                  pl.BlockSpec(memory_space=pl.ANY)],
        out_specs=pl.BlockSpec(memory_space=pl.ANY),
        scratch_shapes=[pltpu.SemaphoreType.DMA],
        input_output_aliases={2: 0},
        compiler_params=_cparams(("arbitrary",)), name="moe_dispatch",
    )(pos_tiles, h2, jnp.zeros((rows, D_MODEL), F32))


def _moe_ffn_kernel(ea_ref, eb_ref, x_ref, wgua_ref, wda_ref, wgub_ref, wdb_ref, y_ref):
    del ea_ref, eb_ref
    xb = x_ref[...].astype(BF16)
    for j, (wgu_ref, wd_ref) in enumerate(((wgua_ref, wda_ref), (wgub_ref, wdb_ref))):
        gu = jnp.dot(xb, wgu_ref[0], preferred_element_type=F32)
        g = gu[:, :D_EXPERT]
        hid = (g * _sigmoid(g) * gu[:, D_EXPERT:]).astype(BF16)
        y_ref[:, j * D_MODEL:(j + 1) * D_MODEL] = jnp.dot(hid, wd_ref[0], preferred_element_type=F32)


def _moe_ffn(blk_ea, blk_eb, x_sorted, lw):
    nblk = blk_ea.shape[0]
    wgu_spec = lambda sel: pl.BlockSpec((1, D_MODEL, 2 * D_EXPERT), lambda i, ea, eb: (sel(ea, eb)[i], 0, 0))
    wd_spec = lambda sel: pl.BlockSpec((1, D_EXPERT, D_MODEL), lambda i, ea, eb: (sel(ea, eb)[i], 0, 0))
    first, second = (lambda ea, eb: ea), (lambda ea, eb: eb)
    return pl.pallas_call(
        _moe_ffn_kernel,
        out_shape=jax.ShapeDtypeStruct((nblk * MOE_BM, TOP_K * D_MODEL), F32),
        grid_spec=pltpu.PrefetchScalarGridSpec(
            num_scalar_prefetch=2, grid=(nblk,),
            in_specs=[pl.BlockSpec((MOE_BM, D_MODEL), lambda i, ea, eb: (i, 0)),
                      wgu_spec(first), wd_spec(first), wgu_spec(second), wd_spec(second)],
            out_specs=pl.BlockSpec((MOE_BM, TOP_K * D_MODEL), lambda i, ea, eb: (i, 0))),
        compiler_params=_cparams(("arbitrary",)), name="moe_ffn",
    )(blk_ea, blk_eb, x_sorted, lw['w_gu'], lw['w_d'], lw['w_gu'], lw['w_d'])


def _moe_combine_kernel(cur_ref, nxt_ref, y_hbm, x_ref, w_ref, gate_ref, o_ref, buf, sem, *, tc):
    slot = _gather_pipeline(y_hbm, cur_ref, nxt_ref, buf, sem, tc)
    w = w_ref[...]
    y = w[:, 0:1] * buf[slot, :, 0:D_MODEL] + w[:, 1:2] * buf[slot, :, D_MODEL:]
    o_ref[...] = x_ref[...] + gate_ref[0] * y
    _gather_drain(y_hbm, buf, sem, tc)


def _moe_combine(pos_tiles, y_sorted, x, w_cols, gate, seq, tc):
    n = x.shape[0]
    tiles_per_seq = seq // tc
    idx_spec = lambda off: pl.BlockSpec((1, 1, tc), lambda i: (i + off, 0, 0), memory_space=pltpu.SMEM)
    return pl.pallas_call(
        functools.partial(_moe_combine_kernel, tc=tc),
        out_shape=jax.ShapeDtypeStruct((n, D_MODEL), F32),
        grid=(n // tc,),
        in_specs=[idx_spec(0), idx_spec(1), pl.BlockSpec(memory_space=pl.ANY),
                  pl.BlockSpec((tc, D_MODEL), lambda i: (i, 0)),
                  pl.BlockSpec((tc, TOP_K), lambda i: (i, 0)),
                  pl.BlockSpec((1, 1, D_MODEL), lambda i: (i // tiles_per_seq, 0, 0))],
        out_specs=pl.BlockSpec((tc, D_MODEL), lambda i: (i, 0)),
        scratch_shapes=[pltpu.VMEM((2, tc, TOP_K * D_MODEL), F32), pltpu.SemaphoreType.DMA((2,))],
        compiler_params=_cparams(("arbitrary",)), name="moe_combine",
    )(pos_tiles, pos_tiles, y_sorted, x, w_cols, gate)


def _cumsum_rows(onehot):
    n, c = onehot.shape
    blk = math.gcd(n, 256)
    x = onehot.astype(F32).reshape(n // blk, blk, c)
    tri = (jnp.arange(blk)[:, None] >= jnp.arange(blk)[None, :]).astype(F32)
    within = jnp.einsum('ij,bjc->bic', tri, x, precision=HIGHEST)
    nb = n // blk
    before = (jnp.arange(nb)[:, None] > jnp.arange(nb)[None, :]).astype(F32)
    offset = jnp.dot(before, within[:, -1, :], precision=HIGHEST)
    return (within + offset[:, None, :]).reshape(n, c).astype(jnp.int32)


def _pair_tables():
    pairs = [(a, b) for a in range(EXP_PER_GROUP) for b in range(a + 1, EXP_PER_GROUP)]
    ea = [g * EXP_PER_GROUP + a for g in range(N_GROUPS) for a, _ in pairs]
    eb = [g * EXP_PER_GROUP + b for g in range(N_GROUPS) for _, b in pairs]
    return jnp.array(ea, jnp.int32), jnp.array(eb, jnp.int32)


def _moe(h2, idx, wts, x, gate, lw, seq):
    n = x.shape[0]
    tab_a, tab_b = _pair_tables()
    n_cls = tab_a.shape[0]
    nblk = n // MOE_BM + n_cls
    swap = idx[0] > idx[1]
    e_lo, e_hi = jnp.minimum(idx[0], idx[1]), jnp.maximum(idx[0], idx[1])
    w_cols = jnp.stack([jnp.where(swap, wts[1], wts[0]), jnp.where(swap, wts[0], wts[1])], axis=1)
    onehot = ((e_lo[:, None] == tab_a[None, :]) & (e_hi[:, None] == tab_b[None, :])).astype(jnp.int32)
    csum = _cumsum_rows(onehot)
    counts = csum[-1]
    rank = jnp.sum(csum * onehot, axis=1) - 1
    seg_len = (counts + MOE_BM - 1) // MOE_BM * MOE_BM
    seg_end = jnp.cumsum(seg_len)
    pos = jnp.sum(onehot * (seg_end - seg_len)[None, :], axis=1) + rank
    blk_start = jnp.arange(nblk, dtype=jnp.int32) * MOE_BM
    blk_cls = jnp.minimum(jnp.sum((seg_end[None, :] <= blk_start[:, None]).astype(jnp.int32), axis=1), n_cls - 1)
    td = min(n, MOE_DISPATCH_TILE)
    x_sorted = _moe_dispatch(pos.reshape(n // td, 1, td), h2, nblk * MOE_BM, td)
    y_sorted = _moe_ffn(tab_a[blk_cls], tab_b[blk_cls], x_sorted, lw)
    tc = min(seq, MOE_COMBINE_TILE)
    pos_next = jnp.concatenate([pos.reshape(n // tc, 1, tc), jnp.zeros((1, 1, tc), jnp.int32)], axis=0)
    return _moe_combine(pos_next, y_sorted, x, w_cols, gate, seq, tc)


def _head_perm():
    rope_idx = list(range(NOPE, QK, 2)) + list(range(NOPE + 1, QK, 2))
    return jnp.array(list(range(NOPE)) + rope_idx, dtype=jnp.int32)


def _block_diag(a, b):
    return jnp.concatenate([jnp.pad(a, ((0, 0), (0, b.shape[1]))), jnp.pad(b, ((0, 0), (a.shape[1], 0)))], axis=0)


def _layer_weights(p, l):
    perm = _head_perm()
    row = lambda v: v.reshape(1, -1)
    w_in = p['w_in'][l].astype(BF16)
    kr_perm = jnp.array(list(range(0, ROPE, 2)) + list(range(1, ROPE, 2)), dtype=jnp.int32)
    w_kr = jnp.pad(w_in[:, OFF_KR:OFF_HY][:, kr_perm], ((0, 0), (0, HEAD_PAD - ROPE)))
    w_in2 = jnp.concatenate([w_in[:, OFF_Q:OFF_KR], w_kr, w_in[:, OFF_HY:]], axis=1)
    w_uq = p['w_uq'][l].reshape(Q_RANK, HEADS, QK)[:, :, perm]
    w_uq = jnp.pad(w_uq, ((0, 0), (0, 0), (0, HEAD_PAD - QK))).reshape(Q_RANK, HEADS * HEAD_PAD).astype(BF16)
    w_ukv = p['w_ukv'][l].reshape(KV_RANK, HEADS, NOPE + VDIM)
    w_k = jnp.pad(w_ukv[:, :, :NOPE], ((0, 0), (0, 0), (0, HEAD_PAD - NOPE)))
    w_k = w_k.reshape(KV_RANK, HEADS * HEAD_PAD).astype(BF16)
    w_vt = w_ukv[:, :, NOPE:].reshape(KV_RANK, HEADS * VDIM).T.astype(BF16)
    pad_gain = lambda g: jnp.pad(g[perm], (0, HEAD_PAD - QK)).reshape(1, HEAD_PAD)
    gm_b = jnp.repeat(p['gm_bs'][l].T, GM_W // GM_GROUPS, axis=1)
    bound = 1.02 * QSCALE * QK * jnp.max(jnp.abs(p['g_qn'][l])) * jnp.max(jnp.abs(p['g_kn'][l]))
    pad_lane = jnp.arange(HEAD_PAD) == QK
    return dict(
        attn_bound=bound,
        q_pad=jnp.where(pad_lane, -bound, 0.0).reshape(1, HEAD_PAD).astype(F32),
        k_pad=jnp.where(pad_lane, 1.0, 0.0).reshape(1, HEAD_PAD).astype(F32),
        g1=row(p['norm1_g'][l]), g2=row(p['norm2_g'][l]), w_in=w_in2,
        g_qa=row(p['g_qa'][l]), w_uq=w_uq, g_qn=pad_gain(p['g_qn'][l]),
        g_kva=row(p['g_kva'][l]), w_k=w_k, w_vt=w_vt, g_kn=pad_gain(p['g_kn'][l]),
        gm_g=row(p['gm_norm_g'][l]), gm_w=p['gm_ws'][l].astype(BF16), gm_b=gm_b,
        hy_conv_w=p['hy_conv_w'][l], hy_conv_b=row(p['hy_conv_b'][l]),
        hy_w1p=_block_diag(p['hy_w1'][l], p['hy_w1'][l]), hy_b1p=row(jnp.tile(p['hy_b1'][l], 2)),
        hy_w2p=_block_diag(p['hy_w2'][l], p['hy_w2'][l]), hy_b2p=row(jnp.tile(p['hy_b2'][l], 2)),
        hy_w3p=_block_diag(p['hy_w3'][l][:, :HY_W], p['hy_w3'][l][:, HY_W:]), hy_b3=row(p['hy_b3'][l]),
        hy_freqp=row(jnp.tile(p['hy_freq'][l], 2)),
        hy_decay=p['hy_decay'][l].reshape(1, 2 * HY_W), hy_bias=row(p['hy_bias'][l]),
        w_pa=p['w_pa'][l].astype(BF16), w_pb=p['w_pb'][l].astype(BF16), w_pc=p['w_pc'][l].astype(BF16),
        w_out=p['w_out'][l].astype(BF16),
        w_gu=jnp.concatenate([p['moe_w_gate'][l], p['moe_w_up'][l]], axis=-1).astype(BF16),
        w_d=p['moe_w_down'][l].astype(BF16))


def _rope_tables(seq):
    rows = seq // GRID_W
    row = jnp.repeat(jnp.arange(rows, dtype=F32), GRID_W)
    col = jnp.tile(jnp.arange(GRID_W, dtype=F32), rows)
    n_freq = ROPE // 4
    inv = ROPE_THETA ** (-jnp.arange(n_freq, dtype=F32) / n_freq)
    ang = jnp.concatenate([row[:, None] * inv, col[:, None] * inv], axis=-1)
    c, s = jnp.cos(ang), jnp.sin(ang)
    z = lambda w: jnp.zeros((seq, w), F32)
    rc = jnp.concatenate([jnp.ones((seq, NOPE), F32), c, c, z(HEAD_PAD - QK)], axis=1)
    rs1 = jnp.concatenate([z(NOPE), z(ROPE // 2), s, z(HEAD_PAD - QK)], axis=1)
    rs2 = jnp.concatenate([z(NOPE), -s, z(ROPE // 2), z(HEAD_PAD - QK)], axis=1)
    return rc, rs1, rs2


def _mixer_and_ffn(x, mods, lw, rw_t, rb, seq, tm, rope_tabs, ctx_kv, filt, dft_tabs):
    shift1, scale1, gate1, shift2, scale2, gate2 = mods
    q, k, vt, m, phy, gt = _premix(x, shift1, scale1, lw, rope_tabs, seq, tm)
    a = _attention(q, k, vt, ctx_kv, seq, min(seq, ATTN_TQ), lw['attn_bound'])
    if dft_tabs is not None:
        b = _hyena_long(phy, lw, filt, seq, dft_tabs)
    else:
        b = _hyena_short(phy, lw, filt, seq)
    xn, h2, idx, wts = _merge(a, b, m, gt, x, gate1, shift2, scale2, lw, rw_t, rb, seq, min(seq, MERGE_TILE))
    return _moe(h2, idx, wts, xn, gate2, lw, seq), k, vt


def _forward(p):
    x, ctx = p['x'], p['ctx']
    batch, seq, _ = x.shape
    lc = ctx.shape[1]
    depth = p['w_mod'].shape[0]

    cvecs = jnp.concatenate([p['c'], p['c_ctx'][None], jnp.zeros((8 - batch - 1, D_MODEL), F32)], axis=0)
    mod_all = _modvec(cvecs, p['w_mod'], p['b_mod'])
    rw_t = p['router_w'].T
    rb = p['router_b'].reshape(N_EXPERTS, 1)
    rope_tabs = _rope_tables(seq)
    dft_tabs = _dft_tables(seq)
    z_lat, z_ctx = _hy_embedding(seq), _hy_embedding(lc)

    xl = x.reshape(batch * seq, D_MODEL)
    xc = ctx.reshape(batch * lc, D_MODEL)
    tm_lat = min(seq, TOKEN_TILE)
    for l in range(depth):
        lw = _layer_weights(p, l)
        mod = mod_all[l].reshape(8, N_MOD, D_MODEL)
        mods_lat = [mod[:batch, j].reshape(batch, 1, D_MODEL) for j in range(N_MOD)]
        mods_ctx = [jnp.broadcast_to(mod[batch, j].reshape(1, 1, D_MODEL), (batch, 1, D_MODEL))
                    for j in range(N_MOD)]
        if l == depth - 1:
            _, k_c, vt_c, _, _, _ = _premix(xc, mods_ctx[0], mods_ctx[1], lw, None, lc, lc)
        else:
            xc, k_c, vt_c = _mixer_and_ffn(xc, mods_ctx, lw, rw_t, rb, lc, lc, None, None,
                                           _hy_filters(z_ctx, lw, lc), None)
        filt = _hy_filters(z_lat, lw, min(seq, HY_FILTER_TILE))
        xl, _, _ = _mixer_and_ffn(xl, mods_lat, lw, rw_t, rb, seq, tm_lat, rope_tabs, (k_c, vt_c), filt, dft_tabs)
    return xl.reshape(batch, seq, D_MODEL)


def kernel(x, c, ctx, c_ctx, w_mod, b_mod, norm1_g, norm2_g, w_in, g_qa, w_uq, g_kva, w_ukv, g_qn, g_kn,
           hy_conv_w, hy_conv_b, hy_w1, hy_b1, hy_w2, hy_b2, hy_w3, hy_b3, hy_freq, hy_decay, hy_bias,
           gm_norm_g, gm_ws, gm_bs, w_pa, w_pb, w_pc, w_out, router_w, router_b,
           moe_w_gate, moe_w_up, moe_w_down):
    return _forward(dict(
        x=x, c=c, ctx=ctx, c_ctx=c_ctx, w_mod=w_mod, b_mod=b_mod, norm1_g=norm1_g, norm2_g=norm2_g, w_in=w_in,
        g_qa=g_qa, w_uq=w_uq, g_kva=g_kva, w_ukv=w_ukv, g_qn=g_qn, g_kn=g_kn, hy_conv_w=hy_conv_w,
        hy_conv_b=hy_conv_b, hy_w1=hy_w1, hy_b1=hy_b1, hy_w2=hy_w2, hy_b2=hy_b2, hy_w3=hy_w3, hy_b3=hy_b3,
        hy_freq=hy_freq, hy_decay=hy_decay, hy_bias=hy_bias, gm_norm_g=gm_norm_g, gm_ws=gm_ws, gm_bs=gm_bs,
        w_pa=w_pa, w_pb=w_pb, w_pc=w_pc, w_out=w_out, router_w=router_w, router_b=router_b,
        moe_w_gate=moe_w_gate, moe_w_up=moe_w_up, moe_w_down=moe_w_down))
```
